```python
import math
import jax, jax.numpy as jnp
from jax import lax
import numpy as np

D_MODEL = 2048
BATCH = 2
SEQ = 4096
DEPTH = 4
DEC_BATCH = 8
DEC_SEQ = 4
PAST_LEN = 16384
PAGE_SIZE = 128

HEAD_DIM = 128
NSA_HEADS = D_MODEL // (2 * HEAD_DIM)
NSA_KV_HEADS = 2
NSA_GROUP = NSA_HEADS // NSA_KV_HEADS
NSA_WIDTH = NSA_HEADS * HEAD_DIM
NSA_BLOCK = 64
NSA_TOP_N = 16
NSA_WINDOW = 512
GDN_HEADS = D_MODEL // (2 * HEAD_DIM)
GDN_WIDTH = GDN_HEADS * HEAD_DIM
GDN_CONV = 4
GDN_CHUNK = 64
MIX_WIDTH = NSA_WIDTH + GDN_WIDTH
D_FF = (11 * D_MODEL) // 4
FFN_CONV = 3
KV_COLS = NSA_KV_HEADS * 2 * HEAD_DIM
IN_COLS = NSA_WIDTH + 3 * KV_COLS + 3 * NSA_HEADS + 4 * GDN_WIDTH + 2 * GDN_HEADS
DN_ALPHA = (2 * DEPTH) ** 0.25
DN_BETA = (8 * DEPTH) ** -0.25
LN_EPS = 1e-5
NORM_EPS = 1e-6
NEG = -1e30
FORCE = 1e4
SLC_QBLOCK = 64
WIN_QBLOCK = 128

kernel_name = 'hybrid_nsa_gdn_decoder_step'

F32 = jnp.float32


def split_sizes():
    sizes = [NSA_WIDTH, KV_COLS, KV_COLS, KV_COLS, 3 * NSA_HEADS,
             3 * GDN_WIDTH, GDN_WIDTH, GDN_HEADS, GDN_HEADS]
    return np.cumsum(sizes)[:-1].tolist()


def alibi_slopes():
    h = np.arange(1, NSA_HEADS + 1, dtype=np.float32)
    m = np.power(np.float32(2.0), -8.0 * h / NSA_HEADS).astype(np.float32)
    return jnp.asarray(m, dtype=F32).reshape(NSA_KV_HEADS, NSA_GROUP)


def layer_norm(x, g, b):
    xf = x.astype(F32)
    mu = jnp.mean(xf, -1, keepdims=True)
    var = jnp.mean(jnp.square(xf - mu), -1, keepdims=True)
    return ((xf - mu) * lax.rsqrt(var + LN_EPS) * g.astype(F32) + b.astype(F32)).astype(x.dtype)


def l2norm(t):
    tf = t.astype(F32)
    return tf * lax.rsqrt(jnp.sum(tf * tf, -1, keepdims=True) + NORM_EPS)


def causal_dwconv(x, buf, w):
    K = w.shape[0]
    T = x.shape[1]
    ext = jnp.concatenate([buf.astype(x.dtype), x], axis=1)
    y = sum(ext[:, j:j + T] * w[j] for j in range(K))
    return y, ext[:, T:]


def compress_kv(kv, pe, w1, w2):
    B, L = kv.shape[:2]
    nc = L // NSA_BLOCK
    blk = kv[:, :nc * NSA_BLOCK].reshape(B, nc, NSA_BLOCK, NSA_KV_HEADS, 2, HEAD_DIM)
    blk = blk + pe[None, None, :, None]
    flat = blk.transpose(0, 1, 3, 4, 2, 5).reshape(B, nc, NSA_KV_HEADS, 2, NSA_BLOCK * HEAD_DIM)
    hid = jax.nn.silu(jnp.einsum('bngcf,cfe->bngce', flat, w1))
    return jnp.einsum('bngce,ced->bngcd', hid, w2)


def cmp_slc_block(q, qpos, ck, cv, fetch, n_blocks):
    scale = HEAD_DIM ** -0.5
    slopes = alibi_slopes()[None, None, :, :, None]
    nc = ck.shape[1]
    cpos = jnp.arange(nc, dtype=jnp.int32) * NSA_BLOCK + (NSA_BLOCK - 1)
    dist_c = (qpos[:, None] - cpos[None, :]).astype(F32)
    vm = (cpos[None, :] <= qpos[:, None])[None, :, None, None, :]
    s = jnp.einsum('bqgrd,bngd->bqgrn', q, ck, preferred_element_type=F32) * scale
    s = s - slopes * dist_c[None, :, None, None, :]
    p = jax.nn.softmax(jnp.where(vm, s, NEG), axis=-1) * vm
    o_cmp = jnp.einsum('bqgrn,bngd->bqgrd', p.astype(cv.dtype), cv)
    imp = jnp.pad(p.sum(axis=3), ((0, 0), (0, 0), (0, 0), (0, n_blocks - nc)))
    bidx = jnp.arange(n_blocks, dtype=jnp.int32)[None, :]
    cur = (qpos // NSA_BLOCK)[:, None]
    forced = ((bidx == 0) | (bidx == cur) | (bidx == cur - 1))[None, :, None, :]
    allowed = (bidx * NSA_BLOCK <= qpos[:, None])[None, :, None, :]
    score = jnp.where(allowed, jnp.where(forced, FORCE, imp), NEG)
    top_n = min(NSA_TOP_N, n_blocks)
    _, sel = lax.top_k(score, top_n)
    pos = sel[..., None] * NSA_BLOCK + jnp.arange(NSA_BLOCK, dtype=jnp.int32)
    pos = pos.reshape(*sel.shape[:3], top_n * NSA_BLOCK)
    kv = fetch(pos)
    ks, vs = kv[..., 0, :], kv[..., 1, :]
    s2 = jnp.einsum('bqgrd,bqgnd->bqgrn', q, ks, preferred_element_type=F32) * scale
    dist_s = (qpos[None, :, None, None] - pos).astype(F32)[:, :, :, None, :]
    s2 = s2 - slopes * dist_s
    m2 = (pos <= qpos[None, :, None, None])[:, :, :, None, :]
    p2 = jax.nn.softmax(jnp.where(m2, s2, NEG), axis=-1)
    o_slc = jnp.einsum('bqgrn,bqgnd->bqgrd', p2.astype(vs.dtype), vs)
    return o_cmp, o_slc


def sweep_queries(fn, q, qpos, qb):
    B, Q = q.shape[:2]
    if Q <= qb or Q % qb:
        return fn(q, qpos)
    n = Q // qb
    qs = jnp.swapaxes(q.reshape(B, n, qb, *q.shape[2:]), 0, 1)
    outs = lax.map(lambda a: fn(a[0], a[1]), (qs, qpos.reshape(n, qb)))
    return tuple(jnp.swapaxes(o, 0, 1).reshape(B, Q, *o.shape[3:]) for o in outs)


def window_core(q, qpos, kv, kpos):
    scale = HEAD_DIM ** -0.5
    slopes = alibi_slopes()[None, None, :, :, None]
    s = jnp.einsum('bqgrd,bkgd->bqgrk', q, kv[..., 0, :], preferred_element_type=F32) * scale
    dist = qpos[:, None] - kpos[None, :]
    s = s - slopes * dist.astype(F32)[None, :, None, None, :]
    m = ((kpos[None, :] >= 0) & (dist >= 0) & (dist < NSA_WINDOW))[None, :, None, None, :]
    p = jax.nn.softmax(jnp.where(m, s, NEG), axis=-1)
    return jnp.einsum('bqgrk,bkgd->bqgrd', p.astype(kv.dtype), kv[..., 1, :])


def window_prompt(q, kv):
    B, S = q.shape[:2]
    qb = WIN_QBLOCK if S % WIN_QBLOCK == 0 else S
    nb = S // qb
    span = NSA_WINDOW + qb
    kvp = jnp.pad(kv, ((0, 0), (NSA_WINDOW, 0), (0, 0), (0, 0), (0, 0)))
    idx = jnp.arange(nb, dtype=jnp.int32)[:, None] * qb + jnp.arange(span, dtype=jnp.int32)[None, :]
    kv_blk = kvp[:, idx]
    kpos = idx - NSA_WINDOW
    qpos = jnp.arange(S, dtype=jnp.int32).reshape(nb, qb)
    q_blk = q.reshape(B, nb, qb, *q.shape[2:])
    o = jax.vmap(window_core, in_axes=(1, 0, 1, 0), out_axes=1)(q_blk, qpos, kv_blk, kpos)
    return o.reshape(q.shape)


def combine_branches(gates, o_cmp, o_slc, o_win):
    B, T = gates.shape[:2]
    g = jax.nn.sigmoid(gates.astype(F32)).reshape(B, T, NSA_KV_HEADS, NSA_GROUP, 3).astype(o_cmp.dtype)
    o = g[..., 0:1] * o_cmp + g[..., 1:2] * o_slc + g[..., 2:3] * o_win
    return o.reshape(B, T, NSA_WIDTH)


def nsa_prompt(q, kv_cmp, kv_slc, kv_win, gates, pe, w1, w2):
    B, S = q.shape[:2]
    ckv = compress_kv(kv_cmp, pe, w1, w2)
    ck, cv = ckv[..., 0, :], ckv[..., 1, :]
    n_blocks = -(-S // NSA_BLOCK)
    b_ix = jnp.arange(B)[:, None, None, None]
    g_ix = jnp.arange(NSA_KV_HEADS)[None, None, :, None]

    def fetch(pos):
        return kv_slc[b_ix, jnp.minimum(pos, S - 1), g_ix]

    o_cmp, o_slc = sweep_queries(lambda qq, pp: cmp_slc_block(qq, pp, ck, cv, fetch, n_blocks),
                                 q, jnp.arange(S, dtype=jnp.int32), SLC_QBLOCK)
    o_win = window_prompt(q, kv_win)
    return combine_branches(gates, o_cmp, o_slc, o_win)


def nsa_sample(q, kv_cmp, kv_slc, kv_win, gates, pe, w1, w2, cache_cmp, cache_slc, win_buf, page_table, layer):
    B, T = q.shape[:2]
    page = cache_cmp.shape[2]
    past = page_table.shape[1] * page
    past_cmp = cache_cmp[layer, page_table].reshape(B, past, NSA_KV_HEADS, 2, HEAD_DIM)
    ckv = compress_kv(jnp.concatenate([past_cmp, kv_cmp.astype(past_cmp.dtype)], axis=1), pe, w1, w2)
    ck, cv = ckv[..., 0, :], ckv[..., 1, :]
    n_blocks = -(-(past + T) // NSA_BLOCK)
    qpos = past + jnp.arange(T, dtype=jnp.int32)
    b_ix = jnp.arange(B)[:, None, None, None]
    g_ix = jnp.arange(NSA_KV_HEADS)[None, None, :, None]

    def fetch(pos):
        pp = jnp.minimum(pos, past - 1)
        phys = page_table[b_ix, pp // page]
        from_pool = cache_slc[layer, phys, pp % page, g_ix]
        from_new = kv_slc[b_ix, jnp.clip(pos - past, 0, T - 1), g_ix].astype(from_pool.dtype)
        return jnp.where((pos < past)[..., None, None], from_pool, from_new)

    o_cmp, o_slc = sweep_queries(lambda qq, pp: cmp_slc_block(qq, pp, ck, cv, fetch, n_blocks),
                                 q, qpos, SLC_QBLOCK)
    wb = win_buf.shape[1]
    keys = jnp.concatenate([win_buf.astype(kv_win.dtype), kv_win], axis=1)
    kpos = past - wb + jnp.arange(wb + T, dtype=jnp.int32)
    o_win = window_core(q, qpos, keys, kpos)
    return combine_branches(gates, o_cmp, o_slc, o_win), keys[:, -wb:]


def chunk_gated_delta(q, k, v, g, beta, s0):
    B, T, H, Dk = q.shape
    Dv = v.shape[-1]
    C = GDN_CHUNK
    n = -(-T // C)
    pad = n * C - T

    def prep(t):
        t = jnp.pad(t.astype(F32), ((0, 0), (0, pad)) + ((0, 0),) * (t.ndim - 2))
        return jnp.moveaxis(t.reshape(B, n, C, H, *t.shape[3:]), 3, 1)

    q, k, v, g, beta = prep(q), prep(k), prep(v), prep(g), prep(beta)
    G = jnp.cumsum(g, axis=-1)
    tril = jnp.tril(jnp.ones((C, C), dtype=bool))
    eye = jnp.eye(C, dtype=F32)
    decay = jnp.exp(jnp.where(tril, G[..., :, None] - G[..., None, :], -jnp.inf))
    kb = k * beta[..., None]
    A = jnp.where(tril & (eye == 0), jnp.einsum('bhncd,bhnsd->bhncs', kb, k) * decay, 0.0)
    Tm = lax.linalg.triangular_solve(eye + A, jnp.broadcast_to(eye, A.shape), left_side=True, lower=True)
    u = Tm @ (v * beta[..., None])
    w = Tm @ (kb * jnp.exp(G)[..., None])
    qk = jnp.einsum('bhncd,bhnsd->bhncs', q, k) * decay
    qg = q * jnp.exp(G)[..., None]
    kd = k * jnp.exp(G[..., -1:] - G)[..., None]
    gl = jnp.exp(G[..., -1])
    xs = tuple(jnp.moveaxis(t, 2, 0) for t in (qg, qk, u, w, kd, gl))

    def step(S, xs_n):
        qg_n, qk_n, u_n, w_n, kd_n, gl_n = xs_n
        v_new = u_n - w_n @ S
        o = qg_n @ S + qk_n @ v_new
        S = S * gl_n[..., None, None] + jnp.swapaxes(kd_n, -1, -2) @ v_new
        return S, o

    S, o = lax.scan(step, s0.astype(F32), xs)
    o = jnp.transpose(o, (1, 0, 3, 2, 4)).reshape(B, n * C, H, Dv)[:, :T]
    return o, S


def gated_deltanet(qkv, z, a, b, conv_buf, s0, conv_w, a_log, dt_bias, norm_g):
    B, T, _ = qkv.shape
    c, new_buf = causal_dwconv(qkv, conv_buf, conv_w)
    c = jax.nn.silu(c)
    q, k, v = [t.reshape(B, T, GDN_HEADS, HEAD_DIM) for t in jnp.split(c, 3, axis=-1)]
    q = l2norm(q) * (HEAD_DIM ** -0.5)
    k = l2norm(k)
    beta = jax.nn.sigmoid(b.astype(F32))
    g = -jnp.exp(a_log.astype(F32)) * jax.nn.softplus(a.astype(F32) + dt_bias.astype(F32))
    o, S = chunk_gated_delta(q, k, v.astype(F32), g, beta, s0)
    o = o * lax.rsqrt(jnp.mean(o * o, -1, keepdims=True) + NORM_EPS) * norm_g.astype(F32)
    o = o * jax.nn.silu(z.astype(F32).reshape(B, T, GDN_HEADS, HEAD_DIM))
    return o.reshape(B, T, GDN_WIDTH), S, new_buf


def conv_ffn(x, buf, w_up, conv_w, w_down):
    u, v = jnp.split(x @ w_up, 2, axis=-1)
    uc, new_buf = causal_dwconv(u, buf, conv_w)
    return (jax.nn.gelu(uc) * v) @ w_down, new_buf


def run_layer(x, nsa_fn, gdn_s0, gdn_buf, ffn_buf, w_in, w_o, gdn_conv_w, gdn_a_log, gdn_dt_bias,
              gdn_norm_g, ln1_g, ln1_b, ffn_w_up, ffn_conv_w, ffn_w_down, ln2_g, ln2_b):
    B, T, _ = x.shape
    q, kvc, kvs, kvw, gates, qkv, z, a, b = jnp.split(x @ w_in, split_sizes(), axis=-1)
    kvc, kvs, kvw = [t.reshape(B, T, NSA_KV_HEADS, 2, HEAD_DIM) for t in (kvc, kvs, kvw)]
    q5 = q.reshape(B, T, NSA_KV_HEADS, NSA_GROUP, HEAD_DIM)
    o_nsa, win_state = nsa_fn(q5, kvc, kvs, kvw, gates)
    o_gdn, s_new, gbuf = gated_deltanet(qkv, z, a, b, gdn_buf, gdn_s0, gdn_conv_w, gdn_a_log,
                                        gdn_dt_bias, gdn_norm_g)
    mixed = jnp.concatenate([o_nsa.astype(x.dtype), o_gdn.astype(x.dtype)], axis=-1) @ w_o
    h = layer_norm(DN_ALPHA * x + mixed, ln1_g, ln1_b)
    f, fbuf = conv_ffn(h, ffn_buf, ffn_w_up, ffn_conv_w, ffn_w_down)
    y = layer_norm(DN_ALPHA * h + f, ln2_g, ln2_b)
    return y, (kvc, kvs, win_state, s_new.astype(x.dtype), gbuf, fbuf)


def setup_inputs(seed: int = 0) -> dict:
    key = jax.random.key(seed)
    ks = jax.random.split(key, 32)
    n_pages = PAST_LEN // PAGE_SIZE
    n_pool = (DEC_BATCH * n_pages * 5) // 4
    wb = min(NSA_WINDOW, PAST_LEN)

    def nrm(k, shape, s=1.0):
        return jax.random.normal(k, shape, F32) * s

    page_table = jax.random.permutation(ks[8], n_pool)[:DEC_BATCH * n_pages]
    page_table = page_table.reshape(DEC_BATCH, n_pages).astype(jnp.int32)
    dt = jnp.exp(jax.random.uniform(ks[14], (DEPTH, GDN_HEADS), F32, math.log(1e-3), math.log(1e-1)))
    return {
        'x_prompt': nrm(ks[0], (BATCH, SEQ, D_MODEL)),
        'x_sample': nrm(ks[1], (DEC_BATCH, DEC_SEQ, D_MODEL)),
        'cache_cmp_kv': nrm(ks[2], (DEPTH, n_pool, PAGE_SIZE, NSA_KV_HEADS, 2, HEAD_DIM)),
        'cache_slc_kv': nrm(ks[3], (DEPTH, n_pool, PAGE_SIZE, NSA_KV_HEADS, 2, HEAD_DIM)),
        'cache_win_kv': nrm(ks[4], (DEPTH, DEC_BATCH, wb, NSA_KV_HEADS, 2, HEAD_DIM)),
        'state_gdn': nrm(ks[5], (DEPTH, DEC_BATCH, GDN_HEADS, HEAD_DIM, HEAD_DIM), 0.5),
        'state_gdn_conv': nrm(ks[6], (DEPTH, DEC_BATCH, GDN_CONV - 1, 3 * GDN_WIDTH)),
        'state_ffn_conv': nrm(ks[7], (DEPTH, DEC_BATCH, FFN_CONV - 1, D_FF)),
        'page_table': page_table,
        'w_in': nrm(ks[9], (DEPTH, D_MODEL, IN_COLS), D_MODEL ** -0.5),
        'w_o': nrm(ks[10], (DEPTH, MIX_WIDTH, D_MODEL), DN_BETA * MIX_WIDTH ** -0.5),
        'cmp_pe': nrm(ks[11], (DEPTH, NSA_BLOCK, 2, HEAD_DIM), 0.02),
        'cmp_w1': nrm(ks[12], (DEPTH, 2, NSA_BLOCK * HEAD_DIM, HEAD_DIM), (NSA_BLOCK * HEAD_DIM) ** -0.5),
        'cmp_w2': nrm(ks[13], (DEPTH, 2, HEAD_DIM, HEAD_DIM), HEAD_DIM ** -0.5),
        'gdn_conv_w': nrm(ks[15], (DEPTH, GDN_CONV, 3 * GDN_WIDTH), GDN_CONV ** -0.5),
        'gdn_a_log': jnp.log(jax.random.uniform(ks[16], (DEPTH, GDN_HEADS), F32, 1.0, 16.0)),
        'gdn_dt_bias': dt + jnp.log(-jnp.expm1(-dt)),
        'gdn_norm_g': 1.0 + nrm(ks[17], (DEPTH, HEAD_DIM), 0.02),
        'ln1_g': 1.0 + nrm(ks[18], (DEPTH, D_MODEL), 0.02),
        'ln1_b': nrm(ks[19], (DEPTH, D_MODEL), 0.02),
        'ffn_w_up': nrm(ks[20], (DEPTH, D_MODEL, 2 * D_FF), D_MODEL ** -0.5),
        'ffn_conv_w': nrm(ks[21], (DEPTH, FFN_CONV, D_FF), FFN_CONV ** -0.5),
        'ffn_w_down': nrm(ks[22], (DEPTH, D_FF, D_MODEL), DN_BETA * D_FF ** -0.5),
        'ln2_g': 1.0 + nrm(ks[23], (DEPTH, D_MODEL), 0.02),
        'ln2_b': nrm(ks[24], (DEPTH, D_MODEL), 0.02),
    }


def reference(x_prompt, x_sample, cache_cmp_kv, cache_slc_kv, cache_win_kv, state_gdn, state_gdn_conv,
              state_ffn_conv, page_table, w_in, w_o, cmp_pe, cmp_w1, cmp_w2, gdn_conv_w, gdn_a_log,
              gdn_dt_bias, gdn_norm_g, ln1_g, ln1_b, ffn_w_up, ffn_conv_w, ffn_w_down, ln2_g, ln2_b):
    B, S, _ = x_prompt.shape
    dtype = x_prompt.dtype
    xp, xs = x_prompt, x_sample
    p_states, s_states = [], []
    for l in range(DEPTH):
        shared = (w_in[l], w_o[l], gdn_conv_w[l], gdn_a_log[l], gdn_dt_bias[l], gdn_norm_g[l],
                  ln1_g[l], ln1_b[l], ffn_w_up[l], ffn_conv_w[l], ffn_w_down[l], ln2_g[l], ln2_b[l])
        pe, w1, w2 = cmp_pe[l], cmp_w1[l], cmp_w2[l]

        def nsa_p(q, kc, kslc, kw, g, pe=pe, w1=w1, w2=w2):
            return nsa_prompt(q, kc, kslc, kw, g, pe, w1, w2), kw[:, -min(NSA_WINDOW, S):]

        def nsa_s(q, kc, kslc, kw, g, pe=pe, w1=w1, w2=w2, l=l):
            return nsa_sample(q, kc, kslc, kw, g, pe, w1, w2, cache_cmp_kv, cache_slc_kv,
                              cache_win_kv[l], page_table, l)

        xp, st_p = run_layer(xp, nsa_p,
                             jnp.zeros((B, GDN_HEADS, HEAD_DIM, HEAD_DIM), dtype),
                             jnp.zeros((B, GDN_CONV - 1, 3 * GDN_WIDTH), dtype),
                             jnp.zeros((B, FFN_CONV - 1, D_FF), dtype), *shared)
        xs, st_s = run_layer(xs, nsa_s, state_gdn[l], state_gdn_conv[l], state_ffn_conv[l], *shared)
        p_states.append(st_p)
        s_states.append(st_s)

    def stk(states, i):
        return jnp.stack([st[i] for st in states], axis=0)

    return (xp, xs,
            stk(p_states, 0), stk(p_states, 1), stk(p_states, 2), stk(p_states, 3), stk(p_states, 4), stk(p_states, 5),
            stk(s_states, 0), stk(s_states, 1), stk(s_states, 2), stk(s_states, 3), stk(s_states, 4), stk(s_states, 5))
```

```python
import functools

import jax
import jax.numpy as jnp
import numpy as np
from jax import lax
from jax.experimental import pallas as pl
from jax.experimental.pallas import tpu as pltpu

F32 = jnp.float32
BF16 = jnp.bfloat16

HEAD_DIM = 128
NSA_HEADS = 8
NSA_KV_HEADS = 2
NSA_GROUP = 4
NSA_WIDTH = 1024
NSA_BLOCK = 64
NSA_TOP_N = 16
NSA_WINDOW = 512
GDN_HEADS = 8
GDN_WIDTH = 1024
GDN_CONV = 4
GDN_CHUNK = 64
FFN_CONV = 3
KV_COLS = 512
LN_EPS = 1e-5
NORM_EPS = 1e-6
NEG = -1e30
FORCE = 1e4
SLOPES = [[2.0 ** -(g * NSA_GROUP + r + 1) for r in range(NSA_GROUP)] for g in range(NSA_KV_HEADS)]

QKV0, Q0, Z0, KVC0, KVS0, KVW0, MISC0, IN_COLS_PAD = 0, 3072, 4096, 5120, 5632, 6144, 6656, 7168
GATE_LANE, A_LANE, B_LANE = 0, 24, 32

ZROWS = 16
ZTOK0 = 8

VMEM_LIMIT_BYTES = 56 * 1024 * 1024


def _cparams(*sem):
    return pltpu.CompilerParams(dimension_semantics=sem, vmem_limit_bytes=VMEM_LIMIT_BYTES)


def _dot(a, b):
    return jnp.dot(a.astype(BF16), b.astype(BF16), preferred_element_type=F32)


def _dot_nt(a, b):
    return lax.dot_general(a.astype(BF16), b.astype(BF16), (((1,), (1,)), ((), ())), preferred_element_type=F32)


def _dot_tn(a, b):
    return lax.dot_general(a.astype(BF16), b.astype(BF16), (((0,), (0,)), ((), ())), preferred_element_type=F32)


def _layer_norm(t, g, b):
    mu = jnp.mean(t, -1, keepdims=True)
    d = t - mu
    var = jnp.mean(d * d, -1, keepdims=True)
    return d * lax.rsqrt(var + LN_EPS) * g + b


def _mm_kernel(x_ref, w_ref, o_ref):
    o_ref[...] = _dot(x_ref[...], w_ref[...])


def matmul(x, w, *, tm, tn):
    M, K = x.shape
    _, N = w.shape
    assert M % tm == 0 and N % tn == 0
    return pl.pallas_call(
        _mm_kernel,
        grid=(M // tm, N // tn),
        in_specs=[pl.BlockSpec((tm, K), lambda i, j: (i, 0)),
                  pl.BlockSpec((K, tn), lambda i, j: (0, j))],
        out_specs=pl.BlockSpec((tm, tn), lambda i, j: (i, j)),
        out_shape=jax.ShapeDtypeStruct((M, N), F32),
        compiler_params=_cparams("parallel", "arbitrary"),
        name="in_proj",
    )(x, w)


def _proj_ln_kernel(a1_ref, a2_ref, w_ref, x_ref, g_ref, b_ref, o_ref, *, alpha):
    k1 = a1_ref.shape[1]
    acc = _dot(a1_ref[...], w_ref[:k1, :]) + _dot(a2_ref[...], w_ref[k1:, :])
    o_ref[...] = _layer_norm(alpha * x_ref[...] + acc, g_ref[...], b_ref[...])


def proj_ln(a1, a2, w, x, g, b, *, alpha, tm):
    M, D = x.shape
    k1, k2 = a1.shape[1], a2.shape[1]
    assert M % tm == 0 and w.shape == (k1 + k2, D)
    return pl.pallas_call(
        functools.partial(_proj_ln_kernel, alpha=alpha),
        grid=(M // tm,),
        in_specs=[pl.BlockSpec((tm, k1), lambda i: (i, 0)),
                  pl.BlockSpec((tm, k2), lambda i: (i, 0)),
                  pl.BlockSpec((k1 + k2, D), lambda i: (0, 0)),
                  pl.BlockSpec((tm, D), lambda i: (i, 0)),
                  pl.BlockSpec((1, D), lambda i: (0, 0)),
                  pl.BlockSpec((1, D), lambda i: (0, 0))],
        out_specs=pl.BlockSpec((tm, D), lambda i: (i, 0)),
        out_shape=jax.ShapeDtypeStruct((M, D), F32),
        compiler_params=_cparams("parallel"),
        name="out_proj_ln",
    )(a1, a2, w, x, g, b)


def _conv3(u, prev8, cw):
    ue = jnp.concatenate([prev8, u], axis=0)
    u1 = pltpu.roll(ue, 1, 0)[8:]
    u2 = pltpu.roll(ue, 2, 0)[8:]
    return cw[2:3, :] * u + cw[1:2, :] * u1 + cw[0:1, :] * u2


def _ffn_up_prompt_kernel(h_ref, wu_ref, wv_ref, cw_ref, act_ref, tail_ref, carry_ref, *, tiles_per_seq):
    i = pl.program_id(1)
    h = h_ref[...]
    u = _dot(h, wu_ref[...])
    v = _dot(h, wv_ref[...])

    @pl.when(i % tiles_per_seq == 0)
    def _():
        carry_ref[...] = jnp.zeros_like(carry_ref)

    uc = _conv3(u, carry_ref[...], cw_ref[...])
    act_ref[...] = (jax.nn.gelu(uc) * v).astype(act_ref.dtype)
    tail = u[u.shape[0] - 8:, :]
    carry_ref[...] = tail
    tail_ref[...] = tail


def ffn_up_prompt(h, w_up, cw, *, n_seq, tm, tf):
    M, D = h.shape
    F = w_up.shape[1] // 2
    T = M // n_seq
    assert M % tm == 0 and F % tf == 0 and T % tm == 0
    nf = F // tf
    tps = T // tm
    return pl.pallas_call(
        functools.partial(_ffn_up_prompt_kernel, tiles_per_seq=tps),
        grid=(nf, M // tm),
        in_specs=[pl.BlockSpec((tm, D), lambda f, i: (i, 0)),
                  pl.BlockSpec((D, tf), lambda f, i: (0, f)),
                  pl.BlockSpec((D, tf), lambda f, i: (0, nf + f)),
                  pl.BlockSpec((FFN_CONV, tf), lambda f, i: (0, f))],
        out_specs=[pl.BlockSpec((tm, tf), lambda f, i: (i, f)),
                   pl.BlockSpec((None, 8, tf), lambda f, i: (i // tps, 0, f))],
        out_shape=[jax.ShapeDtypeStruct((M, F), BF16),
                   jax.ShapeDtypeStruct((n_seq, 8, F), F32)],
        scratch_shapes=[pltpu.VMEM((8, tf), F32)],
        compiler_params=_cparams("arbitrary", "arbitrary"),
        name="ffn_up_prompt",
    )(h, w_up, w_up, cw)


def _ffn_up_sample_kernel(h_ref, wu_ref, wv_ref, cw_ref, buf_ref, act_ref, u_ref):
    h = h_ref[...]
    u = _dot(h, wu_ref[...])
    v = _dot(h, wv_ref[...])
    zrow = lax.broadcasted_iota(jnp.int32, u.shape, 0) % ZROWS
    isbuf = (zrow >= ZTOK0 - (FFN_CONV - 1)) & (zrow < ZTOK0)
    uz = jnp.where(isbuf, buf_ref[...], u)
    cw = cw_ref[...]
    uc = cw[2:3, :] * uz + cw[1:2, :] * pltpu.roll(uz, 1, 0) + cw[0:1, :] * pltpu.roll(uz, 2, 0)
    act_ref[...] = (jax.nn.gelu(uc) * v).astype(act_ref.dtype)
    u_ref[...] = u


def ffn_up_sample(h, w_up, cw, bufz, *, tf):
    M, D = h.shape
    F = w_up.shape[1] // 2
    nf = F // tf
    return pl.pallas_call(
        _ffn_up_sample_kernel,
        grid=(nf,),
        in_specs=[pl.BlockSpec((M, D), lambda f: (0, 0)),
                  pl.BlockSpec((D, tf), lambda f: (0, f)),
                  pl.BlockSpec((D, tf), lambda f: (0, nf + f)),
                  pl.BlockSpec((FFN_CONV, tf), lambda f: (0, f)),
                  pl.BlockSpec((M, tf), lambda f: (0, f))],
        out_specs=[pl.BlockSpec((M, tf), lambda f: (0, f)),
                   pl.BlockSpec((M, tf), lambda f: (0, f))],
        out_shape=[jax.ShapeDtypeStruct((M, F), BF16),
                   jax.ShapeDtypeStruct((M, F), F32)],
        compiler_params=_cparams("parallel"),
        name="ffn_up_sample",
    )(h, w_up, w_up, cw, bufz)


def _ffn_down_ln_kernel(a_ref, w_ref, h_ref, g_ref, b_ref, o_ref, acc_ref, *, alpha):
    k = pl.program_id(1)

    @pl.when(k == 0)
    def _():
        acc_ref[...] = jnp.zeros_like(acc_ref)

    acc_ref[...] += _dot(a_ref[...], w_ref[...])

    @pl.when(k == pl.num_programs(1) - 1)
    def _():
        o_ref[...] = _layer_norm(alpha * h_ref[...] + acc_ref[...], g_ref[...], b_ref[...])


def ffn_down_ln(a, w, h, g, b, *, alpha, tm, tk):
    M, Fd = a.shape
    D = w.shape[1]
    assert M % tm == 0 and Fd % tk == 0
    return pl.pallas_call(
        functools.partial(_ffn_down_ln_kernel, alpha=alpha),
        grid=(M // tm, Fd // tk),
        in_specs=[pl.BlockSpec((tm, tk), lambda i, k: (i, k)),
                  pl.BlockSpec((tk, D), lambda i, k: (k, 0)),
                  pl.BlockSpec((tm, D), lambda i, k: (i, 0)),
                  pl.BlockSpec((1, D), lambda i, k: (0, 0)),
                  pl.BlockSpec((1, D), lambda i, k: (0, 0))],
        out_specs=pl.BlockSpec((tm, D), lambda i, k: (i, 0)),
        out_shape=jax.ShapeDtypeStruct((M, D), F32),
        scratch_shapes=[pltpu.VMEM((tm, D), F32)],
        compiler_params=_cparams("parallel", "arbitrary"),
        name="ffn_down_ln",
    )(a, w, h, g, b)


def _compress_rows(x_refs_by_gc, pe_ref, w1_ref, w2_ref, out_ref, nblk):
    for c in range(2):
        def body(pp, acc):
            p0 = pp * 2
            parts = []
            for dp in range(2):
                pe_row = pe_ref[c, pl.ds(p0 + dp, 1), :]
                rows = [x_refs_by_gc[g * 2 + c][:, p0 + dp, :] + pe_row for g in range(NSA_KV_HEADS)]
                parts.append(jnp.concatenate(rows, axis=0))
            lhs = jnp.concatenate(parts, axis=1)
            w = w1_ref[c, pl.ds(pl.multiple_of(p0 * HEAD_DIM, 2 * HEAD_DIM), 2 * HEAD_DIM), :]
            return acc + _dot(lhs, w)

        acc = lax.fori_loop(0, NSA_BLOCK // 2, body, jnp.zeros((NSA_KV_HEADS * nblk, HEAD_DIM), F32))
        out_ref[c] = _dot(jax.nn.silu(acc), w2_ref[c])


def _compress_prompt_kernel(x0, x1, x2, x3, pe_ref, w1_ref, w2_ref, out_ref, *, nblk):
    _compress_rows([x0, x1, x2, x3], pe_ref, w1_ref, w2_ref, out_ref, nblk)


def compress_prompt(P, pe_t, w1, w2, *, n_seq):
    M = P.shape[0]
    T = M // n_seq
    nblk = T // NSA_BLOCK
    P3 = P.reshape(n_seq * nblk, NSA_BLOCK, IN_COLS_PAD)
    cb0 = KVC0 // HEAD_DIM
    in_specs = [pl.BlockSpec((nblk, NSA_BLOCK, HEAD_DIM), (lambda b, gc=gc: (b, 0, cb0 + gc))) for gc in range(4)]
    in_specs += [pl.BlockSpec(pe_t.shape, lambda b: (0, 0, 0)),
                 pl.BlockSpec(w1.shape, lambda b: (0, 0, 0)),
                 pl.BlockSpec(w2.shape, lambda b: (0, 0, 0))]
    return pl.pallas_call(
        functools.partial(_compress_prompt_kernel, nblk=nblk),
        grid=(n_seq,),
        in_specs=in_specs,
        out_specs=pl.BlockSpec((None, 2, NSA_KV_HEADS * nblk, HEAD_DIM), lambda b: (b, 0, 0, 0)),
        out_shape=jax.ShapeDtypeStruct((n_seq, 2, NSA_KV_HEADS * nblk, HEAD_DIM), F32),
        compiler_params=_cparams("parallel"),
        name="compress_prompt",
    )(P3, P3, P3, P3, pe_t, w1, w2)


def _rank_desc(score, n_real):
    lane = lax.broadcasted_iota(jnp.int32, score.shape, 1)
    cnt = jnp.zeros(score.shape, F32)
    for i in range(n_real):
        col = score[:, i:i + 1]
        ge = jnp.where(col >= score, 1.0, 0.0)
        gt = jnp.where(col > score, 1.0, 0.0)
        cnt = cnt + jnp.where(lane > i, ge, gt)
    return cnt


def _block_scores(imp, qpos, n_blocks):
    bidx = lax.broadcasted_iota(jnp.int32, imp.shape, 1)
    cur = qpos // NSA_BLOCK
    sc = jnp.where(bidx == 0, FORCE, jnp.where(bidx == cur, FORCE, jnp.where(bidx == cur - 1, FORCE, imp)))
    sc = jnp.where(bidx * NSA_BLOCK <= qpos, sc, NEG)
    return jnp.where(bidx < n_blocks, sc, -jnp.inf)


def _online_update(s, v, m_ref, l_ref, acc_ref, r):
    m_old = m_ref[r]
    m_new = jnp.maximum(m_old, jnp.max(s, axis=-1, keepdims=True))
    p = jnp.exp(s - m_new)
    alpha = jnp.exp(m_old - m_new)
    l_ref[r] = alpha * l_ref[r] + jnp.sum(p, axis=-1, keepdims=True)
    acc_ref[r] = alpha * acc_ref[r] + _dot(p, v)
    m_ref[r] = m_new


def _nsa_prompt_kernel(q_ref, misc_ref, kvs_ref, kvw_ref, ckv_ref, o_ref, m_ref, l_ref, acc_ref, *, tq, nblk):
    i = pl.program_id(1)
    t0 = i * tq
    tk = tq
    scale = HEAD_DIM ** -0.5
    qpos = t0 + lax.broadcasted_iota(jnp.int32, (tq, 1), 0)
    gates = jax.nn.sigmoid(misc_ref[...])
    lane_k = lax.broadcasted_iota(jnp.int32, (1, tk), 1)
    blk_iota = lax.broadcasted_iota(jnp.int32, (nblk, tk), 0)
    top_n = min(NSA_TOP_N, nblk)

    def reset():
        m_ref[...] = jnp.full(m_ref.shape, NEG, F32)
        l_ref[...] = jnp.zeros(l_ref.shape, F32)
        acc_ref[...] = jnp.zeros(acc_ref.shape, F32)

    for g in range(NSA_KV_HEADS):
        qs = [(q_ref[:, (g * NSA_GROUP + r) * HEAD_DIM:(g * NSA_GROUP + r + 1) * HEAD_DIM] * scale).astype(BF16)
              for r in range(NSA_GROUP)]
        ck = ckv_ref[0, g * nblk:(g + 1) * nblk, :].astype(BF16)
        cv = ckv_ref[1, g * nblk:(g + 1) * nblk, :].astype(BF16)
        cpos = lax.broadcasted_iota(jnp.int32, (tq, nblk), 1) * NSA_BLOCK + (NSA_BLOCK - 1)
        vm = cpos <= qpos
        cposrel = (cpos - t0).astype(F32)
        imp = jnp.zeros((tq, nblk), F32)
        o_cmp = []
        for r in range(NSA_GROUP):
            s = jnp.where(vm, _dot_nt(qs[r], ck) + SLOPES[g][r] * cposrel, NEG)
            e = jnp.exp(s - jnp.max(s, axis=-1, keepdims=True))
            p = jnp.where(vm, e / jnp.sum(e, axis=-1, keepdims=True), 0.0)
            imp = imp + p
            o_cmp.append(_dot(p, cv))
        rank = _rank_desc(_block_scores(imp, qpos, nblk), nblk)
        sel = jnp.where(rank < top_n, 1.0, 0.0).astype(BF16)

        reset()

        def slc_body(j, carry):
            k0 = pl.multiple_of(j * tk, tk)
            k = kvs_ref[pl.ds(k0, tk), g * 2 * HEAD_DIM:(g * 2 + 1) * HEAD_DIM]
            v = kvs_ref[pl.ds(k0, tk), (g * 2 + 1) * HEAD_DIM:(g * 2 + 2) * HEAD_DIM]
            kpos = k0 + lane_k
            kposrel = (kpos - t0).astype(F32)
            expand = jnp.where(blk_iota == (kpos // NSA_BLOCK), 1.0, 0.0).astype(BF16)
            chosen = jnp.dot(sel, expand, preferred_element_type=F32)
            valid = jnp.where(kpos <= qpos, chosen, 0.0) > 0.5
            for r in range(NSA_GROUP):
                s = jnp.where(valid, _dot_nt(qs[r], k) + SLOPES[g][r] * kposrel, NEG)
                _online_update(s, v, m_ref, l_ref, acc_ref, r)
            return carry

        lax.fori_loop(0, i + 1, slc_body, 0)
        o_slc = [acc_ref[r] / l_ref[r] for r in range(NSA_GROUP)]

        reset()

        def win_body(j, carry):
            k0 = pl.multiple_of(j * tk, tk)
            k = kvw_ref[pl.ds(k0, tk), g * 2 * HEAD_DIM:(g * 2 + 1) * HEAD_DIM]
            v = kvw_ref[pl.ds(k0, tk), (g * 2 + 1) * HEAD_DIM:(g * 2 + 2) * HEAD_DIM]
            kpos = k0 + lane_k
            kposrel = (kpos - t0).astype(F32)
            dist = qpos - kpos
            valid = jnp.abs(2 * dist - (NSA_WINDOW - 1)) <= (NSA_WINDOW - 1)
            for r in range(NSA_GROUP):
                s = jnp.where(valid, _dot_nt(qs[r], k) + SLOPES[g][r] * kposrel, NEG)
                _online_update(s, v, m_ref, l_ref, acc_ref, r)
            return carry

        lax.fori_loop(jnp.maximum(i - NSA_WINDOW // tk, 0), i + 1, win_body, 0)

        for r in range(NSA_GROUP):
            c = g * NSA_GROUP * 3 + r * 3 + GATE_LANE
            o = (gates[:, c:c + 1] * o_cmp[r] + gates[:, c + 1:c + 2] * o_slc[r]
                 + gates[:, c + 2:c + 3] * (acc_ref[r] / l_ref[r]))
            h = g * NSA_GROUP + r
            o_ref[:, h * HEAD_DIM:(h + 1) * HEAD_DIM] = o.astype(o_ref.dtype)


def nsa_prompt(P, Pkv, ckv, *, n_seq, tq):
    M = P.shape[0]
    T = M // n_seq
    nblk = T // NSA_BLOCK
    nq = T // tq
    assert T % tq == 0 and tq % NSA_BLOCK == 0 and NSA_WINDOW % tq == 0
    return pl.pallas_call(
        functools.partial(_nsa_prompt_kernel, tq=tq, nblk=nblk),
        grid=(n_seq, nq),
        in_specs=[pl.BlockSpec((tq, NSA_WIDTH), lambda b, i: (b * nq + i, Q0 // NSA_WIDTH)),
                  pl.BlockSpec((tq, HEAD_DIM), lambda b, i: (b * nq + i, MISC0 // HEAD_DIM)),
                  pl.BlockSpec((T, KV_COLS), lambda b, i: (b, 0)),
                  pl.BlockSpec((T, KV_COLS), lambda b, i: (b, 1)),
                  pl.BlockSpec((None, 2, NSA_KV_HEADS * nblk, HEAD_DIM), lambda b, i: (b, 0, 0, 0))],
        out_specs=pl.BlockSpec((tq, NSA_WIDTH), lambda b, i: (b * nq + i, 0)),
        out_shape=jax.ShapeDtypeStruct((M, NSA_WIDTH), BF16),
        scratch_shapes=[pltpu.VMEM((NSA_GROUP, tq, 1), F32),
                        pltpu.VMEM((NSA_GROUP, tq, 1), F32),
                        pltpu.VMEM((NSA_GROUP, tq, HEAD_DIM), F32)],
        compiler_params=_cparams("parallel", "arbitrary"),
        name="nsa_prompt",
    )(P, P, Pkv, Pkv, ckv)


def _unit_lower_inverse_minus_eye(A, C):
    row = lax.broadcasted_iota(jnp.int32, (C, C), 0)
    col = lax.broadcasted_iota(jnp.int32, (C, C), 1)
    A8 = jnp.where((row // 8) == (col // 8), A, 0.0)
    B2 = _dot(A8, A8)
    B4 = _dot(B2, B2)
    P1 = B2 - A8 - _dot(A8, B2)
    N = P1 + B4 + _dot(P1, B4)
    size = 16
    while size <= C:
        AL = jnp.where((row // size) == (col // size), jnp.where((row // (size // 2)) == (col // (size // 2)), 0.0, A), 0.0)
        X = AL + _dot(N, AL)
        N = N - X - _dot(X, N)
        size *= 2
    return N


def _gdn_prep_kernel(*refs, C, sample):
    if sample:
        x_ref, buf_ref, misc_ref, cw_ref, alog_ref, dtb_ref = refs[:6]
    else:
        x_ref, prev_ref, misc_ref, cw_ref, alog_ref, dtb_ref = refs[:6]
    u_ref, w_ref, qg_ref, kd_ref, qk_ref, gl_ref = refs[6:]
    cw = cw_ref[...]
    x = x_ref[...]
    rowv = lax.broadcasted_iota(jnp.int32, (C, 1), 0)
    if sample:
        isbuf = (rowv >= ZTOK0 - (GDN_CONV - 1)) & (rowv < ZTOK0)
        xz = jnp.where(isbuf, buf_ref[...], x)
        y = (cw[3:4, :] * xz + cw[2:3, :] * pltpu.roll(xz, 1, 0) + cw[1:2, :] * pltpu.roll(xz, 2, 0)
             + cw[0:1, :] * pltpu.roll(xz, 3, 0))
        valid = jnp.where((rowv >= ZTOK0) & (rowv < ZTOK0 + 4), 1.0, 0.0)
    else:
        prev = jnp.where(pl.program_id(1) == 0, 0.0, prev_ref[...])
        xe = jnp.concatenate([prev, x], axis=0)
        y = (cw[3:4, :] * x + cw[2:3, :] * pltpu.roll(xe, 1, 0)[8:] + cw[1:2, :] * pltpu.roll(xe, 2, 0)[8:]
             + cw[0:1, :] * pltpu.roll(xe, 3, 0)[8:])
        valid = None
    y = jax.nn.silu(y)

    misc = misc_ref[...]
    gfull = -jnp.exp(alog_ref[...]) * jax.nn.softplus(misc + dtb_ref[...])
    bfull = jax.nn.sigmoid(misc)
    if valid is not None:
        gfull = gfull * valid
        bfull = bfull * valid
    row128 = lax.broadcasted_iota(jnp.int32, (C, HEAD_DIM), 0)
    G = gfull
    s = 1
    while s < C:
        G = G + jnp.where(row128 >= s, pltpu.roll(G, s, 0), 0.0)
        s *= 2
    expG = jnp.exp(G)
    glast = G[C - 1:C, :]
    kdfac = jnp.exp(glast - G)
    if C < HEAD_DIM:
        Gpad = jnp.concatenate([G, jnp.zeros((HEAD_DIM - C, HEAD_DIM), F32)], axis=0)
    else:
        Gpad = G
    GT = Gpad.T
    gl_ref[...] = jnp.exp(jnp.broadcast_to(GT[A_LANE:A_LANE + GDN_HEADS, C - 1:C], (GDN_HEADS, HEAD_DIM)))

    row = lax.broadcasted_iota(jnp.int32, (C, C), 0)
    col = lax.broadcasted_iota(jnp.int32, (C, C), 1)
    for h in range(GDN_HEADS):
        sl = slice(h * HEAD_DIM, (h + 1) * HEAD_DIM)
        q = y[:, h * HEAD_DIM:(h + 1) * HEAD_DIM]
        k = y[:, GDN_WIDTH + h * HEAD_DIM:GDN_WIDTH + (h + 1) * HEAD_DIM]
        v = y[:, 2 * GDN_WIDTH + h * HEAD_DIM:2 * GDN_WIDTH + (h + 1) * HEAD_DIM]
        q = q * lax.rsqrt(jnp.sum(q * q, -1, keepdims=True) + NORM_EPS) * (HEAD_DIM ** -0.5)
        k = k * lax.rsqrt(jnp.sum(k * k, -1, keepdims=True) + NORM_EPS)
        if valid is not None:
            q, k, v = q * valid, k * valid, v * valid
        beta = bfull[:, B_LANE + h:B_LANE + h + 1]
        gc = G[:, A_LANE + h:A_LANE + h + 1]
        gr = GT[A_LANE + h:A_LANE + h + 1, :C]
        decay = jnp.exp(jnp.where(row >= col, gc - gr, NEG))
        kb = k * beta
        A = jnp.where(row > col, _dot_nt(kb, k) * decay, 0.0)
        N = _unit_lower_inverse_minus_eye(A, C)
        vb = v * beta
        kbg = kb * expG[:, A_LANE + h:A_LANE + h + 1]
        u_ref[:, sl] = vb + _dot(N, vb)
        w_ref[:, sl] = (kbg + _dot(N, kbg)).astype(w_ref.dtype)
        qk_ref[:, h * C:(h + 1) * C] = (_dot_nt(q, k) * decay).astype(qk_ref.dtype)
        qg_ref[:, sl] = (q * expG[:, A_LANE + h:A_LANE + h + 1]).astype(qg_ref.dtype)
        kd_ref[:, sl] = (k * kdfac[:, A_LANE + h:A_LANE + h + 1]).astype(kd_ref.dtype)


def gdn_prep(P, cw, alog_row, dtb_row, *, n_seq, C, bufz=None):
    M = P.shape[0]
    T = M // n_seq
    n = T // C
    sample = bufz is not None
    assert T % C == 0 and C % 8 == 0
    qkv_spec = pl.BlockSpec((C, 3 * GDN_WIDTH), lambda b, c: (b * n + c, 0))
    if sample:
        second = pl.BlockSpec((C, 3 * GDN_WIDTH), lambda b, c: (b * n + c, 0))
        second_arr = bufz
    else:
        second = pl.BlockSpec((8, 3 * GDN_WIDTH), lambda b, c: (jnp.maximum((b * n + c) * (C // 8) - 1, 0), 0))
        second_arr = P
    in_specs = [qkv_spec, second,
                pl.BlockSpec((C, HEAD_DIM), lambda b, c: (b * n + c, MISC0 // HEAD_DIM)),
                pl.BlockSpec((GDN_CONV, 3 * GDN_WIDTH), lambda b, c: (0, 0)),
                pl.BlockSpec((1, HEAD_DIM), lambda b, c: (0, 0)),
                pl.BlockSpec((1, HEAD_DIM), lambda b, c: (0, 0))]
    row_spec = pl.BlockSpec((C, GDN_WIDTH), lambda b, c: (b * n + c, 0))
    out_specs = [row_spec, row_spec, row_spec, row_spec,
                 pl.BlockSpec((C, GDN_HEADS * C), lambda b, c: (b * n + c, 0)),
                 pl.BlockSpec((None, GDN_HEADS, HEAD_DIM), lambda b, c: (b * n + c, 0, 0))]
    out_shape = [jax.ShapeDtypeStruct((M, GDN_WIDTH), F32),
                 jax.ShapeDtypeStruct((M, GDN_WIDTH), BF16),
                 jax.ShapeDtypeStruct((M, GDN_WIDTH), BF16),
                 jax.ShapeDtypeStruct((M, GDN_WIDTH), BF16),
                 jax.ShapeDtypeStruct((M, GDN_HEADS * C), BF16),
                 jax.ShapeDtypeStruct((n_seq * n, GDN_HEADS, HEAD_DIM), F32)]
    return pl.pallas_call(
        functools.partial(_gdn_prep_kernel, C=C, sample=sample),
        grid=(n_seq, n),
        in_specs=in_specs,
        out_specs=out_specs,
        out_shape=out_shape,
        compiler_params=_cparams("parallel", "arbitrary"),
        name="gdn_prep_sample" if sample else "gdn_prep_prompt",
    )(P, second_arr, P, cw, alog_row, dtb_row)


def _gdn_scan_kernel(u_ref, w_ref, qg_ref, kd_ref, qk_ref, gl_ref, z_ref, ng_ref, s0_ref, o_ref, sfin_ref, S_ref, *, C):
    c = pl.program_id(1)

    @pl.when(c == 0)
    def _():
        S_ref[...] = s0_ref[...]

    ng = ng_ref[...]
    for h in range(GDN_HEADS):
        sl = slice(h * HEAD_DIM, (h + 1) * HEAD_DIM)
        S = S_ref[h]
        Sb = S.astype(BF16)
        v_new = u_ref[:, sl] - jnp.dot(w_ref[:, sl], Sb, preferred_element_type=F32)
        vb = v_new.astype(BF16)
        o = (jnp.dot(qg_ref[:, sl], Sb, preferred_element_type=F32)
             + jnp.dot(qk_ref[:, h * C:(h + 1) * C], vb, preferred_element_type=F32))
        S_ref[h] = S * gl_ref[h:h + 1, :] + _dot_tn(kd_ref[:, sl], vb)
        o = o * lax.rsqrt(jnp.mean(o * o, -1, keepdims=True) + NORM_EPS) * ng
        o_ref[:, sl] = (o * jax.nn.silu(z_ref[:, sl])).astype(o_ref.dtype)

    @pl.when(c == pl.num_programs(1) - 1)
    def _():
        sfin_ref[...] = S_ref[...]


def gdn_scan(prep, P, norm_g, s0, *, n_seq, C):
    u, w, qg, kd, qk, gl = prep
    M = u.shape[0]
    T = M // n_seq
    n = T // C
    row_spec = pl.BlockSpec((C, GDN_WIDTH), lambda b, c: (b * n + c, 0))
    st_spec = pl.BlockSpec((None, GDN_HEADS, HEAD_DIM, HEAD_DIM), lambda b, c: (b, 0, 0, 0))
    return pl.pallas_call(
        functools.partial(_gdn_scan_kernel, C=C),
        grid=(n_seq, n),
        in_specs=[row_spec, row_spec, row_spec, row_spec,
                  pl.BlockSpec((C, GDN_HEADS * C), lambda b, c: (b * n + c, 0)),
                  pl.BlockSpec((None, GDN_HEADS, HEAD_DIM), lambda b, c: (b * n + c, 0, 0)),
                  pl.BlockSpec((C, GDN_WIDTH), lambda b, c: (b * n + c, Z0 // GDN_WIDTH)),
                  pl.BlockSpec((1, HEAD_DIM), lambda b, c: (0, 0)),
                  st_spec],
        out_specs=[row_spec, st_spec],
        out_shape=[jax.ShapeDtypeStruct((M, GDN_WIDTH), BF16),
                   jax.ShapeDtypeStruct((n_seq, GDN_HEADS, HEAD_DIM, HEAD_DIM), F32)],
        scratch_shapes=[pltpu.VMEM((GDN_HEADS, HEAD_DIM, HEAD_DIM), F32)],
        compiler_params=_cparams("parallel", "arbitrary"),
        name="gdn_scan",
    )(u, w, qg, kd, qk, gl, P, norm_g, s0)


PAGES_PER_STEP = 32


def _compress_sample_kernel(pt_ref, cache_ref, pe_ref, w1_ref, w2_ref, out_ref, xbuf, sem, *, layer, n_pages, pps):
    b = pl.program_id(0)
    gi = pl.program_id(1)
    ng = pl.num_programs(1)
    step = b * ng + gi
    nsteps = pl.num_programs(0) * ng
    slot = step % 2

    def copies(st, sl):
        bb = st // ng
        g0 = (st % ng) * pps
        out = []
        for jl in range(pps):
            page = pt_ref[bb * n_pages + g0 + jl]
            for gc in range(4):
                out.append(pltpu.make_async_copy(
                    cache_ref.at[layer, page, :, :, pl.ds(gc * HEAD_DIM, HEAD_DIM)],
                    xbuf.at[sl, gc, pl.ds(2 * jl, 2)],
                    sem.at[sl]))
        return out

    @pl.when(step == 0)
    def _():
        for cp in copies(step, slot):
            cp.start()

    @pl.when(step + 1 < nsteps)
    def _():
        for cp in copies(step + 1, 1 - slot):
            cp.start()

    for cp in copies(step, slot):
        cp.wait()

    nblk = 2 * pps
    xs = [xbuf.at[slot, gc] for gc in range(4)]
    _compress_rows(xs, pe_ref, w1_ref, w2_ref, out_ref, nblk)


def compress_sample(page_table, cache5, pe_t, w1, w2, *, layer):
    Bs, n_pages = page_table.shape
    pps = min(PAGES_PER_STEP, n_pages)
    assert n_pages % pps == 0
    ng = n_pages // pps
    nblk = 2 * pps
    grid_spec = pltpu.PrefetchScalarGridSpec(
        num_scalar_prefetch=1,
        grid=(Bs, ng),
        in_specs=[pl.BlockSpec(memory_space=pl.ANY),
                  pl.BlockSpec(pe_t.shape, lambda b, g, pt: (0, 0, 0)),
                  pl.BlockSpec(w1.shape, lambda b, g, pt: (0, 0, 0)),
                  pl.BlockSpec(w2.shape, lambda b, g, pt: (0, 0, 0))],
        out_specs=pl.BlockSpec((None, None, 2, NSA_KV_HEADS * nblk, HEAD_DIM), lambda b, g, pt: (b, g, 0, 0, 0)),
        scratch_shapes=[pltpu.VMEM((2, 4, nblk, NSA_BLOCK, HEAD_DIM), F32),
                        pltpu.SemaphoreType.DMA((2,))],
    )
    return pl.pallas_call(
        functools.partial(_compress_sample_kernel, layer=layer, n_pages=n_pages, pps=pps),
        grid_spec=grid_spec,
        out_shape=jax.ShapeDtypeStruct((Bs, ng, 2, NSA_KV_HEADS * nblk, HEAD_DIM), F32),
        compiler_params=_cparams("arbitrary", "arbitrary"),
        name="compress_sample",
    )(page_table.reshape(-1), cache5, pe_t, w1, w2)


def _stack_heads(q_ref, g, scale):
    return jnp.concatenate(
        [q_ref[:, (g * NSA_GROUP + r) * HEAD_DIM:(g * NSA_GROUP + r + 1) * HEAD_DIM] * scale for r in range(NSA_GROUP)],
        axis=0).astype(BF16)


def _stacked_row_info(past):
    rows = NSA_GROUP * ZROWS
    ridx = lax.broadcasted_iota(jnp.int32, (rows, 1), 0)
    zrow = ridx % ZROWS
    qpos = past + zrow - ZTOK0
    slope = jnp.zeros((rows, 1), F32)
    return ridx, zrow, qpos, slope


def _slope_col(g):
    ridx = lax.broadcasted_iota(jnp.int32, (NSA_GROUP * ZROWS, 1), 0)
    sl = jnp.zeros((NSA_GROUP * ZROWS, 1), F32)
    for r in range(NSA_GROUP):
        sl = jnp.where(ridx // ZROWS == r, SLOPES[g][r], sl)
    return sl


def _nsa_sample_a_kernel(q_ref, kvw_ref, ck_ref, cv_ref, win_ref, ocmp_ref, owin_ref, sel_ref, wout_ref, *, past, nc, n_blocks, seq):
    scale = HEAD_DIM ** -0.5
    _, zrow, qpos, _ = _stacked_row_info(past)
    wb = win_ref.shape[0]
    lanes_pad = sel_ref.shape[-1]
    sel_lanes = ((n_blocks + HEAD_DIM - 1) // HEAD_DIM) * HEAD_DIM
    top_n = min(NSA_TOP_N, n_blocks)
    for g in range(NSA_KV_HEADS):
        q4 = _stack_heads(q_ref, g, scale)
        slope = _slope_col(g)
        cpos = lax.broadcasted_iota(jnp.int32, (1, nc), 1) * NSA_BLOCK + (NSA_BLOCK - 1)
        vm = cpos <= qpos
        s = jnp.where(vm, _dot_nt(q4, ck_ref[g]) + slope * (cpos - past).astype(F32), NEG)
        e = jnp.exp(s - jnp.max(s, axis=-1, keepdims=True))
        p = jnp.where(vm, e / jnp.sum(e, axis=-1, keepdims=True), 0.0)
        ocmp_ref[g] = _dot(p, cv_ref[g])
        imp = p[0:ZROWS]
        for r in range(1, NSA_GROUP):
            imp = imp + p[r * ZROWS:(r + 1) * ZROWS]
        imp = jnp.concatenate([imp, jnp.zeros((ZROWS, sel_lanes - nc), F32)], axis=1)
        rank = _rank_desc(_block_scores(imp, qpos[0:ZROWS], n_blocks), n_blocks)
        lane = lax.broadcasted_iota(jnp.int32, rank.shape, 1)
        olane = lax.broadcasted_iota(jnp.int32, (ZROWS, lanes_pad), 1)
        out = jnp.zeros((ZROWS, lanes_pad), jnp.int32)
        for t in range(top_n):
            idx = jnp.sum(jnp.where(rank == float(t), lane.astype(F32), 0.0), axis=-1, keepdims=True)
            out = jnp.where(olane == t, idx.astype(jnp.int32), out)
        sel_ref[g] = out
        kold = win_ref[:, g * 2 * HEAD_DIM:(g * 2 + 1) * HEAD_DIM]
        vold = win_ref[:, (g * 2 + 1) * HEAD_DIM:(g * 2 + 2) * HEAD_DIM]
        knew = kvw_ref[:, g * 2 * HEAD_DIM:(g * 2 + 1) * HEAD_DIM]
        vnew = kvw_ref[:, (g * 2 + 1) * HEAD_DIM:(g * 2 + 2) * HEAD_DIM]
        kpos_o = past - wb + lax.broadcasted_iota(jnp.int32, (1, wb), 1)
        kz = lax.broadcasted_iota(jnp.int32, (1, ZROWS), 1)
        kpos_n = past + kz - ZTOK0
        d_o = qpos - kpos_o
        d_n = qpos - kpos_n
        ok_o = (kpos_o >= 0) & (d_o >= 0) & (d_o < NSA_WINDOW)
        ok_n = (kz >= ZTOK0) & (kz < ZTOK0 + seq) & (d_n >= 0) & (d_n < NSA_WINDOW)
        s_o = jnp.where(ok_o, _dot_nt(q4, kold) + slope * (kpos_o - past).astype(F32), NEG)
        s_n = jnp.where(ok_n, _dot_nt(q4, knew) + slope * (kpos_n - past).astype(F32), NEG)
        m = jnp.maximum(jnp.max(s_o, axis=-1, keepdims=True), jnp.max(s_n, axis=-1, keepdims=True))
        p_o = jnp.exp(s_o - m)
        p_n = jnp.exp(s_n - m)
        den = jnp.sum(p_o, axis=-1, keepdims=True) + jnp.sum(p_n, axis=-1, keepdims=True)
        owin_ref[g] = (_dot(p_o, vold) + _dot(p_n, vnew)) / den
    old = win_ref[...]
    wout_ref[...] = pltpu.roll(old, wb - seq, 0)
    shifted_new = pltpu.roll(kvw_ref[...], ZROWS - (ZTOK0 + seq - 8), 0)[0:8]
    r8 = lax.broadcasted_iota(jnp.int32, (8, 1), 0)
    tail_old = pltpu.roll(old[wb - 16:wb], 16 - seq, 0)[8:16] if seq < 8 else None
    wout_ref[wb - 8:wb, :] = jnp.where(r8 >= 8 - seq, shifted_new, tail_old)


def nsa_sample_a(Ps, ckv, cache_win4, *, layer, past, seq):
    Bs = Ps.shape[0] // ZROWS
    nc = ckv.shape[3]
    wb = cache_win4.shape[2]
    n_blocks = -(-(past + seq) // NSA_BLOCK)
    assert 1 <= seq <= 4 and wb % 16 == 0
    rows = NSA_GROUP * ZROWS
    big = pl.BlockSpec((None, NSA_KV_HEADS, rows, HEAD_DIM), lambda b: (b, 0, 0, 0))
    return pl.pallas_call(
        functools.partial(_nsa_sample_a_kernel, past=past, nc=nc, n_blocks=n_blocks, seq=seq),
        grid=(Bs,),
        in_specs=[pl.BlockSpec((ZROWS, NSA_WIDTH), lambda b: (b, Q0 // NSA_WIDTH)),
                  pl.BlockSpec((ZROWS, KV_COLS), lambda b: (b, KVW0 // KV_COLS)),
                  pl.BlockSpec((None, None, NSA_KV_HEADS, nc, HEAD_DIM), lambda b: (b, 0, 0, 0, 0)),
                  pl.BlockSpec((None, None, NSA_KV_HEADS, nc, HEAD_DIM), lambda b: (b, 1, 0, 0, 0)),
                  pl.BlockSpec((None, None, wb, KV_COLS), lambda b: (layer, b, 0, 0))],
        out_specs=[big, big,
                   pl.BlockSpec((None, NSA_KV_HEADS, ZROWS, HEAD_DIM), lambda b: (b, 0, 0, 0)),
                   pl.BlockSpec((None, wb, KV_COLS), lambda b: (b, 0, 0))],
        out_shape=[jax.ShapeDtypeStruct((Bs, NSA_KV_HEADS, rows, HEAD_DIM), F32),
                   jax.ShapeDtypeStruct((Bs, NSA_KV_HEADS, rows, HEAD_DIM), F32),
                   jax.ShapeDtypeStruct((Bs, NSA_KV_HEADS, ZROWS, HEAD_DIM), jnp.int32),
                   jax.ShapeDtypeStruct((Bs, wb, KV_COLS), F32)],
        compiler_params=_cparams("parallel"),
        name="nsa_sample_a",
    )(Ps, Ps, ckv, ckv, cache_win4)


def _nsa_sample_b_kernel(pt_ref, sel_ref, q_ref, kvs_ref, misc_ref, ocmp_ref, owin_ref, *rest, past, seq, n_past_blocks):
    blk_refs = rest[:seq]
    o_ref, m_ref, l_ref, acc_ref = rest[seq:]
    b = pl.program_id(0)
    g = pl.program_id(1)
    j = pl.program_id(2)
    nj = pl.num_programs(2)
    scale = HEAD_DIM ** -0.5
    rows = NSA_GROUP * ZROWS
    ridx, zrow, qpos, _ = _stacked_row_info(past)
    q4 = jnp.where(g == 0, _stack_heads(q_ref, 0, scale), _stack_heads(q_ref, 1, scale))
    slope = jnp.where(g == 0, _slope_col(0), _slope_col(1))

    @pl.when(j == 0)
    def _():
        kn = jnp.where(g == 0, kvs_ref[:, 0:HEAD_DIM], kvs_ref[:, 2 * HEAD_DIM:3 * HEAD_DIM])
        vn = jnp.where(g == 0, kvs_ref[:, HEAD_DIM:2 * HEAD_DIM], kvs_ref[:, 3 * HEAD_DIM:4 * HEAD_DIM])
        kz = lax.broadcasted_iota(jnp.int32, (1, ZROWS), 1)
        kpos = past + kz - ZTOK0
        ok = (kz >= ZTOK0) & (kz < ZTOK0 + seq) & (kpos <= qpos)
        s = jnp.where(ok, _dot_nt(q4, kn) + slope * (kpos - past).astype(F32), NEG)
        m = jnp.max(s, axis=-1, keepdims=True)
        p = jnp.exp(s - m)
        m_ref[...] = m
        l_ref[...] = jnp.sum(p, axis=-1, keepdims=True)
        acc_ref[...] = _dot(p, vn)

    lane = lax.broadcasted_iota(jnp.int32, (1, NSA_BLOCK), 1)
    s = jnp.full((rows, NSA_BLOCK), NEG, F32)
    vsel = []
    for t in range(seq):
        blk = sel_ref[((b * seq + t) * NSA_KV_HEADS + g) * NSA_TOP_N + j]
        kv = blk_refs[t][...]
        k = kv[:, 0:HEAD_DIM]
        kpos = blk * NSA_BLOCK + lane
        st = _dot_nt(q4, k) + slope * (kpos - past).astype(F32)
        mine = (zrow == ZTOK0 + t) & (blk < n_past_blocks) & (kpos <= qpos)
        s = jnp.where(mine, st, s)
        vsel.append(kv[:, HEAD_DIM:2 * HEAD_DIM])
    m_old = m_ref[...]
    m_new = jnp.maximum(m_old, jnp.max(s, axis=-1, keepdims=True))
    p = jnp.exp(s - m_new)
    alpha = jnp.exp(m_old - m_new)
    pv = jnp.zeros((rows, HEAD_DIM), F32)
    for t in range(seq):
        pv = pv + jnp.where(zrow == ZTOK0 + t, _dot(p, vsel[t]), 0.0)
    l_ref[...] = alpha * l_ref[...] + jnp.sum(p, axis=-1, keepdims=True)
    acc_ref[...] = alpha * acc_ref[...] + pv
    m_ref[...] = m_new

    @pl.when(j == nj - 1)
    def _():
        o_slc = acc_ref[...] / l_ref[...]
        gates = jax.nn.sigmoid(misc_ref[...])
        o_cmp = ocmp_ref[...]
        o_win = owin_ref[...]
        for r in range(NSA_GROUP):
            rs = slice(r * ZROWS, (r + 1) * ZROWS)
            outs = []
            for gg in range(NSA_KV_HEADS):
                c = gg * NSA_GROUP * 3 + r * 3 + GATE_LANE
                outs.append(gates[:, c:c + 1] * o_cmp[rs] + gates[:, c + 1:c + 2] * o_slc[rs]
                            + gates[:, c + 2:c + 3] * o_win[rs])
            o_ref[:, r * HEAD_DIM:(r + 1) * HEAD_DIM] = jnp.where(g == 0, outs[0], outs[1]).astype(o_ref.dtype)


def nsa_sample_b(page_table, sel, Ps, ocmp, owin, cache5, *, layer, past, seq):
    Bs, n_pages = page_table.shape
    rows = NSA_GROUP * ZROWS
    n_past_blocks = past // NSA_BLOCK
    gw = NSA_GROUP * HEAD_DIM

    def blk_map(t):
        def f(b, g, j, pt, sl):
            blk = jnp.minimum(sl[((b * seq + t) * NSA_KV_HEADS + g) * NSA_TOP_N + j], n_past_blocks - 1)
            return (layer, pt[b * n_pages + blk // 2], blk % 2, 0, g)
        return f

    big = pl.BlockSpec((None, None, rows, HEAD_DIM), lambda b, g, j, pt, sl: (b, g, 0, 0))
    in_specs = [pl.BlockSpec((ZROWS, NSA_WIDTH), lambda b, g, j, pt, sl: (b, Q0 // NSA_WIDTH)),
                pl.BlockSpec((ZROWS, KV_COLS), lambda b, g, j, pt, sl: (b, KVS0 // KV_COLS)),
                pl.BlockSpec((ZROWS, HEAD_DIM), lambda b, g, j, pt, sl: (b, MISC0 // HEAD_DIM)),
                big, big]
    in_specs += [pl.BlockSpec((None, None, None, NSA_BLOCK, 2 * HEAD_DIM), blk_map(t)) for t in range(seq)]
    grid_spec = pltpu.PrefetchScalarGridSpec(
        num_scalar_prefetch=2,
        grid=(Bs, NSA_KV_HEADS, NSA_TOP_N),
        in_specs=in_specs,
        out_specs=pl.BlockSpec((ZROWS, gw), lambda b, g, j, pt, sl: (b, g)),
        scratch_shapes=[pltpu.VMEM((rows, 1), F32), pltpu.VMEM((rows, 1), F32), pltpu.VMEM((rows, HEAD_DIM), F32)],
    )
    return pl.pallas_call(
        functools.partial(_nsa_sample_b_kernel, past=past, seq=seq, n_past_blocks=n_past_blocks),
        grid_spec=grid_spec,
        out_shape=jax.ShapeDtypeStruct((Bs * ZROWS, NSA_WIDTH), BF16),
        compiler_params=_cparams("parallel", "arbitrary", "arbitrary"),
        name="nsa_sample_b",
    )(page_table.reshape(-1), sel, Ps, Ps, Ps, ocmp, owin, *([cache5] * seq))


def _reorder_w_in(w_in):
    q, kvc, kvs, kvw = w_in[..., 0:1024], w_in[..., 1024:1536], w_in[..., 1536:2048], w_in[..., 2048:2560]
    gates, qkv, z, ab = w_in[..., 2560:2584], w_in[..., 2584:5656], w_in[..., 5656:6680], w_in[..., 6680:6696]
    pad = jnp.zeros(w_in.shape[:-1] + (IN_COLS_PAD - MISC0 - 40,), w_in.dtype)
    return jnp.concatenate([qkv, q, z, kvc, kvs, kvw, gates, ab, pad], axis=-1).astype(BF16)


def _lane_row(vec, lane0):
    return jnp.zeros((1, HEAD_DIM), F32).at[0, lane0:lane0 + vec.shape[0]].set(vec.astype(F32))


def _to_zrows(x, first_row):
    Bs, n, C = x.shape
    z = jnp.zeros((Bs, ZROWS, C), x.dtype).at[:, first_row:first_row + n].set(x)
    return z.reshape(Bs * ZROWS, C)


def kernel(x_prompt, x_sample, cache_cmp_kv, cache_slc_kv, cache_win_kv, state_gdn, state_gdn_conv, state_ffn_conv, page_table, w_in, w_o, cmp_pe, cmp_w1, cmp_w2, gdn_conv_w, gdn_a_log, gdn_dt_bias, gdn_norm_g, ln1_g, ln1_b, ffn_w_up, ffn_conv_w, ffn_w_down, ln2_g, ln2_b):
    B, S, D = x_prompt.shape
    Bs, seq, _ = x_sample.shape
    depth = w_in.shape[0]
    n_pool, page = cache_cmp_kv.shape[1], cache_cmp_kv.shape[2]
    n_pages = page_table.shape[1]
    past = n_pages * page
    wb = cache_win_kv.shape[2]
    d_ff = ffn_w_down.shape[1]
    alpha = (2 * depth) ** 0.25
    assert page == 2 * NSA_BLOCK

    w_in_r = _reorder_w_in(w_in)
    w_o_b = w_o.astype(BF16)
    w_up_b = ffn_w_up.astype(BF16)
    w_down_b = ffn_w_down.astype(BF16)
    w1_b = cmp_w1.astype(BF16)
    w2_b = cmp_w2.astype(BF16)
    pe_t = jnp.swapaxes(cmp_pe, 1, 2)
    cache_cmp5 = cache_cmp_kv.reshape(depth, n_pool, 2, NSA_BLOCK, 4 * HEAD_DIM)
    cache_slc5 = cache_slc_kv.reshape(depth, n_pool, 2, NSA_BLOCK, 4 * HEAD_DIM)
    cache_win4 = cache_win_kv.reshape(depth, Bs, wb, 4 * HEAD_DIM)

    xp = x_prompt.reshape(B * S, D)
    xs = _to_zrows(x_sample, ZTOK0)
    tm = 512 if (B * S) % 512 == 0 else B * S
    tq = 128
    p_out = [[] for _ in range(6)]
    s_out = [[] for _ in range(6)]
    zero_state = jnp.zeros((B, GDN_HEADS, HEAD_DIM, HEAD_DIM), F32)

    for l in range(depth):
        alog_row = _lane_row(gdn_a_log[l], A_LANE)
        dtb_row = _lane_row(gdn_dt_bias[l], A_LANE)
        ng = gdn_norm_g[l].reshape(1, HEAD_DIM)
        g1, b1 = ln1_g[l].reshape(1, D), ln1_b[l].reshape(1, D)
        g2, b2 = ln2_g[l].reshape(1, D), ln2_b[l].reshape(1, D)

        P = matmul(xp, w_in_r[l], tm=tm, tn=1024)
        Pkv = P[:, KVS0:KVS0 + 2 * KV_COLS].astype(BF16)
        ckv = compress_prompt(P, pe_t[l], w1_b[l], w2_b[l], n_seq=B)
        o_nsa = nsa_prompt(P, Pkv, ckv, n_seq=B, tq=tq)
        prep = gdn_prep(P, gdn_conv_w[l], alog_row, dtb_row, n_seq=B, C=GDN_CHUNK)
        o_gdn, s_fin = gdn_scan(prep, P, ng, zero_state, n_seq=B, C=GDN_CHUNK)
        h = proj_ln(o_nsa, o_gdn, w_o_b[l], xp, g1, b1, alpha=alpha, tm=256)
        act, utail = ffn_up_prompt(h, w_up_b[l], ffn_conv_w[l], n_seq=B, tm=tm, tf=512)
        xp = ffn_down_ln(act, w_down_b[l], h, g2, b2, alpha=alpha, tm=tm, tk=512)

        P3 = P.reshape(B, S, IN_COLS_PAD)
        p_out[0].append(P3[:, :, KVC0:KVC0 + KV_COLS].reshape(B, S, NSA_KV_HEADS, 2, HEAD_DIM))
        p_out[1].append(P3[:, :, KVS0:KVS0 + KV_COLS].reshape(B, S, NSA_KV_HEADS, 2, HEAD_DIM))
        wn = min(NSA_WINDOW, S)
        p_out[2].append(P3[:, S - wn:, KVW0:KVW0 + KV_COLS].reshape(B, wn, NSA_KV_HEADS, 2, HEAD_DIM))
        p_out[3].append(s_fin)
        p_out[4].append(P3[:, S - (GDN_CONV - 1):, QKV0:QKV0 + 3 * GDN_WIDTH])
        p_out[5].append(utail[:, 8 - (FFN_CONV - 1):, :])

        Ps = matmul(xs, w_in_r[l], tm=Bs * ZROWS, tn=1024)
        ckv_s = compress_sample(page_table, cache_cmp5, pe_t[l], w1_b[l], w2_b[l], layer=l)
        ng_grp = ckv_s.shape[1]
        nblk_step = ckv_s.shape[3] // NSA_KV_HEADS
        ckv_s = ckv_s.reshape(Bs, ng_grp, 2, NSA_KV_HEADS, nblk_step, HEAD_DIM)
        ckv_s = jnp.transpose(ckv_s, (0, 2, 3, 1, 4, 5)).reshape(Bs, 2, NSA_KV_HEADS, ng_grp * nblk_step, HEAD_DIM)
        ocmp, owin, sel, win_new = nsa_sample_a(Ps, ckv_s, cache_win4, layer=l, past=past, seq=seq)
        sel_flat = jnp.transpose(sel[:, :, ZTOK0:ZTOK0 + seq, :NSA_TOP_N], (0, 2, 1, 3)).reshape(-1)
        o_nsa_s = nsa_sample_b(page_table, sel_flat, Ps, ocmp, owin, cache_slc5, layer=l, past=past, seq=seq)
        gbufz = _to_zrows(state_gdn_conv[l], ZTOK0 - (GDN_CONV - 1))
        prep_s = gdn_prep(Ps, gdn_conv_w[l], alog_row, dtb_row, n_seq=Bs, C=ZROWS, bufz=gbufz)
        o_gdn_s, s_fin_s = gdn_scan(prep_s, Ps, ng, state_gdn[l], n_seq=Bs, C=ZROWS)
        hs = proj_ln(o_nsa_s, o_gdn_s, w_o_b[l], xs, g1, b1, alpha=alpha, tm=Bs * ZROWS)
        fbufz = _to_zrows(state_ffn_conv[l], ZTOK0 - (FFN_CONV - 1))
        act_s, u_s = ffn_up_sample(hs, w_up_b[l], ffn_conv_w[l], fbufz, tf=512)
        xs = ffn_down_ln(act_s, w_down_b[l], hs, g2, b2, alpha=alpha, tm=Bs * ZROWS, tk=512)

        Ps3 = Ps.reshape(Bs, ZROWS, IN_COLS_PAD)[:, ZTOK0:ZTOK0 + seq]
        s_out[0].append(Ps3[:, :, KVC0:KVC0 + KV_COLS].reshape(Bs, seq, NSA_KV_HEADS, 2, HEAD_DIM))
        s_out[1].append(Ps3[:, :, KVS0:KVS0 + KV_COLS].reshape(Bs, seq, NSA_KV_HEADS, 2, HEAD_DIM))
        s_out[2].append(win_new.reshape(Bs, wb, NSA_KV_HEADS, 2, HEAD_DIM))
        s_out[3].append(s_fin_s)
        ext_g = jnp.concatenate([state_gdn_conv[l], Ps3[:, :, QKV0:QKV0 + 3 * GDN_WIDTH]], axis=1)
        s_out[4].append(ext_g[:, -(GDN_CONV - 1):])
        u3 = u_s.reshape(Bs, ZROWS, d_ff)[:, ZTOK0:ZTOK0 + seq]
        ext_f = jnp.concatenate([state_ffn_conv[l], u3], axis=1)
        s_out[5].append(ext_f[:, -(FFN_CONV - 1):])

    y_p = xp.reshape(B, S, D)
    y_s = xs.reshape(Bs, ZROWS, D)[:, ZTOK0:ZTOK0 + seq]
    return (y_p, y_s, *[jnp.stack(v, 0) for v in p_out], *[jnp.stack(v, 0) for v in s_out])
```

```python
import functools

import jax
import jax.numpy as jnp
import numpy as np
from jax import lax
from jax.experimental import pallas as pl
from jax.experimental.pallas import tpu as pltpu

F32 = jnp.float32
BF16 = jnp.bfloat16

HEAD_DIM = 128
NSA_HEADS = 8
NSA_KV_HEADS = 2
NSA_GROUP = 4
NSA_WIDTH = 1024
NSA_BLOCK = 64
NSA_TOP_N = 16
NSA_WINDOW = 512
GDN_HEADS = 8
GDN_WIDTH = 1024
GDN_CONV = 4
GDN_CHUNK = 64
FFN_CONV = 3
KV_COLS = 512
LN_EPS = 1e-5
NORM_EPS = 1e-6
NEG = -1e30
FORCE = 1e4
SLOPES = [[2.0 ** -(g * NSA_GROUP + r + 1) for r in range(NSA_GROUP)] for g in range(NSA_KV_HEADS)]

QKV0, Q0, Z0, KVC0, KVS0, KVW0, MISC0, IN_COLS_PAD = 0, 3072, 4096, 5120, 5632, 6144, 6656, 7168
GATE_LANE, A_LANE, B_LANE = 0, 24, 32

ZROWS = 16
ZTOK0 = 8

VMEM_LIMIT_BYTES = 56 * 1024 * 1024


def _cparams(*sem):
    return pltpu.CompilerParams(dimension_semantics=sem, vmem_limit_bytes=VMEM_LIMIT_BYTES)


def _dot(a, b):
    return jnp.dot(a.astype(BF16), b.astype(BF16), preferred_element_type=F32)


def _dot_nt(a, b):
    return lax.dot_general(a.astype(BF16), b.astype(BF16), (((1,), (1,)), ((), ())), preferred_element_type=F32)


def _dot_tn(a, b):
    return lax.dot_general(a.astype(BF16), b.astype(BF16), (((0,), (0,)), ((), ())), preferred_element_type=F32)


def _layer_norm(t, g, b):
    mu = jnp.mean(t, -1, keepdims=True)
    d = t - mu
    var = jnp.mean(d * d, -1, keepdims=True)
    return d * lax.rsqrt(var + LN_EPS) * g + b


def _mm_kernel(x_ref, w_ref, o_ref):
    o_ref[...] = _dot(x_ref[...], w_ref[...])


def matmul(x, w, *, tm, tn):
    M, K = x.shape
    _, N = w.shape
    assert M % tm == 0 and N % tn == 0
    return pl.pallas_call(
        _mm_kernel,
        grid=(M // tm, N // tn),
        in_specs=[pl.BlockSpec((tm, K), lambda i, j: (i, 0)),
                  pl.BlockSpec((K, tn), lambda i, j: (0, j))],
        out_specs=pl.BlockSpec((tm, tn), lambda i, j: (i, j)),
        out_shape=jax.ShapeDtypeStruct((M, N), F32),
        compiler_params=_cparams("parallel", "arbitrary"),
        name="in_proj",
    )(x, w)


def _proj_ln_kernel(a1_ref, a2_ref, w_ref, x_ref, g_ref, b_ref, o_ref, *, alpha):
    k1 = a1_ref.shape[1]
    acc = _dot(a1_ref[...], w_ref[:k1, :]) + _dot(a2_ref[...], w_ref[k1:, :])
    o_ref[...] = _layer_norm(alpha * x_ref[...] + acc, g_ref[...], b_ref[...])


def proj_ln(a1, a2, w, x, g, b, *, alpha, tm):
    M, D = x.shape
    k1, k2 = a1.shape[1], a2.shape[1]
    assert M % tm == 0 and w.shape == (k1 + k2, D)
    return pl.pallas_call(
        functools.partial(_proj_ln_kernel, alpha=alpha),
        grid=(M // tm,),
        in_specs=[pl.BlockSpec((tm, k1), lambda i: (i, 0)),
                  pl.BlockSpec((tm, k2), lambda i: (i, 0)),
                  pl.BlockSpec((k1 + k2, D), lambda i: (0, 0)),
                  pl.BlockSpec((tm, D), lambda i: (i, 0)),
                  pl.BlockSpec((1, D), lambda i: (0, 0)),
                  pl.BlockSpec((1, D), lambda i: (0, 0))],
        out_specs=pl.BlockSpec((tm, D), lambda i: (i, 0)),
        out_shape=jax.ShapeDtypeStruct((M, D), F32),
        compiler_params=_cparams("parallel"),
        name="out_proj_ln",
    )(a1, a2, w, x, g, b)


def _conv3(u, prev8, cw):
    ue = jnp.concatenate([prev8, u], axis=0)
    u1 = pltpu.roll(ue, 1, 0)[8:]
    u2 = pltpu.roll(ue, 2, 0)[8:]
    return cw[2:3, :] * u + cw[1:2, :] * u1 + cw[0:1, :] * u2


def _ffn_up_prompt_kernel(h_ref, wu_ref, wv_ref, cw_ref, act_ref, tail_ref, carry_ref, *, tiles_per_seq):
    i = pl.program_id(1)
    h = h_ref[...]
    u = _dot(h, wu_ref[...])
    v = _dot(h, wv_ref[...])

    @pl.when(i % tiles_per_seq == 0)
    def _():
        carry_ref[...] = jnp.zeros_like(carry_ref)

    uc = _conv3(u, carry_ref[...], cw_ref[...])
    act_ref[...] = (jax.nn.gelu(uc) * v).astype(act_ref.dtype)
    tail = u[u.shape[0] - 8:, :]
    carry_ref[...] = tail
    tail_ref[...] = tail


def ffn_up_prompt(h, w_up, cw, *, n_seq, tm, tf):
    M, D = h.shape
    F = w_up.shape[1] // 2
    T = M // n_seq
    assert M % tm == 0 and F % tf == 0 and T % tm == 0
    nf = F // tf
    tps = T // tm
    return pl.pallas_call(
        functools.partial(_ffn_up_prompt_kernel, tiles_per_seq=tps),
        grid=(nf, M // tm),
        in_specs=[pl.BlockSpec((tm, D), lambda f, i: (i, 0)),
                  pl.BlockSpec((D, tf), lambda f, i: (0, f)),
                  pl.BlockSpec((D, tf), lambda f, i: (0, nf + f)),
                  pl.BlockSpec((FFN_CONV, tf), lambda f, i: (0, f))],
        out_specs=[pl.BlockSpec((tm, tf), lambda f, i: (i, f)),
                   pl.BlockSpec((None, 8, tf), lambda f, i: (i // tps, 0, f))],
        out_shape=[jax.ShapeDtypeStruct((M, F), BF16),
                   jax.ShapeDtypeStruct((n_seq, 8, F), F32)],
        scratch_shapes=[pltpu.VMEM((8, tf), F32)],
        compiler_params=_cparams("arbitrary", "arbitrary"),
        name="ffn_up_prompt",
    )(h, w_up, w_up, cw)


def _ffn_up_sample_kernel(h_ref, wu_ref, wv_ref, cw_ref, buf_ref, act_ref, u_ref):
    h = h_ref[...]
    u = _dot(h, wu_ref[...])
    v = _dot(h, wv_ref[...])
    zrow = lax.broadcasted_iota(jnp.int32, u.shape, 0) % ZROWS
    isbuf = (zrow >= ZTOK0 - (FFN_CONV - 1)) & (zrow < ZTOK0)
    uz = jnp.where(isbuf, buf_ref[...], u)
    cw = cw_ref[...]
    uc = cw[2:3, :] * uz + cw[1:2, :] * pltpu.roll(uz, 1, 0) + cw[0:1, :] * pltpu.roll(uz, 2, 0)
    act_ref[...] = (jax.nn.gelu(uc) * v).astype(act_ref.dtype)
    u_ref[...] = u


def ffn_up_sample(h, w_up, cw, bufz, *, tf):
    M, D = h.shape
    F = w_up.shape[1] // 2
    nf = F // tf
    return pl.pallas_call(
        _ffn_up_sample_kernel,
        grid=(nf,),
        in_specs=[pl.BlockSpec((M, D), lambda f: (0, 0)),
                  pl.BlockSpec((D, tf), lambda f: (0, f)),
                  pl.BlockSpec((D, tf), lambda f: (0, nf + f)),
                  pl.BlockSpec((FFN_CONV, tf), lambda f: (0, f)),
                  pl.BlockSpec((M, tf), lambda f: (0, f))],
        out_specs=[pl.BlockSpec((M, tf), lambda f: (0, f)),
                   pl.BlockSpec((M, tf), lambda f: (0, f))],
        out_shape=[jax.ShapeDtypeStruct((M, F), BF16),
                   jax.ShapeDtypeStruct((M, F), F32)],
        compiler_params=_cparams("parallel"),
        name="ffn_up_sample",
    )(h, w_up, w_up, cw, bufz)


def _ffn_down_ln_kernel(a_ref, w_ref, h_ref, g_ref, b_ref, o_ref, acc_ref, *, alpha):
    k = pl.program_id(1)

    @pl.when(k == 0)
    def _():
        acc_ref[...] = jnp.zeros_like(acc_ref)

    acc_ref[...] += _dot(a_ref[...], w_ref[...])

    @pl.when(k == pl.num_programs(1) - 1)
    def _():
        o_ref[...] = _layer_norm(alpha * h_ref[...] + acc_ref[...], g_ref[...], b_ref[...])


def ffn_down_ln(a, w, h, g, b, *, alpha, tm, tk):
    M, Fd = a.shape
    D = w.shape[1]
    assert M % tm == 0 and Fd % tk == 0
    return pl.pallas_call(
        functools.partial(_ffn_down_ln_kernel, alpha=alpha),
        grid=(M // tm, Fd // tk),
        in_specs=[pl.BlockSpec((tm, tk), lambda i, k: (i, k)),
                  pl.BlockSpec((tk, D), lambda i, k: (k, 0)),
                  pl.BlockSpec((tm, D), lambda i, k: (i, 0)),
                  pl.BlockSpec((1, D), lambda i, k: (0, 0)),
                  pl.BlockSpec((1, D), lambda i, k: (0, 0))],
        out_specs=pl.BlockSpec((tm, D), lambda i, k: (i, 0)),
        out_shape=jax.ShapeDtypeStruct((M, D), F32),
        scratch_shapes=[pltpu.VMEM((tm, D), F32)],
        compiler_params=_cparams("parallel", "arbitrary"),
        name="ffn_down_ln",
    )(a, w, h, g, b)


def _compress_rows(read_rows, pe_ref, w1_ref, w2_ref, out_ref, nblk):
    for c in range(2):
        def body(pp, acc):
            p0 = pp * 2
            parts = []
            for dp in range(2):
                pe_row = pe_ref[c, pl.ds(p0 + dp, 1), :]
                rows = [read_rows(g * 2 + c, p0 + dp) + pe_row for g in range(NSA_KV_HEADS)]
                parts.append(jnp.concatenate(rows, axis=0))
            lhs = jnp.concatenate(parts, axis=1)
            w = w1_ref[c, pl.ds(pl.multiple_of(p0 * HEAD_DIM, 2 * HEAD_DIM), 2 * HEAD_DIM), :]
            return acc + _dot(lhs, w)

        acc = lax.fori_loop(0, NSA_BLOCK // 2, body, jnp.zeros((NSA_KV_HEADS * nblk, HEAD_DIM), F32))
        out_ref[c] = _dot(jax.nn.silu(acc), w2_ref[c])


def _compress_prompt_kernel(x0, x1, x2, x3, pe_ref, w1_ref, w2_ref, out_ref, *, nblk):
    xs = [x0, x1, x2, x3]
    _compress_rows(lambda gc, p: xs[gc][:, p, :], pe_ref, w1_ref, w2_ref, out_ref, nblk)


def compress_prompt(P, pe_t, w1, w2, *, n_seq):
    M = P.shape[0]
    T = M // n_seq
    nblk = T // NSA_BLOCK
    P3 = P.reshape(n_seq * nblk, NSA_BLOCK, IN_COLS_PAD)
    cb0 = KVC0 // HEAD_DIM
    in_specs = [pl.BlockSpec((nblk, NSA_BLOCK, HEAD_DIM), (lambda b, gc=gc: (b, 0, cb0 + gc))) for gc in range(4)]
    in_specs += [pl.BlockSpec(pe_t.shape, lambda b: (0, 0, 0)),
                 pl.BlockSpec(w1.shape, lambda b: (0, 0, 0)),
                 pl.BlockSpec(w2.shape, lambda b: (0, 0, 0))]
    return pl.pallas_call(
        functools.partial(_compress_prompt_kernel, nblk=nblk),
        grid=(n_seq,),
        in_specs=in_specs,
        out_specs=pl.BlockSpec((None, 2, NSA_KV_HEADS * nblk, HEAD_DIM), lambda b: (b, 0, 0, 0)),
        out_shape=jax.ShapeDtypeStruct((n_seq, 2, NSA_KV_HEADS * nblk, HEAD_DIM), F32),
        compiler_params=_cparams("parallel"),
        name="compress_prompt",
    )(P3, P3, P3, P3, pe_t, w1, w2)


def _rank_desc(score, n_real):
    lane = lax.broadcasted_iota(jnp.int32, score.shape, 1)
    cnt = jnp.zeros(score.shape, F32)
    for i in range(n_real):
        col = score[:, i:i + 1]
        ge = jnp.where(col >= score, 1.0, 0.0)
        gt = jnp.where(col > score, 1.0, 0.0)
        cnt = cnt + jnp.where(lane > i, ge, gt)
    return cnt


def _block_scores(imp, qpos, n_blocks):
    bidx = lax.broadcasted_iota(jnp.int32, imp.shape, 1)
    cur = qpos // NSA_BLOCK
    sc = jnp.where(bidx == 0, FORCE, jnp.where(bidx == cur, FORCE, jnp.where(bidx == cur - 1, FORCE, imp)))
    sc = jnp.where(bidx * NSA_BLOCK <= qpos, sc, NEG)
    return jnp.where(bidx < n_blocks, sc, -jnp.inf)


def _rank_desc_t(score, n_real):
    L = score.shape[0]
    assert L % 8 == 0
    groups = [score[8 * a:8 * a + 8, :] for a in range(L // 8)]
    sub = lax.broadcasted_iota(jnp.int32, groups[0].shape, 0)
    cnts = [jnp.zeros(groups[0].shape, F32) for _ in groups]
    for i in range(n_real):
        row = score[i:i + 1, :]
        for a, x in enumerate(groups):
            if 8 * a > i:
                c = jnp.where(row >= x, 1.0, 0.0)
            elif 8 * a + 7 < i:
                c = jnp.where(row > x, 1.0, 0.0)
            else:
                c = jnp.where(sub + 8 * a > i, jnp.where(row >= x, 1.0, 0.0), jnp.where(row > x, 1.0, 0.0))
            cnts[a] = cnts[a] + c
    return jnp.concatenate(cnts, axis=0)


def _block_scores_t(imp, qpos, n_blocks):
    bidx = lax.broadcasted_iota(jnp.int32, imp.shape, 0)
    cur = qpos // NSA_BLOCK
    sc = jnp.where(bidx == 0, FORCE, jnp.where(bidx == cur, FORCE, jnp.where(bidx == cur - 1, FORCE, imp)))
    sc = jnp.where(bidx * NSA_BLOCK <= qpos, sc, NEG)
    return jnp.where(bidx < n_blocks, sc, -jnp.inf)


def _nsa_prompt_kernel(q_ref, misc_ref, k_ref, vts_ref, vtw_ref, ck_ref, cvt_ref, o_ref, m_ref, l_ref, acc_ref, *, tq, nblk):
    i = pl.program_id(1)
    t0 = i * tq
    tk = tq
    scale = HEAD_DIM ** -0.5
    top_n = min(NSA_TOP_N, nblk)
    qpos_row = t0 + lax.broadcasted_iota(jnp.int32, (1, tq), 1)
    qpos_full = t0 + lax.broadcasted_iota(jnp.int32, (tk, tq), 1)
    sub_k = lax.broadcasted_iota(jnp.int32, (tk, tq), 0)
    e_blk = lax.broadcasted_iota(jnp.int32, (tk, nblk), 1)
    e_key = lax.broadcasted_iota(jnp.int32, (tk, nblk), 0)
    gates_t = jax.nn.sigmoid(misc_ref[...]).T

    def reset():
        m_ref[...] = jnp.full(m_ref.shape, NEG, F32)
        l_ref[...] = jnp.zeros(l_ref.shape, F32)
        acc_ref[...] = jnp.zeros(acc_ref.shape, F32)

    def online_update(s4, vt):
        m_old = m_ref[...]
        m_new = jnp.maximum(m_old, jnp.max(s4, axis=0, keepdims=True))
        p = jnp.exp(s4 - m_new)
        alpha = jnp.exp(m_old - m_new)
        l_ref[...] = alpha * l_ref[...] + jnp.sum(p, axis=0, keepdims=True)
        acc_ref[...] = alpha * acc_ref[...] + jnp.dot(vt, p.astype(BF16), preferred_element_type=F32)
        m_ref[...] = m_new

    def masked_scores(s, valid, base, g):
        return jnp.concatenate(
            [jnp.where(valid, s[:, r * tq:(r + 1) * tq] + SLOPES[g][r] * base, NEG) for r in range(NSA_GROUP)], axis=1)

    for g in range(NSA_KV_HEADS):
        q4t = jnp.concatenate(
            [(q_ref[:, (g * NSA_GROUP + r) * HEAD_DIM:(g * NSA_GROUP + r + 1) * HEAD_DIM] * scale).T
             for r in range(NSA_GROUP)], axis=1).astype(BF16)
        sc = jnp.dot(ck_ref[g], q4t, preferred_element_type=F32)
        cpos = lax.broadcasted_iota(jnp.int32, (nblk, tq), 0) * NSA_BLOCK + (NSA_BLOCK - 1)
        vm = cpos <= qpos_row
        cposrel = (cpos - t0).astype(F32)
        imp = jnp.zeros((nblk, tq), F32)
        parts = []
        for r in range(NSA_GROUP):
            s = jnp.where(vm, sc[:, r * tq:(r + 1) * tq] + SLOPES[g][r] * cposrel, NEG)
            e = jnp.exp(s - jnp.max(s, axis=0, keepdims=True))
            p = jnp.where(vm, e / jnp.sum(e, axis=0, keepdims=True), 0.0)
            imp = imp + p
            parts.append(p)
        ocmp_t = jnp.dot(cvt_ref[g], jnp.concatenate(parts, axis=1).astype(BF16), preferred_element_type=F32)
        rank = _rank_desc_t(_block_scores_t(imp, qpos_row, nblk), nblk)
        sel_t = jnp.where(rank < top_n, 1.0, 0.0).astype(BF16)

        reset()

        def slc_body(j, carry):
            k0 = pl.multiple_of(j * tk, tk)
            k = k_ref[pl.ds(k0, tk), g * HEAD_DIM:(g + 1) * HEAD_DIM]
            vt = vts_ref[g, :, pl.ds(k0, tk)]
            kpos = k0 + sub_k
            expand = jnp.where(e_blk == (k0 + e_key) // NSA_BLOCK, 1.0, 0.0).astype(BF16)
            chosen = jnp.dot(expand, sel_t, preferred_element_type=F32)
            valid = jnp.where(kpos <= qpos_full, chosen, 0.0) > 0.5
            s = jnp.dot(k, q4t, preferred_element_type=F32)
            online_update(masked_scores(s, valid, (kpos - t0).astype(F32), g), vt)
            return carry

        lax.fori_loop(0, i + 1, slc_body, 0)
        oslc_t = acc_ref[...] / l_ref[...]

        reset()

        def win_body(j, carry):
            k0 = pl.multiple_of(j * tk, tk)
            k = k_ref[pl.ds(k0, tk), (NSA_KV_HEADS + g) * HEAD_DIM:(NSA_KV_HEADS + g + 1) * HEAD_DIM]
            vt = vtw_ref[g, :, pl.ds(k0, tk)]
            kpos = k0 + sub_k
            dist = qpos_full - kpos
            valid = jnp.abs(2 * dist - (NSA_WINDOW - 1)) <= (NSA_WINDOW - 1)
            s = jnp.dot(k, q4t, preferred_element_type=F32)
            online_update(masked_scores(s, valid, (kpos - t0).astype(F32), g), vt)
            return carry

        lax.fori_loop(jnp.maximum(i - NSA_WINDOW // tk, 0), i + 1, win_body, 0)
        owin_t = acc_ref[...] / l_ref[...]

        for r in range(NSA_GROUP):
            c = g * NSA_GROUP * 3 + r * 3 + GATE_LANE
            rs = slice(r * tq, (r + 1) * tq)
            o_t = (gates_t[c:c + 1, :] * ocmp_t[:, rs] + gates_t[c + 1:c + 2, :] * oslc_t[:, rs]
                   + gates_t[c + 2:c + 3, :] * owin_t[:, rs])
            h = g * NSA_GROUP + r
            o_ref[:, h * HEAD_DIM:(h + 1) * HEAD_DIM] = o_t.T.astype(o_ref.dtype)


def nsa_prompt(P, kmat, vt, ck, cvt, *, n_seq, tq):
    M = P.shape[0]
    T = M // n_seq
    nblk = T // NSA_BLOCK
    nq = T // tq
    W = NSA_GROUP * tq
    assert T % tq == 0 and tq % NSA_BLOCK == 0 and NSA_WINDOW % tq == 0
    return pl.pallas_call(
        functools.partial(_nsa_prompt_kernel, tq=tq, nblk=nblk),
        grid=(n_seq, nq),
        in_specs=[pl.BlockSpec((tq, NSA_WIDTH), lambda b, i: (b * nq + i, Q0 // NSA_WIDTH)),
                  pl.BlockSpec((tq, HEAD_DIM), lambda b, i: (b * nq + i, MISC0 // HEAD_DIM)),
                  pl.BlockSpec((T, 2 * NSA_KV_HEADS * HEAD_DIM), lambda b, i: (b, 0)),
                  pl.BlockSpec((None, NSA_KV_HEADS, HEAD_DIM, T), lambda b, i: (b, 0, 0, 0)),
                  pl.BlockSpec((None, NSA_KV_HEADS, HEAD_DIM, T), lambda b, i: (b, 1, 0, 0)),
                  pl.BlockSpec((None, NSA_KV_HEADS, nblk, HEAD_DIM), lambda b, i: (b, 0, 0, 0)),
                  pl.BlockSpec((None, NSA_KV_HEADS, HEAD_DIM, nblk), lambda b, i: (b, 0, 0, 0))],
        out_specs=pl.BlockSpec((tq, NSA_WIDTH), lambda b, i: (b * nq + i, 0)),
        out_shape=jax.ShapeDtypeStruct((M, NSA_WIDTH), BF16),
        scratch_shapes=[pltpu.VMEM((1, W), F32),
                        pltpu.VMEM((1, W), F32),
                        pltpu.VMEM((HEAD_DIM, W), F32)],
        compiler_params=_cparams("parallel", "arbitrary"),
        name="nsa_prompt",
    )(P, P, kmat, vt, vt, ck, cvt)


def _bdot(a, b):
    return lax.dot_general(a.astype(BF16), b.astype(BF16), (((2,), (1,)), ((0,), (0,))), preferred_element_type=F32)


def _bdot_nt(a, b):
    return lax.dot_general(a.astype(BF16), b.astype(BF16), (((2,), (2,)), ((0,), (0,))), preferred_element_type=F32)


def _bdot_tn(a, b):
    return lax.dot_general(a.astype(BF16), b.astype(BF16), (((1,), (1,)), ((0,), (0,))), preferred_element_type=F32)


def _head_stack(a, off, width=HEAD_DIM):
    return jnp.stack([a[:, off + h * width:off + (h + 1) * width] for h in range(GDN_HEADS)], axis=0)


def _unit_lower_inverse_minus_eye(A, C):
    row = lax.broadcasted_iota(jnp.int32, (C, C), 0)
    col = lax.broadcasted_iota(jnp.int32, (C, C), 1)
    A8 = jnp.where((row // 8) == (col // 8), A, 0.0)
    B2 = _bdot(A8, A8)
    B4 = _bdot(B2, B2)
    P1 = B2 - A8 - _bdot(A8, B2)
    N = P1 + B4 + _bdot(P1, B4)
    size = 16
    while size <= C:
        AL = jnp.where((row // size) == (col // size), jnp.where((row // (size // 2)) == (col // (size // 2)), 0.0, A), 0.0)
        X = AL + _bdot(N, AL)
        N = N - X - _bdot(X, N)
        size *= 2
    return N


def _gdn_prep_kernel(*refs, C, sample):
    if sample:
        x_ref, buf_ref, misc_ref, cw_ref, alog_ref, dtb_ref = refs[:6]
    else:
        x_ref, prev_ref, misc_ref, cw_ref, alog_ref, dtb_ref = refs[:6]
    u_ref, w_ref, qg_ref, kd_ref, qk_ref, gl_ref = refs[6:]
    cw = cw_ref[...]
    x = x_ref[...]
    rowv = lax.broadcasted_iota(jnp.int32, (C, 1), 0)
    if sample:
        isbuf = (rowv >= ZTOK0 - (GDN_CONV - 1)) & (rowv < ZTOK0)
        xz = jnp.where(isbuf, buf_ref[...], x)
        y = (cw[3:4, :] * xz + cw[2:3, :] * pltpu.roll(xz, 1, 0) + cw[1:2, :] * pltpu.roll(xz, 2, 0)
             + cw[0:1, :] * pltpu.roll(xz, 3, 0))
        valid = jnp.where((rowv >= ZTOK0) & (rowv < ZTOK0 + 4), 1.0, 0.0)
    else:
        prev = jnp.where(pl.program_id(1) == 0, 0.0, prev_ref[...])
        xe = jnp.concatenate([prev, x], axis=0)
        y = (cw[3:4, :] * x + cw[2:3, :] * pltpu.roll(xe, 1, 0)[8:] + cw[1:2, :] * pltpu.roll(xe, 2, 0)[8:]
             + cw[0:1, :] * pltpu.roll(xe, 3, 0)[8:])
        valid = None
    y = jax.nn.silu(y)

    misc = misc_ref[...]
    gfull = -jnp.exp(alog_ref[...]) * jax.nn.softplus(misc + dtb_ref[...])
    bfull = jax.nn.sigmoid(misc)
    if valid is not None:
        gfull = gfull * valid
        bfull = bfull * valid
    row128 = lax.broadcasted_iota(jnp.int32, (C, HEAD_DIM), 0)
    G = gfull
    s = 1
    while s < C:
        G = G + jnp.where(row128 >= s, pltpu.roll(G, s, 0), 0.0)
        s *= 2
    expG = jnp.exp(G)
    glast = G[C - 1:C, :]
    kdfac = jnp.exp(glast - G)
    if C < HEAD_DIM:
        Gpad = jnp.concatenate([G, jnp.zeros((HEAD_DIM - C, HEAD_DIM), F32)], axis=0)
    else:
        Gpad = G
    GT = Gpad.T
    gl_ref[...] = jnp.exp(jnp.broadcast_to(GT[A_LANE:A_LANE + GDN_HEADS, C - 1:C], (GDN_HEADS, HEAD_DIM)))

    row = lax.broadcasted_iota(jnp.int32, (C, C), 0)
    col = lax.broadcasted_iota(jnp.int32, (C, C), 1)
    q = _head_stack(y, 0)
    k = _head_stack(y, GDN_WIDTH)
    v = _head_stack(y, 2 * GDN_WIDTH)
    q = q * lax.rsqrt(jnp.sum(q * q, -1, keepdims=True) + NORM_EPS) * (HEAD_DIM ** -0.5)
    k = k * lax.rsqrt(jnp.sum(k * k, -1, keepdims=True) + NORM_EPS)
    if valid is not None:
        q, k, v = q * valid, k * valid, v * valid
    beta = _head_stack(bfull, B_LANE, 1)
    gcol = _head_stack(G, A_LANE, 1)
    egc = _head_stack(expG, A_LANE, 1)
    kdf = _head_stack(kdfac, A_LANE, 1)
    grow = jnp.stack([GT[A_LANE + h:A_LANE + h + 1, :C] for h in range(GDN_HEADS)], axis=0)
    decay = jnp.exp(jnp.where(row >= col, gcol - grow, NEG))
    kb = k * beta
    A = jnp.where(row > col, _bdot_nt(kb, k) * decay, 0.0)
    N = _unit_lower_inverse_minus_eye(A, C)
    vb = v * beta
    kbg = kb * egc
    U = vb + _bdot(N, vb)
    Wm = kbg + _bdot(N, kbg)
    QK = _bdot_nt(q, k) * decay
    QG = q * egc
    KD = k * kdf
    for h in range(GDN_HEADS):
        sl = slice(h * HEAD_DIM, (h + 1) * HEAD_DIM)
        u_ref[:, sl] = U[h]
        w_ref[:, sl] = Wm[h].astype(w_ref.dtype)
        qk_ref[:, h * C:(h + 1) * C] = QK[h].astype(qk_ref.dtype)
        qg_ref[:, sl] = QG[h].astype(qg_ref.dtype)
        kd_ref[:, sl] = KD[h].astype(kd_ref.dtype)


def gdn_prep(P, cw, alog_row, dtb_row, *, n_seq, C, bufz=None):
    M = P.shape[0]
    T = M // n_seq
    n = T // C
    sample = bufz is not None
    assert T % C == 0 and C % 8 == 0
    qkv_spec = pl.BlockSpec((C, 3 * GDN_WIDTH), lambda b, c: (b * n + c, 0))
    if sample:
        second = pl.BlockSpec((C, 3 * GDN_WIDTH), lambda b, c: (b * n + c, 0))
        second_arr = bufz
    else:
        second = pl.BlockSpec((8, 3 * GDN_WIDTH), lambda b, c: (jnp.maximum((b * n + c) * (C // 8) - 1, 0), 0))
        second_arr = P
    in_specs = [qkv_spec, second,
                pl.BlockSpec((C, HEAD_DIM), lambda b, c: (b * n + c, MISC0 // HEAD_DIM)),
                pl.BlockSpec((GDN_CONV, 3 * GDN_WIDTH), lambda b, c: (0, 0)),
                pl.BlockSpec((1, HEAD_DIM), lambda b, c: (0, 0)),
                pl.BlockSpec((1, HEAD_DIM), lambda b, c: (0, 0))]
    row_spec = pl.BlockSpec((C, GDN_WIDTH), lambda b, c: (b * n + c, 0))
    out_specs = [row_spec, row_spec, row_spec, row_spec,
                 pl.BlockSpec((C, GDN_HEADS * C), lambda b, c: (b * n + c, 0)),
                 pl.BlockSpec((None, GDN_HEADS, HEAD_DIM), lambda b, c: (b * n + c, 0, 0))]
    out_shape = [jax.ShapeDtypeStruct((M, GDN_WIDTH), F32),
                 jax.ShapeDtypeStruct((M, GDN_WIDTH), BF16),
                 jax.ShapeDtypeStruct((M, GDN_WIDTH), BF16),
                 jax.ShapeDtypeStruct((M, GDN_WIDTH), BF16),
                 jax.ShapeDtypeStruct((M, GDN_HEADS * C), BF16),
                 jax.ShapeDtypeStruct((n_seq * n, GDN_HEADS, HEAD_DIM), F32)]
    return pl.pallas_call(
        functools.partial(_gdn_prep_kernel, C=C, sample=sample),
        grid=(n_seq, n),
        in_specs=in_specs,
        out_specs=out_specs,
        out_shape=out_shape,
        compiler_params=_cparams("parallel", "arbitrary"),
        name="gdn_prep_sample" if sample else "gdn_prep_prompt",
    )(P, second_arr, P, cw, alog_row, dtb_row)


def _gdn_scan_kernel(u_ref, w_ref, qg_ref, kd_ref, qk_ref, gl_ref, z_ref, ng_ref, s0_ref, o_ref, sfin_ref, S_ref, *, C):
    c = pl.program_id(1)

    @pl.when(c == 0)
    def _():
        S_ref[...] = s0_ref[...]

    S = S_ref[...]
    Sb = S.astype(BF16)
    v_new = _head_stack(u_ref[...], 0) - _bdot(_head_stack(w_ref[...], 0), Sb)
    vb = v_new.astype(BF16)
    o = _bdot(_head_stack(qg_ref[...], 0), Sb) + _bdot(_head_stack(qk_ref[...], 0, C), vb)
    gl = jnp.stack([gl_ref[h:h + 1, :] for h in range(GDN_HEADS)], axis=0)
    S_ref[...] = S * gl + _bdot_tn(_head_stack(kd_ref[...], 0), vb)
    o = o * lax.rsqrt(jnp.mean(o * o, -1, keepdims=True) + NORM_EPS) * ng_ref[...]
    o = o * jax.nn.silu(_head_stack(z_ref[...], 0))
    for h in range(GDN_HEADS):
        o_ref[:, h * HEAD_DIM:(h + 1) * HEAD_DIM] = o[h].astype(o_ref.dtype)

    @pl.when(c == pl.num_programs(1) - 1)
    def _():
        sfin_ref[...] = S_ref[...]


def gdn_scan(prep, P, norm_g, s0, *, n_seq, C):
    u, w, qg, kd, qk, gl = prep
    M = u.shape[0]
    T = M // n_seq
    n = T // C
    row_spec = pl.BlockSpec((C, GDN_WIDTH), lambda b, c: (b * n + c, 0))
    st_spec = pl.BlockSpec((None, GDN_HEADS, HEAD_DIM, HEAD_DIM), lambda b, c: (b, 0, 0, 0))
    return pl.pallas_call(
        functools.partial(_gdn_scan_kernel, C=C),
        grid=(n_seq, n),
        in_specs=[row_spec, row_spec, row_spec, row_spec,
                  pl.BlockSpec((C, GDN_HEADS * C), lambda b, c: (b * n + c, 0)),
                  pl.BlockSpec((None, GDN_HEADS, HEAD_DIM), lambda b, c: (b * n + c, 0, 0)),
                  pl.BlockSpec((C, GDN_WIDTH), lambda b, c: (b * n + c, Z0 // GDN_WIDTH)),
                  pl.BlockSpec((1, HEAD_DIM), lambda b, c: (0, 0)),
                  st_spec],
        out_specs=[row_spec, st_spec],
        out_shape=[jax.ShapeDtypeStruct((M, GDN_WIDTH), BF16),
                   jax.ShapeDtypeStruct((n_seq, GDN_HEADS, HEAD_DIM, HEAD_DIM), F32)],
        scratch_shapes=[pltpu.VMEM((GDN_HEADS, HEAD_DIM, HEAD_DIM), F32)],
        compiler_params=_cparams("parallel", "arbitrary"),
        name="gdn_scan",
    )(u, w, qg, kd, qk, gl, P, norm_g, s0)


PAGES_PER_STEP = 32
PAGE_ROWS = 2 * NSA_BLOCK * 4


def _compress_sample_kernel(pt_ref, cache_ref, pe_ref, w1_ref, w2_ref, out_ref, xbuf, sem, *, layer, n_pages, pps):
    b = pl.program_id(0)
    gi = pl.program_id(1)
    ng = pl.num_programs(1)
    step = b * ng + gi
    nsteps = pl.num_programs(0) * ng
    slot = step % 2

    def copies(st, sl):
        bb = st // ng
        g0 = (st % ng) * pps
        out = []
        for jl in range(pps):
            page = pt_ref[bb * n_pages + g0 + jl]
            out.append(pltpu.make_async_copy(
                cache_ref.at[layer, page],
                xbuf.at[sl, pl.ds(jl * PAGE_ROWS, PAGE_ROWS)],
                sem.at[sl]))
        return out

    @pl.when(step == 0)
    def _():
        for cp in copies(step, slot):
            cp.start()

    @pl.when(step + 1 < nsteps)
    def _():
        for cp in copies(step + 1, 1 - slot):
            cp.start()

    for cp in copies(step, slot):
        cp.wait()

    nblk = 2 * pps

    def read_rows(gc, p):
        return xbuf[slot, pl.ds(p * 4 + gc, nblk, stride=NSA_BLOCK * 4), :]

    _compress_rows(read_rows, pe_ref, w1_ref, w2_ref, out_ref, nblk)


def compress_sample(page_table, cache_rows, pe_t, w1, w2, *, layer):
    Bs, n_pages = page_table.shape
    pps = min(PAGES_PER_STEP, n_pages)
    assert n_pages % pps == 0
    ng = n_pages // pps
    nblk = 2 * pps
    grid_spec = pltpu.PrefetchScalarGridSpec(
        num_scalar_prefetch=1,
        grid=(Bs, ng),
        in_specs=[pl.BlockSpec(memory_space=pl.ANY),
                  pl.BlockSpec(pe_t.shape, lambda b, g, pt: (0, 0, 0)),
                  pl.BlockSpec(w1.shape, lambda b, g, pt: (0, 0, 0)),
                  pl.BlockSpec(w2.shape, lambda b, g, pt: (0, 0, 0))],
        out_specs=pl.BlockSpec((None, None, 2, NSA_KV_HEADS * nblk, HEAD_DIM), lambda b, g, pt: (b, g, 0, 0, 0)),
        scratch_shapes=[pltpu.VMEM((2, pps * PAGE_ROWS, HEAD_DIM), F32),
                        pltpu.SemaphoreType.DMA((2,))],
    )
    return pl.pallas_call(
        functools.partial(_compress_sample_kernel, layer=layer, n_pages=n_pages, pps=pps),
        grid_spec=grid_spec,
        out_shape=jax.ShapeDtypeStruct((Bs, ng, 2, NSA_KV_HEADS * nblk, HEAD_DIM), F32),
        compiler_params=_cparams("arbitrary", "arbitrary"),
        name="compress_sample",
    )(page_table.reshape(-1), cache_rows, pe_t, w1, w2)


def _stack_heads(q_ref, g, scale):
    return jnp.concatenate(
        [q_ref[:, (g * NSA_GROUP + r) * HEAD_DIM:(g * NSA_GROUP + r + 1) * HEAD_DIM] * scale for r in range(NSA_GROUP)],
        axis=0).astype(BF16)


def _stacked_row_info(past):
    rows = NSA_GROUP * ZROWS
    ridx = lax.broadcasted_iota(jnp.int32, (rows, 1), 0)
    zrow = ridx % ZROWS
    qpos = past + zrow - ZTOK0
    slope = jnp.zeros((rows, 1), F32)
    return ridx, zrow, qpos, slope


def _slope_col(g):
    ridx = lax.broadcasted_iota(jnp.int32, (NSA_GROUP * ZROWS, 1), 0)
    sl = jnp.zeros((NSA_GROUP * ZROWS, 1), F32)
    for r in range(NSA_GROUP):
        sl = jnp.where(ridx // ZROWS == r, SLOPES[g][r], sl)
    return sl


def _nsa_sample_a_kernel(q_ref, kvw_ref, ck_ref, cv_ref, win_ref, ocmp_ref, owin_ref, sel_ref, wout_ref, *, past, nc, n_blocks, seq):
    scale = HEAD_DIM ** -0.5
    _, zrow, qpos, _ = _stacked_row_info(past)
    wb = win_ref.shape[0] // 4
    lanes_pad = sel_ref.shape[-1]
    sel_lanes = ((n_blocks + HEAD_DIM - 1) // HEAD_DIM) * HEAD_DIM
    top_n = min(NSA_TOP_N, n_blocks)
    for g in range(NSA_KV_HEADS):
        q4 = _stack_heads(q_ref, g, scale)
        slope = _slope_col(g)
        cpos = lax.broadcasted_iota(jnp.int32, (1, nc), 1) * NSA_BLOCK + (NSA_BLOCK - 1)
        vm = cpos <= qpos
        s = jnp.where(vm, _dot_nt(q4, ck_ref[g]) + slope * (cpos - past).astype(F32), NEG)
        e = jnp.exp(s - jnp.max(s, axis=-1, keepdims=True))
        p = jnp.where(vm, e / jnp.sum(e, axis=-1, keepdims=True), 0.0)
        ocmp_ref[g] = _dot(p, cv_ref[g])
        imp = p[0:ZROWS]
        for r in range(1, NSA_GROUP):
            imp = imp + p[r * ZROWS:(r + 1) * ZROWS]
        imp = jnp.concatenate([imp, jnp.zeros((ZROWS, sel_lanes - nc), F32)], axis=1)
        rank = _rank_desc(_block_scores(imp, qpos[0:ZROWS], n_blocks), n_blocks)
        lane = lax.broadcasted_iota(jnp.int32, rank.shape, 1)
        olane = lax.broadcasted_iota(jnp.int32, (ZROWS, lanes_pad), 1)
        out = jnp.zeros((ZROWS, lanes_pad), jnp.int32)
        for t in range(top_n):
            idx = jnp.sum(jnp.where(rank == float(t), lane.astype(F32), 0.0), axis=-1, keepdims=True)
            out = jnp.where(olane == t, idx.astype(jnp.int32), out)
        sel_ref[g] = out
        kold = win_ref[pl.ds(g * 2, wb, stride=4), :]
        vold = win_ref[pl.ds(g * 2 + 1, wb, stride=4), :]
        knew = kvw_ref[:, g * 2 * HEAD_DIM:(g * 2 + 1) * HEAD_DIM]
        vnew = kvw_ref[:, (g * 2 + 1) * HEAD_DIM:(g * 2 + 2) * HEAD_DIM]
        kpos_o = past - wb + lax.broadcasted_iota(jnp.int32, (1, wb), 1)
        kz = lax.broadcasted_iota(jnp.int32, (1, ZROWS), 1)
        kpos_n = past + kz - ZTOK0
        d_o = qpos - kpos_o
        d_n = qpos - kpos_n
        ok_o = (kpos_o >= 0) & (d_o >= 0) & (d_o < NSA_WINDOW)
        ok_n = (kz >= ZTOK0) & (kz < ZTOK0 + seq) & (d_n >= 0) & (d_n < NSA_WINDOW)
        s_o = jnp.where(ok_o, _dot_nt(q4, kold) + slope * (kpos_o - past).astype(F32), NEG)
        s_n = jnp.where(ok_n, _dot_nt(q4, knew) + slope * (kpos_n - past).astype(F32), NEG)
        m = jnp.maximum(jnp.max(s_o, axis=-1, keepdims=True), jnp.max(s_n, axis=-1, keepdims=True))
        p_o = jnp.exp(s_o - m)
        p_n = jnp.exp(s_n - m)
        den = jnp.sum(p_o, axis=-1, keepdims=True) + jnp.sum(p_n, axis=-1, keepdims=True)
        owin_ref[g] = (_dot(p_o, vold) + _dot(p_n, vnew)) / den
    keep = (wb - seq) * 4
    wout_ref[0:keep, :] = win_ref[seq * 4:wb * 4, :]
    ridx = lax.broadcasted_iota(jnp.int32, (seq * 4, 1), 0)
    new_rows = jnp.zeros((seq * 4, HEAD_DIM), F32)
    for t in range(seq):
        for gc in range(4):
            new_rows = jnp.where(ridx == t * 4 + gc,
                                 kvw_ref[ZTOK0 + t:ZTOK0 + t + 1, gc * HEAD_DIM:(gc + 1) * HEAD_DIM], new_rows)
    wout_ref[keep:wb * 4, :] = new_rows


def nsa_sample_a(Ps, ckv, cache_win_rows, *, layer, past, seq):
    Bs = Ps.shape[0] // ZROWS
    nc = ckv.shape[3]
    wrows = cache_win_rows.shape[2]
    n_blocks = -(-(past + seq) // NSA_BLOCK)
    assert 1 <= seq <= 4 and (seq * 4) % 8 == 0 and wrows % 8 == 0
    rows = NSA_GROUP * ZROWS
    big = pl.BlockSpec((None, NSA_KV_HEADS, rows, HEAD_DIM), lambda b: (b, 0, 0, 0))
    return pl.pallas_call(
        functools.partial(_nsa_sample_a_kernel, past=past, nc=nc, n_blocks=n_blocks, seq=seq),
        grid=(Bs,),
        in_specs=[pl.BlockSpec((ZROWS, NSA_WIDTH), lambda b: (b, Q0 // NSA_WIDTH)),
                  pl.BlockSpec((ZROWS, KV_COLS), lambda b: (b, KVW0 // KV_COLS)),
                  pl.BlockSpec((None, None, NSA_KV_HEADS, nc, HEAD_DIM), lambda b: (b, 0, 0, 0, 0)),
                  pl.BlockSpec((None, None, NSA_KV_HEADS, nc, HEAD_DIM), lambda b: (b, 1, 0, 0, 0)),
                  pl.BlockSpec((None, None, wrows, HEAD_DIM), lambda b: (layer, b, 0, 0))],
        out_specs=[big, big,
                   pl.BlockSpec((None, NSA_KV_HEADS, ZROWS, HEAD_DIM), lambda b: (b, 0, 0, 0)),
                   pl.BlockSpec((None, wrows, HEAD_DIM), lambda b: (b, 0, 0))],
        out_shape=[jax.ShapeDtypeStruct((Bs, NSA_KV_HEADS, rows, HEAD_DIM), F32),
                   jax.ShapeDtypeStruct((Bs, NSA_KV_HEADS, rows, HEAD_DIM), F32),
                   jax.ShapeDtypeStruct((Bs, NSA_KV_HEADS, ZROWS, HEAD_DIM), jnp.int32),
                   jax.ShapeDtypeStruct((Bs, wrows, HEAD_DIM), F32)],
        compiler_params=_cparams("parallel"),
        name="nsa_sample_a",
    )(Ps, Ps, ckv, ckv, cache_win_rows)


def _nsa_sample_b_kernel(pt_ref, sel_ref, q_ref, kvs_ref, misc_ref, ocmp_ref, owin_ref, *rest, past, seq, n_past_blocks):
    blk_refs = rest[:seq]
    o_ref, m_ref, l_ref, acc_ref = rest[seq:]
    b = pl.program_id(0)
    g = pl.program_id(1)
    j = pl.program_id(2)
    nj = pl.num_programs(2)
    scale = HEAD_DIM ** -0.5
    rows = NSA_GROUP * ZROWS
    ridx, zrow, qpos, _ = _stacked_row_info(past)
    q4 = jnp.where(g == 0, _stack_heads(q_ref, 0, scale), _stack_heads(q_ref, 1, scale))
    slope = jnp.where(g == 0, _slope_col(0), _slope_col(1))

    @pl.when(j == 0)
    def _():
        kn = jnp.where(g == 0, kvs_ref[:, 0:HEAD_DIM], kvs_ref[:, 2 * HEAD_DIM:3 * HEAD_DIM])
        vn = jnp.where(g == 0, kvs_ref[:, HEAD_DIM:2 * HEAD_DIM], kvs_ref[:, 3 * HEAD_DIM:4 * HEAD_DIM])
        kz = lax.broadcasted_iota(jnp.int32, (1, ZROWS), 1)
        kpos = past + kz - ZTOK0
        ok = (kz >= ZTOK0) & (kz < ZTOK0 + seq) & (kpos <= qpos)
        s = jnp.where(ok, _dot_nt(q4, kn) + slope * (kpos - past).astype(F32), NEG)
        m = jnp.max(s, axis=-1, keepdims=True)
        p = jnp.exp(s - m)
        m_ref[...] = m
        l_ref[...] = jnp.sum(p, axis=-1, keepdims=True)
        acc_ref[...] = _dot(p, vn)

    lane = lax.broadcasted_iota(jnp.int32, (1, NSA_BLOCK), 1)
    s = jnp.full((rows, NSA_BLOCK), NEG, F32)
    vsel = []
    for t in range(seq):
        blk = sel_ref[((b * seq + t) * NSA_KV_HEADS + g) * NSA_TOP_N + j]
        k = blk_refs[t][pl.ds(g * 2, NSA_BLOCK, stride=4), :]
        kpos = blk * NSA_BLOCK + lane
        st = _dot_nt(q4, k) + slope * (kpos - past).astype(F32)
        mine = (zrow == ZTOK0 + t) & (blk < n_past_blocks) & (kpos <= qpos)
        s = jnp.where(mine, st, s)
        vsel.append(blk_refs[t][pl.ds(g * 2 + 1, NSA_BLOCK, stride=4), :])
    m_old = m_ref[...]
    m_new = jnp.maximum(m_old, jnp.max(s, axis=-1, keepdims=True))
    p = jnp.exp(s - m_new)
    alpha = jnp.exp(m_old - m_new)
    pv = jnp.zeros((rows, HEAD_DIM), F32)
    for t in range(seq):
        pv = pv + jnp.where(zrow == ZTOK0 + t, _dot(p, vsel[t]), 0.0)
    l_ref[...] = alpha * l_ref[...] + jnp.sum(p, axis=-1, keepdims=True)
    acc_ref[...] = alpha * acc_ref[...] + pv
    m_ref[...] = m_new

    @pl.when(j == nj - 1)
    def _():
        o_slc = acc_ref[...] / l_ref[...]
        gates = jax.nn.sigmoid(misc_ref[...])
        o_cmp = ocmp_ref[...]
        o_win = owin_ref[...]
        for r in range(NSA_GROUP):
            rs = slice(r * ZROWS, (r + 1) * ZROWS)
            outs = []
            for gg in range(NSA_KV_HEADS):
                c = gg * NSA_GROUP * 3 + r * 3 + GATE_LANE
                outs.append(gates[:, c:c + 1] * o_cmp[rs] + gates[:, c + 1:c + 2] * o_slc[rs]
                            + gates[:, c + 2:c + 3] * o_win[rs])
            o_ref[:, r * HEAD_DIM:(r + 1) * HEAD_DIM] = jnp.where(g == 0, outs[0], outs[1]).astype(o_ref.dtype)


def nsa_sample_b(page_table, sel, Ps, ocmp, owin, cache_rows, *, layer, past, seq):
    Bs, n_pages = page_table.shape
    rows = NSA_GROUP * ZROWS
    n_past_blocks = past // NSA_BLOCK
    gw = NSA_GROUP * HEAD_DIM

    def blk_map(t):
        def f(b, g, j, pt, sl):
            blk = jnp.minimum(sl[((b * seq + t) * NSA_KV_HEADS + g) * NSA_TOP_N + j], n_past_blocks - 1)
            return (layer, pt[b * n_pages + blk // 2], blk % 2, 0)
        return f

    big = pl.BlockSpec((None, None, rows, HEAD_DIM), lambda b, g, j, pt, sl: (b, g, 0, 0))
    in_specs = [pl.BlockSpec((ZROWS, NSA_WIDTH), lambda b, g, j, pt, sl: (b, Q0 // NSA_WIDTH)),
                pl.BlockSpec((ZROWS, KV_COLS), lambda b, g, j, pt, sl: (b, KVS0 // KV_COLS)),
                pl.BlockSpec((ZROWS, HEAD_DIM), lambda b, g, j, pt, sl: (b, MISC0 // HEAD_DIM)),
                big, big]
    in_specs += [pl.BlockSpec((None, None, PAGE_ROWS // 2, HEAD_DIM), blk_map(t)) for t in range(seq)]
    grid_spec = pltpu.PrefetchScalarGridSpec(
        num_scalar_prefetch=2,
        grid=(Bs, NSA_KV_HEADS, NSA_TOP_N),
        in_specs=in_specs,
        out_specs=pl.BlockSpec((ZROWS, gw), lambda b, g, j, pt, sl: (b, g)),
        scratch_shapes=[pltpu.VMEM((rows, 1), F32), pltpu.VMEM((rows, 1), F32), pltpu.VMEM((rows, HEAD_DIM), F32)],
    )
    return pl.pallas_call(
        functools.partial(_nsa_sample_b_kernel, past=past, seq=seq, n_past_blocks=n_past_blocks),
        grid_spec=grid_spec,
        out_shape=jax.ShapeDtypeStruct((Bs * ZROWS, NSA_WIDTH), BF16),
        compiler_params=_cparams("parallel", "arbitrary", "arbitrary"),
        name="nsa_sample_b",
    )(page_table.reshape(-1), sel, Ps, Ps, Ps, ocmp, owin, *([cache_rows] * seq))


def _reorder_w_in(w_in):
    q, kvc, kvs, kvw = w_in[..., 0:1024], w_in[..., 1024:1536], w_in[..., 1536:2048], w_in[..., 2048:2560]
    gates, qkv, z, ab = w_in[..., 2560:2584], w_in[..., 2584:5656], w_in[..., 5656:6680], w_in[..., 6680:6696]
    pad = jnp.zeros(w_in.shape[:-1] + (IN_COLS_PAD - MISC0 - 40,), w_in.dtype)
    return jnp.concatenate([qkv, q, z, kvc, kvs, kvw, gates, ab, pad], axis=-1).astype(BF16)


def _lane_row(vec, lane0):
    return jnp.zeros((1, HEAD_DIM), F32).at[0, lane0:lane0 + vec.shape[0]].set(vec.astype(F32))


def _to_zrows(x, first_row):
    Bs, n, C = x.shape
    z = jnp.zeros((Bs, ZROWS, C), x.dtype).at[:, first_row:first_row + n].set(x)
    return z.reshape(Bs * ZROWS, C)


def kernel(x_prompt, x_sample, cache_cmp_kv, cache_slc_kv, cache_win_kv, state_gdn, state_gdn_conv, state_ffn_conv, page_table, w_in, w_o, cmp_pe, cmp_w1, cmp_w2, gdn_conv_w, gdn_a_log, gdn_dt_bias, gdn_norm_g, ln1_g, ln1_b, ffn_w_up, ffn_conv_w, ffn_w_down, ln2_g, ln2_b):
    B, S, D = x_prompt.shape
    Bs, seq, _ = x_sample.shape
    depth = w_in.shape[0]
    n_pool, page = cache_cmp_kv.shape[1], cache_cmp_kv.shape[2]
    n_pages = page_table.shape[1]
    past = n_pages * page
    wb = cache_win_kv.shape[2]
    d_ff = ffn_w_down.shape[1]
    alpha = (2 * depth) ** 0.25
    assert page == 2 * NSA_BLOCK

    w_in_r = _reorder_w_in(w_in)
    w_o_b = w_o.astype(BF16)
    w_up_b = ffn_w_up.astype(BF16)
    w_down_b = ffn_w_down.astype(BF16)
    w1_b = cmp_w1.astype(BF16)
    w2_b = cmp_w2.astype(BF16)
    pe_t = jnp.swapaxes(cmp_pe, 1, 2)
    cache_cmp_rows = cache_cmp_kv.reshape(depth, n_pool, PAGE_ROWS, HEAD_DIM)
    cache_slc_rows = cache_slc_kv.reshape(depth, n_pool, PAGE_ROWS, HEAD_DIM)
    cache_win_rows = cache_win_kv.reshape(depth, Bs, wb * 4, HEAD_DIM)

    xp = x_prompt.reshape(B * S, D)
    xs = _to_zrows(x_sample, ZTOK0)
    tm = 512 if (B * S) % 512 == 0 else B * S
    tq = 128
    p_out = [[] for _ in range(6)]
    s_out = [[] for _ in range(6)]
    zero_state = jnp.zeros((B, GDN_HEADS, HEAD_DIM, HEAD_DIM), F32)

    for l in range(depth):
        alog_row = _lane_row(gdn_a_log[l], A_LANE)
        dtb_row = _lane_row(gdn_dt_bias[l], A_LANE)
        ng = gdn_norm_g[l].reshape(1, HEAD_DIM)
        g1, b1 = ln1_g[l].reshape(1, D), ln1_b[l].reshape(1, D)
        g2, b2 = ln2_g[l].reshape(1, D), ln2_b[l].reshape(1, D)

        P = matmul(xp, w_in_r[l], tm=tm, tn=1024)
        pkv = P[:, KVS0:KVS0 + 2 * KV_COLS].astype(BF16).reshape(B, S, 4, 2, HEAD_DIM)
        kmat = pkv[:, :, :, 0, :].reshape(B * S, 4 * HEAD_DIM)
        vt = jnp.transpose(pkv[:, :, :, 1, :], (0, 2, 3, 1))
        ckv = compress_prompt(P, pe_t[l], w1_b[l], w2_b[l], n_seq=B)
        ckv = ckv.reshape(B, 2, NSA_KV_HEADS, S // NSA_BLOCK, HEAD_DIM).astype(BF16)
        o_nsa = nsa_prompt(P, kmat, vt, ckv[:, 0], jnp.swapaxes(ckv[:, 1], -1, -2), n_seq=B, tq=tq)
        prep = gdn_prep(P, gdn_conv_w[l], alog_row, dtb_row, n_seq=B, C=GDN_CHUNK)
        o_gdn, s_fin = gdn_scan(prep, P, ng, zero_state, n_seq=B, C=GDN_CHUNK)
        h = proj_ln(o_nsa, o_gdn, w_o_b[l], xp, g1, b1, alpha=alpha, tm=256)
        act, utail = ffn_up_prompt(h, w_up_b[l], ffn_conv_w[l], n_seq=B, tm=tm, tf=512)
        xp = ffn_down_ln(act, w_down_b[l], h, g2, b2, alpha=alpha, tm=tm, tk=512)

        P3 = P.reshape(B, S, IN_COLS_PAD)
        p_out[0].append(P3[:, :, KVC0:KVC0 + KV_COLS].reshape(B, S, NSA_KV_HEADS, 2, HEAD_DIM))
        p_out[1].append(P3[:, :, KVS0:KVS0 + KV_COLS].reshape(B, S, NSA_KV_HEADS, 2, HEAD_DIM))
        wn = min(NSA_WINDOW, S)
        p_out[2].append(P3[:, S - wn:, KVW0:KVW0 + KV_COLS].reshape(B, wn, NSA_KV_HEADS, 2, HEAD_DIM))
        p_out[3].append(s_fin)
        p_out[4].append(P3[:, S - (GDN_CONV - 1):, QKV0:QKV0 + 3 * GDN_WIDTH])
        p_out[5].append(utail[:, 8 - (FFN_CONV - 1):, :])

        Ps = matmul(xs, w_in_r[l], tm=Bs * ZROWS, tn=1024)
        ckv_s = compress_sample(page_table, cache_cmp_rows, pe_t[l], w1_b[l], w2_b[l], layer=l)
        ng_grp = ckv_s.shape[1]
        nblk_step = ckv_s.shape[3] // NSA_KV_HEADS
        ckv_s = ckv_s.reshape(Bs, ng_grp, 2, NSA_KV_HEADS, nblk_step, HEAD_DIM)
        ckv_s = jnp.transpose(ckv_s, (0, 2, 3, 1, 4, 5)).reshape(Bs, 2, NSA_KV_HEADS, ng_grp * nblk_step, HEAD_DIM)
        ocmp, owin, sel, win_new = nsa_sample_a(Ps, ckv_s, cache_win_rows, layer=l, past=past, seq=seq)
        sel_flat = jnp.transpose(sel[:, :, ZTOK0:ZTOK0 + seq, :NSA_TOP_N], (0, 2, 1, 3)).reshape(-1)
        o_nsa_s = nsa_sample_b(page_table, sel_flat, Ps, ocmp, owin, cache_slc_rows, layer=l, past=past, seq=seq)
        gbufz = _to_zrows(state_gdn_conv[l], ZTOK0 - (GDN_CONV - 1))
        prep_s = gdn_prep(Ps, gdn_conv_w[l], alog_row, dtb_row, n_seq=Bs, C=ZROWS, bufz=gbufz)
        o_gdn_s, s_fin_s = gdn_scan(prep_s, Ps, ng, state_gdn[l], n_seq=Bs, C=ZROWS)
        hs = proj_ln(o_nsa_s, o_gdn_s, w_o_b[l], xs, g1, b1, alpha=alpha, tm=Bs * ZROWS)
        fbufz = _to_zrows(state_ffn_conv[l], ZTOK0 - (FFN_CONV - 1))
        act_s, u_s = ffn_up_sample(hs, w_up_b[l], ffn_conv_w[l], fbufz, tf=512)
        xs = ffn_down_ln(act_s, w_down_b[l], hs, g2, b2, alpha=alpha, tm=Bs * ZROWS, tk=512)

        Ps3 = Ps.reshape(Bs, ZROWS, IN_COLS_PAD)[:, ZTOK0:ZTOK0 + seq]
        s_out[0].append(Ps3[:, :, KVC0:KVC0 + KV_COLS].reshape(Bs, seq, NSA_KV_HEADS, 2, HEAD_DIM))
        s_out[1].append(Ps3[:, :, KVS0:KVS0 + KV_COLS].reshape(Bs, seq, NSA_KV_HEADS, 2, HEAD_DIM))
        s_out[2].append(win_new.reshape(Bs, wb, NSA_KV_HEADS, 2, HEAD_DIM))
        s_out[3].append(s_fin_s)
        ext_g = jnp.concatenate([state_gdn_conv[l], Ps3[:, :, QKV0:QKV0 + 3 * GDN_WIDTH]], axis=1)
        s_out[4].append(ext_g[:, -(GDN_CONV - 1):])
        u3 = u_s.reshape(Bs, ZROWS, d_ff)[:, ZTOK0:ZTOK0 + seq]
        ext_f = jnp.concatenate([state_ffn_conv[l], u3], axis=1)
        s_out[5].append(ext_f[:, -(FFN_CONV - 1):])

    y_p = xp.reshape(B, S, D)
    y_s = xs.reshape(Bs, ZROWS, D)[:, ZTOK0:ZTOK0 + seq]
    return (y_p, y_s, *[jnp.stack(v, 0) for v in p_out], *[jnp.stack(v, 0) for v in s_out])
```

```python
import functools

import jax
import jax.numpy as jnp
import numpy as np
from jax import lax
from jax.experimental import pallas as pl
from jax.experimental.pallas import tpu as pltpu

F32 = jnp.float32
BF16 = jnp.bfloat16

HEAD_DIM = 128
NSA_HEADS = 8
NSA_KV_HEADS = 2
NSA_GROUP = 4
NSA_WIDTH = 1024
NSA_BLOCK = 64
NSA_TOP_N = 16
NSA_WINDOW = 512
GDN_HEADS = 8
GDN_WIDTH = 1024
GDN_CONV = 4
GDN_CHUNK = 64
FFN_CONV = 3
KV_COLS = 512
LN_EPS = 1e-5
NORM_EPS = 1e-6
NEG = -1e30
FORCE = 1e4
LOG2E = 1.4426950408889634
SLOPES = [[2.0 ** -(g * NSA_GROUP + r + 1) for r in range(NSA_GROUP)] for g in range(NSA_KV_HEADS)]

QKV0, Q0, Z0, KVC0, KVS0, KVW0, MISC0, IN_COLS_PAD = 0, 3072, 4096, 5120, 5632, 6144, 6656, 7168
GATE_LANE, A_LANE, B_LANE = 0, 24, 32

ZROWS = 16
ZTOK0 = 8

VMEM_LIMIT_BYTES = 56 * 1024 * 1024


def _cparams(*sem):
    return pltpu.CompilerParams(dimension_semantics=sem, vmem_limit_bytes=VMEM_LIMIT_BYTES)


def _dot(a, b):
    return jnp.dot(a.astype(BF16), b.astype(BF16), preferred_element_type=F32)


def _dot_nt(a, b):
    return lax.dot_general(a.astype(BF16), b.astype(BF16), (((1,), (1,)), ((), ())), preferred_element_type=F32)


def _dot_tn(a, b):
    return lax.dot_general(a.astype(BF16), b.astype(BF16), (((0,), (0,)), ((), ())), preferred_element_type=F32)


def _layer_norm(t, g, b):
    mu = jnp.mean(t, -1, keepdims=True)
    d = t - mu
    var = jnp.mean(d * d, -1, keepdims=True)
    return d * lax.rsqrt(var + LN_EPS) * g + b


def _mm_kernel(x_ref, w_ref, o_ref):
    o_ref[...] = _dot(x_ref[...], w_ref[...])


def matmul(x, w, *, tm, tn):
    M, K = x.shape
    _, N = w.shape
    assert M % tm == 0 and N % tn == 0
    return pl.pallas_call(
        _mm_kernel,
        grid=(M // tm, N // tn),
        in_specs=[pl.BlockSpec((tm, K), lambda i, j: (i, 0)),
                  pl.BlockSpec((K, tn), lambda i, j: (0, j))],
        out_specs=pl.BlockSpec((tm, tn), lambda i, j: (i, j)),
        out_shape=jax.ShapeDtypeStruct((M, N), F32),
        compiler_params=_cparams("parallel", "arbitrary"),
        name="in_proj",
    )(x, w)


def _proj_ln_kernel(a1_ref, a2_ref, w_ref, x_ref, g_ref, b_ref, o_ref, ob_ref, *, alpha):
    k1 = a1_ref.shape[1]
    acc = _dot(a1_ref[...], w_ref[:k1, :]) + _dot(a2_ref[...], w_ref[k1:, :])
    o = _layer_norm(alpha * x_ref[...] + acc, g_ref[...], b_ref[...])
    o_ref[...] = o
    ob_ref[...] = o.astype(ob_ref.dtype)


def proj_ln(a1, a2, w, x, g, b, *, alpha, tm):
    M, D = x.shape
    k1, k2 = a1.shape[1], a2.shape[1]
    assert M % tm == 0 and w.shape == (k1 + k2, D)
    return pl.pallas_call(
        functools.partial(_proj_ln_kernel, alpha=alpha),
        grid=(M // tm,),
        in_specs=[pl.BlockSpec((tm, k1), lambda i: (i, 0)),
                  pl.BlockSpec((tm, k2), lambda i: (i, 0)),
                  pl.BlockSpec((k1 + k2, D), lambda i: (0, 0)),
                  pl.BlockSpec((tm, D), lambda i: (i, 0)),
                  pl.BlockSpec((1, D), lambda i: (0, 0)),
                  pl.BlockSpec((1, D), lambda i: (0, 0))],
        out_specs=[pl.BlockSpec((tm, D), lambda i: (i, 0)), pl.BlockSpec((tm, D), lambda i: (i, 0))],
        out_shape=[jax.ShapeDtypeStruct((M, D), F32), jax.ShapeDtypeStruct((M, D), BF16)],
        compiler_params=_cparams("parallel"),
        name="out_proj_ln",
    )(a1, a2, w, x, g, b)


def _conv3(u, prev8, cw):
    ue = jnp.concatenate([prev8, u], axis=0)
    u1 = pltpu.roll(ue, 1, 0)[8:]
    u2 = pltpu.roll(ue, 2, 0)[8:]
    return cw[2:3, :] * u + cw[1:2, :] * u1 + cw[0:1, :] * u2


def _ffn_up_prompt_kernel(h_ref, wu_ref, wv_ref, cw_ref, act_ref, tail_ref, carry_ref, *, tiles_per_seq):
    i = pl.program_id(1)
    h = h_ref[...]
    u = _dot(h, wu_ref[...])
    v = _dot(h, wv_ref[...])

    @pl.when(i % tiles_per_seq == 0)
    def _():
        carry_ref[...] = jnp.zeros_like(carry_ref)

    uc = _conv3(u, carry_ref[...], cw_ref[...])
    act_ref[...] = (jax.nn.gelu(uc) * v).astype(act_ref.dtype)
    tail = u[u.shape[0] - 8:, :]
    carry_ref[...] = tail
    tail_ref[...] = tail


def ffn_up_prompt(h, w_up, cw, *, n_seq, tm, tf):
    M, D = h.shape
    F = w_up.shape[1] // 2
    T = M // n_seq
    assert M % tm == 0 and F % tf == 0 and T % tm == 0
    nf = F // tf
    tps = T // tm
    return pl.pallas_call(
        functools.partial(_ffn_up_prompt_kernel, tiles_per_seq=tps),
        grid=(nf, M // tm),
        in_specs=[pl.BlockSpec((tm, D), lambda f, i: (i, 0)),
                  pl.BlockSpec((D, tf), lambda f, i: (0, f)),
                  pl.BlockSpec((D, tf), lambda f, i: (0, nf + f)),
                  pl.BlockSpec((FFN_CONV, tf), lambda f, i: (0, f))],
        out_specs=[pl.BlockSpec((tm, tf), lambda f, i: (i, f)),
                   pl.BlockSpec((None, 8, tf), lambda f, i: (i // tps, 0, f))],
        out_shape=[jax.ShapeDtypeStruct((M, F), BF16),
                   jax.ShapeDtypeStruct((n_seq, 8, F), F32)],
        scratch_shapes=[pltpu.VMEM((8, tf), F32)],
        compiler_params=_cparams("arbitrary", "arbitrary"),
        name="ffn_up_prompt",
    )(h, w_up, w_up, cw)


def _ffn_up_sample_kernel(h_ref, wu_ref, wv_ref, cw_ref, buf_ref, act_ref, u_ref):
    h = h_ref[...]
    u = _dot(h, wu_ref[...])
    v = _dot(h, wv_ref[...])
    zrow = lax.broadcasted_iota(jnp.int32, u.shape, 0) % ZROWS
    isbuf = (zrow >= ZTOK0 - (FFN_CONV - 1)) & (zrow < ZTOK0)
    uz = jnp.where(isbuf, buf_ref[...], u)
    cw = cw_ref[...]
    uc = cw[2:3, :] * uz + cw[1:2, :] * pltpu.roll(uz, 1, 0) + cw[0:1, :] * pltpu.roll(uz, 2, 0)
    act_ref[...] = (jax.nn.gelu(uc) * v).astype(act_ref.dtype)
    u_ref[...] = u


def ffn_up_sample(h, w_up, cw, bufz, *, tf):
    M, D = h.shape
    F = w_up.shape[1] // 2
    nf = F // tf
    return pl.pallas_call(
        _ffn_up_sample_kernel,
        grid=(nf,),
        in_specs=[pl.BlockSpec((M, D), lambda f: (0, 0)),
                  pl.BlockSpec((D, tf), lambda f: (0, f)),
                  pl.BlockSpec((D, tf), lambda f: (0, nf + f)),
                  pl.BlockSpec((FFN_CONV, tf), lambda f: (0, f)),
                  pl.BlockSpec((M, tf), lambda f: (0, f))],
        out_specs=[pl.BlockSpec((M, tf), lambda f: (0, f)),
                   pl.BlockSpec((M, tf), lambda f: (0, f))],
        out_shape=[jax.ShapeDtypeStruct((M, F), BF16),
                   jax.ShapeDtypeStruct((M, F), F32)],
        compiler_params=_cparams("parallel"),
        name="ffn_up_sample",
    )(h, w_up, w_up, cw, bufz)


def _ffn_down_ln_kernel(a_ref, w_ref, h_ref, g_ref, b_ref, o_ref, ob_ref, acc_ref, *, alpha):
    k = pl.program_id(1)

    @pl.when(k == 0)
    def _():
        acc_ref[...] = jnp.zeros_like(acc_ref)

    acc_ref[...] += _dot(a_ref[...], w_ref[...])

    @pl.when(k == pl.num_programs(1) - 1)
    def _():
        o = _layer_norm(alpha * h_ref[...] + acc_ref[...], g_ref[...], b_ref[...])
        o_ref[...] = o
        ob_ref[...] = o.astype(ob_ref.dtype)


def ffn_down_ln(a, w, h, g, b, *, alpha, tm, tk):
    M, Fd = a.shape
    D = w.shape[1]
    assert M % tm == 0 and Fd % tk == 0
    return pl.pallas_call(
        functools.partial(_ffn_down_ln_kernel, alpha=alpha),
        grid=(M // tm, Fd // tk),
        in_specs=[pl.BlockSpec((tm, tk), lambda i, k: (i, k)),
                  pl.BlockSpec((tk, D), lambda i, k: (k, 0)),
                  pl.BlockSpec((tm, D), lambda i, k: (i, 0)),
                  pl.BlockSpec((1, D), lambda i, k: (0, 0)),
                  pl.BlockSpec((1, D), lambda i, k: (0, 0))],
        out_specs=[pl.BlockSpec((tm, D), lambda i, k: (i, 0)), pl.BlockSpec((tm, D), lambda i, k: (i, 0))],
        out_shape=[jax.ShapeDtypeStruct((M, D), F32), jax.ShapeDtypeStruct((M, D), BF16)],
        scratch_shapes=[pltpu.VMEM((tm, D), F32)],
        compiler_params=_cparams("parallel", "arbitrary"),
        name="ffn_down_ln",
    )(a, w, h, g, b)


def _compress_rows(read_rows, pe_ref, w1_ref, w2_ref, out_ref, nblk):
    for c in range(2):
        def body(pp, acc):
            p0 = pp * 2
            parts = []
            for dp in range(2):
                pe_row = pe_ref[c, pl.ds(p0 + dp, 1), :]
                rows = [read_rows(g * 2 + c, p0 + dp) + pe_row for g in range(NSA_KV_HEADS)]
                parts.append(jnp.concatenate(rows, axis=0))
            lhs = jnp.concatenate(parts, axis=1)
            w = w1_ref[c, pl.ds(pl.multiple_of(p0 * HEAD_DIM, 2 * HEAD_DIM), 2 * HEAD_DIM), :]
            return acc + _dot(lhs, w)

        acc = lax.fori_loop(0, NSA_BLOCK // 2, body, jnp.zeros((NSA_KV_HEADS * nblk, HEAD_DIM), F32))
        out_ref[c] = _dot(jax.nn.silu(acc), w2_ref[c])


def _compress_prompt_kernel(x0, x1, x2, x3, pe_ref, w1_ref, w2_ref, out_ref, *, nblk):
    xs = [x0, x1, x2, x3]
    _compress_rows(lambda gc, p: xs[gc][:, p, :], pe_ref, w1_ref, w2_ref, out_ref, nblk)


def compress_prompt(P, pe_t, w1, w2, *, n_seq):
    M = P.shape[0]
    T = M // n_seq
    nblk = T // NSA_BLOCK
    P3 = P.reshape(n_seq * nblk, NSA_BLOCK, IN_COLS_PAD)
    cb0 = KVC0 // HEAD_DIM
    in_specs = [pl.BlockSpec((nblk, NSA_BLOCK, HEAD_DIM), (lambda b, gc=gc: (b, 0, cb0 + gc))) for gc in range(4)]
    in_specs += [pl.BlockSpec(pe_t.shape, lambda b: (0, 0, 0)),
                 pl.BlockSpec(w1.shape, lambda b: (0, 0, 0)),
                 pl.BlockSpec(w2.shape, lambda b: (0, 0, 0))]
    return pl.pallas_call(
        functools.partial(_compress_prompt_kernel, nblk=nblk),
        grid=(n_seq,),
        in_specs=in_specs,
        out_specs=pl.BlockSpec((None, 2, NSA_KV_HEADS * nblk, HEAD_DIM), lambda b: (b, 0, 0, 0)),
        out_shape=jax.ShapeDtypeStruct((n_seq, 2, NSA_KV_HEADS * nblk, HEAD_DIM), F32),
        compiler_params=_cparams("parallel"),
        name="compress_prompt",
    )(P3, P3, P3, P3, pe_t, w1, w2)


def _rank_desc(score, n_real):
    lane = lax.broadcasted_iota(jnp.int32, score.shape, 1)
    cnt = jnp.zeros(score.shape, F32)
    for i in range(n_real):
        col = score[:, i:i + 1]
        ge = jnp.where(col >= score, 1.0, 0.0)
        gt = jnp.where(col > score, 1.0, 0.0)
        cnt = cnt + jnp.where(lane > i, ge, gt)
    return cnt


def _block_scores(imp, qpos, n_blocks):
    bidx = lax.broadcasted_iota(jnp.int32, imp.shape, 1)
    cur = qpos // NSA_BLOCK
    sc = jnp.where(bidx == 0, FORCE, jnp.where(bidx == cur, FORCE, jnp.where(bidx == cur - 1, FORCE, imp)))
    sc = jnp.where(bidx * NSA_BLOCK <= qpos, sc, NEG)
    return jnp.where(bidx < n_blocks, sc, -jnp.inf)


def _rank_desc_t(score, n_real):
    L = score.shape[0]
    assert L % 8 == 0
    groups = [score[8 * a:8 * a + 8, :] for a in range(L // 8)]
    sub = lax.broadcasted_iota(jnp.int32, groups[0].shape, 0)
    cnts = [jnp.zeros(groups[0].shape, F32) for _ in groups]
    for i in range(n_real):
        row = score[i:i + 1, :]
        for a, x in enumerate(groups):
            if 8 * a > i:
                c = jnp.where(row >= x, 1.0, 0.0)
            elif 8 * a + 7 < i:
                c = jnp.where(row > x, 1.0, 0.0)
            else:
                c = jnp.where(sub + 8 * a > i, jnp.where(row >= x, 1.0, 0.0), jnp.where(row > x, 1.0, 0.0))
            cnts[a] = cnts[a] + c
    return jnp.concatenate(cnts, axis=0)


def _block_scores_t(imp, qpos, n_blocks):
    bidx = lax.broadcasted_iota(jnp.int32, imp.shape, 0)
    cur = qpos // NSA_BLOCK
    sc = jnp.where(bidx == 0, FORCE, jnp.where(bidx == cur, FORCE, jnp.where(bidx == cur - 1, FORCE, imp)))
    sc = jnp.where(bidx * NSA_BLOCK <= qpos, sc, NEG)
    return jnp.where(bidx < n_blocks, sc, -jnp.inf)


def _nsa_prompt_kernel(q_ref, misc_ref, k_ref, vts_ref, vtw_ref, ck_ref, cvt_ref, o_ref, m_ref, l_ref, acc_ref, flag_ref, *, tq, tk, nblk):
    i = pl.program_id(1)
    t0 = i * tq
    bpt = tk // NSA_BLOCK
    scale = HEAD_DIM ** -0.5 * LOG2E
    top_n = min(NSA_TOP_N, nblk)
    qpos_row = t0 + lax.broadcasted_iota(jnp.int32, (1, tq), 1)
    qpos_full = t0 + lax.broadcasted_iota(jnp.int32, (tk, tq), 1)
    sub_k = lax.broadcasted_iota(jnp.int32, (tk, tq), 0)
    e_blk = lax.broadcasted_iota(jnp.int32, (tk, nblk), 1)
    e_key = lax.broadcasted_iota(jnp.int32, (tk, nblk), 0)
    gates_t = jax.nn.sigmoid(misc_ref[...]).T

    def reset():
        m_ref[...] = jnp.full(m_ref.shape, NEG, F32)
        l_ref[...] = jnp.zeros(l_ref.shape, F32)
        acc_ref[...] = jnp.zeros(acc_ref.shape, F32)

    def online_update(s4, vt):
        m_old = m_ref[...]
        m_new = jnp.maximum(m_old, jnp.max(s4, axis=0, keepdims=True))
        p = jnp.exp2(s4 - m_new)
        alpha = jnp.exp2(m_old - m_new)
        l_ref[...] = alpha * l_ref[...] + jnp.sum(p, axis=0, keepdims=True)
        acc_ref[...] = alpha * acc_ref[...] + jnp.dot(vt, p.astype(BF16), preferred_element_type=F32)
        m_ref[...] = m_new

    def masked_scores(s, valid, base, g):
        return jnp.concatenate(
            [jnp.where(valid, s[:, r * tq:(r + 1) * tq] + (SLOPES[g][r] * LOG2E) * base, NEG)
             for r in range(NSA_GROUP)], axis=1)

    for g in range(NSA_KV_HEADS):
        q4t = jnp.concatenate(
            [(q_ref[:, (g * NSA_GROUP + r) * HEAD_DIM:(g * NSA_GROUP + r + 1) * HEAD_DIM] * scale).T
             for r in range(NSA_GROUP)], axis=1).astype(BF16)
        sc = jnp.dot(ck_ref[g], q4t, preferred_element_type=F32)
        cpos = lax.broadcasted_iota(jnp.int32, (nblk, tq), 0) * NSA_BLOCK + (NSA_BLOCK - 1)
        vm = cpos <= qpos_row
        cposrel = (cpos - t0).astype(F32)
        imp = jnp.zeros((nblk, tq), F32)
        parts = []
        for r in range(NSA_GROUP):
            s = jnp.where(vm, sc[:, r * tq:(r + 1) * tq] + (SLOPES[g][r] * LOG2E) * cposrel, NEG)
            e = jnp.exp2(s - jnp.max(s, axis=0, keepdims=True))
            p = jnp.where(vm, e / jnp.sum(e, axis=0, keepdims=True), 0.0)
            imp = imp + p
            parts.append(p)
        ocmp_t = jnp.dot(cvt_ref[g], jnp.concatenate(parts, axis=1).astype(BF16), preferred_element_type=F32)
        rank = _rank_desc_t(_block_scores_t(imp, qpos_row, nblk), nblk)
        sel_f = jnp.where(rank < top_n, 1.0, 0.0)
        sel_t = sel_f.astype(BF16)
        blk_any = jnp.max(sel_f, axis=1, keepdims=True)
        for jt in range(nblk // bpt):
            flag_ref[jt] = (jnp.max(blk_any[jt * bpt:(jt + 1) * bpt, :]) > 0.5).astype(jnp.int32)

        reset()

        def slc_body(j, carry):
            k0 = pl.multiple_of(j * tk, tk)

            @pl.when(flag_ref[j] > 0)
            def _():
                k = k_ref[pl.ds(k0, tk), g * HEAD_DIM:(g + 1) * HEAD_DIM]
                vt = vts_ref[g, :, pl.ds(k0, tk)]
                kpos = k0 + sub_k
                expand = jnp.where(e_blk == (k0 + e_key) // NSA_BLOCK, 1.0, 0.0).astype(BF16)
                chosen = jnp.dot(expand, sel_t, preferred_element_type=F32)
                valid = jnp.where(kpos <= qpos_full, chosen, 0.0) > 0.5
                s = jnp.dot(k, q4t, preferred_element_type=F32)
                online_update(masked_scores(s, valid, (kpos - t0).astype(F32), g), vt)

            return carry

        lax.fori_loop(0, (t0 + tq + tk - 1) // tk, slc_body, 0)
        oslc_t = acc_ref[...] / l_ref[...]

        reset()

        def win_body(j, carry):
            k0 = pl.multiple_of(j * tk, tk)
            k = k_ref[pl.ds(k0, tk), (NSA_KV_HEADS + g) * HEAD_DIM:(NSA_KV_HEADS + g + 1) * HEAD_DIM]
            vt = vtw_ref[g, :, pl.ds(k0, tk)]
            kpos = k0 + sub_k
            dist = qpos_full - kpos
            valid = jnp.abs(2 * dist - (NSA_WINDOW - 1)) <= (NSA_WINDOW - 1)
            s = jnp.dot(k, q4t, preferred_element_type=F32)
            online_update(masked_scores(s, valid, (kpos - t0).astype(F32), g), vt)
            return carry

        lax.fori_loop(jnp.maximum(t0 - (NSA_WINDOW - 1), 0) // tk, (t0 + tq + tk - 1) // tk, win_body, 0)
        owin_t = acc_ref[...] / l_ref[...]

        for r in range(NSA_GROUP):
            c = g * NSA_GROUP * 3 + r * 3 + GATE_LANE
            rs = slice(r * tq, (r + 1) * tq)
            o_t = (gates_t[c:c + 1, :] * ocmp_t[:, rs] + gates_t[c + 1:c + 2, :] * oslc_t[:, rs]
                   + gates_t[c + 2:c + 3, :] * owin_t[:, rs])
            h = g * NSA_GROUP + r
            o_ref[:, h * HEAD_DIM:(h + 1) * HEAD_DIM] = o_t.T.astype(o_ref.dtype)


def nsa_prompt(P, kmat, vt, ck, cvt, *, n_seq, tq, tk):
    M = P.shape[0]
    T = M // n_seq
    nblk = T // NSA_BLOCK
    nq = T // tq
    W = NSA_GROUP * tq
    assert T % tq == 0 and T % tk == 0 and tk % NSA_BLOCK == 0 and tk % HEAD_DIM == 0
    return pl.pallas_call(
        functools.partial(_nsa_prompt_kernel, tq=tq, tk=tk, nblk=nblk),
        grid=(n_seq, nq),
        in_specs=[pl.BlockSpec((tq, NSA_WIDTH), lambda b, i: (b * nq + i, Q0 // NSA_WIDTH)),
                  pl.BlockSpec((tq, HEAD_DIM), lambda b, i: (b * nq + i, MISC0 // HEAD_DIM)),
                  pl.BlockSpec((T, 2 * NSA_KV_HEADS * HEAD_DIM), lambda b, i: (b, 0)),
                  pl.BlockSpec((None, NSA_KV_HEADS, HEAD_DIM, T), lambda b, i: (b, 0, 0, 0)),
                  pl.BlockSpec((None, NSA_KV_HEADS, HEAD_DIM, T), lambda b, i: (b, 1, 0, 0)),
                  pl.BlockSpec((None, NSA_KV_HEADS, nblk, HEAD_DIM), lambda b, i: (b, 0, 0, 0)),
                  pl.BlockSpec((None, NSA_KV_HEADS, HEAD_DIM, nblk), lambda b, i: (b, 0, 0, 0))],
        out_specs=pl.BlockSpec((tq, NSA_WIDTH), lambda b, i: (b * nq + i, 0)),
        out_shape=jax.ShapeDtypeStruct((M, NSA_WIDTH), BF16),
        scratch_shapes=[pltpu.VMEM((1, W), F32),
                        pltpu.VMEM((1, W), F32),
                        pltpu.VMEM((HEAD_DIM, W), F32),
                        pltpu.SMEM((T // tk,), jnp.int32)],
        compiler_params=_cparams("parallel", "arbitrary"),
        name="nsa_prompt",
    )(P, P, kmat, vt, vt, ck, cvt)


def _bdot(a, b):
    return lax.dot_general(a.astype(BF16), b.astype(BF16), (((2,), (1,)), ((0,), (0,))), preferred_element_type=F32)


def _bdot_nt(a, b):
    return lax.dot_general(a.astype(BF16), b.astype(BF16), (((2,), (2,)), ((0,), (0,))), preferred_element_type=F32)


def _bdot_tn(a, b):
    return lax.dot_general(a.astype(BF16), b.astype(BF16), (((1,), (1,)), ((0,), (0,))), preferred_element_type=F32)


def _head_stack(a, off, width=HEAD_DIM):
    return jnp.stack([a[:, off + h * width:off + (h + 1) * width] for h in range(GDN_HEADS)], axis=0)


def _unit_lower_inverse_minus_eye(A, C):
    row = lax.broadcasted_iota(jnp.int32, (C, C), 0)
    col = lax.broadcasted_iota(jnp.int32, (C, C), 1)
    A8 = jnp.where((row // 8) == (col // 8), A, 0.0)
    B2 = _bdot(A8, A8)
    B4 = _bdot(B2, B2)
    P1 = B2 - A8 - _bdot(A8, B2)
    N = P1 + B4 + _bdot(P1, B4)
    size = 16
    while size <= C:
        AL = jnp.where((row // size) == (col // size), jnp.where((row // (size // 2)) == (col // (size // 2)), 0.0, A), 0.0)
        X = AL + _bdot(N, AL)
        N = N - X - _bdot(X, N)
        size *= 2
    return N


def _gdn_prep_kernel(*refs, C, sample):
    if sample:
        x_ref, buf_ref, misc_ref, cw_ref, alog_ref, dtb_ref = refs[:6]
    else:
        x_ref, prev_ref, misc_ref, cw_ref, alog_ref, dtb_ref = refs[:6]
    u_ref, w_ref, qg_ref, kd_ref, qk_ref, gl_ref = refs[6:]
    cw = cw_ref[...]
    x = x_ref[...]
    rowv = lax.broadcasted_iota(jnp.int32, (C, 1), 0)
    if sample:
        isbuf = (rowv >= ZTOK0 - (GDN_CONV - 1)) & (rowv < ZTOK0)
        xz = jnp.where(isbuf, buf_ref[...], x)
        y = (cw[3:4, :] * xz + cw[2:3, :] * pltpu.roll(xz, 1, 0) + cw[1:2, :] * pltpu.roll(xz, 2, 0)
             + cw[0:1, :] * pltpu.roll(xz, 3, 0))
        valid = jnp.where((rowv >= ZTOK0) & (rowv < ZTOK0 + 4), 1.0, 0.0)
    else:
        prev = jnp.where(pl.program_id(1) == 0, 0.0, prev_ref[...])
        xe = jnp.concatenate([prev, x], axis=0)
        y = (cw[3:4, :] * x + cw[2:3, :] * pltpu.roll(xe, 1, 0)[8:] + cw[1:2, :] * pltpu.roll(xe, 2, 0)[8:]
             + cw[0:1, :] * pltpu.roll(xe, 3, 0)[8:])
        valid = None
    y = jax.nn.silu(y)

    misc = misc_ref[...]
    gfull = -jnp.exp(alog_ref[...]) * jax.nn.softplus(misc + dtb_ref[...])
    bfull = jax.nn.sigmoid(misc)
    if valid is not None:
        gfull = gfull * valid
        bfull = bfull * valid
    row128 = lax.broadcasted_iota(jnp.int32, (C, HEAD_DIM), 0)
    G = gfull
    s = 1
    while s < C:
        G = G + jnp.where(row128 >= s, pltpu.roll(G, s, 0), 0.0)
        s *= 2
    expG = jnp.exp(G)
    glast = G[C - 1:C, :]
    kdfac = jnp.exp(glast - G)
    if C < HEAD_DIM:
        Gpad = jnp.concatenate([G, jnp.zeros((HEAD_DIM - C, HEAD_DIM), F32)], axis=0)
    else:
        Gpad = G
    GT = Gpad.T
    gl_ref[...] = jnp.exp(jnp.broadcast_to(GT[A_LANE:A_LANE + GDN_HEADS, C - 1:C], (GDN_HEADS, HEAD_DIM)))

    row = lax.broadcasted_iota(jnp.int32, (C, C), 0)
    col = lax.broadcasted_iota(jnp.int32, (C, C), 1)
    q = _head_stack(y, 0)
    k = _head_stack(y, GDN_WIDTH)
    v = _head_stack(y, 2 * GDN_WIDTH)
    q = q * lax.rsqrt(jnp.sum(q * q, -1, keepdims=True) + NORM_EPS) * (HEAD_DIM ** -0.5)
    k = k * lax.rsqrt(jnp.sum(k * k, -1, keepdims=True) + NORM_EPS)
    if valid is not None:
        q, k, v = q * valid, k * valid, v * valid
    beta = _head_stack(bfull, B_LANE, 1)
    gcol = _head_stack(G, A_LANE, 1)
    egc = _head_stack(expG, A_LANE, 1)
    kdf = _head_stack(kdfac, A_LANE, 1)
    grow = jnp.stack([GT[A_LANE + h:A_LANE + h + 1, :C] for h in range(GDN_HEADS)], axis=0)
    decay = jnp.exp(jnp.where(row >= col, gcol - grow, NEG))
    kb = k * beta
    A = jnp.where(row > col, _bdot_nt(kb, k) * decay, 0.0)
    N = _unit_lower_inverse_minus_eye(A, C)
    vb = v * beta
    kbg = kb * egc
    U = vb + _bdot(N, vb)
    Wm = kbg + _bdot(N, kbg)
    QK = _bdot_nt(q, k) * decay
    QG = q * egc
    KD = k * kdf
    for h in range(GDN_HEADS):
        sl = slice(h * HEAD_DIM, (h + 1) * HEAD_DIM)
        u_ref[:, sl] = U[h]
        w_ref[:, sl] = Wm[h].astype(w_ref.dtype)
        qk_ref[:, h * C:(h + 1) * C] = QK[h].astype(qk_ref.dtype)
        qg_ref[:, sl] = QG[h].astype(qg_ref.dtype)
        kd_ref[:, sl] = KD[h].astype(kd_ref.dtype)


def gdn_prep(P, cw, alog_row, dtb_row, *, n_seq, C, bufz=None):
    M = P.shape[0]
    T = M // n_seq
    n = T // C
    sample = bufz is not None
    assert T % C == 0 and C % 8 == 0
    qkv_spec = pl.BlockSpec((C, 3 * GDN_WIDTH), lambda b, c: (b * n + c, 0))
    if sample:
        second = pl.BlockSpec((C, 3 * GDN_WIDTH), lambda b, c: (b * n + c, 0))
        second_arr = bufz
    else:
        second = pl.BlockSpec((8, 3 * GDN_WIDTH), lambda b, c: (jnp.maximum((b * n + c) * (C // 8) - 1, 0), 0))
        second_arr = P
    in_specs = [qkv_spec, second,
                pl.BlockSpec((C, HEAD_DIM), lambda b, c: (b * n + c, MISC0 // HEAD_DIM)),
                pl.BlockSpec((GDN_CONV, 3 * GDN_WIDTH), lambda b, c: (0, 0)),
                pl.BlockSpec((1, HEAD_DIM), lambda b, c: (0, 0)),
                pl.BlockSpec((1, HEAD_DIM), lambda b, c: (0, 0))]
    row_spec = pl.BlockSpec((C, GDN_WIDTH), lambda b, c: (b * n + c, 0))
    out_specs = [row_spec, row_spec, row_spec, row_spec,
                 pl.BlockSpec((C, GDN_HEADS * C), lambda b, c: (b * n + c, 0)),
                 pl.BlockSpec((None, GDN_HEADS, HEAD_DIM), lambda b, c: (b * n + c, 0, 0))]
    out_shape = [jax.ShapeDtypeStruct((M, GDN_WIDTH), F32),
                 jax.ShapeDtypeStruct((M, GDN_WIDTH), BF16),
                 jax.ShapeDtypeStruct((M, GDN_WIDTH), BF16),
                 jax.ShapeDtypeStruct((M, GDN_WIDTH), BF16),
                 jax.ShapeDtypeStruct((M, GDN_HEADS * C), BF16),
                 jax.ShapeDtypeStruct((n_seq * n, GDN_HEADS, HEAD_DIM), F32)]
    return pl.pallas_call(
        functools.partial(_gdn_prep_kernel, C=C, sample=sample),
        grid=(n_seq, n),
        in_specs=in_specs,
        out_specs=out_specs,
        out_shape=out_shape,
        compiler_params=_cparams("parallel", "arbitrary"),
        name="gdn_prep_sample" if sample else "gdn_prep_prompt",
    )(P, second_arr, P, cw, alog_row, dtb_row)


def _gdn_scan_kernel(u_ref, w_ref, qg_ref, kd_ref, qk_ref, gl_ref, z_ref, ng_ref, s0_ref, o_ref, sfin_ref, S_ref, *, C):
    c = pl.program_id(1)

    @pl.when(c == 0)
    def _():
        S_ref[...] = s0_ref[...]

    S = S_ref[...]
    Sb = S.astype(BF16)
    v_new = _head_stack(u_ref[...], 0) - _bdot(_head_stack(w_ref[...], 0), Sb)
    vb = v_new.astype(BF16)
    o = _bdot(_head_stack(qg_ref[...], 0), Sb) + _bdot(_head_stack(qk_ref[...], 0, C), vb)
    gl = jnp.stack([gl_ref[h:h + 1, :] for h in range(GDN_HEADS)], axis=0)
    S_ref[...] = S * gl + _bdot_tn(_head_stack(kd_ref[...], 0), vb)
    o = o * lax.rsqrt(jnp.mean(o * o, -1, keepdims=True) + NORM_EPS) * ng_ref[...]
    o = o * jax.nn.silu(_head_stack(z_ref[...], 0))
    for h in range(GDN_HEADS):
        o_ref[:, h * HEAD_DIM:(h + 1) * HEAD_DIM] = o[h].astype(o_ref.dtype)

    @pl.when(c == pl.num_programs(1) - 1)
    def _():
        sfin_ref[...] = S_ref[...]


def gdn_scan(prep, P, norm_g, s0, *, n_seq, C):
    u, w, qg, kd, qk, gl = prep
    M = u.shape[0]
    T = M // n_seq
    n = T // C
    row_spec = pl.BlockSpec((C, GDN_WIDTH), lambda b, c: (b * n + c, 0))
    st_spec = pl.BlockSpec((None, GDN_HEADS, HEAD_DIM, HEAD_DIM), lambda b, c: (b, 0, 0, 0))
    return pl.pallas_call(
        functools.partial(_gdn_scan_kernel, C=C),
        grid=(n_seq, n),
        in_specs=[row_spec, row_spec, row_spec, row_spec,
                  pl.BlockSpec((C, GDN_HEADS * C), lambda b, c: (b * n + c, 0)),
                  pl.BlockSpec((None, GDN_HEADS, HEAD_DIM), lambda b, c: (b * n + c, 0, 0)),
                  pl.BlockSpec((C, GDN_WIDTH), lambda b, c: (b * n + c, Z0 // GDN_WIDTH)),
                  pl.BlockSpec((1, HEAD_DIM), lambda b, c: (0, 0)),
                  st_spec],
        out_specs=[row_spec, st_spec],
        out_shape=[jax.ShapeDtypeStruct((M, GDN_WIDTH), BF16),
                   jax.ShapeDtypeStruct((n_seq, GDN_HEADS, HEAD_DIM, HEAD_DIM), F32)],
        scratch_shapes=[pltpu.VMEM((GDN_HEADS, HEAD_DIM, HEAD_DIM), F32)],
        compiler_params=_cparams("parallel", "arbitrary"),
        name="gdn_scan",
    )(u, w, qg, kd, qk, gl, P, norm_g, s0)


PAGES_PER_STEP = 32
BLOCK_ROWS = NSA_BLOCK * 4
PAGE_ROWS = 2 * BLOCK_ROWS


def _compress_sample_kernel(pt_ref, cache_ref, pe_ref, w1_ref, w2_ref, out_ref, xbuf, sem, *, layer, n_pages, pps):
    b = pl.program_id(0)
    gi = pl.program_id(1)
    ng = pl.num_programs(1)
    step = b * ng + gi
    nsteps = pl.num_programs(0) * ng
    slot = step % 2

    def copies(st, sl):
        bb = st // ng
        g0 = (st % ng) * pps
        out = []
        for jl in range(pps):
            page = pt_ref[bb * n_pages + g0 + jl]
            for half in range(2):
                out.append(pltpu.make_async_copy(
                    cache_ref.at[layer, page, pl.ds(half * BLOCK_ROWS, BLOCK_ROWS), :],
                    xbuf.at[sl, :, 2 * jl + half, :],
                    sem.at[sl]))
        return out

    @pl.when(step == 0)
    def _():
        for cp in copies(step, slot):
            cp.start()

    @pl.when(step + 1 < nsteps)
    def _():
        for cp in copies(step + 1, 1 - slot):
            cp.start()

    for cp in copies(step, slot):
        cp.wait()

    nblk = 2 * pps

    _compress_rows(lambda gc, p: xbuf[slot, p * 4 + gc], pe_ref, w1_ref, w2_ref, out_ref, nblk)


def compress_sample(page_table, cache_rows, pe_t, w1, w2, *, layer):
    Bs, n_pages = page_table.shape
    pps = min(PAGES_PER_STEP, n_pages)
    assert n_pages % pps == 0
    ng = n_pages // pps
    nblk = 2 * pps
    grid_spec = pltpu.PrefetchScalarGridSpec(
        num_scalar_prefetch=1,
        grid=(Bs, ng),
        in_specs=[pl.BlockSpec(memory_space=pl.ANY),
                  pl.BlockSpec(pe_t.shape, lambda b, g, pt: (0, 0, 0)),
                  pl.BlockSpec(w1.shape, lambda b, g, pt: (0, 0, 0)),
                  pl.BlockSpec(w2.shape, lambda b, g, pt: (0, 0, 0))],
        out_specs=pl.BlockSpec((None, None, 2, NSA_KV_HEADS * nblk, HEAD_DIM), lambda b, g, pt: (b, g, 0, 0, 0)),
        scratch_shapes=[pltpu.VMEM((2, BLOCK_ROWS, nblk, HEAD_DIM), F32),
                        pltpu.SemaphoreType.DMA((2,))],
    )
    return pl.pallas_call(
        functools.partial(_compress_sample_kernel, layer=layer, n_pages=n_pages, pps=pps),
        grid_spec=grid_spec,
        out_shape=jax.ShapeDtypeStruct((Bs, ng, 2, NSA_KV_HEADS * nblk, HEAD_DIM), F32),
        compiler_params=_cparams("arbitrary", "arbitrary"),
        name="compress_sample",
    )(page_table.reshape(-1), cache_rows, pe_t, w1, w2)


def _stack_heads(q_ref, g, scale):
    return jnp.concatenate(
        [q_ref[:, (g * NSA_GROUP + r) * HEAD_DIM:(g * NSA_GROUP + r + 1) * HEAD_DIM] * scale for r in range(NSA_GROUP)],
        axis=0).astype(BF16)


def _stacked_row_info(past):
    rows = NSA_GROUP * ZROWS
    ridx = lax.broadcasted_iota(jnp.int32, (rows, 1), 0)
    zrow = ridx % ZROWS
    qpos = past + zrow - ZTOK0
    slope = jnp.zeros((rows, 1), F32)
    return ridx, zrow, qpos, slope


def _slope_col(g):
    ridx = lax.broadcasted_iota(jnp.int32, (NSA_GROUP * ZROWS, 1), 0)
    sl = jnp.zeros((NSA_GROUP * ZROWS, 1), F32)
    for r in range(NSA_GROUP):
        sl = jnp.where(ridx // ZROWS == r, SLOPES[g][r], sl)
    return sl


def _nsa_sample_a_kernel(q_ref, kvw_ref, ck_ref, cv_ref, win_ref, ocmp_ref, owin_ref, sel_ref, wout_ref, *, past, nc, n_blocks, seq):
    scale = HEAD_DIM ** -0.5
    _, zrow, qpos, _ = _stacked_row_info(past)
    wb = win_ref.shape[0] // 4
    lanes_pad = sel_ref.shape[-1]
    sel_lanes = ((n_blocks + HEAD_DIM - 1) // HEAD_DIM) * HEAD_DIM
    top_n = min(NSA_TOP_N, n_blocks)
    for g in range(NSA_KV_HEADS):
        q4 = _stack_heads(q_ref, g, scale)
        slope = _slope_col(g)
        cpos = lax.broadcasted_iota(jnp.int32, (1, nc), 1) * NSA_BLOCK + (NSA_BLOCK - 1)
        vm = cpos <= qpos
        s = jnp.where(vm, _dot_nt(q4, ck_ref[g]) + slope * (cpos - past).astype(F32), NEG)
        e = jnp.exp(s - jnp.max(s, axis=-1, keepdims=True))
        p = jnp.where(vm, e / jnp.sum(e, axis=-1, keepdims=True), 0.0)
        ocmp_ref[g] = _dot(p, cv_ref[g])
        imp = p[0:ZROWS]
        for r in range(1, NSA_GROUP):
            imp = imp + p[r * ZROWS:(r + 1) * ZROWS]
        imp = jnp.concatenate([imp, jnp.zeros((ZROWS, sel_lanes - nc), F32)], axis=1)
        rank = _rank_desc(_block_scores(imp, qpos[0:ZROWS], n_blocks), n_blocks)
        lane = lax.broadcasted_iota(jnp.int32, rank.shape, 1)
        olane = lax.broadcasted_iota(jnp.int32, (ZROWS, lanes_pad), 1)
        out = jnp.zeros((ZROWS, lanes_pad), jnp.int32)
        for t in range(top_n):
            idx = jnp.sum(jnp.where(rank == float(t), lane.astype(F32), 0.0), axis=-1, keepdims=True)
            out = jnp.where(olane == t, idx.astype(jnp.int32), out)
        sel_ref[g] = out
        kold = win_ref[pl.ds(g * 2, wb, stride=4), :]
        vold = win_ref[pl.ds(g * 2 + 1, wb, stride=4), :]
        knew = kvw_ref[:, g * 2 * HEAD_DIM:(g * 2 + 1) * HEAD_DIM]
        vnew = kvw_ref[:, (g * 2 + 1) * HEAD_DIM:(g * 2 + 2) * HEAD_DIM]
        kpos_o = past - wb + lax.broadcasted_iota(jnp.int32, (1, wb), 1)
        kz = lax.broadcasted_iota(jnp.int32, (1, ZROWS), 1)
        kpos_n = past + kz - ZTOK0
        d_o = qpos - kpos_o
        d_n = qpos - kpos_n
        ok_o = (kpos_o >= 0) & (d_o >= 0) & (d_o < NSA_WINDOW)
        ok_n = (kz >= ZTOK0) & (kz < ZTOK0 + seq) & (d_n >= 0) & (d_n < NSA_WINDOW)
        s_o = jnp.where(ok_o, _dot_nt(q4, kold) + slope * (kpos_o - past).astype(F32), NEG)
        s_n = jnp.where(ok_n, _dot_nt(q4, knew) + slope * (kpos_n - past).astype(F32), NEG)
        m = jnp.maximum(jnp.max(s_o, axis=-1, keepdims=True), jnp.max(s_n, axis=-1, keepdims=True))
        p_o = jnp.exp(s_o - m)
        p_n = jnp.exp(s_n - m)
        den = jnp.sum(p_o, axis=-1, keepdims=True) + jnp.sum(p_n, axis=-1, keepdims=True)
        owin_ref[g] = (_dot(p_o, vold) + _dot(p_n, vnew)) / den
    keep = (wb - seq) * 4
    wout_ref[0:keep, :] = win_ref[seq * 4:wb * 4, :]
    ridx = lax.broadcasted_iota(jnp.int32, (seq * 4, 1), 0)
    new_rows = jnp.zeros((seq * 4, HEAD_DIM), F32)
    for t in range(seq):
        for gc in range(4):
            new_rows = jnp.where(ridx == t * 4 + gc,
                                 kvw_ref[ZTOK0 + t:ZTOK0 + t + 1, gc * HEAD_DIM:(gc + 1) * HEAD_DIM], new_rows)
    wout_ref[keep:wb * 4, :] = new_rows


def nsa_sample_a(Ps, ckv, cache_win_rows, *, layer, past, seq):
    Bs = Ps.shape[0] // ZROWS
    nc = ckv.shape[3]
    wrows = cache_win_rows.shape[2]
    n_blocks = -(-(past + seq) // NSA_BLOCK)
    assert 1 <= seq <= 4 and (seq * 4) % 8 == 0 and wrows % 8 == 0
    rows = NSA_GROUP * ZROWS
    big = pl.BlockSpec((None, NSA_KV_HEADS, rows, HEAD_DIM), lambda b: (b, 0, 0, 0))
    return pl.pallas_call(
        functools.partial(_nsa_sample_a_kernel, past=past, nc=nc, n_blocks=n_blocks, seq=seq),
        grid=(Bs,),
        in_specs=[pl.BlockSpec((ZROWS, NSA_WIDTH), lambda b: (b, Q0 // NSA_WIDTH)),
                  pl.BlockSpec((ZROWS, KV_COLS), lambda b: (b, KVW0 // KV_COLS)),
                  pl.BlockSpec((None, None, NSA_KV_HEADS, nc, HEAD_DIM), lambda b: (b, 0, 0, 0, 0)),
                  pl.BlockSpec((None, None, NSA_KV_HEADS, nc, HEAD_DIM), lambda b: (b, 1, 0, 0, 0)),
                  pl.BlockSpec((None, None, wrows, HEAD_DIM), lambda b: (layer, b, 0, 0))],
        out_specs=[big, big,
                   pl.BlockSpec((None, NSA_KV_HEADS, ZROWS, HEAD_DIM), lambda b: (b, 0, 0, 0)),
                   pl.BlockSpec((None, wrows, HEAD_DIM), lambda b: (b, 0, 0))],
        out_shape=[jax.ShapeDtypeStruct((Bs, NSA_KV_HEADS, rows, HEAD_DIM), F32),
                   jax.ShapeDtypeStruct((Bs, NSA_KV_HEADS, rows, HEAD_DIM), F32),
                   jax.ShapeDtypeStruct((Bs, NSA_KV_HEADS, ZROWS, HEAD_DIM), jnp.int32),
                   jax.ShapeDtypeStruct((Bs, wrows, HEAD_DIM), F32)],
        compiler_params=_cparams("parallel"),
        name="nsa_sample_a",
    )(Ps, Ps, ckv, ckv, cache_win_rows)


def _nsa_sample_b_kernel(pt_ref, sel_ref, q_ref, kvs_ref, misc_ref, ocmp_ref, owin_ref, *rest, past, seq, n_past_blocks):
    blk_refs = rest[:seq]
    o_ref, m_ref, l_ref, acc_ref = rest[seq:]
    b = pl.program_id(0)
    g = pl.program_id(1)
    j = pl.program_id(2)
    nj = pl.num_programs(2)
    scale = HEAD_DIM ** -0.5
    rows = NSA_GROUP * ZROWS
    ridx, zrow, qpos, _ = _stacked_row_info(past)
    q4 = jnp.where(g == 0, _stack_heads(q_ref, 0, scale), _stack_heads(q_ref, 1, scale))
    slope = jnp.where(g == 0, _slope_col(0), _slope_col(1))

    @pl.when(j == 0)
    def _():
        kn = jnp.where(g == 0, kvs_ref[:, 0:HEAD_DIM], kvs_ref[:, 2 * HEAD_DIM:3 * HEAD_DIM])
        vn = jnp.where(g == 0, kvs_ref[:, HEAD_DIM:2 * HEAD_DIM], kvs_ref[:, 3 * HEAD_DIM:4 * HEAD_DIM])
        kz = lax.broadcasted_iota(jnp.int32, (1, ZROWS), 1)
        kpos = past + kz - ZTOK0
        ok = (kz >= ZTOK0) & (kz < ZTOK0 + seq) & (kpos <= qpos)
        s = jnp.where(ok, _dot_nt(q4, kn) + slope * (kpos - past).astype(F32), NEG)
        m = jnp.max(s, axis=-1, keepdims=True)
        p = jnp.exp(s - m)
        m_ref[...] = m
        l_ref[...] = jnp.sum(p, axis=-1, keepdims=True)
        acc_ref[...] = _dot(p, vn)

    lane = lax.broadcasted_iota(jnp.int32, (1, NSA_BLOCK), 1)
    s = jnp.full((rows, NSA_BLOCK), NEG, F32)
    vsel = []
    for t in range(seq):
        blk = sel_ref[((b * seq + t) * NSA_KV_HEADS + g) * NSA_TOP_N + j]
        k = blk_refs[t][pl.ds(g * 2, NSA_BLOCK, stride=4), :]
        kpos = blk * NSA_BLOCK + lane
        st = _dot_nt(q4, k) + slope * (kpos - past).astype(F32)
        mine = (zrow == ZTOK0 + t) & (blk < n_past_blocks) & (kpos <= qpos)
        s = jnp.where(mine, st, s)
        vsel.append(blk_refs[t][pl.ds(g * 2 + 1, NSA_BLOCK, stride=4), :])
    m_old = m_ref[...]
    m_new = jnp.maximum(m_old, jnp.max(s, axis=-1, keepdims=True))
    p = jnp.exp(s - m_new)
    alpha = jnp.exp(m_old - m_new)
    pv = jnp.zeros((rows, HEAD_DIM), F32)
    for t in range(seq):
        pv = pv + jnp.where(zrow == ZTOK0 + t, _dot(p, vsel[t]), 0.0)
    l_ref[...] = alpha * l_ref[...] + jnp.sum(p, axis=-1, keepdims=True)
    acc_ref[...] = alpha * acc_ref[...] + pv
    m_ref[...] = m_new

    @pl.when(j == nj - 1)
    def _():
        o_slc = acc_ref[...] / l_ref[...]
        gates = jax.nn.sigmoid(misc_ref[...])
        o_cmp = ocmp_ref[...]
        o_win = owin_ref[...]
        for r in range(NSA_GROUP):
            rs = slice(r * ZROWS, (r + 1) * ZROWS)
            outs = []
            for gg in range(NSA_KV_HEADS):
                c = gg * NSA_GROUP * 3 + r * 3 + GATE_LANE
                outs.append(gates[:, c:c + 1] * o_cmp[rs] + gates[:, c + 1:c + 2] * o_slc[rs]
                            + gates[:, c + 2:c + 3] * o_win[rs])
            o_ref[:, r * HEAD_DIM:(r + 1) * HEAD_DIM] = jnp.where(g == 0, outs[0], outs[1]).astype(o_ref.dtype)


def nsa_sample_b(page_table, sel, Ps, ocmp, owin, cache_rows, *, layer, past, seq):
    Bs, n_pages = page_table.shape
    rows = NSA_GROUP * ZROWS
    n_past_blocks = past // NSA_BLOCK
    gw = NSA_GROUP * HEAD_DIM

    def blk_map(t):
        def f(b, g, j, pt, sl):
            blk = jnp.minimum(sl[((b * seq + t) * NSA_KV_HEADS + g) * NSA_TOP_N + j], n_past_blocks - 1)
            return (layer, pt[b * n_pages + blk // 2], blk % 2, 0)
        return f

    big = pl.BlockSpec((None, None, rows, HEAD_DIM), lambda b, g, j, pt, sl: (b, g, 0, 0))
    in_specs = [pl.BlockSpec((ZROWS, NSA_WIDTH), lambda b, g, j, pt, sl: (b, Q0 // NSA_WIDTH)),
                pl.BlockSpec((ZROWS, KV_COLS), lambda b, g, j, pt, sl: (b, KVS0 // KV_COLS)),
                pl.BlockSpec((ZROWS, HEAD_DIM), lambda b, g, j, pt, sl: (b, MISC0 // HEAD_DIM)),
                big, big]
    in_specs += [pl.BlockSpec((None, None, PAGE_ROWS // 2, HEAD_DIM), blk_map(t)) for t in range(seq)]
    grid_spec = pltpu.PrefetchScalarGridSpec(
        num_scalar_prefetch=2,
        grid=(Bs, NSA_KV_HEADS, NSA_TOP_N),
        in_specs=in_specs,
        out_specs=pl.BlockSpec((ZROWS, gw), lambda b, g, j, pt, sl: (b, g)),
        scratch_shapes=[pltpu.VMEM((rows, 1), F32), pltpu.VMEM((rows, 1), F32), pltpu.VMEM((rows, HEAD_DIM), F32)],
    )
    return pl.pallas_call(
        functools.partial(_nsa_sample_b_kernel, past=past, seq=seq, n_past_blocks=n_past_blocks),
        grid_spec=grid_spec,
        out_shape=jax.ShapeDtypeStruct((Bs * ZROWS, NSA_WIDTH), BF16),
        compiler_params=_cparams("parallel", "arbitrary", "arbitrary"),
        name="nsa_sample_b",
    )(page_table.reshape(-1), sel, Ps, Ps, Ps, ocmp, owin, *([cache_rows] * seq))


def _reorder_w_in(w_in):
    q, kvc, kvs, kvw = w_in[..., 0:1024], w_in[..., 1024:1536], w_in[..., 1536:2048], w_in[..., 2048:2560]
    gates, qkv, z, ab = w_in[..., 2560:2584], w_in[..., 2584:5656], w_in[..., 5656:6680], w_in[..., 6680:6696]
    pad = jnp.zeros(w_in.shape[:-1] + (IN_COLS_PAD - MISC0 - 40,), w_in.dtype)
    return jnp.concatenate([qkv, q, z, kvc, kvs, kvw, gates, ab, pad], axis=-1).astype(BF16)


def _lane_row(vec, lane0):
    return jnp.zeros((1, HEAD_DIM), F32).at[0, lane0:lane0 + vec.shape[0]].set(vec.astype(F32))


def _to_zrows(x, first_row):
    Bs, n, C = x.shape
    z = jnp.zeros((Bs, ZROWS, C), x.dtype).at[:, first_row:first_row + n].set(x)
    return z.reshape(Bs * ZROWS, C)


def kernel(x_prompt, x_sample, cache_cmp_kv, cache_slc_kv, cache_win_kv, state_gdn, state_gdn_conv, state_ffn_conv, page_table, w_in, w_o, cmp_pe, cmp_w1, cmp_w2, gdn_conv_w, gdn_a_log, gdn_dt_bias, gdn_norm_g, ln1_g, ln1_b, ffn_w_up, ffn_conv_w, ffn_w_down, ln2_g, ln2_b):
    B, S, D = x_prompt.shape
    Bs, seq, _ = x_sample.shape
    depth = w_in.shape[0]
    n_pool, page = cache_cmp_kv.shape[1], cache_cmp_kv.shape[2]
    n_pages = page_table.shape[1]
    past = n_pages * page
    wb = cache_win_kv.shape[2]
    d_ff = ffn_w_down.shape[1]
    alpha = (2 * depth) ** 0.25
    assert page == 2 * NSA_BLOCK

    w_in_r = _reorder_w_in(w_in)
    w_o_b = w_o.astype(BF16)
    w_up_b = ffn_w_up.astype(BF16)
    w_down_b = ffn_w_down.astype(BF16)
    w1_b = cmp_w1.astype(BF16)
    w2_b = cmp_w2.astype(BF16)
    pe_t = jnp.swapaxes(cmp_pe, 1, 2)
    cache_cmp_rows = cache_cmp_kv.reshape(depth, n_pool, PAGE_ROWS, HEAD_DIM)
    cache_slc_rows = cache_slc_kv.reshape(depth, n_pool, PAGE_ROWS, HEAD_DIM)
    cache_win_rows = cache_win_kv.reshape(depth, Bs, wb * 4, HEAD_DIM)

    xp = x_prompt.reshape(B * S, D)
    xs = _to_zrows(x_sample, ZTOK0)
    xp_b, xs_b = xp, xs
    tm = 512 if (B * S) % 512 == 0 else B * S
    tm_in = 1024 if (B * S) % 1024 == 0 else tm
    tk_down = 1408 if d_ff % 1408 == 0 else 512
    tq = 128
    p_out = [[] for _ in range(6)]
    s_out = [[] for _ in range(6)]
    zero_state = jnp.zeros((B, GDN_HEADS, HEAD_DIM, HEAD_DIM), F32)

    for l in range(depth):
        alog_row = _lane_row(gdn_a_log[l], A_LANE)
        dtb_row = _lane_row(gdn_dt_bias[l], A_LANE)
        ng = gdn_norm_g[l].reshape(1, HEAD_DIM)
        g1, b1 = ln1_g[l].reshape(1, D), ln1_b[l].reshape(1, D)
        g2, b2 = ln2_g[l].reshape(1, D), ln2_b[l].reshape(1, D)

        P = matmul(xp_b, w_in_r[l], tm=tm_in, tn=1024)
        pkv = P[:, KVS0:KVS0 + 2 * KV_COLS].astype(BF16).reshape(B, S, 4, 2, HEAD_DIM)
        kmat = pkv[:, :, :, 0, :].reshape(B * S, 4 * HEAD_DIM)
        vt = jnp.transpose(pkv[:, :, :, 1, :], (0, 2, 3, 1))
        ckv = compress_prompt(P, pe_t[l], w1_b[l], w2_b[l], n_seq=B)
        ckv = ckv.reshape(B, 2, NSA_KV_HEADS, S // NSA_BLOCK, HEAD_DIM).astype(BF16)
        o_nsa = nsa_prompt(P, kmat, vt, ckv[:, 0], jnp.swapaxes(ckv[:, 1], -1, -2), n_seq=B, tq=tq,
                           tk=256 if S % 256 == 0 else tq)
        prep = gdn_prep(P, gdn_conv_w[l], alog_row, dtb_row, n_seq=B, C=GDN_CHUNK)
        o_gdn, s_fin = gdn_scan(prep, P, ng, zero_state, n_seq=B, C=GDN_CHUNK)
        h, h_b = proj_ln(o_nsa, o_gdn, w_o_b[l], xp, g1, b1, alpha=alpha, tm=256)
        act, utail = ffn_up_prompt(h_b, w_up_b[l], ffn_conv_w[l], n_seq=B, tm=tm, tf=512)
        xp, xp_b = ffn_down_ln(act, w_down_b[l], h, g2, b2, alpha=alpha, tm=tm, tk=tk_down)

        P3 = P.reshape(B, S, IN_COLS_PAD)
        p_out[0].append(P3[:, :, KVC0:KVC0 + KV_COLS].reshape(B, S, NSA_KV_HEADS, 2, HEAD_DIM))
        p_out[1].append(P3[:, :, KVS0:KVS0 + KV_COLS].reshape(B, S, NSA_KV_HEADS, 2, HEAD_DIM))
        wn = min(NSA_WINDOW, S)
        p_out[2].append(P3[:, S - wn:, KVW0:KVW0 + KV_COLS].reshape(B, wn, NSA_KV_HEADS, 2, HEAD_DIM))
        p_out[3].append(s_fin)
        p_out[4].append(P3[:, S - (GDN_CONV - 1):, QKV0:QKV0 + 3 * GDN_WIDTH])
        p_out[5].append(utail[:, 8 - (FFN_CONV - 1):, :])

        Ps = matmul(xs_b, w_in_r[l], tm=Bs * ZROWS, tn=1024)
        ckv_s = compress_sample(page_table, cache_cmp_rows, pe_t[l], w1_b[l], w2_b[l], layer=l)
        ng_grp = ckv_s.shape[1]
        nblk_step = ckv_s.shape[3] // NSA_KV_HEADS
        ckv_s = ckv_s.reshape(Bs, ng_grp, 2, NSA_KV_HEADS, nblk_step, HEAD_DIM)
        ckv_s = jnp.transpose(ckv_s, (0, 2, 3, 1, 4, 5)).reshape(Bs, 2, NSA_KV_HEADS, ng_grp * nblk_step, HEAD_DIM)
        ocmp, owin, sel, win_new = nsa_sample_a(Ps, ckv_s, cache_win_rows, layer=l, past=past, seq=seq)
        sel_flat = jnp.transpose(sel[:, :, ZTOK0:ZTOK0 + seq, :NSA_TOP_N], (0, 2, 1, 3)).reshape(-1)
        o_nsa_s = nsa_sample_b(page_table, sel_flat, Ps, ocmp, owin, cache_slc_rows, layer=l, past=past, seq=seq)
        gbufz = _to_zrows(state_gdn_conv[l], ZTOK0 - (GDN_CONV - 1))
        prep_s = gdn_prep(Ps, gdn_conv_w[l], alog_row, dtb_row, n_seq=Bs, C=ZROWS, bufz=gbufz)
        o_gdn_s, s_fin_s = gdn_scan(prep_s, Ps, ng, state_gdn[l], n_seq=Bs, C=ZROWS)
        hs, hs_b = proj_ln(o_nsa_s, o_gdn_s, w_o_b[l], xs, g1, b1, alpha=alpha, tm=Bs * ZROWS)
        fbufz = _to_zrows(state_ffn_conv[l], ZTOK0 - (FFN_CONV - 1))
        act_s, u_s = ffn_up_sample(hs_b, w_up_b[l], ffn_conv_w[l], fbufz, tf=512)
        xs, xs_b = ffn_down_ln(act_s, w_down_b[l], hs, g2, b2, alpha=alpha, tm=Bs * ZROWS, tk=tk_down)

        Ps3 = Ps.reshape(Bs, ZROWS, IN_COLS_PAD)[:, ZTOK0:ZTOK0 + seq]
        s_out[0].append(Ps3[:, :, KVC0:KVC0 + KV_COLS].reshape(Bs, seq, NSA_KV_HEADS, 2, HEAD_DIM))
        s_out[1].append(Ps3[:, :, KVS0:KVS0 + KV_COLS].reshape(Bs, seq, NSA_KV_HEADS, 2, HEAD_DIM))
        s_out[2].append(win_new.reshape(Bs, wb, NSA_KV_HEADS, 2, HEAD_DIM))
        s_out[3].append(s_fin_s)
        ext_g = jnp.concatenate([state_gdn_conv[l], Ps3[:, :, QKV0:QKV0 + 3 * GDN_WIDTH]], axis=1)
        s_out[4].append(ext_g[:, -(GDN_CONV - 1):])
        u3 = u_s.reshape(Bs, ZROWS, d_ff)[:, ZTOK0:ZTOK0 + seq]
        ext_f = jnp.concatenate([state_ffn_conv[l], u3], axis=1)
        s_out[5].append(ext_f[:, -(FFN_CONV - 1):])

    y_p = xp.reshape(B, S, D)
    y_s = xs.reshape(Bs, ZROWS, D)[:, ZTOK0:ZTOK0 + seq]
    return (y_p, y_s, *[jnp.stack(v, 0) for v in p_out], *[jnp.stack(v, 0) for v in s_out])
```

```python
import functools

import jax
import jax.numpy as jnp
import numpy as np
from jax import lax
from jax.experimental import pallas as pl
from jax.experimental.pallas import tpu as pltpu

F32 = jnp.float32
BF16 = jnp.bfloat16

HEAD_DIM = 128
NSA_HEADS = 8
NSA_KV_HEADS = 2
NSA_GROUP = 4
NSA_WIDTH = 1024
NSA_BLOCK = 64
NSA_TOP_N = 16
NSA_WINDOW = 512
GDN_HEADS = 8
GDN_WIDTH = 1024
GDN_CONV = 4
GDN_CHUNK = 64
FFN_CONV = 3
KV_COLS = 512
LN_EPS = 1e-5
NORM_EPS = 1e-6
NEG = -1e30
FORCE = 1e4
LOG2E = 1.4426950408889634
SLOPES = [[2.0 ** -(g * NSA_GROUP + r + 1) for r in range(NSA_GROUP)] for g in range(NSA_KV_HEADS)]

QKV0, Q0, Z0, KVC0, KVS0, KVW0, MISC0, IN_COLS_PAD = 0, 3072, 4096, 5120, 5632, 6144, 6656, 7168
GATE_LANE, A_LANE, B_LANE = 0, 24, 32

ZROWS = 16
ZTOK0 = 8

VMEM_LIMIT_BYTES = 56 * 1024 * 1024


def _cparams(*sem):
    return pltpu.CompilerParams(dimension_semantics=sem, vmem_limit_bytes=VMEM_LIMIT_BYTES)


def _dot(a, b):
    return jnp.dot(a.astype(BF16), b.astype(BF16), preferred_element_type=F32)


def _dot_nt(a, b):
    return lax.dot_general(a.astype(BF16), b.astype(BF16), (((1,), (1,)), ((), ())), preferred_element_type=F32)


def _dot_tn(a, b):
    return lax.dot_general(a.astype(BF16), b.astype(BF16), (((0,), (0,)), ((), ())), preferred_element_type=F32)


def _layer_norm(t, g, b):
    mu = jnp.mean(t, -1, keepdims=True)
    d = t - mu
    var = jnp.mean(d * d, -1, keepdims=True)
    return d * lax.rsqrt(var + LN_EPS) * g + b


def _mm_kernel(x_ref, w_ref, o_ref):
    o_ref[...] = _dot(x_ref[...], w_ref[...])


def matmul(x, w, *, tm, tn):
    M, K = x.shape
    _, N = w.shape
    assert M % tm == 0 and N % tn == 0
    return pl.pallas_call(
        _mm_kernel,
        grid=(M // tm, N // tn),
        in_specs=[pl.BlockSpec((tm, K), lambda i, j: (i, 0)),
                  pl.BlockSpec((K, tn), lambda i, j: (0, j))],
        out_specs=pl.BlockSpec((tm, tn), lambda i, j: (i, j)),
        out_shape=jax.ShapeDtypeStruct((M, N), F32),
        compiler_params=_cparams("parallel", "arbitrary"),
        name="in_proj",
    )(x, w)


def _proj_ln_kernel(a1_ref, a2_ref, w_ref, x_ref, g_ref, b_ref, o_ref, ob_ref, *, alpha):
    k1 = a1_ref.shape[1]
    acc = _dot(a1_ref[...], w_ref[:k1, :]) + _dot(a2_ref[...], w_ref[k1:, :])
    o = _layer_norm(alpha * x_ref[...] + acc, g_ref[...], b_ref[...])
    o_ref[...] = o
    ob_ref[...] = o.astype(ob_ref.dtype)


def proj_ln(a1, a2, w, x, g, b, *, alpha, tm):
    M, D = x.shape
    k1, k2 = a1.shape[1], a2.shape[1]
    assert M % tm == 0 and w.shape == (k1 + k2, D)
    return pl.pallas_call(
        functools.partial(_proj_ln_kernel, alpha=alpha),
        grid=(M // tm,),
        in_specs=[pl.BlockSpec((tm, k1), lambda i: (i, 0)),
                  pl.BlockSpec((tm, k2), lambda i: (i, 0)),
                  pl.BlockSpec((k1 + k2, D), lambda i: (0, 0)),
                  pl.BlockSpec((tm, D), lambda i: (i, 0)),
                  pl.BlockSpec((1, D), lambda i: (0, 0)),
                  pl.BlockSpec((1, D), lambda i: (0, 0))],
        out_specs=[pl.BlockSpec((tm, D), lambda i: (i, 0)), pl.BlockSpec((tm, D), lambda i: (i, 0))],
        out_shape=[jax.ShapeDtypeStruct((M, D), F32), jax.ShapeDtypeStruct((M, D), BF16)],
        compiler_params=_cparams("parallel"),
        name="out_proj_ln",
    )(a1, a2, w, x, g, b)


def _conv3(u, prev8, cw):
    ue = jnp.concatenate([prev8, u], axis=0)
    u1 = pltpu.roll(ue, 1, 0)[8:]
    u2 = pltpu.roll(ue, 2, 0)[8:]
    return cw[2:3, :] * u + cw[1:2, :] * u1 + cw[0:1, :] * u2


def _ffn_up_prompt_kernel(h_ref, w_ref, cw_ref, act_ref, tail_ref, carry_ref, *, tiles_per_seq):
    i = pl.program_id(1)
    h = h_ref[...]
    tm = h.shape[0]

    @pl.when(i % tiles_per_seq == 0)
    def _():
        carry_ref[...] = jnp.zeros_like(carry_ref)

    for c in range(act_ref.shape[1] // HEAD_DIM):
        cs = slice(c * HEAD_DIM, (c + 1) * HEAD_DIM)
        r = _dot(h, w_ref[:, 2 * c * HEAD_DIM:2 * (c + 1) * HEAD_DIM])
        u, v = r[:, :HEAD_DIM], r[:, HEAD_DIM:]
        uc = _conv3(u, carry_ref[:, cs], cw_ref[:, cs])
        act_ref[:, cs] = (jax.nn.gelu(uc) * v).astype(act_ref.dtype)
        tail = u[tm - 8:, :]
        carry_ref[:, cs] = tail
        tail_ref[:, cs] = tail


def ffn_up_prompt(h, w_up_il, cw, *, n_seq, tm, tf):
    M, D = h.shape
    F = w_up_il.shape[1] // 2
    T = M // n_seq
    assert M % tm == 0 and F % tf == 0 and T % tm == 0 and tf % HEAD_DIM == 0
    nf = F // tf
    tps = T // tm
    return pl.pallas_call(
        functools.partial(_ffn_up_prompt_kernel, tiles_per_seq=tps),
        grid=(nf, M // tm),
        in_specs=[pl.BlockSpec((tm, D), lambda f, i: (i, 0)),
                  pl.BlockSpec((D, 2 * tf), lambda f, i: (0, f)),
                  pl.BlockSpec((FFN_CONV, tf), lambda f, i: (0, f))],
        out_specs=[pl.BlockSpec((tm, tf), lambda f, i: (i, f)),
                   pl.BlockSpec((None, 8, tf), lambda f, i: (i // tps, 0, f))],
        out_shape=[jax.ShapeDtypeStruct((M, F), BF16),
                   jax.ShapeDtypeStruct((n_seq, 8, F), F32)],
        scratch_shapes=[pltpu.VMEM((8, tf), F32)],
        compiler_params=_cparams("arbitrary", "arbitrary"),
        name="ffn_up_prompt",
    )(h, w_up_il, cw)


def _ffn_up_sample_kernel(h_ref, w_ref, cw_ref, buf_ref, act_ref, u_ref):
    h = h_ref[...]
    zrow = lax.broadcasted_iota(jnp.int32, (h.shape[0], 1), 0) % ZROWS
    isbuf = (zrow >= ZTOK0 - (FFN_CONV - 1)) & (zrow < ZTOK0)
    for c in range(act_ref.shape[1] // HEAD_DIM):
        cs = slice(c * HEAD_DIM, (c + 1) * HEAD_DIM)
        r = _dot(h, w_ref[:, 2 * c * HEAD_DIM:2 * (c + 1) * HEAD_DIM])
        u, v = r[:, :HEAD_DIM], r[:, HEAD_DIM:]
        uz = jnp.where(isbuf, buf_ref[:, cs], u)
        cw = cw_ref[:, cs]
        uc = cw[2:3, :] * uz + cw[1:2, :] * pltpu.roll(uz, 1, 0) + cw[0:1, :] * pltpu.roll(uz, 2, 0)
        act_ref[:, cs] = (jax.nn.gelu(uc) * v).astype(act_ref.dtype)
        u_ref[:, cs] = u


def ffn_up_sample(h, w_up_il, cw, bufz, *, tf):
    M, D = h.shape
    F = w_up_il.shape[1] // 2
    nf = F // tf
    return pl.pallas_call(
        _ffn_up_sample_kernel,
        grid=(nf,),
        in_specs=[pl.BlockSpec((M, D), lambda f: (0, 0)),
                  pl.BlockSpec((D, 2 * tf), lambda f: (0, f)),
                  pl.BlockSpec((FFN_CONV, tf), lambda f: (0, f)),
                  pl.BlockSpec((M, tf), lambda f: (0, f))],
        out_specs=[pl.BlockSpec((M, tf), lambda f: (0, f)),
                   pl.BlockSpec((M, tf), lambda f: (0, f))],
        out_shape=[jax.ShapeDtypeStruct((M, F), BF16),
                   jax.ShapeDtypeStruct((M, F), F32)],
        compiler_params=_cparams("parallel"),
        name="ffn_up_sample",
    )(h, w_up_il, cw, bufz)


def _ffn_down_ln_kernel(a_ref, w_ref, h_ref, g_ref, b_ref, o_ref, ob_ref, acc_ref, *, alpha):
    k = pl.program_id(1)

    @pl.when(k == 0)
    def _():
        acc_ref[...] = jnp.zeros_like(acc_ref)

    acc_ref[...] += _dot(a_ref[...], w_ref[...])

    @pl.when(k == pl.num_programs(1) - 1)
    def _():
        o = _layer_norm(alpha * h_ref[...] + acc_ref[...], g_ref[...], b_ref[...])
        o_ref[...] = o
        ob_ref[...] = o.astype(ob_ref.dtype)


def ffn_down_ln(a, w, h, g, b, *, alpha, tm, tk):
    M, Fd = a.shape
    D = w.shape[1]
    assert M % tm == 0 and Fd % tk == 0
    return pl.pallas_call(
        functools.partial(_ffn_down_ln_kernel, alpha=alpha),
        grid=(M // tm, Fd // tk),
        in_specs=[pl.BlockSpec((tm, tk), lambda i, k: (i, k)),
                  pl.BlockSpec((tk, D), lambda i, k: (k, 0)),
                  pl.BlockSpec((tm, D), lambda i, k: (i, 0)),
                  pl.BlockSpec((1, D), lambda i, k: (0, 0)),
                  pl.BlockSpec((1, D), lambda i, k: (0, 0))],
        out_specs=[pl.BlockSpec((tm, D), lambda i, k: (i, 0)), pl.BlockSpec((tm, D), lambda i, k: (i, 0))],
        out_shape=[jax.ShapeDtypeStruct((M, D), F32), jax.ShapeDtypeStruct((M, D), BF16)],
        scratch_shapes=[pltpu.VMEM((tm, D), F32)],
        compiler_params=_cparams("parallel", "arbitrary"),
        name="ffn_down_ln",
    )(a, w, h, g, b)


def _compress_rows(read_rows, pe_ref, w1_ref, w2_ref, out_ref, nblk):
    del nblk
    for c in range(2):
        parts = []
        for p in range(NSA_BLOCK):
            pe_row = pe_ref[c, p:p + 1, :]
            rows = [read_rows(g * 2 + c, p) + pe_row for g in range(NSA_KV_HEADS)]
            parts.append(jnp.concatenate(rows, axis=0).astype(BF16))
        hid = jnp.dot(jnp.concatenate(parts, axis=1), w1_ref[c], preferred_element_type=F32)
        out_ref[c] = _dot(jax.nn.silu(hid), w2_ref[c])


def _compress_prompt_kernel(x0, x1, x2, x3, pe_ref, w1_ref, w2_ref, out_ref, *, nblk):
    xs = [x0, x1, x2, x3]
    _compress_rows(lambda gc, p: xs[gc][:, p, :], pe_ref, w1_ref, w2_ref, out_ref, nblk)


def compress_prompt(P, pe_t, w1, w2, *, n_seq):
    M = P.shape[0]
    T = M // n_seq
    nblk = T // NSA_BLOCK
    P3 = P.reshape(n_seq * nblk, NSA_BLOCK, IN_COLS_PAD)
    cb0 = KVC0 // HEAD_DIM
    in_specs = [pl.BlockSpec((nblk, NSA_BLOCK, HEAD_DIM), (lambda b, gc=gc: (b, 0, cb0 + gc))) for gc in range(4)]
    in_specs += [pl.BlockSpec(pe_t.shape, lambda b: (0, 0, 0)),
                 pl.BlockSpec(w1.shape, lambda b: (0, 0, 0)),
                 pl.BlockSpec(w2.shape, lambda b: (0, 0, 0))]
    return pl.pallas_call(
        functools.partial(_compress_prompt_kernel, nblk=nblk),
        grid=(n_seq,),
        in_specs=in_specs,
        out_specs=pl.BlockSpec((None, 2, NSA_KV_HEADS * nblk, HEAD_DIM), lambda b: (b, 0, 0, 0)),
        out_shape=jax.ShapeDtypeStruct((n_seq, 2, NSA_KV_HEADS * nblk, HEAD_DIM), F32),
        compiler_params=_cparams("parallel"),
        name="compress_prompt",
    )(P3, P3, P3, P3, pe_t, w1, w2)


def _rank_desc(score, n_real):
    lane = lax.broadcasted_iota(jnp.int32, score.shape, 1)
    cnt = jnp.zeros(score.shape, F32)
    for i in range(n_real):
        col = score[:, i:i + 1]
        ge = jnp.where(col >= score, 1.0, 0.0)
        gt = jnp.where(col > score, 1.0, 0.0)
        cnt = cnt + jnp.where(lane > i, ge, gt)
    return cnt


def _block_scores(imp, qpos, n_blocks):
    bidx = lax.broadcasted_iota(jnp.int32, imp.shape, 1)
    cur = qpos // NSA_BLOCK
    sc = jnp.where(bidx == 0, FORCE, jnp.where(bidx == cur, FORCE, jnp.where(bidx == cur - 1, FORCE, imp)))
    sc = jnp.where(bidx * NSA_BLOCK <= qpos, sc, NEG)
    return jnp.where(bidx < n_blocks, sc, -jnp.inf)


def _rank_desc_t(score, n_real):
    L = score.shape[0]
    assert L % 8 == 0
    groups = [score[8 * a:8 * a + 8, :] for a in range(L // 8)]
    sub = lax.broadcasted_iota(jnp.int32, groups[0].shape, 0)
    cnts = [jnp.zeros(groups[0].shape, F32) for _ in groups]
    for i in range(n_real):
        row = score[i:i + 1, :]
        for a, x in enumerate(groups):
            if 8 * a > i:
                c = jnp.where(row >= x, 1.0, 0.0)
            elif 8 * a + 7 < i:
                c = jnp.where(row > x, 1.0, 0.0)
            else:
                c = jnp.where(sub + 8 * a > i, jnp.where(row >= x, 1.0, 0.0), jnp.where(row > x, 1.0, 0.0))
            cnts[a] = cnts[a] + c
    return jnp.concatenate(cnts, axis=0)


def _block_scores_t(imp, qpos, n_blocks):
    bidx = lax.broadcasted_iota(jnp.int32, imp.shape, 0)
    cur = qpos // NSA_BLOCK
    sc = jnp.where(bidx == 0, FORCE, jnp.where(bidx == cur, FORCE, jnp.where(bidx == cur - 1, FORCE, imp)))
    sc = jnp.where(bidx * NSA_BLOCK <= qpos, sc, NEG)
    return jnp.where(bidx < n_blocks, sc, -jnp.inf)


def _nsa_prompt_kernel(q_ref, misc_ref, k_ref, vts_ref, vtw_ref, ck_ref, cvt_ref, o_ref, m_ref, l_ref, acc_ref, flag_ref, *, tq, tk, nblk):
    i = pl.program_id(1)
    t0 = i * tq
    bpt = tk // NSA_BLOCK
    scale = HEAD_DIM ** -0.5 * LOG2E
    top_n = min(NSA_TOP_N, nblk)
    qpos_row = t0 + lax.broadcasted_iota(jnp.int32, (1, tq), 1)
    qpos_full = t0 + lax.broadcasted_iota(jnp.int32, (tk, tq), 1)
    sub_k = lax.broadcasted_iota(jnp.int32, (tk, tq), 0)
    e_blk = lax.broadcasted_iota(jnp.int32, (tk, nblk), 1)
    e_key = lax.broadcasted_iota(jnp.int32, (tk, nblk), 0)
    gates_t = jax.nn.sigmoid(misc_ref[...]).T

    def reset():
        m_ref[...] = jnp.full(m_ref.shape, NEG, F32)
        l_ref[...] = jnp.zeros(l_ref.shape, F32)
        acc_ref[...] = jnp.zeros(acc_ref.shape, F32)

    def online_update(s4, vt):
        m_old = m_ref[...]
        m_new = jnp.maximum(m_old, jnp.max(s4, axis=0, keepdims=True))
        p = jnp.exp2(s4 - m_new)
        alpha = jnp.exp2(m_old - m_new)
        l_ref[...] = alpha * l_ref[...] + jnp.sum(p, axis=0, keepdims=True)
        acc_ref[...] = alpha * acc_ref[...] + jnp.dot(vt, p.astype(BF16), preferred_element_type=F32)
        m_ref[...] = m_new

    def masked_scores(s, valid, base, g):
        return jnp.concatenate(
            [jnp.where(valid, s[:, r * tq:(r + 1) * tq] + (SLOPES[g][r] * LOG2E) * base, NEG)
             for r in range(NSA_GROUP)], axis=1)

    for g in range(NSA_KV_HEADS):
        q4t = jnp.concatenate(
            [(q_ref[:, (g * NSA_GROUP + r) * HEAD_DIM:(g * NSA_GROUP + r + 1) * HEAD_DIM] * scale).T
             for r in range(NSA_GROUP)], axis=1).astype(BF16)
        sc = jnp.dot(ck_ref[g], q4t, preferred_element_type=F32)
        cpos = lax.broadcasted_iota(jnp.int32, (nblk, tq), 0) * NSA_BLOCK + (NSA_BLOCK - 1)
        vm = cpos <= qpos_row
        cposrel = (cpos - t0).astype(F32)
        imp = jnp.zeros((nblk, tq), F32)
        parts = []
        for r in range(NSA_GROUP):
            s = jnp.where(vm, sc[:, r * tq:(r + 1) * tq] + (SLOPES[g][r] * LOG2E) * cposrel, NEG)
            e = jnp.exp2(s - jnp.max(s, axis=0, keepdims=True))
            p = jnp.where(vm, e / jnp.sum(e, axis=0, keepdims=True), 0.0)
            imp = imp + p
            parts.append(p)
        ocmp_t = jnp.dot(cvt_ref[g], jnp.concatenate(parts, axis=1).astype(BF16), preferred_element_type=F32)
        rank = _rank_desc_t(_block_scores_t(imp, qpos_row, nblk), nblk)
        sel_f = jnp.where(rank < top_n, 1.0, 0.0)
        sel_t = sel_f.astype(BF16)
        blk_any = jnp.max(sel_f, axis=1, keepdims=True)
        for jt in range(nblk // bpt):
            flag_ref[jt] = (jnp.max(blk_any[jt * bpt:(jt + 1) * bpt, :]) > 0.5).astype(jnp.int32)

        reset()

        def slc_body(j, carry):
            k0 = pl.multiple_of(j * tk, tk)

            @pl.when(flag_ref[j] > 0)
            def _():
                k = k_ref[pl.ds(k0, tk), g * HEAD_DIM:(g + 1) * HEAD_DIM]
                vt = vts_ref[g, :, pl.ds(k0, tk)]
                kpos = k0 + sub_k
                expand = jnp.where(e_blk == (k0 + e_key) // NSA_BLOCK, 1.0, 0.0).astype(BF16)
                chosen = jnp.dot(expand, sel_t, preferred_element_type=F32)
                valid = jnp.where(kpos <= qpos_full, chosen, 0.0) > 0.5
                s = jnp.dot(k, q4t, preferred_element_type=F32)
                online_update(masked_scores(s, valid, (kpos - t0).astype(F32), g), vt)

            return carry

        lax.fori_loop(0, (t0 + tq + tk - 1) // tk, slc_body, 0)
        oslc_t = acc_ref[...] / l_ref[...]

        reset()

        def win_body(j, carry):
            k0 = pl.multiple_of(j * tk, tk)
            k = k_ref[pl.ds(k0, tk), (NSA_KV_HEADS + g) * HEAD_DIM:(NSA_KV_HEADS + g + 1) * HEAD_DIM]
            vt = vtw_ref[g, :, pl.ds(k0, tk)]
            kpos = k0 + sub_k
            dist = qpos_full - kpos
            valid = jnp.abs(2 * dist - (NSA_WINDOW - 1)) <= (NSA_WINDOW - 1)
            s = jnp.dot(k, q4t, preferred_element_type=F32)
            online_update(masked_scores(s, valid, (kpos - t0).astype(F32), g), vt)
            return carry

        lax.fori_loop(jnp.maximum(t0 - (NSA_WINDOW - 1), 0) // tk, (t0 + tq + tk - 1) // tk, win_body, 0)
        owin_t = acc_ref[...] / l_ref[...]

        for r in range(NSA_GROUP):
            c = g * NSA_GROUP * 3 + r * 3 + GATE_LANE
            rs = slice(r * tq, (r + 1) * tq)
            o_t = (gates_t[c:c + 1, :] * ocmp_t[:, rs] + gates_t[c + 1:c + 2, :] * oslc_t[:, rs]
                   + gates_t[c + 2:c + 3, :] * owin_t[:, rs])
            h = g * NSA_GROUP + r
            o_ref[:, h * HEAD_DIM:(h + 1) * HEAD_DIM] = o_t.T.astype(o_ref.dtype)


def nsa_prompt(P, kmat, vt, ck, cvt, *, n_seq, tq, tk):
    M = P.shape[0]
    T = M // n_seq
    nblk = T // NSA_BLOCK
    nq = T // tq
    W = NSA_GROUP * tq
    assert T % tq == 0 and T % tk == 0 and tk % NSA_BLOCK == 0 and tk % HEAD_DIM == 0
    return pl.pallas_call(
        functools.partial(_nsa_prompt_kernel, tq=tq, tk=tk, nblk=nblk),
        grid=(n_seq, nq),
        in_specs=[pl.BlockSpec((tq, NSA_WIDTH), lambda b, i: (b * nq + i, Q0 // NSA_WIDTH)),
                  pl.BlockSpec((tq, HEAD_DIM), lambda b, i: (b * nq + i, MISC0 // HEAD_DIM)),
                  pl.BlockSpec((T, 2 * NSA_KV_HEADS * HEAD_DIM), lambda b, i: (b, 0)),
                  pl.BlockSpec((None, NSA_KV_HEADS, HEAD_DIM, T), lambda b, i: (b, 0, 0, 0)),
                  pl.BlockSpec((None, NSA_KV_HEADS, HEAD_DIM, T), lambda b, i: (b, 1, 0, 0)),
                  pl.BlockSpec((None, NSA_KV_HEADS, nblk, HEAD_DIM), lambda b, i: (b, 0, 0, 0)),
                  pl.BlockSpec((None, NSA_KV_HEADS, HEAD_DIM, nblk), lambda b, i: (b, 0, 0, 0))],
        out_specs=pl.BlockSpec((tq, NSA_WIDTH), lambda b, i: (b * nq + i, 0)),
        out_shape=jax.ShapeDtypeStruct((M, NSA_WIDTH), BF16),
        scratch_shapes=[pltpu.VMEM((1, W), F32),
                        pltpu.VMEM((1, W), F32),
                        pltpu.VMEM((HEAD_DIM, W), F32),
                        pltpu.SMEM((T // tk,), jnp.int32)],
        compiler_params=_cparams("parallel", "arbitrary"),
        name="nsa_prompt",
    )(P, P, kmat, vt, vt, ck, cvt)


def _bdot(a, b):
    return lax.dot_general(a.astype(BF16), b.astype(BF16), (((2,), (1,)), ((0,), (0,))), preferred_element_type=F32)


def _bdot_nt(a, b):
    return lax.dot_general(a.astype(BF16), b.astype(BF16), (((2,), (2,)), ((0,), (0,))), preferred_element_type=F32)


def _bdot_tn(a, b):
    return lax.dot_general(a.astype(BF16), b.astype(BF16), (((1,), (1,)), ((0,), (0,))), preferred_element_type=F32)


def _head_stack(a, off, width=HEAD_DIM):
    return jnp.stack([a[:, off + h * width:off + (h + 1) * width] for h in range(GDN_HEADS)], axis=0)


def _unit_lower_inverse_minus_eye(A, C):
    row = lax.broadcasted_iota(jnp.int32, (C, C), 0)
    col = lax.broadcasted_iota(jnp.int32, (C, C), 1)
    A8 = jnp.where((row // 8) == (col // 8), A, 0.0)
    B2 = _bdot(A8, A8)
    B4 = _bdot(B2, B2)
    P1 = B2 - A8 - _bdot(A8, B2)
    N = P1 + B4 + _bdot(P1, B4)
    size = 16
    while size <= C:
        AL = jnp.where((row // size) == (col // size), jnp.where((row // (size // 2)) == (col // (size // 2)), 0.0, A), 0.0)
        X = AL + _bdot(N, AL)
        N = N - X - _bdot(X, N)
        size *= 2
    return N


def _gdn_prep_kernel(*refs, C, sample):
    if sample:
        x_ref, buf_ref, misc_ref, cw_ref, alog_ref, dtb_ref = refs[:6]
    else:
        x_ref, prev_ref, misc_ref, cw_ref, alog_ref, dtb_ref = refs[:6]
    u_ref, w_ref, qg_ref, kd_ref, qk_ref, gl_ref = refs[6:]
    cw = cw_ref[...]
    x = x_ref[...]
    rowv = lax.broadcasted_iota(jnp.int32, (C, 1), 0)
    if sample:
        isbuf = (rowv >= ZTOK0 - (GDN_CONV - 1)) & (rowv < ZTOK0)
        xz = jnp.where(isbuf, buf_ref[...], x)
        y = (cw[3:4, :] * xz + cw[2:3, :] * pltpu.roll(xz, 1, 0) + cw[1:2, :] * pltpu.roll(xz, 2, 0)
             + cw[0:1, :] * pltpu.roll(xz, 3, 0))
        valid = jnp.where((rowv >= ZTOK0) & (rowv < ZTOK0 + 4), 1.0, 0.0)
    else:
        prev = jnp.where(pl.program_id(1) == 0, 0.0, prev_ref[...])
        xe = jnp.concatenate([prev, x], axis=0)
        y = (cw[3:4, :] * x + cw[2:3, :] * pltpu.roll(xe, 1, 0)[8:] + cw[1:2, :] * pltpu.roll(xe, 2, 0)[8:]
             + cw[0:1, :] * pltpu.roll(xe, 3, 0)[8:])
        valid = None
    y = jax.nn.silu(y)

    misc = misc_ref[...]
    gfull = -jnp.exp(alog_ref[...]) * jax.nn.softplus(misc + dtb_ref[...])
    bfull = jax.nn.sigmoid(misc)
    if valid is not None:
        gfull = gfull * valid
        bfull = bfull * valid
    row128 = lax.broadcasted_iota(jnp.int32, (C, HEAD_DIM), 0)
    G = gfull
    s = 1
    while s < C:
        G = G + jnp.where(row128 >= s, pltpu.roll(G, s, 0), 0.0)
        s *= 2
    expG = jnp.exp(G)
    glast = G[C - 1:C, :]
    kdfac = jnp.exp(glast - G)
    if C < HEAD_DIM:
        Gpad = jnp.concatenate([G, jnp.zeros((HEAD_DIM - C, HEAD_DIM), F32)], axis=0)
    else:
        Gpad = G
    GT = Gpad.T
    gl_ref[...] = jnp.exp(jnp.broadcast_to(GT[A_LANE:A_LANE + GDN_HEADS, C - 1:C], (GDN_HEADS, HEAD_DIM)))

    row = lax.broadcasted_iota(jnp.int32, (C, C), 0)
    col = lax.broadcasted_iota(jnp.int32, (C, C), 1)
    q = _head_stack(y, 0)
    k = _head_stack(y, GDN_WIDTH)
    v = _head_stack(y, 2 * GDN_WIDTH)
    q = q * lax.rsqrt(jnp.sum(q * q, -1, keepdims=True) + NORM_EPS) * (HEAD_DIM ** -0.5)
    k = k * lax.rsqrt(jnp.sum(k * k, -1, keepdims=True) + NORM_EPS)
    if valid is not None:
        q, k, v = q * valid, k * valid, v * valid
    beta = _head_stack(bfull, B_LANE, 1)
    gcol = _head_stack(G, A_LANE, 1)
    egc = _head_stack(expG, A_LANE, 1)
    kdf = _head_stack(kdfac, A_LANE, 1)
    grow = jnp.stack([GT[A_LANE + h:A_LANE + h + 1, :C] for h in range(GDN_HEADS)], axis=0)
    decay = jnp.exp(jnp.where(row >= col, gcol - grow, NEG))
    kb = k * beta
    A = jnp.where(row > col, _bdot_nt(kb, k) * decay, 0.0)
    N = _unit_lower_inverse_minus_eye(A, C)
    vb = v * beta
    kbg = kb * egc
    U = vb + _bdot(N, vb)
    Wm = kbg + _bdot(N, kbg)
    QK = _bdot_nt(q, k) * decay
    QG = q * egc
    KD = k * kdf
    for h in range(GDN_HEADS):
        sl = slice(h * HEAD_DIM, (h + 1) * HEAD_DIM)
        u_ref[:, sl] = U[h]
        w_ref[:, sl] = Wm[h].astype(w_ref.dtype)
        qk_ref[:, h * C:(h + 1) * C] = QK[h].astype(qk_ref.dtype)
        qg_ref[:, sl] = QG[h].astype(qg_ref.dtype)
        kd_ref[:, sl] = KD[h].astype(kd_ref.dtype)


def gdn_prep(P, cw, alog_row, dtb_row, *, n_seq, C, bufz=None):
    M = P.shape[0]
    T = M // n_seq
    n = T // C
    sample = bufz is not None
    assert T % C == 0 and C % 8 == 0
    qkv_spec = pl.BlockSpec((C, 3 * GDN_WIDTH), lambda b, c: (b * n + c, 0))
    if sample:
        second = pl.BlockSpec((C, 3 * GDN_WIDTH), lambda b, c: (b * n + c, 0))
        second_arr = bufz
    else:
        second = pl.BlockSpec((8, 3 * GDN_WIDTH), lambda b, c: (jnp.maximum((b * n + c) * (C // 8) - 1, 0), 0))
        second_arr = P
    in_specs = [qkv_spec, second,
                pl.BlockSpec((C, HEAD_DIM), lambda b, c: (b * n + c, MISC0 // HEAD_DIM)),
                pl.BlockSpec((GDN_CONV, 3 * GDN_WIDTH), lambda b, c: (0, 0)),
                pl.BlockSpec((1, HEAD_DIM), lambda b, c: (0, 0)),
                pl.BlockSpec((1, HEAD_DIM), lambda b, c: (0, 0))]
    row_spec = pl.BlockSpec((C, GDN_WIDTH), lambda b, c: (b * n + c, 0))
    out_specs = [row_spec, row_spec, row_spec, row_spec,
                 pl.BlockSpec((C, GDN_HEADS * C), lambda b, c: (b * n + c, 0)),
                 pl.BlockSpec((None, GDN_HEADS, HEAD_DIM), lambda b, c: (b * n + c, 0, 0))]
    out_shape = [jax.ShapeDtypeStruct((M, GDN_WIDTH), F32),
                 jax.ShapeDtypeStruct((M, GDN_WIDTH), BF16),
                 jax.ShapeDtypeStruct((M, GDN_WIDTH), BF16),
                 jax.ShapeDtypeStruct((M, GDN_WIDTH), BF16),
                 jax.ShapeDtypeStruct((M, GDN_HEADS * C), BF16),
                 jax.ShapeDtypeStruct((n_seq * n, GDN_HEADS, HEAD_DIM), F32)]
    return pl.pallas_call(
        functools.partial(_gdn_prep_kernel, C=C, sample=sample),
        grid=(n_seq, n),
        in_specs=in_specs,
        out_specs=out_specs,
        out_shape=out_shape,
        compiler_params=_cparams("parallel", "arbitrary"),
        name="gdn_prep_sample" if sample else "gdn_prep_prompt",
    )(P, second_arr, P, cw, alog_row, dtb_row)


def _gdn_scan_kernel(u_ref, w_ref, qg_ref, kd_ref, qk_ref, gl_ref, z_ref, ng_ref, s0_ref, o_ref, sfin_ref, S_ref, *, C):
    c = pl.program_id(1)

    @pl.when(c == 0)
    def _():
        S_ref[...] = s0_ref[...]

    S = S_ref[...]
    Sb = S.astype(BF16)
    v_new = _head_stack(u_ref[...], 0) - _bdot(_head_stack(w_ref[...], 0), Sb)
    vb = v_new.astype(BF16)
    o = _bdot(_head_stack(qg_ref[...], 0), Sb) + _bdot(_head_stack(qk_ref[...], 0, C), vb)
    gl = jnp.stack([gl_ref[h:h + 1, :] for h in range(GDN_HEADS)], axis=0)
    S_ref[...] = S * gl + _bdot_tn(_head_stack(kd_ref[...], 0), vb)
    o = o * lax.rsqrt(jnp.mean(o * o, -1, keepdims=True) + NORM_EPS) * ng_ref[...]
    o = o * jax.nn.silu(_head_stack(z_ref[...], 0))
    for h in range(GDN_HEADS):
        o_ref[:, h * HEAD_DIM:(h + 1) * HEAD_DIM] = o[h].astype(o_ref.dtype)

    @pl.when(c == pl.num_programs(1) - 1)
    def _():
        sfin_ref[...] = S_ref[...]


def gdn_scan(prep, P, norm_g, s0, *, n_seq, C):
    u, w, qg, kd, qk, gl = prep
    M = u.shape[0]
    T = M // n_seq
    n = T // C
    row_spec = pl.BlockSpec((C, GDN_WIDTH), lambda b, c: (b * n + c, 0))
    st_spec = pl.BlockSpec((None, GDN_HEADS, HEAD_DIM, HEAD_DIM), lambda b, c: (b, 0, 0, 0))
    return pl.pallas_call(
        functools.partial(_gdn_scan_kernel, C=C),
        grid=(n_seq, n),
        in_specs=[row_spec, row_spec, row_spec, row_spec,
                  pl.BlockSpec((C, GDN_HEADS * C), lambda b, c: (b * n + c, 0)),
                  pl.BlockSpec((None, GDN_HEADS, HEAD_DIM), lambda b, c: (b * n + c, 0, 0)),
                  pl.BlockSpec((C, GDN_WIDTH), lambda b, c: (b * n + c, Z0 // GDN_WIDTH)),
                  pl.BlockSpec((1, HEAD_DIM), lambda b, c: (0, 0)),
                  st_spec],
        out_specs=[row_spec, st_spec],
        out_shape=[jax.ShapeDtypeStruct((M, GDN_WIDTH), BF16),
                   jax.ShapeDtypeStruct((n_seq, GDN_HEADS, HEAD_DIM, HEAD_DIM), F32)],
        scratch_shapes=[pltpu.VMEM((GDN_HEADS, HEAD_DIM, HEAD_DIM), F32)],
        compiler_params=_cparams("parallel", "arbitrary"),
        name="gdn_scan",
    )(u, w, qg, kd, qk, gl, P, norm_g, s0)


PAGES_PER_STEP = 32
BLOCK_ROWS = NSA_BLOCK * 4
PAGE_ROWS = 2 * BLOCK_ROWS


def _compress_sample_kernel(pt_ref, cache_ref, pe_ref, w1_ref, w2_ref, out_ref, xbuf, sem, *, layer, n_pages, pps):
    b = pl.program_id(0)
    gi = pl.program_id(1)
    ng = pl.num_programs(1)
    step = b * ng + gi
    nsteps = pl.num_programs(0) * ng
    slot = step % 2

    def copies(st, sl):
        bb = st // ng
        g0 = (st % ng) * pps
        out = []
        for jl in range(pps):
            page = pt_ref[bb * n_pages + g0 + jl]
            for half in range(2):
                out.append(pltpu.make_async_copy(
                    cache_ref.at[layer, page, pl.ds(half * BLOCK_ROWS, BLOCK_ROWS), :],
                    xbuf.at[sl, :, 2 * jl + half, :],
                    sem.at[sl]))
        return out

    @pl.when(step == 0)
    def _():
        for cp in copies(step, slot):
            cp.start()

    @pl.when(step + 1 < nsteps)
    def _():
        for cp in copies(step + 1, 1 - slot):
            cp.start()

    for cp in copies(step, slot):
        cp.wait()

    nblk = 2 * pps

    _compress_rows(lambda gc, p: xbuf[slot, p * 4 + gc], pe_ref, w1_ref, w2_ref, out_ref, nblk)


def compress_sample(page_table, cache_rows, pe_t, w1, w2, *, layer):
    Bs, n_pages = page_table.shape
    pps = min(PAGES_PER_STEP, n_pages)
    assert n_pages % pps == 0
    ng = n_pages // pps
    nblk = 2 * pps
    grid_spec = pltpu.PrefetchScalarGridSpec(
        num_scalar_prefetch=1,
        grid=(Bs, ng),
        in_specs=[pl.BlockSpec(memory_space=pl.ANY),
                  pl.BlockSpec(pe_t.shape, lambda b, g, pt: (0, 0, 0)),
                  pl.BlockSpec(w1.shape, lambda b, g, pt: (0, 0, 0)),
                  pl.BlockSpec(w2.shape, lambda b, g, pt: (0, 0, 0))],
        out_specs=pl.BlockSpec((None, None, 2, NSA_KV_HEADS * nblk, HEAD_DIM), lambda b, g, pt: (b, g, 0, 0, 0)),
        scratch_shapes=[pltpu.VMEM((2, BLOCK_ROWS, nblk, HEAD_DIM), F32),
                        pltpu.SemaphoreType.DMA((2,))],
    )
    return pl.pallas_call(
        functools.partial(_compress_sample_kernel, layer=layer, n_pages=n_pages, pps=pps),
        grid_spec=grid_spec,
        out_shape=jax.ShapeDtypeStruct((Bs, ng, 2, NSA_KV_HEADS * nblk, HEAD_DIM), F32),
        compiler_params=_cparams("arbitrary", "arbitrary"),
        name="compress_sample",
    )(page_table.reshape(-1), cache_rows, pe_t, w1, w2)


def _stack_heads(q_ref, g, scale):
    return jnp.concatenate(
        [q_ref[:, (g * NSA_GROUP + r) * HEAD_DIM:(g * NSA_GROUP + r + 1) * HEAD_DIM] * scale for r in range(NSA_GROUP)],
        axis=0).astype(BF16)


def _stacked_row_info(past):
    rows = NSA_GROUP * ZROWS
    ridx = lax.broadcasted_iota(jnp.int32, (rows, 1), 0)
    zrow = ridx % ZROWS
    qpos = past + zrow - ZTOK0
    slope = jnp.zeros((rows, 1), F32)
    return ridx, zrow, qpos, slope


def _slope_col(g):
    ridx = lax.broadcasted_iota(jnp.int32, (NSA_GROUP * ZROWS, 1), 0)
    sl = jnp.zeros((NSA_GROUP * ZROWS, 1), F32)
    for r in range(NSA_GROUP):
        sl = jnp.where(ridx // ZROWS == r, SLOPES[g][r], sl)
    return sl


def _nsa_sample_a_kernel(q_ref, kvw_ref, ck_ref, cv_ref, win_ref, ocmp_ref, owin_ref, sel_ref, wout_ref, *, past, nc, n_blocks, seq):
    scale = HEAD_DIM ** -0.5
    _, zrow, qpos, _ = _stacked_row_info(past)
    wb = win_ref.shape[0] // 4
    lanes_pad = sel_ref.shape[-1]
    sel_lanes = ((n_blocks + HEAD_DIM - 1) // HEAD_DIM) * HEAD_DIM
    top_n = min(NSA_TOP_N, n_blocks)
    for g in range(NSA_KV_HEADS):
        q4 = _stack_heads(q_ref, g, scale)
        slope = _slope_col(g)
        cpos = lax.broadcasted_iota(jnp.int32, (1, nc), 1) * NSA_BLOCK + (NSA_BLOCK - 1)
        vm = cpos <= qpos
        s = jnp.where(vm, _dot_nt(q4, ck_ref[g]) + slope * (cpos - past).astype(F32), NEG)
        e = jnp.exp(s - jnp.max(s, axis=-1, keepdims=True))
        p = jnp.where(vm, e / jnp.sum(e, axis=-1, keepdims=True), 0.0)
        ocmp_ref[g] = _dot(p, cv_ref[g])
        imp = p[0:ZROWS]
        for r in range(1, NSA_GROUP):
            imp = imp + p[r * ZROWS:(r + 1) * ZROWS]
        imp = jnp.concatenate([imp, jnp.zeros((ZROWS, sel_lanes - nc), F32)], axis=1)
        rank = _rank_desc(_block_scores(imp, qpos[0:ZROWS], n_blocks), n_blocks)
        lane = lax.broadcasted_iota(jnp.int32, rank.shape, 1)
        olane = lax.broadcasted_iota(jnp.int32, (ZROWS, lanes_pad), 1)
        out = jnp.zeros((ZROWS, lanes_pad), jnp.int32)
        for t in range(top_n):
            idx = jnp.sum(jnp.where(rank == float(t), lane.astype(F32), 0.0), axis=-1, keepdims=True)
            out = jnp.where(olane == t, idx.astype(jnp.int32), out)
        sel_ref[g] = out
        kold = win_ref[pl.ds(g * 2, wb, stride=4), :]
        vold = win_ref[pl.ds(g * 2 + 1, wb, stride=4), :]
        knew = kvw_ref[:, g * 2 * HEAD_DIM:(g * 2 + 1) * HEAD_DIM]
        vnew = kvw_ref[:, (g * 2 + 1) * HEAD_DIM:(g * 2 + 2) * HEAD_DIM]
        kpos_o = past - wb + lax.broadcasted_iota(jnp.int32, (1, wb), 1)
        kz = lax.broadcasted_iota(jnp.int32, (1, ZROWS), 1)
        kpos_n = past + kz - ZTOK0
        d_o = qpos - kpos_o
        d_n = qpos - kpos_n
        ok_o = (kpos_o >= 0) & (d_o >= 0) & (d_o < NSA_WINDOW)
        ok_n = (kz >= ZTOK0) & (kz < ZTOK0 + seq) & (d_n >= 0) & (d_n < NSA_WINDOW)
        s_o = jnp.where(ok_o, _dot_nt(q4, kold) + slope * (kpos_o - past).astype(F32), NEG)
        s_n = jnp.where(ok_n, _dot_nt(q4, knew) + slope * (kpos_n - past).astype(F32), NEG)
        m = jnp.maximum(jnp.max(s_o, axis=-1, keepdims=True), jnp.max(s_n, axis=-1, keepdims=True))
        p_o = jnp.exp(s_o - m)
        p_n = jnp.exp(s_n - m)
        den = jnp.sum(p_o, axis=-1, keepdims=True) + jnp.sum(p_n, axis=-1, keepdims=True)
        owin_ref[g] = (_dot(p_o, vold) + _dot(p_n, vnew)) / den
    keep = (wb - seq) * 4
    wout_ref[0:keep, :] = win_ref[seq * 4:wb * 4, :]
    ridx = lax.broadcasted_iota(jnp.int32, (seq * 4, 1), 0)
    new_rows = jnp.zeros((seq * 4, HEAD_DIM), F32)
    for t in range(seq):
        for gc in range(4):
            new_rows = jnp.where(ridx == t * 4 + gc,
                                 kvw_ref[ZTOK0 + t:ZTOK0 + t + 1, gc * HEAD_DIM:(gc + 1) * HEAD_DIM], new_rows)
    wout_ref[keep:wb * 4, :] = new_rows


def nsa_sample_a(Ps, ckv, cache_win_rows, *, layer, past, seq):
    Bs = Ps.shape[0] // ZROWS
    nc = ckv.shape[3]
    wrows = cache_win_rows.shape[2]
    n_blocks = -(-(past + seq) // NSA_BLOCK)
    assert 1 <= seq <= 4 and (seq * 4) % 8 == 0 and wrows % 8 == 0
    rows = NSA_GROUP * ZROWS
    big = pl.BlockSpec((None, NSA_KV_HEADS, rows, HEAD_DIM), lambda b: (b, 0, 0, 0))
    return pl.pallas_call(
        functools.partial(_nsa_sample_a_kernel, past=past, nc=nc, n_blocks=n_blocks, seq=seq),
        grid=(Bs,),
        in_specs=[pl.BlockSpec((ZROWS, NSA_WIDTH), lambda b: (b, Q0 // NSA_WIDTH)),
                  pl.BlockSpec((ZROWS, KV_COLS), lambda b: (b, KVW0 // KV_COLS)),
                  pl.BlockSpec((None, None, NSA_KV_HEADS, nc, HEAD_DIM), lambda b: (b, 0, 0, 0, 0)),
                  pl.BlockSpec((None, None, NSA_KV_HEADS, nc, HEAD_DIM), lambda b: (b, 1, 0, 0, 0)),
                  pl.BlockSpec((None, None, wrows, HEAD_DIM), lambda b: (layer, b, 0, 0))],
        out_specs=[big, big,
                   pl.BlockSpec((None, NSA_KV_HEADS, ZROWS, HEAD_DIM), lambda b: (b, 0, 0, 0)),
                   pl.BlockSpec((None, wrows, HEAD_DIM), lambda b: (b, 0, 0))],
        out_shape=[jax.ShapeDtypeStruct((Bs, NSA_KV_HEADS, rows, HEAD_DIM), F32),
                   jax.ShapeDtypeStruct((Bs, NSA_KV_HEADS, rows, HEAD_DIM), F32),
                   jax.ShapeDtypeStruct((Bs, NSA_KV_HEADS, ZROWS, HEAD_DIM), jnp.int32),
                   jax.ShapeDtypeStruct((Bs, wrows, HEAD_DIM), F32)],
        compiler_params=_cparams("parallel"),
        name="nsa_sample_a",
    )(Ps, Ps, ckv, ckv, cache_win_rows)


def _nsa_sample_b_kernel(pt_ref, sel_ref, q_ref, kvs_ref, misc_ref, ocmp_ref, owin_ref, cache_ref, o_ref, kvbuf, sem,
                         *, layer, past, seq, n_past_blocks, n_pages):
    b = pl.program_id(0)
    g = pl.program_id(1)
    ngrp = pl.num_programs(1)
    step = b * ngrp + g
    nsteps = pl.num_programs(0) * ngrp
    slot = step % 2

    def copies(st, sl):
        bb = st // ngrp
        gg = st % ngrp
        out = []
        for t in range(seq):
            for j in range(NSA_TOP_N):
                blk = jnp.minimum(sel_ref[((bb * seq + t) * NSA_KV_HEADS + gg) * NSA_TOP_N + j], n_past_blocks - 1)
                page = pt_ref[bb * n_pages + blk // 2]
                row0 = pl.multiple_of((blk % 2) * BLOCK_ROWS, BLOCK_ROWS)
                out.append(pltpu.make_async_copy(
                    cache_ref.at[layer, page, pl.ds(row0, BLOCK_ROWS), :],
                    kvbuf.at[sl, t * NSA_TOP_N + j],
                    sem.at[sl]))
        return out

    @pl.when(step == 0)
    def _():
        for cp in copies(step, slot):
            cp.start()

    @pl.when(step + 1 < nsteps)
    def _():
        for cp in copies(step + 1, 1 - slot):
            cp.start()

    for cp in copies(step, slot):
        cp.wait()

    scale = HEAD_DIM ** -0.5
    nkeys = NSA_TOP_N * NSA_BLOCK
    _, zrow, qpos, _ = _stacked_row_info(past)
    q4 = jnp.where(g == 0, _stack_heads(q_ref, 0, scale), _stack_heads(q_ref, 1, scale))
    slope = jnp.where(g == 0, _slope_col(0), _slope_col(1))

    kn = jnp.where(g == 0, kvs_ref[:, 0:HEAD_DIM], kvs_ref[:, 2 * HEAD_DIM:3 * HEAD_DIM])
    vn = jnp.where(g == 0, kvs_ref[:, HEAD_DIM:2 * HEAD_DIM], kvs_ref[:, 3 * HEAD_DIM:4 * HEAD_DIM])
    kz = lax.broadcasted_iota(jnp.int32, (1, ZROWS), 1)
    kpos_n = past + kz - ZTOK0
    ok_n = (kz >= ZTOK0) & (kz < ZTOK0 + seq) & (kpos_n <= qpos)
    s_n = jnp.where(ok_n, _dot_nt(q4, kn) + slope * (kpos_n - past).astype(F32), NEG)

    lane = lax.broadcasted_iota(jnp.int32, (1, nkeys), 1)
    slot_of_lane = lane // NSA_BLOCK
    s_g = jnp.full((NSA_GROUP * ZROWS, nkeys), NEG, F32)
    v_tok = []
    for t in range(seq):
        blk_of_lane = jnp.zeros((1, nkeys), jnp.int32)
        for j in range(NSA_TOP_N):
            blk = sel_ref[((b * seq + t) * NSA_KV_HEADS + g) * NSA_TOP_N + j]
            blk_of_lane = jnp.where(slot_of_lane == j, blk, blk_of_lane)
        k_t = jnp.concatenate([kvbuf[slot, t * NSA_TOP_N + j, pl.ds(g * 2, NSA_BLOCK, stride=4), :]
                               for j in range(NSA_TOP_N)], axis=0)
        v_tok.append(jnp.concatenate([kvbuf[slot, t * NSA_TOP_N + j, pl.ds(g * 2 + 1, NSA_BLOCK, stride=4), :]
                                      for j in range(NSA_TOP_N)], axis=0))
        kpos = blk_of_lane * NSA_BLOCK + lane % NSA_BLOCK
        st = _dot_nt(q4, k_t) + slope * (kpos - past).astype(F32)
        mine = (zrow == ZTOK0 + t) & (blk_of_lane < n_past_blocks) & (kpos <= qpos)
        s_g = jnp.where(mine, st, s_g)

    m = jnp.maximum(jnp.max(s_g, axis=-1, keepdims=True), jnp.max(s_n, axis=-1, keepdims=True))
    p_g = jnp.exp(s_g - m)
    p_n = jnp.exp(s_n - m)
    den = jnp.sum(p_g, axis=-1, keepdims=True) + jnp.sum(p_n, axis=-1, keepdims=True)
    pv = _dot(p_n, vn)
    for t in range(seq):
        pv = pv + jnp.where(zrow == ZTOK0 + t, _dot(p_g, v_tok[t]), 0.0)
    o_slc = pv / den

    gates = jax.nn.sigmoid(misc_ref[...])
    o_cmp = ocmp_ref[...]
    o_win = owin_ref[...]
    for r in range(NSA_GROUP):
        rs = slice(r * ZROWS, (r + 1) * ZROWS)
        outs = []
        for gg in range(NSA_KV_HEADS):
            c = gg * NSA_GROUP * 3 + r * 3 + GATE_LANE
            outs.append(gates[:, c:c + 1] * o_cmp[rs] + gates[:, c + 1:c + 2] * o_slc[rs]
                        + gates[:, c + 2:c + 3] * o_win[rs])
        o_ref[:, r * HEAD_DIM:(r + 1) * HEAD_DIM] = jnp.where(g == 0, outs[0], outs[1]).astype(o_ref.dtype)


def nsa_sample_b(page_table, sel, Ps, ocmp, owin, cache_rows, *, layer, past, seq):
    Bs, n_pages = page_table.shape
    rows = NSA_GROUP * ZROWS
    n_past_blocks = past // NSA_BLOCK
    gw = NSA_GROUP * HEAD_DIM

    big = pl.BlockSpec((None, None, rows, HEAD_DIM), lambda b, g, pt, sl: (b, g, 0, 0))
    in_specs = [pl.BlockSpec((ZROWS, NSA_WIDTH), lambda b, g, pt, sl: (b, Q0 // NSA_WIDTH)),
                pl.BlockSpec((ZROWS, KV_COLS), lambda b, g, pt, sl: (b, KVS0 // KV_COLS)),
                pl.BlockSpec((ZROWS, HEAD_DIM), lambda b, g, pt, sl: (b, MISC0 // HEAD_DIM)),
                big, big,
                pl.BlockSpec(memory_space=pl.ANY)]
    grid_spec = pltpu.PrefetchScalarGridSpec(
        num_scalar_prefetch=2,
        grid=(Bs, NSA_KV_HEADS),
        in_specs=in_specs,
        out_specs=pl.BlockSpec((ZROWS, gw), lambda b, g, pt, sl: (b, g)),
        scratch_shapes=[pltpu.VMEM((2, seq * NSA_TOP_N, BLOCK_ROWS, HEAD_DIM), F32),
                        pltpu.SemaphoreType.DMA((2,))],
    )
    return pl.pallas_call(
        functools.partial(_nsa_sample_b_kernel, layer=layer, past=past, seq=seq, n_past_blocks=n_past_blocks,
                          n_pages=n_pages),
        grid_spec=grid_spec,
        out_shape=jax.ShapeDtypeStruct((Bs * ZROWS, NSA_WIDTH), BF16),
        compiler_params=_cparams("arbitrary", "arbitrary"),
        name="nsa_sample_b",
    )(page_table.reshape(-1), sel, Ps, Ps, Ps, ocmp, owin, cache_rows)


def _reorder_w_in(w_in):
    q, kvc, kvs, kvw = w_in[..., 0:1024], w_in[..., 1024:1536], w_in[..., 1536:2048], w_in[..., 2048:2560]
    gates, qkv, z, ab = w_in[..., 2560:2584], w_in[..., 2584:5656], w_in[..., 5656:6680], w_in[..., 6680:6696]
    pad = jnp.zeros(w_in.shape[:-1] + (IN_COLS_PAD - MISC0 - 40,), w_in.dtype)
    return jnp.concatenate([qkv, q, z, kvc, kvs, kvw, gates, ab, pad], axis=-1).astype(BF16)


def _lane_row(vec, lane0):
    return jnp.zeros((1, HEAD_DIM), F32).at[0, lane0:lane0 + vec.shape[0]].set(vec.astype(F32))


def _to_zrows(x, first_row):
    Bs, n, C = x.shape
    z = jnp.zeros((Bs, ZROWS, C), x.dtype).at[:, first_row:first_row + n].set(x)
    return z.reshape(Bs * ZROWS, C)


def kernel(x_prompt, x_sample, cache_cmp_kv, cache_slc_kv, cache_win_kv, state_gdn, state_gdn_conv, state_ffn_conv, page_table, w_in, w_o, cmp_pe, cmp_w1, cmp_w2, gdn_conv_w, gdn_a_log, gdn_dt_bias, gdn_norm_g, ln1_g, ln1_b, ffn_w_up, ffn_conv_w, ffn_w_down, ln2_g, ln2_b):
    B, S, D = x_prompt.shape
    Bs, seq, _ = x_sample.shape
    depth = w_in.shape[0]
    n_pool, page = cache_cmp_kv.shape[1], cache_cmp_kv.shape[2]
    n_pages = page_table.shape[1]
    past = n_pages * page
    wb = cache_win_kv.shape[2]
    d_ff = ffn_w_down.shape[1]
    alpha = (2 * depth) ** 0.25
    assert page == 2 * NSA_BLOCK

    w_in_r = _reorder_w_in(w_in)
    w_o_b = w_o.astype(BF16)
    w_up_b = jnp.stack([ffn_w_up[..., :d_ff].reshape(depth, D, d_ff // HEAD_DIM, HEAD_DIM),
                        ffn_w_up[..., d_ff:].reshape(depth, D, d_ff // HEAD_DIM, HEAD_DIM)],
                       axis=3).reshape(depth, D, 2 * d_ff).astype(BF16)
    w_down_b = ffn_w_down.astype(BF16)
    w1_b = cmp_w1.astype(BF16)
    w2_b = cmp_w2.astype(BF16)
    pe_t = jnp.swapaxes(cmp_pe, 1, 2)
    cache_cmp_rows = cache_cmp_kv.reshape(depth, n_pool, PAGE_ROWS, HEAD_DIM)
    cache_slc_rows = cache_slc_kv.reshape(depth, n_pool, PAGE_ROWS, HEAD_DIM)
    cache_win_rows = cache_win_kv.reshape(depth, Bs, wb * 4, HEAD_DIM)

    xp = x_prompt.reshape(B * S, D)
    xs = _to_zrows(x_sample, ZTOK0)
    xp_b, xs_b = xp, xs
    tm = 512 if (B * S) % 512 == 0 else B * S
    tm_in = 1024 if (B * S) % 1024 == 0 else tm
    tk_down = 1408 if d_ff % 1408 == 0 else 512
    tq = 128
    p_out = [[] for _ in range(6)]
    s_out = [[] for _ in range(6)]
    zero_state = jnp.zeros((B, GDN_HEADS, HEAD_DIM, HEAD_DIM), F32)

    for l in range(depth):
        alog_row = _lane_row(gdn_a_log[l], A_LANE)
        dtb_row = _lane_row(gdn_dt_bias[l], A_LANE)
        ng = gdn_norm_g[l].reshape(1, HEAD_DIM)
        g1, b1 = ln1_g[l].reshape(1, D), ln1_b[l].reshape(1, D)
        g2, b2 = ln2_g[l].reshape(1, D), ln2_b[l].reshape(1, D)

        P = matmul(xp_b, w_in_r[l], tm=tm_in, tn=1024)
        pkv = P[:, KVS0:KVS0 + 2 * KV_COLS].astype(BF16).reshape(B, S, 4, 2, HEAD_DIM)
        kmat = pkv[:, :, :, 0, :].reshape(B * S, 4 * HEAD_DIM)
        vt = jnp.transpose(pkv[:, :, :, 1, :], (0, 2, 3, 1))
        ckv = compress_prompt(P, pe_t[l], w1_b[l], w2_b[l], n_seq=B)
        ckv = ckv.reshape(B, 2, NSA_KV_HEADS, S // NSA_BLOCK, HEAD_DIM).astype(BF16)
        o_nsa = nsa_prompt(P, kmat, vt, ckv[:, 0], jnp.swapaxes(ckv[:, 1], -1, -2), n_seq=B, tq=tq,
                           tk=256 if S % 256 == 0 else tq)
        prep = gdn_prep(P, gdn_conv_w[l], alog_row, dtb_row, n_seq=B, C=GDN_CHUNK)
        o_gdn, s_fin = gdn_scan(prep, P, ng, zero_state, n_seq=B, C=GDN_CHUNK)
        h, h_b = proj_ln(o_nsa, o_gdn, w_o_b[l], xp, g1, b1, alpha=alpha, tm=256)
        act, utail = ffn_up_prompt(h_b, w_up_b[l], ffn_conv_w[l], n_seq=B, tm=tm, tf=tk_down)
        xp, xp_b = ffn_down_ln(act, w_down_b[l], h, g2, b2, alpha=alpha, tm=tm, tk=tk_down)

        P3 = P.reshape(B, S, IN_COLS_PAD)
        p_out[0].append(P3[:, :, KVC0:KVC0 + KV_COLS].reshape(B, S, NSA_KV_HEADS, 2, HEAD_DIM))
        p_out[1].append(P3[:, :, KVS0:KVS0 + KV_COLS].reshape(B, S, NSA_KV_HEADS, 2, HEAD_DIM))
        wn = min(NSA_WINDOW, S)
        p_out[2].append(P3[:, S - wn:, KVW0:KVW0 + KV_COLS].reshape(B, wn, NSA_KV_HEADS, 2, HEAD_DIM))
        p_out[3].append(s_fin)
        p_out[4].append(P3[:, S - (GDN_CONV - 1):, QKV0:QKV0 + 3 * GDN_WIDTH])
        p_out[5].append(utail[:, 8 - (FFN_CONV - 1):, :])

        Ps = matmul(xs_b, w_in_r[l], tm=Bs * ZROWS, tn=1024)
        ckv_s = compress_sample(page_table, cache_cmp_rows, pe_t[l], w1_b[l], w2_b[l], layer=l)
        ng_grp = ckv_s.shape[1]
        nblk_step = ckv_s.shape[3] // NSA_KV_HEADS
        ckv_s = ckv_s.reshape(Bs, ng_grp, 2, NSA_KV_HEADS, nblk_step, HEAD_DIM)
        ckv_s = jnp.transpose(ckv_s, (0, 2, 3, 1, 4, 5)).reshape(Bs, 2, NSA_KV_HEADS, ng_grp * nblk_step, HEAD_DIM)
        ocmp, owin, sel, win_new = nsa_sample_a(Ps, ckv_s, cache_win_rows, layer=l, past=past, seq=seq)
        sel_flat = jnp.transpose(sel[:, :, ZTOK0:ZTOK0 + seq, :NSA_TOP_N], (0, 2, 1, 3)).reshape(-1)
        o_nsa_s = nsa_sample_b(page_table, sel_flat, Ps, ocmp, owin, cache_slc_rows, layer=l, past=past, seq=seq)
        gbufz = _to_zrows(state_gdn_conv[l], ZTOK0 - (GDN_CONV - 1))
        prep_s = gdn_prep(Ps, gdn_conv_w[l], alog_row, dtb_row, n_seq=Bs, C=ZROWS, bufz=gbufz)
        o_gdn_s, s_fin_s = gdn_scan(prep_s, Ps, ng, state_gdn[l], n_seq=Bs, C=ZROWS)
        hs, hs_b = proj_ln(o_nsa_s, o_gdn_s, w_o_b[l], xs, g1, b1, alpha=alpha, tm=Bs * ZROWS)
        fbufz = _to_zrows(state_ffn_conv[l], ZTOK0 - (FFN_CONV - 1))
        act_s, u_s = ffn_up_sample(hs_b, w_up_b[l], ffn_conv_w[l], fbufz, tf=tk_down)
        xs, xs_b = ffn_down_ln(act_s, w_down_b[l], hs, g2, b2, alpha=alpha, tm=Bs * ZROWS, tk=tk_down)

        Ps3 = Ps.reshape(Bs, ZROWS, IN_COLS_PAD)[:, ZTOK0:ZTOK0 + seq]
        s_out[0].append(Ps3[:, :, KVC0:KVC0 + KV_COLS].reshape(Bs, seq, NSA_KV_HEADS, 2, HEAD_DIM))
        s_out[1].append(Ps3[:, :, KVS0:KVS0 + KV_COLS].reshape(Bs, seq, NSA_KV_HEADS, 2, HEAD_DIM))
        s_out[2].append(win_new.reshape(Bs, wb, NSA_KV_HEADS, 2, HEAD_DIM))
        s_out[3].append(s_fin_s)
        ext_g = jnp.concatenate([state_gdn_conv[l], Ps3[:, :, QKV0:QKV0 + 3 * GDN_WIDTH]], axis=1)
        s_out[4].append(ext_g[:, -(GDN_CONV - 1):])
        u3 = u_s.reshape(Bs, ZROWS, d_ff)[:, ZTOK0:ZTOK0 + seq]
        ext_f = jnp.concatenate([state_ffn_conv[l], u3], axis=1)
        s_out[5].append(ext_f[:, -(FFN_CONV - 1):])

    y_p = xp.reshape(B, S, D)
    y_s = xs.reshape(Bs, ZROWS, D)[:, ZTOK0:ZTOK0 + seq]
    return (y_p, y_s, *[jnp.stack(v, 0) for v in p_out], *[jnp.stack(v, 0) for v in s_out])
```

```python
import functools

import jax
import jax.numpy as jnp
import numpy as np
from jax import lax
from jax.experimental import pallas as pl
from jax.experimental.pallas import tpu as pltpu

F32 = jnp.float32
BF16 = jnp.bfloat16

HEAD_DIM = 128
NSA_HEADS = 8
NSA_KV_HEADS = 2
NSA_GROUP = 4
NSA_WIDTH = 1024
NSA_BLOCK = 64
NSA_TOP_N = 16
NSA_WINDOW = 512
GDN_HEADS = 8
GDN_WIDTH = 1024
GDN_CONV = 4
GDN_CHUNK = 64
FFN_CONV = 3
KV_COLS = 512
LN_EPS = 1e-5
NORM_EPS = 1e-6
NEG = -1e30
FORCE = 1e4
LOG2E = 1.4426950408889634
SLOPES = [[2.0 ** -(g * NSA_GROUP + r + 1) for r in range(NSA_GROUP)] for g in range(NSA_KV_HEADS)]

QKV0, Q0, Z0, KVC0, KVS0, KVW0, MISC0, IN_COLS_PAD = 0, 3072, 4096, 5120, 5632, 6144, 6656, 7168
GATE_LANE, A_LANE, B_LANE = 0, 24, 32

ZROWS = 16
ZTOK0 = 8

VMEM_LIMIT_BYTES = 56 * 1024 * 1024


def _cparams(*sem):
    return pltpu.CompilerParams(dimension_semantics=sem, vmem_limit_bytes=VMEM_LIMIT_BYTES)


def _dot(a, b):
    return jnp.dot(a.astype(BF16), b.astype(BF16), preferred_element_type=F32)


def _dot_nt(a, b):
    return lax.dot_general(a.astype(BF16), b.astype(BF16), (((1,), (1,)), ((), ())), preferred_element_type=F32)


def _dot_tn(a, b):
    return lax.dot_general(a.astype(BF16), b.astype(BF16), (((0,), (0,)), ((), ())), preferred_element_type=F32)


def _layer_norm(t, g, b):
    mu = jnp.mean(t, -1, keepdims=True)
    d = t - mu
    var = jnp.mean(d * d, -1, keepdims=True)
    return d * lax.rsqrt(var + LN_EPS) * g + b


def _mm_kernel(x_ref, w_ref, o_ref):
    o_ref[...] = _dot(x_ref[...], w_ref[...])


def matmul(x, w, *, tm, tn):
    M, K = x.shape
    _, N = w.shape
    assert M % tm == 0 and N % tn == 0
    return pl.pallas_call(
        _mm_kernel,
        grid=(M // tm, N // tn),
        in_specs=[pl.BlockSpec((tm, K), lambda i, j: (i, 0)),
                  pl.BlockSpec((K, tn), lambda i, j: (0, j))],
        out_specs=pl.BlockSpec((tm, tn), lambda i, j: (i, j)),
        out_shape=jax.ShapeDtypeStruct((M, N), F32),
        compiler_params=_cparams("parallel", "arbitrary"),
        name="in_proj",
    )(x, w)


def _proj_ln_kernel(a1_ref, a2_ref, w_ref, x_ref, g_ref, b_ref, o_ref, ob_ref, *, alpha):
    k1 = a1_ref.shape[1]
    acc = _dot(a1_ref[...], w_ref[:k1, :]) + _dot(a2_ref[...], w_ref[k1:, :])
    o = _layer_norm(alpha * x_ref[...] + acc, g_ref[...], b_ref[...])
    o_ref[...] = o
    ob_ref[...] = o.astype(ob_ref.dtype)


def proj_ln(a1, a2, w, x, g, b, *, alpha, tm):
    M, D = x.shape
    k1, k2 = a1.shape[1], a2.shape[1]
    assert M % tm == 0 and w.shape == (k1 + k2, D)
    return pl.pallas_call(
        functools.partial(_proj_ln_kernel, alpha=alpha),
        grid=(M // tm,),
        in_specs=[pl.BlockSpec((tm, k1), lambda i: (i, 0)),
                  pl.BlockSpec((tm, k2), lambda i: (i, 0)),
                  pl.BlockSpec((k1 + k2, D), lambda i: (0, 0)),
                  pl.BlockSpec((tm, D), lambda i: (i, 0)),
                  pl.BlockSpec((1, D), lambda i: (0, 0)),
                  pl.BlockSpec((1, D), lambda i: (0, 0))],
        out_specs=[pl.BlockSpec((tm, D), lambda i: (i, 0)), pl.BlockSpec((tm, D), lambda i: (i, 0))],
        out_shape=[jax.ShapeDtypeStruct((M, D), F32), jax.ShapeDtypeStruct((M, D), BF16)],
        compiler_params=_cparams("parallel"),
        name="out_proj_ln",
    )(a1, a2, w, x, g, b)


def _conv3(u, prev8, cw):
    ue = jnp.concatenate([prev8, u], axis=0)
    u1 = pltpu.roll(ue, 1, 0)[8:]
    u2 = pltpu.roll(ue, 2, 0)[8:]
    return cw[2:3, :] * u + cw[1:2, :] * u1 + cw[0:1, :] * u2


MXU_COLS = 256


def _col_chunks(width):
    return [slice(c0, min(c0 + MXU_COLS, width)) for c0 in range(0, width, MXU_COLS)]


def _ffn_up_prompt_kernel(h_ref, wu_ref, wv_ref, cw_ref, act_ref, tail_ref, carry_ref, *, tiles_per_seq):
    i = pl.program_id(1)
    h = h_ref[...]
    tm = h.shape[0]

    @pl.when(i % tiles_per_seq == 0)
    def _():
        carry_ref[...] = jnp.zeros_like(carry_ref)

    for cs in _col_chunks(act_ref.shape[1]):
        u = _dot(h, wu_ref[:, cs])
        v = _dot(h, wv_ref[:, cs])
        uc = _conv3(u, carry_ref[:, cs], cw_ref[:, cs])
        act_ref[:, cs] = (jax.nn.gelu(uc) * v).astype(act_ref.dtype)
        tail = u[tm - 8:, :]
        carry_ref[:, cs] = tail
        tail_ref[:, cs] = tail


def ffn_up_prompt(h, w_up, cw, *, n_seq, tm, tf):
    M, D = h.shape
    F = w_up.shape[1] // 2
    T = M // n_seq
    assert M % tm == 0 and F % tf == 0 and T % tm == 0 and tf % HEAD_DIM == 0
    nf = F // tf
    tps = T // tm
    return pl.pallas_call(
        functools.partial(_ffn_up_prompt_kernel, tiles_per_seq=tps),
        grid=(nf, M // tm),
        in_specs=[pl.BlockSpec((tm, D), lambda f, i: (i, 0)),
                  pl.BlockSpec((D, tf), lambda f, i: (0, f)),
                  pl.BlockSpec((D, tf), lambda f, i: (0, nf + f)),
                  pl.BlockSpec((FFN_CONV, tf), lambda f, i: (0, f))],
        out_specs=[pl.BlockSpec((tm, tf), lambda f, i: (i, f)),
                   pl.BlockSpec((None, 8, tf), lambda f, i: (i // tps, 0, f))],
        out_shape=[jax.ShapeDtypeStruct((M, F), BF16),
                   jax.ShapeDtypeStruct((n_seq, 8, F), F32)],
        scratch_shapes=[pltpu.VMEM((8, tf), F32)],
        compiler_params=_cparams("arbitrary", "arbitrary"),
        name="ffn_up_prompt",
    )(h, w_up, w_up, cw)


def _ffn_up_sample_kernel(h_ref, wu_ref, wv_ref, cw_ref, buf_ref, act_ref, u_ref):
    h = h_ref[...]
    zrow = lax.broadcasted_iota(jnp.int32, (h.shape[0], 1), 0) % ZROWS
    isbuf = (zrow >= ZTOK0 - (FFN_CONV - 1)) & (zrow < ZTOK0)
    for cs in _col_chunks(act_ref.shape[1]):
        u = _dot(h, wu_ref[:, cs])
        v = _dot(h, wv_ref[:, cs])
        uz = jnp.where(isbuf, buf_ref[:, cs], u)
        cw = cw_ref[:, cs]
        uc = cw[2:3, :] * uz + cw[1:2, :] * pltpu.roll(uz, 1, 0) + cw[0:1, :] * pltpu.roll(uz, 2, 0)
        act_ref[:, cs] = (jax.nn.gelu(uc) * v).astype(act_ref.dtype)
        u_ref[:, cs] = u


def ffn_up_sample(h, w_up, cw, bufz, *, tf):
    M, D = h.shape
    F = w_up.shape[1] // 2
    nf = F // tf
    return pl.pallas_call(
        _ffn_up_sample_kernel,
        grid=(nf,),
        in_specs=[pl.BlockSpec((M, D), lambda f: (0, 0)),
                  pl.BlockSpec((D, tf), lambda f: (0, f)),
                  pl.BlockSpec((D, tf), lambda f: (0, nf + f)),
                  pl.BlockSpec((FFN_CONV, tf), lambda f: (0, f)),
                  pl.BlockSpec((M, tf), lambda f: (0, f))],
        out_specs=[pl.BlockSpec((M, tf), lambda f: (0, f)),
                   pl.BlockSpec((M, tf), lambda f: (0, f))],
        out_shape=[jax.ShapeDtypeStruct((M, F), BF16),
                   jax.ShapeDtypeStruct((M, F), F32)],
        compiler_params=_cparams("parallel"),
        name="ffn_up_sample",
    )(h, w_up, w_up, cw, bufz)


def _ffn_down_ln_kernel(a_ref, w_ref, h_ref, g_ref, b_ref, o_ref, ob_ref, acc_ref, *, alpha):
    k = pl.program_id(1)

    @pl.when(k == 0)
    def _():
        acc_ref[...] = jnp.zeros_like(acc_ref)

    acc_ref[...] += _dot(a_ref[...], w_ref[...])

    @pl.when(k == pl.num_programs(1) - 1)
    def _():
        o = _layer_norm(alpha * h_ref[...] + acc_ref[...], g_ref[...], b_ref[...])
        o_ref[...] = o
        ob_ref[...] = o.astype(ob_ref.dtype)


def ffn_down_ln(a, w, h, g, b, *, alpha, tm, tk):
    M, Fd = a.shape
    D = w.shape[1]
    assert M % tm == 0 and Fd % tk == 0
    return pl.pallas_call(
        functools.partial(_ffn_down_ln_kernel, alpha=alpha),
        grid=(M // tm, Fd // tk),
        in_specs=[pl.BlockSpec((tm, tk), lambda i, k: (i, k)),
                  pl.BlockSpec((tk, D), lambda i, k: (k, 0)),
                  pl.BlockSpec((tm, D), lambda i, k: (i, 0)),
                  pl.BlockSpec((1, D), lambda i, k: (0, 0)),
                  pl.BlockSpec((1, D), lambda i, k: (0, 0))],
        out_specs=[pl.BlockSpec((tm, D), lambda i, k: (i, 0)), pl.BlockSpec((tm, D), lambda i, k: (i, 0))],
        out_shape=[jax.ShapeDtypeStruct((M, D), F32), jax.ShapeDtypeStruct((M, D), BF16)],
        scratch_shapes=[pltpu.VMEM((tm, D), F32)],
        compiler_params=_cparams("parallel", "arbitrary"),
        name="ffn_down_ln",
    )(a, w, h, g, b)


def _compress_rows(read_rows, pe_ref, w1_ref, w2_ref, out_ref, nblk):
    del nblk
    for c in range(2):
        parts = []
        for p in range(NSA_BLOCK):
            pe_row = pe_ref[c, p:p + 1, :]
            rows = [read_rows(g * 2 + c, p) + pe_row for g in range(NSA_KV_HEADS)]
            parts.append(jnp.concatenate(rows, axis=0).astype(BF16))
        hid = jnp.dot(jnp.concatenate(parts, axis=1), w1_ref[c], preferred_element_type=F32)
        out_ref[c] = _dot(jax.nn.silu(hid), w2_ref[c])


def _compress_prompt_kernel(x0, x1, x2, x3, pe_ref, w1_ref, w2_ref, out_ref, *, nblk):
    xs = [x0, x1, x2, x3]
    _compress_rows(lambda gc, p: xs[gc][:, p, :], pe_ref, w1_ref, w2_ref, out_ref, nblk)


def compress_prompt(P, pe_t, w1, w2, *, n_seq):
    M = P.shape[0]
    T = M // n_seq
    nblk = T // NSA_BLOCK
    P3 = P.reshape(n_seq * nblk, NSA_BLOCK, IN_COLS_PAD)
    cb0 = KVC0 // HEAD_DIM
    in_specs = [pl.BlockSpec((nblk, NSA_BLOCK, HEAD_DIM), (lambda b, gc=gc: (b, 0, cb0 + gc))) for gc in range(4)]
    in_specs += [pl.BlockSpec(pe_t.shape, lambda b: (0, 0, 0)),
                 pl.BlockSpec(w1.shape, lambda b: (0, 0, 0)),
                 pl.BlockSpec(w2.shape, lambda b: (0, 0, 0))]
    return pl.pallas_call(
        functools.partial(_compress_prompt_kernel, nblk=nblk),
        grid=(n_seq,),
        in_specs=in_specs,
        out_specs=pl.BlockSpec((None, 2, NSA_KV_HEADS * nblk, HEAD_DIM), lambda b: (b, 0, 0, 0)),
        out_shape=jax.ShapeDtypeStruct((n_seq, 2, NSA_KV_HEADS * nblk, HEAD_DIM), F32),
        compiler_params=_cparams("parallel"),
        name="compress_prompt",
    )(P3, P3, P3, P3, pe_t, w1, w2)


def _rank_desc(score, n_real):
    lane = lax.broadcasted_iota(jnp.int32, score.shape, 1)
    cnt = jnp.zeros(score.shape, F32)
    for i in range(n_real):
        col = score[:, i:i + 1]
        ge = jnp.where(col >= score, 1.0, 0.0)
        gt = jnp.where(col > score, 1.0, 0.0)
        cnt = cnt + jnp.where(lane > i, ge, gt)
    return cnt


def _block_scores(imp, qpos, n_blocks):
    bidx = lax.broadcasted_iota(jnp.int32, imp.shape, 1)
    cur = qpos // NSA_BLOCK
    sc = jnp.where(bidx == 0, FORCE, jnp.where(bidx == cur, FORCE, jnp.where(bidx == cur - 1, FORCE, imp)))
    sc = jnp.where(bidx * NSA_BLOCK <= qpos, sc, NEG)
    return jnp.where(bidx < n_blocks, sc, -jnp.inf)


def _rank_desc_t(score, n_real):
    L = score.shape[0]
    assert L % 8 == 0
    groups = [score[8 * a:8 * a + 8, :] for a in range(L // 8)]
    sub = lax.broadcasted_iota(jnp.int32, groups[0].shape, 0)
    cnts = [jnp.zeros(groups[0].shape, F32) for _ in groups]
    for i in range(n_real):
        row = score[i:i + 1, :]
        for a, x in enumerate(groups):
            if 8 * a > i:
                c = jnp.where(row >= x, 1.0, 0.0)
            elif 8 * a + 7 < i:
                c = jnp.where(row > x, 1.0, 0.0)
            else:
                c = jnp.where(sub + 8 * a > i, jnp.where(row >= x, 1.0, 0.0), jnp.where(row > x, 1.0, 0.0))
            cnts[a] = cnts[a] + c
    return jnp.concatenate(cnts, axis=0)


def _block_scores_t(imp, qpos, n_blocks):
    bidx = lax.broadcasted_iota(jnp.int32, imp.shape, 0)
    cur = qpos // NSA_BLOCK
    sc = jnp.where(bidx == 0, FORCE, jnp.where(bidx == cur, FORCE, jnp.where(bidx == cur - 1, FORCE, imp)))
    sc = jnp.where(bidx * NSA_BLOCK <= qpos, sc, NEG)
    return jnp.where(bidx < n_blocks, sc, -jnp.inf)


def _nsa_prompt_kernel(q_ref, misc_ref, k_ref, vts_ref, vtw_ref, ck_ref, cvt_ref, o_ref, m_ref, l_ref, acc_ref, flag_ref, *, tq, tk, nblk):
    i = pl.program_id(1)
    t0 = i * tq
    bpt = tk // NSA_BLOCK
    scale = HEAD_DIM ** -0.5 * LOG2E
    top_n = min(NSA_TOP_N, nblk)
    qpos_row = t0 + lax.broadcasted_iota(jnp.int32, (1, tq), 1)
    sub_k = lax.broadcasted_iota(jnp.int32, (tk, tq), 0)
    d_kq = sub_k - lax.broadcasted_iota(jnp.int32, (tk, tq), 1)
    sub_kf = sub_k.astype(F32)
    win_code = -2 * d_kq - (NSA_WINDOW - 1)
    e_rel = (lax.broadcasted_iota(jnp.int32, (tk, nblk), 1)
             - lax.broadcasted_iota(jnp.int32, (tk, nblk), 0) // NSA_BLOCK)
    gates_t = jax.nn.sigmoid(misc_ref[...]).T

    def reset():
        m_ref[...] = jnp.full(m_ref.shape, NEG, F32)
        l_ref[...] = jnp.zeros(l_ref.shape, F32)
        acc_ref[...] = jnp.zeros(acc_ref.shape, F32)

    def online_update(s4, vt):
        m_old = m_ref[...]
        m_new = jnp.maximum(m_old, jnp.max(s4, axis=0, keepdims=True))
        p = jnp.exp2(s4 - m_new)
        alpha = jnp.exp2(m_old - m_new)
        l_ref[...] = alpha * l_ref[...] + jnp.sum(p, axis=0, keepdims=True)
        acc_ref[...] = alpha * acc_ref[...] + jnp.dot(vt, p.astype(BF16), preferred_element_type=F32)
        m_ref[...] = m_new

    def masked_scores(s, valid, base, g):
        return jnp.concatenate(
            [jnp.where(valid, s[:, r * tq:(r + 1) * tq] + (SLOPES[g][r] * LOG2E) * base, NEG)
             for r in range(NSA_GROUP)], axis=1)

    for g in range(NSA_KV_HEADS):
        q4t = jnp.concatenate(
            [(q_ref[:, (g * NSA_GROUP + r) * HEAD_DIM:(g * NSA_GROUP + r + 1) * HEAD_DIM] * scale).T
             for r in range(NSA_GROUP)], axis=1).astype(BF16)
        sc = jnp.dot(ck_ref[g], q4t, preferred_element_type=F32)
        cpos = lax.broadcasted_iota(jnp.int32, (nblk, tq), 0) * NSA_BLOCK + (NSA_BLOCK - 1)
        vm = cpos <= qpos_row
        cposrel = (cpos - t0).astype(F32)
        imp = jnp.zeros((nblk, tq), F32)
        parts = []
        for r in range(NSA_GROUP):
            s = jnp.where(vm, sc[:, r * tq:(r + 1) * tq] + (SLOPES[g][r] * LOG2E) * cposrel, NEG)
            e = jnp.exp2(s - jnp.max(s, axis=0, keepdims=True))
            p = jnp.where(vm, e / jnp.sum(e, axis=0, keepdims=True), 0.0)
            imp = imp + p
            parts.append(p)
        ocmp_t = jnp.dot(cvt_ref[g], jnp.concatenate(parts, axis=1).astype(BF16), preferred_element_type=F32)
        rank = _rank_desc_t(_block_scores_t(imp, qpos_row, nblk), nblk)
        sel_f = jnp.where(rank < top_n, 1.0, 0.0)
        sel_t = sel_f.astype(BF16)
        blk_any = jnp.max(sel_f, axis=1, keepdims=True)
        for jt in range(nblk // bpt):
            flag_ref[jt] = (jnp.max(blk_any[jt * bpt:(jt + 1) * bpt, :]) > 0.5).astype(jnp.int32)

        reset()

        def slc_body(j, carry):
            k0 = pl.multiple_of(j * tk, tk)

            @pl.when(flag_ref[j] > 0)
            def _():
                k = k_ref[pl.ds(k0, tk), g * HEAD_DIM:(g + 1) * HEAD_DIM]
                vt = vts_ref[g, :, pl.ds(k0, tk)]
                off = k0 - t0
                expand = jnp.where(e_rel == k0 // NSA_BLOCK, 1.0, 0.0).astype(BF16)
                chosen = jnp.dot(expand, sel_t, preferred_element_type=F32)
                valid = jnp.where(d_kq <= -off, chosen, 0.0) > 0.5
                s = jnp.dot(k, q4t, preferred_element_type=F32)
                online_update(masked_scores(s, valid, sub_kf + off.astype(F32), g), vt)

            return carry

        lax.fori_loop(0, (t0 + tq + tk - 1) // tk, slc_body, 0)
        oslc_t = acc_ref[...] / l_ref[...]

        reset()

        def win_body(j, carry):
            k0 = pl.multiple_of(j * tk, tk)
            k = k_ref[pl.ds(k0, tk), (NSA_KV_HEADS + g) * HEAD_DIM:(NSA_KV_HEADS + g + 1) * HEAD_DIM]
            vt = vtw_ref[g, :, pl.ds(k0, tk)]
            off = k0 - t0
            valid = jnp.abs(win_code - 2 * off) <= (NSA_WINDOW - 1)
            s = jnp.dot(k, q4t, preferred_element_type=F32)
            online_update(masked_scores(s, valid, sub_kf + off.astype(F32), g), vt)
            return carry

        lax.fori_loop(jnp.maximum(t0 - (NSA_WINDOW - 1), 0) // tk, (t0 + tq + tk - 1) // tk, win_body, 0)
        owin_t = acc_ref[...] / l_ref[...]

        for r in range(NSA_GROUP):
            c = g * NSA_GROUP * 3 + r * 3 + GATE_LANE
            rs = slice(r * tq, (r + 1) * tq)
            o_t = (gates_t[c:c + 1, :] * ocmp_t[:, rs] + gates_t[c + 1:c + 2, :] * oslc_t[:, rs]
                   + gates_t[c + 2:c + 3, :] * owin_t[:, rs])
            h = g * NSA_GROUP + r
            o_ref[:, h * HEAD_DIM:(h + 1) * HEAD_DIM] = o_t.T.astype(o_ref.dtype)


def nsa_prompt(P, kmat, vt, ck, cvt, *, n_seq, tq, tk):
    M = P.shape[0]
    T = M // n_seq
    nblk = T // NSA_BLOCK
    nq = T // tq
    W = NSA_GROUP * tq
    assert T % tq == 0 and T % tk == 0 and tk % NSA_BLOCK == 0 and tk % HEAD_DIM == 0
    return pl.pallas_call(
        functools.partial(_nsa_prompt_kernel, tq=tq, tk=tk, nblk=nblk),
        grid=(n_seq, nq),
        in_specs=[pl.BlockSpec((tq, NSA_WIDTH), lambda b, i: (b * nq + i, Q0 // NSA_WIDTH)),
                  pl.BlockSpec((tq, HEAD_DIM), lambda b, i: (b * nq + i, MISC0 // HEAD_DIM)),
                  pl.BlockSpec((T, 2 * NSA_KV_HEADS * HEAD_DIM), lambda b, i: (b, 0)),
                  pl.BlockSpec((None, NSA_KV_HEADS, HEAD_DIM, T), lambda b, i: (b, 0, 0, 0)),
                  pl.BlockSpec((None, NSA_KV_HEADS, HEAD_DIM, T), lambda b, i: (b, 1, 0, 0)),
                  pl.BlockSpec((None, NSA_KV_HEADS, nblk, HEAD_DIM), lambda b, i: (b, 0, 0, 0)),
                  pl.BlockSpec((None, NSA_KV_HEADS, HEAD_DIM, nblk), lambda b, i: (b, 0, 0, 0))],
        out_specs=pl.BlockSpec((tq, NSA_WIDTH), lambda b, i: (b * nq + i, 0)),
        out_shape=jax.ShapeDtypeStruct((M, NSA_WIDTH), BF16),
        scratch_shapes=[pltpu.VMEM((1, W), F32),
                        pltpu.VMEM((1, W), F32),
                        pltpu.VMEM((HEAD_DIM, W), F32),
                        pltpu.SMEM((T // tk,), jnp.int32)],
        compiler_params=_cparams("parallel", "arbitrary"),
        name="nsa_prompt",
    )(P, P, kmat, vt, vt, ck, cvt)


def _bdot(a, b):
    return lax.dot_general(a.astype(BF16), b.astype(BF16), (((2,), (1,)), ((0,), (0,))), preferred_element_type=F32)


def _bdot_nt(a, b):
    return lax.dot_general(a.astype(BF16), b.astype(BF16), (((2,), (2,)), ((0,), (0,))), preferred_element_type=F32)


def _bdot_tn(a, b):
    return lax.dot_general(a.astype(BF16), b.astype(BF16), (((1,), (1,)), ((0,), (0,))), preferred_element_type=F32)


def _head_stack(a, off, width=HEAD_DIM):
    return jnp.stack([a[:, off + h * width:off + (h + 1) * width] for h in range(GDN_HEADS)], axis=0)


def _unit_lower_inverse_minus_eye(A, C):
    row = lax.broadcasted_iota(jnp.int32, (C, C), 0)
    col = lax.broadcasted_iota(jnp.int32, (C, C), 1)
    A8 = jnp.where((row // 8) == (col // 8), A, 0.0)
    B2 = _bdot(A8, A8)
    B4 = _bdot(B2, B2)
    P1 = B2 - A8 - _bdot(A8, B2)
    N = P1 + B4 + _bdot(P1, B4)
    size = 16
    while size <= C:
        AL = jnp.where((row // size) == (col // size), jnp.where((row // (size // 2)) == (col // (size // 2)), 0.0, A), 0.0)
        X = AL + _bdot(N, AL)
        N = N - X - _bdot(X, N)
        size *= 2
    return N


def _gdn_prep_kernel(*refs, C, sample):
    if sample:
        x_ref, buf_ref, misc_ref, cw_ref, alog_ref, dtb_ref = refs[:6]
    else:
        x_ref, prev_ref, misc_ref, cw_ref, alog_ref, dtb_ref = refs[:6]
    u_ref, w_ref, qg_ref, kd_ref, qk_ref, gl_ref = refs[6:]
    cw = cw_ref[...]
    x = x_ref[...]
    rowv = lax.broadcasted_iota(jnp.int32, (C, 1), 0)
    if sample:
        isbuf = (rowv >= ZTOK0 - (GDN_CONV - 1)) & (rowv < ZTOK0)
        xz = jnp.where(isbuf, buf_ref[...], x)
        y = (cw[3:4, :] * xz + cw[2:3, :] * pltpu.roll(xz, 1, 0) + cw[1:2, :] * pltpu.roll(xz, 2, 0)
             + cw[0:1, :] * pltpu.roll(xz, 3, 0))
        valid = jnp.where((rowv >= ZTOK0) & (rowv < ZTOK0 + 4), 1.0, 0.0)
    else:
        prev = jnp.where(pl.program_id(1) == 0, 0.0, prev_ref[...])
        xe = jnp.concatenate([prev, x], axis=0)
        y = (cw[3:4, :] * x + cw[2:3, :] * pltpu.roll(xe, 1, 0)[8:] + cw[1:2, :] * pltpu.roll(xe, 2, 0)[8:]
             + cw[0:1, :] * pltpu.roll(xe, 3, 0)[8:])
        valid = None
    y = jax.nn.silu(y)

    misc = misc_ref[...]
    gfull = -jnp.exp(alog_ref[...]) * jax.nn.softplus(misc + dtb_ref[...])
    bfull = jax.nn.sigmoid(misc)
    if valid is not None:
        gfull = gfull * valid
        bfull = bfull * valid
    row128 = lax.broadcasted_iota(jnp.int32, (C, HEAD_DIM), 0)
    G = gfull
    s = 1
    while s < C:
        G = G + jnp.where(row128 >= s, pltpu.roll(G, s, 0), 0.0)
        s *= 2
    expG = jnp.exp(G)
    glast = G[C - 1:C, :]
    kdfac = jnp.exp(glast - G)
    if C < HEAD_DIM:
        Gpad = jnp.concatenate([G, jnp.zeros((HEAD_DIM - C, HEAD_DIM), F32)], axis=0)
    else:
        Gpad = G
    GT = Gpad.T
    gl_ref[...] = jnp.exp(jnp.broadcast_to(GT[A_LANE:A_LANE + GDN_HEADS, C - 1:C], (GDN_HEADS, HEAD_DIM)))

    row = lax.broadcasted_iota(jnp.int32, (C, C), 0)
    col = lax.broadcasted_iota(jnp.int32, (C, C), 1)
    q = _head_stack(y, 0)
    k = _head_stack(y, GDN_WIDTH)
    v = _head_stack(y, 2 * GDN_WIDTH)
    q = q * lax.rsqrt(jnp.sum(q * q, -1, keepdims=True) + NORM_EPS) * (HEAD_DIM ** -0.5)
    k = k * lax.rsqrt(jnp.sum(k * k, -1, keepdims=True) + NORM_EPS)
    if valid is not None:
        q, k, v = q * valid, k * valid, v * valid
    beta = _head_stack(bfull, B_LANE, 1)
    gcol = _head_stack(G, A_LANE, 1)
    egc = _head_stack(expG, A_LANE, 1)
    kdf = _head_stack(kdfac, A_LANE, 1)
    grow = jnp.stack([GT[A_LANE + h:A_LANE + h + 1, :C] for h in range(GDN_HEADS)], axis=0)
    decay = jnp.exp(jnp.where(row >= col, gcol - grow, NEG))
    kb = k * beta
    A = jnp.where(row > col, _bdot_nt(kb, k) * decay, 0.0)
    N = _unit_lower_inverse_minus_eye(A, C)
    vb = v * beta
    kbg = kb * egc
    U = vb + _bdot(N, vb)
    Wm = kbg + _bdot(N, kbg)
    QK = _bdot_nt(q, k) * decay
    QG = q * egc
    KD = k * kdf
    for h in range(GDN_HEADS):
        sl = slice(h * HEAD_DIM, (h + 1) * HEAD_DIM)
        u_ref[:, sl] = U[h]
        w_ref[:, sl] = Wm[h].astype(w_ref.dtype)
        qk_ref[:, h * C:(h + 1) * C] = QK[h].astype(qk_ref.dtype)
        qg_ref[:, sl] = QG[h].astype(qg_ref.dtype)
        kd_ref[:, sl] = KD[h].astype(kd_ref.dtype)


def gdn_prep(P, cw, alog_row, dtb_row, *, n_seq, C, bufz=None):
    M = P.shape[0]
    T = M // n_seq
    n = T // C
    sample = bufz is not None
    assert T % C == 0 and C % 8 == 0
    qkv_spec = pl.BlockSpec((C, 3 * GDN_WIDTH), lambda b, c: (b * n + c, 0))
    if sample:
        second = pl.BlockSpec((C, 3 * GDN_WIDTH), lambda b, c: (b * n + c, 0))
        second_arr = bufz
    else:
        second = pl.BlockSpec((8, 3 * GDN_WIDTH), lambda b, c: (jnp.maximum((b * n + c) * (C // 8) - 1, 0), 0))
        second_arr = P
    in_specs = [qkv_spec, second,
                pl.BlockSpec((C, HEAD_DIM), lambda b, c: (b * n + c, MISC0 // HEAD_DIM)),
                pl.BlockSpec((GDN_CONV, 3 * GDN_WIDTH), lambda b, c: (0, 0)),
                pl.BlockSpec((1, HEAD_DIM), lambda b, c: (0, 0)),
                pl.BlockSpec((1, HEAD_DIM), lambda b, c: (0, 0))]
    row_spec = pl.BlockSpec((C, GDN_WIDTH), lambda b, c: (b * n + c, 0))
    out_specs = [row_spec, row_spec, row_spec, row_spec,
                 pl.BlockSpec((C, GDN_HEADS * C), lambda b, c: (b * n + c, 0)),
                 pl.BlockSpec((None, GDN_HEADS, HEAD_DIM), lambda b, c: (b * n + c, 0, 0))]
    out_shape = [jax.ShapeDtypeStruct((M, GDN_WIDTH), F32),
                 jax.ShapeDtypeStruct((M, GDN_WIDTH), BF16),
                 jax.ShapeDtypeStruct((M, GDN_WIDTH), BF16),
                 jax.ShapeDtypeStruct((M, GDN_WIDTH), BF16),
                 jax.ShapeDtypeStruct((M, GDN_HEADS * C), BF16),
                 jax.ShapeDtypeStruct((n_seq * n, GDN_HEADS, HEAD_DIM), F32)]
    return pl.pallas_call(
        functools.partial(_gdn_prep_kernel, C=C, sample=sample),
        grid=(n_seq, n),
        in_specs=in_specs,
        out_specs=out_specs,
        out_shape=out_shape,
        compiler_params=_cparams("parallel", "arbitrary"),
        name="gdn_prep_sample" if sample else "gdn_prep_prompt",
    )(P, second_arr, P, cw, alog_row, dtb_row)


def _gdn_scan_kernel(u_ref, w_ref, qg_ref, kd_ref, qk_ref, gl_ref, z_ref, ng_ref, s0_ref, o_ref, sfin_ref, S_ref, *, C):
    c = pl.program_id(1)

    @pl.when(c == 0)
    def _():
        S_ref[...] = s0_ref[...]

    S = S_ref[...]
    Sb = S.astype(BF16)
    v_new = _head_stack(u_ref[...], 0) - _bdot(_head_stack(w_ref[...], 0), Sb)
    vb = v_new.astype(BF16)
    o = _bdot(_head_stack(qg_ref[...], 0), Sb) + _bdot(_head_stack(qk_ref[...], 0, C), vb)
    gl = jnp.stack([gl_ref[h:h + 1, :] for h in range(GDN_HEADS)], axis=0)
    S_ref[...] = S * gl + _bdot_tn(_head_stack(kd_ref[...], 0), vb)
    o = o * lax.rsqrt(jnp.mean(o * o, -1, keepdims=True) + NORM_EPS) * ng_ref[...]
    o = o * jax.nn.silu(_head_stack(z_ref[...], 0))
    for h in range(GDN_HEADS):
        o_ref[:, h * HEAD_DIM:(h + 1) * HEAD_DIM] = o[h].astype(o_ref.dtype)

    @pl.when(c == pl.num_programs(1) - 1)
    def _():
        sfin_ref[...] = S_ref[...]


def gdn_scan(prep, P, norm_g, s0, *, n_seq, C):
    u, w, qg, kd, qk, gl = prep
    M = u.shape[0]
    T = M // n_seq
    n = T // C
    row_spec = pl.BlockSpec((C, GDN_WIDTH), lambda b, c: (b * n + c, 0))
    st_spec = pl.BlockSpec((None, GDN_HEADS, HEAD_DIM, HEAD_DIM), lambda b, c: (b, 0, 0, 0))
    return pl.pallas_call(
        functools.partial(_gdn_scan_kernel, C=C),
        grid=(n_seq, n),
        in_specs=[row_spec, row_spec, row_spec, row_spec,
                  pl.BlockSpec((C, GDN_HEADS * C), lambda b, c: (b * n + c, 0)),
                  pl.BlockSpec((None, GDN_HEADS, HEAD_DIM), lambda b, c: (b * n + c, 0, 0)),
                  pl.BlockSpec((C, GDN_WIDTH), lambda b, c: (b * n + c, Z0 // GDN_WIDTH)),
                  pl.BlockSpec((1, HEAD_DIM), lambda b, c: (0, 0)),
                  st_spec],
        out_specs=[row_spec, st_spec],
        out_shape=[jax.ShapeDtypeStruct((M, GDN_WIDTH), BF16),
                   jax.ShapeDtypeStruct((n_seq, GDN_HEADS, HEAD_DIM, HEAD_DIM), F32)],
        scratch_shapes=[pltpu.VMEM((GDN_HEADS, HEAD_DIM, HEAD_DIM), F32)],
        compiler_params=_cparams("parallel", "arbitrary"),
        name="gdn_scan",
    )(u, w, qg, kd, qk, gl, P, norm_g, s0)


PAGES_PER_STEP = 32
BLOCK_ROWS = NSA_BLOCK * 4
PAGE_ROWS = 2 * BLOCK_ROWS


def _compress_sample_kernel(pt_ref, cache_ref, pe_ref, w1_ref, w2_ref, out_ref, xbuf, sem, *, layer, n_pages, pps):
    b = pl.program_id(0)
    gi = pl.program_id(1)
    ng = pl.num_programs(1)
    step = b * ng + gi
    nsteps = pl.num_programs(0) * ng
    slot = step % 2

    def copies(st, sl):
        bb = st // ng
        g0 = (st % ng) * pps
        out = []
        for jl in range(pps):
            page = pt_ref[bb * n_pages + g0 + jl]
            for half in range(2):
                out.append(pltpu.make_async_copy(
                    cache_ref.at[layer, page, pl.ds(half * BLOCK_ROWS, BLOCK_ROWS), :],
                    xbuf.at[sl, :, 2 * jl + half, :],
                    sem.at[sl]))
        return out

    @pl.when(step == 0)
    def _():
        for cp in copies(step, slot):
            cp.start()

    @pl.when(step + 1 < nsteps)
    def _():
        for cp in copies(step + 1, 1 - slot):
            cp.start()

    for cp in copies(step, slot):
        cp.wait()

    nblk = 2 * pps

    _compress_rows(lambda gc, p: xbuf[slot, p * 4 + gc], pe_ref, w1_ref, w2_ref, out_ref, nblk)


def compress_sample(page_table, cache_rows, pe_t, w1, w2, *, layer):
    Bs, n_pages = page_table.shape
    pps = min(PAGES_PER_STEP, n_pages)
    assert n_pages % pps == 0
    ng = n_pages // pps
    nblk = 2 * pps
    grid_spec = pltpu.PrefetchScalarGridSpec(
        num_scalar_prefetch=1,
        grid=(Bs, ng),
        in_specs=[pl.BlockSpec(memory_space=pl.ANY),
                  pl.BlockSpec(pe_t.shape, lambda b, g, pt: (0, 0, 0)),
                  pl.BlockSpec(w1.shape, lambda b, g, pt: (0, 0, 0)),
                  pl.BlockSpec(w2.shape, lambda b, g, pt: (0, 0, 0))],
        out_specs=pl.BlockSpec((None, None, 2, NSA_KV_HEADS * nblk, HEAD_DIM), lambda b, g, pt: (b, g, 0, 0, 0)),
        scratch_shapes=[pltpu.VMEM((2, BLOCK_ROWS, nblk, HEAD_DIM), F32),
                        pltpu.SemaphoreType.DMA((2,))],
    )
    return pl.pallas_call(
        functools.partial(_compress_sample_kernel, layer=layer, n_pages=n_pages, pps=pps),
        grid_spec=grid_spec,
        out_shape=jax.ShapeDtypeStruct((Bs, ng, 2, NSA_KV_HEADS * nblk, HEAD_DIM), F32),
        compiler_params=_cparams("arbitrary", "arbitrary"),
        name="compress_sample",
    )(page_table.reshape(-1), cache_rows, pe_t, w1, w2)


def _stack_heads(q_ref, g, scale):
    return jnp.concatenate(
        [q_ref[:, (g * NSA_GROUP + r) * HEAD_DIM:(g * NSA_GROUP + r + 1) * HEAD_DIM] * scale for r in range(NSA_GROUP)],
        axis=0).astype(BF16)


def _stacked_row_info(past):
    rows = NSA_GROUP * ZROWS
    ridx = lax.broadcasted_iota(jnp.int32, (rows, 1), 0)
    zrow = ridx % ZROWS
    qpos = past + zrow - ZTOK0
    slope = jnp.zeros((rows, 1), F32)
    return ridx, zrow, qpos, slope


def _slope_col(g):
    ridx = lax.broadcasted_iota(jnp.int32, (NSA_GROUP * ZROWS, 1), 0)
    sl = jnp.zeros((NSA_GROUP * ZROWS, 1), F32)
    for r in range(NSA_GROUP):
        sl = jnp.where(ridx // ZROWS == r, SLOPES[g][r], sl)
    return sl


def _nsa_sample_a_kernel(q_ref, kvw_ref, ck_ref, cv_ref, win_ref, ocmp_ref, owin_ref, sel_ref, wout_ref, *, past, nc, n_blocks, seq):
    scale = HEAD_DIM ** -0.5
    _, zrow, qpos, _ = _stacked_row_info(past)
    wb = win_ref.shape[0] // 4
    lanes_pad = sel_ref.shape[-1]
    sel_lanes = ((n_blocks + HEAD_DIM - 1) // HEAD_DIM) * HEAD_DIM
    top_n = min(NSA_TOP_N, n_blocks)
    for g in range(NSA_KV_HEADS):
        q4 = _stack_heads(q_ref, g, scale)
        slope = _slope_col(g)
        cpos = lax.broadcasted_iota(jnp.int32, (1, nc), 1) * NSA_BLOCK + (NSA_BLOCK - 1)
        vm = cpos <= qpos
        s = jnp.where(vm, _dot_nt(q4, ck_ref[g]) + slope * (cpos - past).astype(F32), NEG)
        e = jnp.exp(s - jnp.max(s, axis=-1, keepdims=True))
        p = jnp.where(vm, e / jnp.sum(e, axis=-1, keepdims=True), 0.0)
        ocmp_ref[g] = _dot(p, cv_ref[g])
        imp = p[0:ZROWS]
        for r in range(1, NSA_GROUP):
            imp = imp + p[r * ZROWS:(r + 1) * ZROWS]
        imp = jnp.concatenate([imp, jnp.zeros((ZROWS, sel_lanes - nc), F32)], axis=1)
        rank = _rank_desc(_block_scores(imp, qpos[0:ZROWS], n_blocks), n_blocks)
        lane = lax.broadcasted_iota(jnp.int32, rank.shape, 1)
        olane = lax.broadcasted_iota(jnp.int32, (ZROWS, lanes_pad), 1)
        out = jnp.zeros((ZROWS, lanes_pad), jnp.int32)
        for t in range(top_n):
            idx = jnp.sum(jnp.where(rank == float(t), lane.astype(F32), 0.0), axis=-1, keepdims=True)
            out = jnp.where(olane == t, idx.astype(jnp.int32), out)
        sel_ref[g] = out
        kold = win_ref[pl.ds(g * 2, wb, stride=4), :]
        vold = win_ref[pl.ds(g * 2 + 1, wb, stride=4), :]
        knew = kvw_ref[:, g * 2 * HEAD_DIM:(g * 2 + 1) * HEAD_DIM]
        vnew = kvw_ref[:, (g * 2 + 1) * HEAD_DIM:(g * 2 + 2) * HEAD_DIM]
        kpos_o = past - wb + lax.broadcasted_iota(jnp.int32, (1, wb), 1)
        kz = lax.broadcasted_iota(jnp.int32, (1, ZROWS), 1)
        kpos_n = past + kz - ZTOK0
        d_o = qpos - kpos_o
        d_n = qpos - kpos_n
        ok_o = (kpos_o >= 0) & (d_o >= 0) & (d_o < NSA_WINDOW)
        ok_n = (kz >= ZTOK0) & (kz < ZTOK0 + seq) & (d_n >= 0) & (d_n < NSA_WINDOW)
        s_o = jnp.where(ok_o, _dot_nt(q4, kold) + slope * (kpos_o - past).astype(F32), NEG)
        s_n = jnp.where(ok_n, _dot_nt(q4, knew) + slope * (kpos_n - past).astype(F32), NEG)
        m = jnp.maximum(jnp.max(s_o, axis=-1, keepdims=True), jnp.max(s_n, axis=-1, keepdims=True))
        p_o = jnp.exp(s_o - m)
        p_n = jnp.exp(s_n - m)
        den = jnp.sum(p_o, axis=-1, keepdims=True) + jnp.sum(p_n, axis=-1, keepdims=True)
        owin_ref[g] = (_dot(p_o, vold) + _dot(p_n, vnew)) / den
    keep = (wb - seq) * 4
    wout_ref[0:keep, :] = win_ref[seq * 4:wb * 4, :]
    ridx = lax.broadcasted_iota(jnp.int32, (seq * 4, 1), 0)
    new_rows = jnp.zeros((seq * 4, HEAD_DIM), F32)
    for t in range(seq):
        for gc in range(4):
            new_rows = jnp.where(ridx == t * 4 + gc,
                                 kvw_ref[ZTOK0 + t:ZTOK0 + t + 1, gc * HEAD_DIM:(gc + 1) * HEAD_DIM], new_rows)
    wout_ref[keep:wb * 4, :] = new_rows


def nsa_sample_a(Ps, ckv, cache_win_rows, *, layer, past, seq):
    Bs = Ps.shape[0] // ZROWS
    nc = ckv.shape[3]
    wrows = cache_win_rows.shape[2]
    n_blocks = -(-(past + seq) // NSA_BLOCK)
    assert 1 <= seq <= 4 and (seq * 4) % 8 == 0 and wrows % 8 == 0
    rows = NSA_GROUP * ZROWS
    big = pl.BlockSpec((None, NSA_KV_HEADS, rows, HEAD_DIM), lambda b: (b, 0, 0, 0))
    return pl.pallas_call(
        functools.partial(_nsa_sample_a_kernel, past=past, nc=nc, n_blocks=n_blocks, seq=seq),
        grid=(Bs,),
        in_specs=[pl.BlockSpec((ZROWS, NSA_WIDTH), lambda b: (b, Q0 // NSA_WIDTH)),
                  pl.BlockSpec((ZROWS, KV_COLS), lambda b: (b, KVW0 // KV_COLS)),
                  pl.BlockSpec((None, None, NSA_KV_HEADS, nc, HEAD_DIM), lambda b: (b, 0, 0, 0, 0)),
                  pl.BlockSpec((None, None, NSA_KV_HEADS, nc, HEAD_DIM), lambda b: (b, 1, 0, 0, 0)),
                  pl.BlockSpec((None, None, wrows, HEAD_DIM), lambda b: (layer, b, 0, 0))],
        out_specs=[big, big,
                   pl.BlockSpec((None, NSA_KV_HEADS, ZROWS, HEAD_DIM), lambda b: (b, 0, 0, 0)),
                   pl.BlockSpec((None, wrows, HEAD_DIM), lambda b: (b, 0, 0))],
        out_shape=[jax.ShapeDtypeStruct((Bs, NSA_KV_HEADS, rows, HEAD_DIM), F32),
                   jax.ShapeDtypeStruct((Bs, NSA_KV_HEADS, rows, HEAD_DIM), F32),
                   jax.ShapeDtypeStruct((Bs, NSA_KV_HEADS, ZROWS, HEAD_DIM), jnp.int32),
                   jax.ShapeDtypeStruct((Bs, wrows, HEAD_DIM), F32)],
        compiler_params=_cparams("parallel"),
        name="nsa_sample_a",
    )(Ps, Ps, ckv, ckv, cache_win_rows)


def _nsa_sample_b_kernel(pt_ref, sel_ref, q_ref, kvs_ref, misc_ref, ocmp_ref, owin_ref, cache_ref, o_ref, kvbuf, sem,
                         *, layer, past, seq, n_past_blocks, n_pages):
    b = pl.program_id(0)
    g = pl.program_id(1)
    ngrp = pl.num_programs(1)
    step = b * ngrp + g
    nsteps = pl.num_programs(0) * ngrp
    slot = step % 2

    def copies(st, sl):
        bb = st // ngrp
        gg = st % ngrp
        out = []
        for t in range(seq):
            for j in range(NSA_TOP_N):
                blk = jnp.minimum(sel_ref[((bb * seq + t) * NSA_KV_HEADS + gg) * NSA_TOP_N + j], n_past_blocks - 1)
                page = pt_ref[bb * n_pages + blk // 2]
                row0 = pl.multiple_of((blk % 2) * BLOCK_ROWS, BLOCK_ROWS)
                out.append(pltpu.make_async_copy(
                    cache_ref.at[layer, page, pl.ds(row0, BLOCK_ROWS), :],
                    kvbuf.at[sl, t * NSA_TOP_N + j],
                    sem.at[sl]))
        return out

    @pl.when(step == 0)
    def _():
        for cp in copies(step, slot):
            cp.start()

    @pl.when(step + 1 < nsteps)
    def _():
        for cp in copies(step + 1, 1 - slot):
            cp.start()

    for cp in copies(step, slot):
        cp.wait()

    scale = HEAD_DIM ** -0.5
    nkeys = NSA_TOP_N * NSA_BLOCK
    _, zrow, qpos, _ = _stacked_row_info(past)
    q4 = jnp.where(g == 0, _stack_heads(q_ref, 0, scale), _stack_heads(q_ref, 1, scale))
    slope = jnp.where(g == 0, _slope_col(0), _slope_col(1))

    kn = jnp.where(g == 0, kvs_ref[:, 0:HEAD_DIM], kvs_ref[:, 2 * HEAD_DIM:3 * HEAD_DIM])
    vn = jnp.where(g == 0, kvs_ref[:, HEAD_DIM:2 * HEAD_DIM], kvs_ref[:, 3 * HEAD_DIM:4 * HEAD_DIM])
    kz = lax.broadcasted_iota(jnp.int32, (1, ZROWS), 1)
    kpos_n = past + kz - ZTOK0
    ok_n = (kz >= ZTOK0) & (kz < ZTOK0 + seq) & (kpos_n <= qpos)
    s_n = jnp.where(ok_n, _dot_nt(q4, kn) + slope * (kpos_n - past).astype(F32), NEG)

    lane = lax.broadcasted_iota(jnp.int32, (1, nkeys), 1)
    slot_of_lane = lane // NSA_BLOCK
    s_g = jnp.full((NSA_GROUP * ZROWS, nkeys), NEG, F32)
    v_tok = []
    for t in range(seq):
        blk_of_lane = jnp.zeros((1, nkeys), jnp.int32)
        for j in range(NSA_TOP_N):
            blk = sel_ref[((b * seq + t) * NSA_KV_HEADS + g) * NSA_TOP_N + j]
            blk_of_lane = jnp.where(slot_of_lane == j, blk, blk_of_lane)
        k_t = jnp.concatenate([kvbuf[slot, t * NSA_TOP_N + j, pl.ds(g * 2, NSA_BLOCK, stride=4), :]
                               for j in range(NSA_TOP_N)], axis=0)
        v_tok.append(jnp.concatenate([kvbuf[slot, t * NSA_TOP_N + j, pl.ds(g * 2 + 1, NSA_BLOCK, stride=4), :]
                                      for j in range(NSA_TOP_N)], axis=0))
        kpos = blk_of_lane * NSA_BLOCK + lane % NSA_BLOCK
        st = _dot_nt(q4, k_t) + slope * (kpos - past).astype(F32)
        mine = (zrow == ZTOK0 + t) & (blk_of_lane < n_past_blocks) & (kpos <= qpos)
        s_g = jnp.where(mine, st, s_g)

    m = jnp.maximum(jnp.max(s_g, axis=-1, keepdims=True), jnp.max(s_n, axis=-1, keepdims=True))
    p_g = jnp.exp(s_g - m)
    p_n = jnp.exp(s_n - m)
    den = jnp.sum(p_g, axis=-1, keepdims=True) + jnp.sum(p_n, axis=-1, keepdims=True)
    pv = _dot(p_n, vn)
    for t in range(seq):
        pv = pv + jnp.where(zrow == ZTOK0 + t, _dot(p_g, v_tok[t]), 0.0)
    o_slc = pv / den

    gates = jax.nn.sigmoid(misc_ref[...])
    o_cmp = ocmp_ref[...]
    o_win = owin_ref[...]
    for r in range(NSA_GROUP):
        rs = slice(r * ZROWS, (r + 1) * ZROWS)
        outs = []
        for gg in range(NSA_KV_HEADS):
            c = gg * NSA_GROUP * 3 + r * 3 + GATE_LANE
            outs.append(gates[:, c:c + 1] * o_cmp[rs] + gates[:, c + 1:c + 2] * o_slc[rs]
                        + gates[:, c + 2:c + 3] * o_win[rs])
        o_ref[:, r * HEAD_DIM:(r + 1) * HEAD_DIM] = jnp.where(g == 0, outs[0], outs[1]).astype(o_ref.dtype)


def nsa_sample_b(page_table, sel, Ps, ocmp, owin, cache_rows, *, layer, past, seq):
    Bs, n_pages = page_table.shape
    rows = NSA_GROUP * ZROWS
    n_past_blocks = past // NSA_BLOCK
    gw = NSA_GROUP * HEAD_DIM

    big = pl.BlockSpec((None, None, rows, HEAD_DIM), lambda b, g, pt, sl: (b, g, 0, 0))
    in_specs = [pl.BlockSpec((ZROWS, NSA_WIDTH), lambda b, g, pt, sl: (b, Q0 // NSA_WIDTH)),
                pl.BlockSpec((ZROWS, KV_COLS), lambda b, g, pt, sl: (b, KVS0 // KV_COLS)),
                pl.BlockSpec((ZROWS, HEAD_DIM), lambda b, g, pt, sl: (b, MISC0 // HEAD_DIM)),
                big, big,
                pl.BlockSpec(memory_space=pl.ANY)]
    grid_spec = pltpu.PrefetchScalarGridSpec(
        num_scalar_prefetch=2,
        grid=(Bs, NSA_KV_HEADS),
        in_specs=in_specs,
        out_specs=pl.BlockSpec((ZROWS, gw), lambda b, g, pt, sl: (b, g)),
        scratch_shapes=[pltpu.VMEM((2, seq * NSA_TOP_N, BLOCK_ROWS, HEAD_DIM), F32),
                        pltpu.SemaphoreType.DMA((2,))],
    )
    return pl.pallas_call(
        functools.partial(_nsa_sample_b_kernel, layer=layer, past=past, seq=seq, n_past_blocks=n_past_blocks,
                          n_pages=n_pages),
        grid_spec=grid_spec,
        out_shape=jax.ShapeDtypeStruct((Bs * ZROWS, NSA_WIDTH), BF16),
        compiler_params=_cparams("arbitrary", "arbitrary"),
        name="nsa_sample_b",
    )(page_table.reshape(-1), sel, Ps, Ps, Ps, ocmp, owin, cache_rows)


def _reorder_w_in(w_in):
    q, kvc, kvs, kvw = w_in[..., 0:1024], w_in[..., 1024:1536], w_in[..., 1536:2048], w_in[..., 2048:2560]
    gates, qkv, z, ab = w_in[..., 2560:2584], w_in[..., 2584:5656], w_in[..., 5656:6680], w_in[..., 6680:6696]
    pad = jnp.zeros(w_in.shape[:-1] + (IN_COLS_PAD - MISC0 - 40,), w_in.dtype)
    return jnp.concatenate([qkv, q, z, kvc, kvs, kvw, gates, ab, pad], axis=-1).astype(BF16)


def _lane_row(vec, lane0):
    return jnp.zeros((1, HEAD_DIM), F32).at[0, lane0:lane0 + vec.shape[0]].set(vec.astype(F32))


def _to_zrows(x, first_row):
    Bs, n, C = x.shape
    z = jnp.zeros((Bs, ZROWS, C), x.dtype).at[:, first_row:first_row + n].set(x)
    return z.reshape(Bs * ZROWS, C)


def kernel(x_prompt, x_sample, cache_cmp_kv, cache_slc_kv, cache_win_kv, state_gdn, state_gdn_conv, state_ffn_conv, page_table, w_in, w_o, cmp_pe, cmp_w1, cmp_w2, gdn_conv_w, gdn_a_log, gdn_dt_bias, gdn_norm_g, ln1_g, ln1_b, ffn_w_up, ffn_conv_w, ffn_w_down, ln2_g, ln2_b):
    B, S, D = x_prompt.shape
    Bs, seq, _ = x_sample.shape
    depth = w_in.shape[0]
    n_pool, page = cache_cmp_kv.shape[1], cache_cmp_kv.shape[2]
    n_pages = page_table.shape[1]
    past = n_pages * page
    wb = cache_win_kv.shape[2]
    d_ff = ffn_w_down.shape[1]
    alpha = (2 * depth) ** 0.25
    assert page == 2 * NSA_BLOCK

    w_in_r = _reorder_w_in(w_in)
    w_o_b = w_o.astype(BF16)
    w_up_b = ffn_w_up.astype(BF16)
    w_down_b = ffn_w_down.astype(BF16)
    w1_b = cmp_w1.astype(BF16)
    w2_b = cmp_w2.astype(BF16)
    pe_t = jnp.swapaxes(cmp_pe, 1, 2)
    cache_cmp_rows = cache_cmp_kv.reshape(depth, n_pool, PAGE_ROWS, HEAD_DIM)
    cache_slc_rows = cache_slc_kv.reshape(depth, n_pool, PAGE_ROWS, HEAD_DIM)
    cache_win_rows = cache_win_kv.reshape(depth, Bs, wb * 4, HEAD_DIM)

    xp = x_prompt.reshape(B * S, D)
    xs = _to_zrows(x_sample, ZTOK0)
    xp_b, xs_b = xp, xs
    tm = 512 if (B * S) % 512 == 0 else B * S
    tm_in = 1024 if (B * S) % 1024 == 0 else tm
    tk_down = 1408 if d_ff % 1408 == 0 else 512
    tq = 128
    p_out = [[] for _ in range(6)]
    s_out = [[] for _ in range(6)]
    zero_state = jnp.zeros((B, GDN_HEADS, HEAD_DIM, HEAD_DIM), F32)

    for l in range(depth):
        alog_row = _lane_row(gdn_a_log[l], A_LANE)
        dtb_row = _lane_row(gdn_dt_bias[l], A_LANE)
        ng = gdn_norm_g[l].reshape(1, HEAD_DIM)
        g1, b1 = ln1_g[l].reshape(1, D), ln1_b[l].reshape(1, D)
        g2, b2 = ln2_g[l].reshape(1, D), ln2_b[l].reshape(1, D)

        P = matmul(xp_b, w_in_r[l], tm=tm_in, tn=1024)
        pkv = P[:, KVS0:KVS0 + 2 * KV_COLS].astype(BF16).reshape(B, S, 4, 2, HEAD_DIM)
        kmat = pkv[:, :, :, 0, :].reshape(B * S, 4 * HEAD_DIM)
        vt = jnp.transpose(pkv[:, :, :, 1, :], (0, 2, 3, 1))
        ckv = compress_prompt(P, pe_t[l], w1_b[l], w2_b[l], n_seq=B)
        ckv = ckv.reshape(B, 2, NSA_KV_HEADS, S // NSA_BLOCK, HEAD_DIM).astype(BF16)
        o_nsa = nsa_prompt(P, kmat, vt, ckv[:, 0], jnp.swapaxes(ckv[:, 1], -1, -2), n_seq=B, tq=tq,
                           tk=256 if S % 256 == 0 else tq)
        prep = gdn_prep(P, gdn_conv_w[l], alog_row, dtb_row, n_seq=B, C=GDN_CHUNK)
        o_gdn, s_fin = gdn_scan(prep, P, ng, zero_state, n_seq=B, C=GDN_CHUNK)
        h, h_b = proj_ln(o_nsa, o_gdn, w_o_b[l], xp, g1, b1, alpha=alpha, tm=256)
        act, utail = ffn_up_prompt(h_b, w_up_b[l], ffn_conv_w[l], n_seq=B, tm=tm, tf=tk_down)
        xp, xp_b = ffn_down_ln(act, w_down_b[l], h, g2, b2, alpha=alpha, tm=tm, tk=tk_down)

        P3 = P.reshape(B, S, IN_COLS_PAD)
        p_out[0].append(P3[:, :, KVC0:KVC0 + KV_COLS].reshape(B, S, NSA_KV_HEADS, 2, HEAD_DIM))
        p_out[1].append(P3[:, :, KVS0:KVS0 + KV_COLS].reshape(B, S, NSA_KV_HEADS, 2, HEAD_DIM))
        wn = min(NSA_WINDOW, S)
        p_out[2].append(P3[:, S - wn:, KVW0:KVW0 + KV_COLS].reshape(B, wn, NSA_KV_HEADS, 2, HEAD_DIM))
        p_out[3].append(s_fin)
        p_out[4].append(P3[:, S - (GDN_CONV - 1):, QKV0:QKV0 + 3 * GDN_WIDTH])
        p_out[5].append(utail[:, 8 - (FFN_CONV - 1):, :])

        Ps = matmul(xs_b, w_in_r[l], tm=Bs * ZROWS, tn=1024)
        ckv_s = compress_sample(page_table, cache_cmp_rows, pe_t[l], w1_b[l], w2_b[l], layer=l)
        ng_grp = ckv_s.shape[1]
        nblk_step = ckv_s.shape[3] // NSA_KV_HEADS
        ckv_s = ckv_s.reshape(Bs, ng_grp, 2, NSA_KV_HEADS, nblk_step, HEAD_DIM)
        ckv_s = jnp.transpose(ckv_s, (0, 2, 3, 1, 4, 5)).reshape(Bs, 2, NSA_KV_HEADS, ng_grp * nblk_step, HEAD_DIM)
        ocmp, owin, sel, win_new = nsa_sample_a(Ps, ckv_s, cache_win_rows, layer=l, past=past, seq=seq)
        sel_flat = jnp.transpose(sel[:, :, ZTOK0:ZTOK0 + seq, :NSA_TOP_N], (0, 2, 1, 3)).reshape(-1)
        o_nsa_s = nsa_sample_b(page_table, sel_flat, Ps, ocmp, owin, cache_slc_rows, layer=l, past=past, seq=seq)
        gbufz = _to_zrows(state_gdn_conv[l], ZTOK0 - (GDN_CONV - 1))
        prep_s = gdn_prep(Ps, gdn_conv_w[l], alog_row, dtb_row, n_seq=Bs, C=ZROWS, bufz=gbufz)
        o_gdn_s, s_fin_s = gdn_scan(prep_s, Ps, ng, state_gdn[l], n_seq=Bs, C=ZROWS)
        hs, hs_b = proj_ln(o_nsa_s, o_gdn_s, w_o_b[l], xs, g1, b1, alpha=alpha, tm=Bs * ZROWS)
        fbufz = _to_zrows(state_ffn_conv[l], ZTOK0 - (FFN_CONV - 1))
        act_s, u_s = ffn_up_sample(hs_b, w_up_b[l], ffn_conv_w[l], fbufz, tf=tk_down)
        xs, xs_b = ffn_down_ln(act_s, w_down_b[l], hs, g2, b2, alpha=alpha, tm=Bs * ZROWS, tk=tk_down)

        Ps3 = Ps.reshape(Bs, ZROWS, IN_COLS_PAD)[:, ZTOK0:ZTOK0 + seq]
        s_out[0].append(Ps3[:, :, KVC0:KVC0 + KV_COLS].reshape(Bs, seq, NSA_KV_HEADS, 2, HEAD_DIM))
        s_out[1].append(Ps3[:, :, KVS0:KVS0 + KV_COLS].reshape(Bs, seq, NSA_KV_HEADS, 2, HEAD_DIM))
        s_out[2].append(win_new.reshape(Bs, wb, NSA_KV_HEADS, 2, HEAD_DIM))
        s_out[3].append(s_fin_s)
        ext_g = jnp.concatenate([state_gdn_conv[l], Ps3[:, :, QKV0:QKV0 + 3 * GDN_WIDTH]], axis=1)
        s_out[4].append(ext_g[:, -(GDN_CONV - 1):])
        u3 = u_s.reshape(Bs, ZROWS, d_ff)[:, ZTOK0:ZTOK0 + seq]
        ext_f = jnp.concatenate([state_ffn_conv[l], u3], axis=1)
        s_out[5].append(ext_f[:, -(FFN_CONV - 1):])

    y_p = xp.reshape(B, S, D)
    y_s = xs.reshape(Bs, ZROWS, D)[:, ZTOK0:ZTOK0 + seq]
    return (y_p, y_s, *[jnp.stack(v, 0) for v in p_out], *[jnp.stack(v, 0) for v in s_out])
```

```python
import functools

import jax
import jax.numpy as jnp
import numpy as np
from jax import lax
from jax.experimental import pallas as pl
from jax.experimental.pallas import tpu as pltpu

F32 = jnp.float32
BF16 = jnp.bfloat16

HEAD_DIM = 128
NSA_HEADS = 8
NSA_KV_HEADS = 2
NSA_GROUP = 4
NSA_WIDTH = 1024
NSA_BLOCK = 64
NSA_TOP_N = 16
NSA_WINDOW = 512
GDN_HEADS = 8
GDN_WIDTH = 1024
GDN_CONV = 4
GDN_CHUNK = 64
FFN_CONV = 3
KV_COLS = 512
LN_EPS = 1e-5
NORM_EPS = 1e-6
NEG = -1e30
FORCE = 1e4
LOG2E = 1.4426950408889634
SLOPES = [[2.0 ** -(g * NSA_GROUP + r + 1) for r in range(NSA_GROUP)] for g in range(NSA_KV_HEADS)]

QKV0, Q0, Z0, KVC0, KVS0, KVW0, MISC0, IN_COLS_PAD = 0, 3072, 4096, 5120, 5632, 6144, 6656, 7168
GATE_LANE, A_LANE, B_LANE = 0, 24, 32

ZROWS = 16
ZTOK0 = 8

VMEM_LIMIT_BYTES = 56 * 1024 * 1024


def _cparams(*sem):
    return pltpu.CompilerParams(dimension_semantics=sem, vmem_limit_bytes=VMEM_LIMIT_BYTES)


def _dot(a, b):
    return jnp.dot(a.astype(BF16), b.astype(BF16), preferred_element_type=F32)


def _dot_nt(a, b):
    return lax.dot_general(a.astype(BF16), b.astype(BF16), (((1,), (1,)), ((), ())), preferred_element_type=F32)


def _dot_tn(a, b):
    return lax.dot_general(a.astype(BF16), b.astype(BF16), (((0,), (0,)), ((), ())), preferred_element_type=F32)


def _layer_norm(t, g, b):
    mu = jnp.mean(t, -1, keepdims=True)
    d = t - mu
    var = jnp.mean(d * d, -1, keepdims=True)
    return d * lax.rsqrt(var + LN_EPS) * g + b


def _mm_kernel(x_ref, w_ref, o_ref):
    o_ref[...] = _dot(x_ref[...], w_ref[...])


def matmul(x, w, *, tm, tn):
    M, K = x.shape
    _, N = w.shape
    assert M % tm == 0 and N % tn == 0
    return pl.pallas_call(
        _mm_kernel,
        grid=(M // tm, N // tn),
        in_specs=[pl.BlockSpec((tm, K), lambda i, j: (i, 0)),
                  pl.BlockSpec((K, tn), lambda i, j: (0, j))],
        out_specs=pl.BlockSpec((tm, tn), lambda i, j: (i, j)),
        out_shape=jax.ShapeDtypeStruct((M, N), F32),
        compiler_params=_cparams("parallel", "arbitrary"),
        name="in_proj",
    )(x, w)


def _proj_ln_kernel(a1_ref, a2_ref, w_ref, x_ref, g_ref, b_ref, o_ref, ob_ref, *, alpha):
    k1 = a1_ref.shape[1]
    acc = _dot(a1_ref[...], w_ref[:k1, :]) + _dot(a2_ref[...], w_ref[k1:, :])
    o = _layer_norm(alpha * x_ref[...] + acc, g_ref[...], b_ref[...])
    o_ref[...] = o
    ob_ref[...] = o.astype(ob_ref.dtype)


def proj_ln(a1, a2, w, x, g, b, *, alpha, tm):
    M, D = x.shape
    k1, k2 = a1.shape[1], a2.shape[1]
    assert M % tm == 0 and w.shape == (k1 + k2, D)
    return pl.pallas_call(
        functools.partial(_proj_ln_kernel, alpha=alpha),
        grid=(M // tm,),
        in_specs=[pl.BlockSpec((tm, k1), lambda i: (i, 0)),
                  pl.BlockSpec((tm, k2), lambda i: (i, 0)),
                  pl.BlockSpec((k1 + k2, D), lambda i: (0, 0)),
                  pl.BlockSpec((tm, D), lambda i: (i, 0)),
                  pl.BlockSpec((1, D), lambda i: (0, 0)),
                  pl.BlockSpec((1, D), lambda i: (0, 0))],
        out_specs=[pl.BlockSpec((tm, D), lambda i: (i, 0)), pl.BlockSpec((tm, D), lambda i: (i, 0))],
        out_shape=[jax.ShapeDtypeStruct((M, D), F32), jax.ShapeDtypeStruct((M, D), BF16)],
        compiler_params=_cparams("parallel"),
        name="out_proj_ln",
    )(a1, a2, w, x, g, b)


def _conv3(u, prev8, cw):
    ue = jnp.concatenate([prev8, u], axis=0)
    u1 = pltpu.roll(ue, 1, 0)[8:]
    u2 = pltpu.roll(ue, 2, 0)[8:]
    return cw[2:3, :] * u + cw[1:2, :] * u1 + cw[0:1, :] * u2


MXU_COLS = 256


def _col_chunks(width):
    return [slice(c0, min(c0 + MXU_COLS, width)) for c0 in range(0, width, MXU_COLS)]


def _ffn_up_prompt_kernel(h_ref, wu_ref, wv_ref, cw_ref, act_ref, tail_ref, carry_ref, *, tiles_per_seq):
    i = pl.program_id(1)
    h = h_ref[...]
    tm = h.shape[0]

    @pl.when(i % tiles_per_seq == 0)
    def _():
        carry_ref[...] = jnp.zeros_like(carry_ref)

    for cs in _col_chunks(act_ref.shape[1]):
        u = _dot(h, wu_ref[:, cs])
        v = _dot(h, wv_ref[:, cs])
        uc = _conv3(u, carry_ref[:, cs], cw_ref[:, cs])
        act_ref[:, cs] = (jax.nn.gelu(uc) * v).astype(act_ref.dtype)
        tail = u[tm - 8:, :]
        carry_ref[:, cs] = tail
        tail_ref[:, cs] = tail


def ffn_up_prompt(h, w_up, cw, *, n_seq, tm, tf):
    M, D = h.shape
    F = w_up.shape[1] // 2
    T = M // n_seq
    assert M % tm == 0 and F % tf == 0 and T % tm == 0 and tf % HEAD_DIM == 0
    nf = F // tf
    tps = T // tm
    return pl.pallas_call(
        functools.partial(_ffn_up_prompt_kernel, tiles_per_seq=tps),
        grid=(nf, M // tm),
        in_specs=[pl.BlockSpec((tm, D), lambda f, i: (i, 0)),
                  pl.BlockSpec((D, tf), lambda f, i: (0, f)),
                  pl.BlockSpec((D, tf), lambda f, i: (0, nf + f)),
                  pl.BlockSpec((FFN_CONV, tf), lambda f, i: (0, f))],
        out_specs=[pl.BlockSpec((tm, tf), lambda f, i: (i, f)),
                   pl.BlockSpec((None, 8, tf), lambda f, i: (i // tps, 0, f))],
        out_shape=[jax.ShapeDtypeStruct((M, F), BF16),
                   jax.ShapeDtypeStruct((n_seq, 8, F), F32)],
        scratch_shapes=[pltpu.VMEM((8, tf), F32)],
        compiler_params=_cparams("arbitrary", "arbitrary"),
        name="ffn_up_prompt",
    )(h, w_up, w_up, cw)


def _ffn_up_sample_kernel(h_ref, wu_ref, wv_ref, cw_ref, buf_ref, act_ref, u_ref):
    h = h_ref[...]
    zrow = lax.broadcasted_iota(jnp.int32, (h.shape[0], 1), 0) % ZROWS
    isbuf = (zrow >= ZTOK0 - (FFN_CONV - 1)) & (zrow < ZTOK0)
    for cs in _col_chunks(act_ref.shape[1]):
        u = _dot(h, wu_ref[:, cs])
        v = _dot(h, wv_ref[:, cs])
        uz = jnp.where(isbuf, buf_ref[:, cs], u)
        cw = cw_ref[:, cs]
        uc = cw[2:3, :] * uz + cw[1:2, :] * pltpu.roll(uz, 1, 0) + cw[0:1, :] * pltpu.roll(uz, 2, 0)
        act_ref[:, cs] = (jax.nn.gelu(uc) * v).astype(act_ref.dtype)
        u_ref[:, cs] = u


def ffn_up_sample(h, w_up, cw, bufz, *, tf):
    M, D = h.shape
    F = w_up.shape[1] // 2
    nf = F // tf
    return pl.pallas_call(
        _ffn_up_sample_kernel,
        grid=(nf,),
        in_specs=[pl.BlockSpec((M, D), lambda f: (0, 0)),
                  pl.BlockSpec((D, tf), lambda f: (0, f)),
                  pl.BlockSpec((D, tf), lambda f: (0, nf + f)),
                  pl.BlockSpec((FFN_CONV, tf), lambda f: (0, f)),
                  pl.BlockSpec((M, tf), lambda f: (0, f))],
        out_specs=[pl.BlockSpec((M, tf), lambda f: (0, f)),
                   pl.BlockSpec((M, tf), lambda f: (0, f))],
        out_shape=[jax.ShapeDtypeStruct((M, F), BF16),
                   jax.ShapeDtypeStruct((M, F), F32)],
        compiler_params=_cparams("parallel"),
        name="ffn_up_sample",
    )(h, w_up, w_up, cw, bufz)


def _ffn_down_ln_kernel(a_ref, w_ref, h_ref, g_ref, b_ref, o_ref, ob_ref, acc_ref, *, alpha):
    k = pl.program_id(1)

    @pl.when(k == 0)
    def _():
        acc_ref[...] = jnp.zeros_like(acc_ref)

    acc_ref[...] += _dot(a_ref[...], w_ref[...])

    @pl.when(k == pl.num_programs(1) - 1)
    def _():
        o = _layer_norm(alpha * h_ref[...] + acc_ref[...], g_ref[...], b_ref[...])
        o_ref[...] = o
        ob_ref[...] = o.astype(ob_ref.dtype)


def _ffn_down_ln_resident_kernel(a_ref, w_ref, h_ref, g_ref, b_ref, o_ref, ob_ref, *, alpha):
    acc = jnp.dot(a_ref[...], w_ref[...], preferred_element_type=F32)
    o = _layer_norm(alpha * h_ref[...] + acc, g_ref[...], b_ref[...])
    o_ref[...] = o
    ob_ref[...] = o.astype(ob_ref.dtype)


def ffn_down_ln_resident(a, w, h, g, b, *, alpha, tm):
    M, Fd = a.shape
    D = w.shape[1]
    assert M % tm == 0
    return pl.pallas_call(
        functools.partial(_ffn_down_ln_resident_kernel, alpha=alpha),
        grid=(M // tm,),
        in_specs=[pl.BlockSpec((tm, Fd), lambda i: (i, 0)),
                  pl.BlockSpec((Fd, D), lambda i: (0, 0), pipeline_mode=pl.Buffered(1)),
                  pl.BlockSpec((tm, D), lambda i: (i, 0)),
                  pl.BlockSpec((1, D), lambda i: (0, 0)),
                  pl.BlockSpec((1, D), lambda i: (0, 0))],
        out_specs=[pl.BlockSpec((tm, D), lambda i: (i, 0)), pl.BlockSpec((tm, D), lambda i: (i, 0))],
        out_shape=[jax.ShapeDtypeStruct((M, D), F32), jax.ShapeDtypeStruct((M, D), BF16)],
        compiler_params=_cparams("parallel"),
        name="ffn_down_ln",
    )(a, w, h, g, b)


def ffn_down_ln(a, w, h, g, b, *, alpha, tm, tk):
    M, Fd = a.shape
    D = w.shape[1]
    assert M % tm == 0 and Fd % tk == 0
    return pl.pallas_call(
        functools.partial(_ffn_down_ln_kernel, alpha=alpha),
        grid=(M // tm, Fd // tk),
        in_specs=[pl.BlockSpec((tm, tk), lambda i, k: (i, k)),
                  pl.BlockSpec((tk, D), lambda i, k: (k, 0)),
                  pl.BlockSpec((tm, D), lambda i, k: (i, 0)),
                  pl.BlockSpec((1, D), lambda i, k: (0, 0)),
                  pl.BlockSpec((1, D), lambda i, k: (0, 0))],
        out_specs=[pl.BlockSpec((tm, D), lambda i, k: (i, 0)), pl.BlockSpec((tm, D), lambda i, k: (i, 0))],
        out_shape=[jax.ShapeDtypeStruct((M, D), F32), jax.ShapeDtypeStruct((M, D), BF16)],
        scratch_shapes=[pltpu.VMEM((tm, D), F32)],
        compiler_params=_cparams("parallel", "arbitrary"),
        name="ffn_down_ln",
    )(a, w, h, g, b)


def _compress_rows(read_rows, pe_ref, w1_ref, w2_ref, out_ref, nblk):
    del nblk
    for c in range(2):
        parts = []
        for p in range(NSA_BLOCK):
            pe_row = pe_ref[c, p:p + 1, :]
            rows = [read_rows(g * 2 + c, p) + pe_row for g in range(NSA_KV_HEADS)]
            parts.append(jnp.concatenate(rows, axis=0).astype(BF16))
        hid = jnp.dot(jnp.concatenate(parts, axis=1), w1_ref[c], preferred_element_type=F32)
        out_ref[c] = _dot(jax.nn.silu(hid), w2_ref[c])


def _compress_prompt_kernel(x0, x1, x2, x3, pe_ref, w1_ref, w2_ref, out_ref, *, nblk):
    xs = [x0, x1, x2, x3]
    _compress_rows(lambda gc, p: xs[gc][:, p, :], pe_ref, w1_ref, w2_ref, out_ref, nblk)


def compress_prompt(P, pe_t, w1, w2, *, n_seq):
    M = P.shape[0]
    T = M // n_seq
    nblk = T // NSA_BLOCK
    P3 = P.reshape(n_seq * nblk, NSA_BLOCK, IN_COLS_PAD)
    cb0 = KVC0 // HEAD_DIM
    in_specs = [pl.BlockSpec((nblk, NSA_BLOCK, HEAD_DIM), (lambda b, gc=gc: (b, 0, cb0 + gc))) for gc in range(4)]
    in_specs += [pl.BlockSpec(pe_t.shape, lambda b: (0, 0, 0)),
                 pl.BlockSpec(w1.shape, lambda b: (0, 0, 0)),
                 pl.BlockSpec(w2.shape, lambda b: (0, 0, 0))]
    return pl.pallas_call(
        functools.partial(_compress_prompt_kernel, nblk=nblk),
        grid=(n_seq,),
        in_specs=in_specs,
        out_specs=pl.BlockSpec((None, 2, NSA_KV_HEADS * nblk, HEAD_DIM), lambda b: (b, 0, 0, 0)),
        out_shape=jax.ShapeDtypeStruct((n_seq, 2, NSA_KV_HEADS * nblk, HEAD_DIM), F32),
        compiler_params=_cparams("parallel"),
        name="compress_prompt",
    )(P3, P3, P3, P3, pe_t, w1, w2)


def _rank_desc(score, n_real):
    lane = lax.broadcasted_iota(jnp.int32, score.shape, 1)
    cnt = jnp.zeros(score.shape, F32)
    for i in range(n_real):
        col = score[:, i:i + 1]
        ge = jnp.where(col >= score, 1.0, 0.0)
        gt = jnp.where(col > score, 1.0, 0.0)
        cnt = cnt + jnp.where(lane > i, ge, gt)
    return cnt


def _block_scores(imp, qpos, n_blocks):
    bidx = lax.broadcasted_iota(jnp.int32, imp.shape, 1)
    cur = qpos // NSA_BLOCK
    sc = jnp.where(bidx == 0, FORCE, jnp.where(bidx == cur, FORCE, jnp.where(bidx == cur - 1, FORCE, imp)))
    sc = jnp.where(bidx * NSA_BLOCK <= qpos, sc, NEG)
    return jnp.where(bidx < n_blocks, sc, -jnp.inf)


def _rank_desc_t(score, n_real):
    L = score.shape[0]
    assert L % 8 == 0
    groups = [score[8 * a:8 * a + 8, :] for a in range(L // 8)]
    sub = lax.broadcasted_iota(jnp.int32, groups[0].shape, 0)
    cnts = [jnp.zeros(groups[0].shape, F32) for _ in groups]
    for i in range(n_real):
        row = score[i:i + 1, :]
        for a, x in enumerate(groups):
            if 8 * a > i:
                c = jnp.where(row >= x, 1.0, 0.0)
            elif 8 * a + 7 < i:
                c = jnp.where(row > x, 1.0, 0.0)
            else:
                c = jnp.where(sub + 8 * a > i, jnp.where(row >= x, 1.0, 0.0), jnp.where(row > x, 1.0, 0.0))
            cnts[a] = cnts[a] + c
    return jnp.concatenate(cnts, axis=0)


def _block_scores_t(imp, qpos, n_blocks):
    bidx = lax.broadcasted_iota(jnp.int32, imp.shape, 0)
    cur = qpos // NSA_BLOCK
    sc = jnp.where(bidx == 0, FORCE, jnp.where(bidx == cur, FORCE, jnp.where(bidx == cur - 1, FORCE, imp)))
    sc = jnp.where(bidx * NSA_BLOCK <= qpos, sc, NEG)
    return jnp.where(bidx < n_blocks, sc, -jnp.inf)


def _nsa_prompt_kernel(q_ref, misc_ref, k_ref, vts_ref, vtw_ref, ck_ref, cvt_ref, o_ref, m_ref, l_ref, acc_ref, flag_ref, *, tq, tk, nblk):
    i = pl.program_id(1)
    t0 = i * tq
    bpt = tk // NSA_BLOCK
    wspan = min(NSA_WINDOW + tq + HEAD_DIM, k_ref.shape[0])
    scale = HEAD_DIM ** -0.5 * LOG2E
    top_n = min(NSA_TOP_N, nblk)
    qpos_row = t0 + lax.broadcasted_iota(jnp.int32, (1, tq), 1)
    sub_k = lax.broadcasted_iota(jnp.int32, (tk, tq), 0)
    d_kq = sub_k - lax.broadcasted_iota(jnp.int32, (tk, tq), 1)
    sub_kf = sub_k.astype(F32)
    wsub_k = lax.broadcasted_iota(jnp.int32, (wspan, tq), 0)
    wsub_kf = wsub_k.astype(F32)
    wwin_code = -2 * (wsub_k - lax.broadcasted_iota(jnp.int32, (wspan, tq), 1)) - (NSA_WINDOW - 1)
    e_rel = (lax.broadcasted_iota(jnp.int32, (tk, nblk), 1)
             - lax.broadcasted_iota(jnp.int32, (tk, nblk), 0) // NSA_BLOCK)
    gates_t = jax.nn.sigmoid(misc_ref[...]).T

    def reset():
        m_ref[...] = jnp.full(m_ref.shape, NEG, F32)
        l_ref[...] = jnp.zeros(l_ref.shape, F32)
        acc_ref[...] = jnp.zeros(acc_ref.shape, F32)

    def online_update(s4, vt):
        m_old = m_ref[...]
        m_new = jnp.maximum(m_old, jnp.max(s4, axis=0, keepdims=True))
        p = jnp.exp2(s4 - m_new)
        alpha = jnp.exp2(m_old - m_new)
        l_ref[...] = alpha * l_ref[...] + jnp.sum(p, axis=0, keepdims=True)
        acc_ref[...] = alpha * acc_ref[...] + jnp.dot(vt, p.astype(BF16), preferred_element_type=F32)
        m_ref[...] = m_new

    def masked_scores(s, valid, base, g):
        return jnp.concatenate(
            [jnp.where(valid, s[:, r * tq:(r + 1) * tq] + (SLOPES[g][r] * LOG2E) * base, NEG)
             for r in range(NSA_GROUP)], axis=1)

    for g in range(NSA_KV_HEADS):
        q4t = jnp.concatenate(
            [(q_ref[:, (g * NSA_GROUP + r) * HEAD_DIM:(g * NSA_GROUP + r + 1) * HEAD_DIM] * scale).T
             for r in range(NSA_GROUP)], axis=1).astype(BF16)
        sc = jnp.dot(ck_ref[g], q4t, preferred_element_type=F32)
        cpos = lax.broadcasted_iota(jnp.int32, (nblk, tq), 0) * NSA_BLOCK + (NSA_BLOCK - 1)
        vm = cpos <= qpos_row
        cposrel = (cpos - t0).astype(F32)
        imp = jnp.zeros((nblk, tq), F32)
        parts = []
        for r in range(NSA_GROUP):
            s = jnp.where(vm, sc[:, r * tq:(r + 1) * tq] + (SLOPES[g][r] * LOG2E) * cposrel, NEG)
            e = jnp.exp2(s - jnp.max(s, axis=0, keepdims=True))
            p = jnp.where(vm, e / jnp.sum(e, axis=0, keepdims=True), 0.0)
            imp = imp + p
            parts.append(p)
        ocmp_t = jnp.dot(cvt_ref[g], jnp.concatenate(parts, axis=1).astype(BF16), preferred_element_type=F32)
        rank = _rank_desc_t(_block_scores_t(imp, qpos_row, nblk), nblk)
        sel_f = jnp.where(rank < top_n, 1.0, 0.0)
        sel_t = sel_f.astype(BF16)
        blk_any = jnp.max(sel_f, axis=1, keepdims=True)
        for jt in range(nblk // bpt):
            flag_ref[jt] = (jnp.max(blk_any[jt * bpt:(jt + 1) * bpt, :]) > 0.5).astype(jnp.int32)

        reset()

        def slc_body(j, carry):
            k0 = pl.multiple_of(j * tk, tk)

            @pl.when(flag_ref[j] > 0)
            def _():
                k = k_ref[pl.ds(k0, tk), g * HEAD_DIM:(g + 1) * HEAD_DIM]
                vt = vts_ref[g, :, pl.ds(k0, tk)]
                off = k0 - t0
                expand = jnp.where(e_rel == k0 // NSA_BLOCK, 1.0, 0.0).astype(BF16)
                chosen = jnp.dot(expand, sel_t, preferred_element_type=F32)
                valid = jnp.where(d_kq <= -off, chosen, 0.0) > 0.5
                s = jnp.dot(k, q4t, preferred_element_type=F32)
                online_update(masked_scores(s, valid, sub_kf + off.astype(F32), g), vt)

            return carry

        lax.fori_loop(0, (t0 + tq + tk - 1) // tk, slc_body, 0)
        oslc_t = acc_ref[...] / l_ref[...]

        w0 = pl.multiple_of(jnp.maximum(t0 + tq - wspan, 0), HEAD_DIM)
        kw = k_ref[pl.ds(w0, wspan), (NSA_KV_HEADS + g) * HEAD_DIM:(NSA_KV_HEADS + g + 1) * HEAD_DIM]
        woff = w0 - t0
        wvalid = jnp.abs(wwin_code - 2 * woff) <= (NSA_WINDOW - 1)
        wbase = wsub_kf + woff.astype(F32)
        sw = jnp.dot(kw, q4t, preferred_element_type=F32)
        sw = jnp.concatenate(
            [jnp.where(wvalid, sw[:, r * tq:(r + 1) * tq] + (SLOPES[g][r] * LOG2E) * wbase, NEG)
             for r in range(NSA_GROUP)], axis=1)
        pw = jnp.exp2(sw - jnp.max(sw, axis=0, keepdims=True))
        owin_t = (jnp.dot(vtw_ref[g, :, pl.ds(w0, wspan)], pw.astype(BF16), preferred_element_type=F32)
                  / jnp.sum(pw, axis=0, keepdims=True))

        for r in range(NSA_GROUP):
            c = g * NSA_GROUP * 3 + r * 3 + GATE_LANE
            rs = slice(r * tq, (r + 1) * tq)
            o_t = (gates_t[c:c + 1, :] * ocmp_t[:, rs] + gates_t[c + 1:c + 2, :] * oslc_t[:, rs]
                   + gates_t[c + 2:c + 3, :] * owin_t[:, rs])
            h = g * NSA_GROUP + r
            o_ref[:, h * HEAD_DIM:(h + 1) * HEAD_DIM] = o_t.T.astype(o_ref.dtype)


def _kv_prep_kernel(kvs_ref, kvw_ref, kmat_ref, vt_ref):
    for idx, (src, g) in enumerate([(kvs_ref, 0), (kvs_ref, 1), (kvw_ref, 0), (kvw_ref, 1)]):
        kmat_ref[:, idx * HEAD_DIM:(idx + 1) * HEAD_DIM] = src[:, 2 * g * HEAD_DIM:(2 * g + 1) * HEAD_DIM].astype(BF16)
        vt_ref[idx] = src[:, (2 * g + 1) * HEAD_DIM:(2 * g + 2) * HEAD_DIM].T.astype(BF16)


def kv_prep(P, *, n_seq, tm):
    M = P.shape[0]
    T = M // n_seq
    assert T % tm == 0
    tps = T // tm
    return pl.pallas_call(
        _kv_prep_kernel,
        grid=(M // tm,),
        in_specs=[pl.BlockSpec((tm, KV_COLS), lambda i: (i, KVS0 // KV_COLS)),
                  pl.BlockSpec((tm, KV_COLS), lambda i: (i, KVW0 // KV_COLS))],
        out_specs=[pl.BlockSpec((tm, 4 * HEAD_DIM), lambda i: (i, 0)),
                   pl.BlockSpec((None, 4, HEAD_DIM, tm), lambda i: (i // tps, 0, 0, i % tps))],
        out_shape=[jax.ShapeDtypeStruct((M, 4 * HEAD_DIM), BF16),
                   jax.ShapeDtypeStruct((n_seq, 4, HEAD_DIM, T), BF16)],
        compiler_params=_cparams("parallel"),
        name="kv_prep",
    )(P, P)


def nsa_prompt(P, kmat, vt, ck, cvt, *, n_seq, tq, tk):
    M = P.shape[0]
    T = M // n_seq
    nblk = T // NSA_BLOCK
    nq = T // tq
    W = NSA_GROUP * tq
    assert T % tq == 0 and T % tk == 0 and tk % NSA_BLOCK == 0 and tk % HEAD_DIM == 0
    return pl.pallas_call(
        functools.partial(_nsa_prompt_kernel, tq=tq, tk=tk, nblk=nblk),
        grid=(n_seq, nq),
        in_specs=[pl.BlockSpec((tq, NSA_WIDTH), lambda b, i: (b * nq + i, Q0 // NSA_WIDTH)),
                  pl.BlockSpec((tq, HEAD_DIM), lambda b, i: (b * nq + i, MISC0 // HEAD_DIM)),
                  pl.BlockSpec((T, 2 * NSA_KV_HEADS * HEAD_DIM), lambda b, i: (b, 0)),
                  pl.BlockSpec((None, NSA_KV_HEADS, HEAD_DIM, T), lambda b, i: (b, 0, 0, 0)),
                  pl.BlockSpec((None, NSA_KV_HEADS, HEAD_DIM, T), lambda b, i: (b, 1, 0, 0)),
                  pl.BlockSpec((None, NSA_KV_HEADS, nblk, HEAD_DIM), lambda b, i: (b, 0, 0, 0)),
                  pl.BlockSpec((None, NSA_KV_HEADS, HEAD_DIM, nblk), lambda b, i: (b, 0, 0, 0))],
        out_specs=pl.BlockSpec((tq, NSA_WIDTH), lambda b, i: (b * nq + i, 0)),
        out_shape=jax.ShapeDtypeStruct((M, NSA_WIDTH), BF16),
        scratch_shapes=[pltpu.VMEM((1, W), F32),
                        pltpu.VMEM((1, W), F32),
                        pltpu.VMEM((HEAD_DIM, W), F32),
                        pltpu.SMEM((T // tk,), jnp.int32)],
        compiler_params=_cparams("parallel", "arbitrary"),
        name="nsa_prompt",
    )(P, P, kmat, vt, vt, ck, cvt)


def _bdot(a, b):
    return lax.dot_general(a.astype(BF16), b.astype(BF16), (((2,), (1,)), ((0,), (0,))), preferred_element_type=F32)


def _bdot_nt(a, b):
    return lax.dot_general(a.astype(BF16), b.astype(BF16), (((2,), (2,)), ((0,), (0,))), preferred_element_type=F32)


def _bdot_tn(a, b):
    return lax.dot_general(a.astype(BF16), b.astype(BF16), (((1,), (1,)), ((0,), (0,))), preferred_element_type=F32)


def _head_stack(a, off, width=HEAD_DIM):
    return jnp.stack([a[:, off + h * width:off + (h + 1) * width] for h in range(GDN_HEADS)], axis=0)


def _unit_lower_inverse_minus_eye(A, C):
    row = lax.broadcasted_iota(jnp.int32, (C, C), 0)
    col = lax.broadcasted_iota(jnp.int32, (C, C), 1)
    A8 = jnp.where((row // 8) == (col // 8), A, 0.0)
    B2 = _bdot(A8, A8)
    B4 = _bdot(B2, B2)
    P1 = B2 - A8 - _bdot(A8, B2)
    N = P1 + B4 + _bdot(P1, B4)
    size = 16
    while size <= C:
        AL = jnp.where((row // size) == (col // size), jnp.where((row // (size // 2)) == (col // (size // 2)), 0.0, A), 0.0)
        X = AL + _bdot(N, AL)
        N = N - X - _bdot(X, N)
        size *= 2
    return N


def _gdn_prep_kernel(*refs, C, sample):
    if sample:
        x_ref, buf_ref, misc_ref, cw_ref, alog_ref, dtb_ref = refs[:6]
    else:
        x_ref, prev_ref, misc_ref, cw_ref, alog_ref, dtb_ref = refs[:6]
    u_ref, w_ref, qg_ref, kd_ref, qk_ref, gl_ref = refs[6:12]
    cw = cw_ref[...]
    x = x_ref[...]
    rowv = lax.broadcasted_iota(jnp.int32, (C, 1), 0)
    if sample:
        isbuf = (rowv >= ZTOK0 - (GDN_CONV - 1)) & (rowv < ZTOK0)
        xz = jnp.where(isbuf, buf_ref[...], x)
        y = (cw[3:4, :] * xz + cw[2:3, :] * pltpu.roll(xz, 1, 0) + cw[1:2, :] * pltpu.roll(xz, 2, 0)
             + cw[0:1, :] * pltpu.roll(xz, 3, 0))
        valid = jnp.where((rowv >= ZTOK0) & (rowv < ZTOK0 + 4), 1.0, 0.0)
    else:
        prev = jnp.where(pl.program_id(1) == 0, 0.0, prev_ref[...])
        xe = jnp.concatenate([prev, x], axis=0)
        y = (cw[3:4, :] * x + cw[2:3, :] * pltpu.roll(xe, 1, 0)[8:] + cw[1:2, :] * pltpu.roll(xe, 2, 0)[8:]
             + cw[0:1, :] * pltpu.roll(xe, 3, 0)[8:])
        valid = None
    y = jax.nn.silu(y)

    misc = misc_ref[...]
    gfull = -jnp.exp(alog_ref[...]) * jax.nn.softplus(misc + dtb_ref[...])
    bfull = jax.nn.sigmoid(misc)
    if valid is not None:
        gfull = gfull * valid
        bfull = bfull * valid
    row128 = lax.broadcasted_iota(jnp.int32, (C, HEAD_DIM), 0)
    G = gfull
    s = 1
    while s < C:
        G = G + jnp.where(row128 >= s, pltpu.roll(G, s, 0), 0.0)
        s *= 2
    expG = jnp.exp(G)
    glast = G[C - 1:C, :]
    kdfac = jnp.exp(glast - G)
    if C < HEAD_DIM:
        Gpad = jnp.concatenate([G, jnp.zeros((HEAD_DIM - C, HEAD_DIM), F32)], axis=0)
    else:
        Gpad = G
    GT = Gpad.T
    gl_ref[...] = jnp.exp(jnp.broadcast_to(GT[A_LANE:A_LANE + GDN_HEADS, C - 1:C], (GDN_HEADS, HEAD_DIM)))

    row = lax.broadcasted_iota(jnp.int32, (C, C), 0)
    col = lax.broadcasted_iota(jnp.int32, (C, C), 1)
    q = _head_stack(y, 0)
    k = _head_stack(y, GDN_WIDTH)
    v = _head_stack(y, 2 * GDN_WIDTH)
    q = q * lax.rsqrt(jnp.sum(q * q, -1, keepdims=True) + NORM_EPS) * (HEAD_DIM ** -0.5)
    k = k * lax.rsqrt(jnp.sum(k * k, -1, keepdims=True) + NORM_EPS)
    if valid is not None:
        q, k, v = q * valid, k * valid, v * valid
    beta = _head_stack(bfull, B_LANE, 1)
    gcol = _head_stack(G, A_LANE, 1)
    egc = _head_stack(expG, A_LANE, 1)
    kdf = _head_stack(kdfac, A_LANE, 1)
    grow = jnp.stack([GT[A_LANE + h:A_LANE + h + 1, :C] for h in range(GDN_HEADS)], axis=0)
    decay = jnp.exp(jnp.where(row >= col, gcol - grow, NEG))
    kb = k * beta
    A = jnp.where(row > col, _bdot_nt(kb, k) * decay, 0.0)
    N = _unit_lower_inverse_minus_eye(A, C)
    vb = v * beta
    kbg = kb * egc
    U = vb + _bdot(N, vb)
    Wm = kbg + _bdot(N, kbg)
    QK = _bdot_nt(q, k) * decay
    QG = q * egc
    KD = k * kdf
    for h in range(GDN_HEADS):
        sl = slice(h * HEAD_DIM, (h + 1) * HEAD_DIM)
        u_ref[:, sl] = U[h]
        w_ref[:, sl] = Wm[h].astype(w_ref.dtype)
        qk_ref[:, h * C:(h + 1) * C] = QK[h].astype(qk_ref.dtype)
        qg_ref[:, sl] = QG[h].astype(qg_ref.dtype)
        kd_ref[:, sl] = KD[h].astype(kd_ref.dtype)


def gdn_prep(P, cw, alog_row, dtb_row, *, n_seq, C, bufz=None):
    M = P.shape[0]
    T = M // n_seq
    n = T // C
    sample = bufz is not None
    assert T % C == 0 and C % 8 == 0
    qkv_spec = pl.BlockSpec((C, 3 * GDN_WIDTH), lambda b, c: (b * n + c, 0))
    if sample:
        second = pl.BlockSpec((C, 3 * GDN_WIDTH), lambda b, c: (b * n + c, 0))
        second_arr = bufz
    else:
        second = pl.BlockSpec((8, 3 * GDN_WIDTH), lambda b, c: (jnp.maximum((b * n + c) * (C // 8) - 1, 0), 0))
        second_arr = P
    in_specs = [qkv_spec, second,
                pl.BlockSpec((C, HEAD_DIM), lambda b, c: (b * n + c, MISC0 // HEAD_DIM)),
                pl.BlockSpec((GDN_CONV, 3 * GDN_WIDTH), lambda b, c: (0, 0)),
                pl.BlockSpec((1, HEAD_DIM), lambda b, c: (0, 0)),
                pl.BlockSpec((1, HEAD_DIM), lambda b, c: (0, 0))]
    row_spec = pl.BlockSpec((C, GDN_WIDTH), lambda b, c: (b * n + c, 0))
    out_specs = [row_spec, row_spec, row_spec, row_spec,
                 pl.BlockSpec((C, GDN_HEADS * C), lambda b, c: (b * n + c, 0)),
                 pl.BlockSpec((None, GDN_HEADS, HEAD_DIM), lambda b, c: (b * n + c, 0, 0))]
    out_shape = [jax.ShapeDtypeStruct((M, GDN_WIDTH), F32),
                 jax.ShapeDtypeStruct((M, GDN_WIDTH), BF16),
                 jax.ShapeDtypeStruct((M, GDN_WIDTH), BF16),
                 jax.ShapeDtypeStruct((M, GDN_WIDTH), BF16),
                 jax.ShapeDtypeStruct((M, GDN_HEADS * C), BF16),
                 jax.ShapeDtypeStruct((n_seq * n, GDN_HEADS, HEAD_DIM), F32)]
    return pl.pallas_call(
        functools.partial(_gdn_prep_kernel, C=C, sample=sample),
        grid=(n_seq, n),
        in_specs=in_specs,
        out_specs=out_specs,
        out_shape=out_shape,
        compiler_params=_cparams("parallel", "arbitrary"),
        name="gdn_prep_sample" if sample else "gdn_prep_prompt",
    )(P, second_arr, P, cw, alog_row, dtb_row)


def _gdn_scan_kernel(u_ref, w_ref, qg_ref, kd_ref, qk_ref, gl_ref, z_ref, ng_ref, s0_ref, o_ref, sfin_ref, S_ref, *, C):
    c = pl.program_id(1)

    @pl.when(c == 0)
    def _():
        S_ref[...] = s0_ref[...]

    S = S_ref[...]
    Sb = S.astype(BF16)
    v_new = _head_stack(u_ref[...], 0) - _bdot(_head_stack(w_ref[...], 0), Sb)
    vb = v_new.astype(BF16)
    o = _bdot(_head_stack(qg_ref[...], 0), Sb) + _bdot(_head_stack(qk_ref[...], 0, C), vb)
    gl = jnp.stack([gl_ref[h:h + 1, :] for h in range(GDN_HEADS)], axis=0)
    S_ref[...] = S * gl + _bdot_tn(_head_stack(kd_ref[...], 0), vb)
    o = o * lax.rsqrt(jnp.mean(o * o, -1, keepdims=True) + NORM_EPS) * ng_ref[...]
    o = o * jax.nn.silu(_head_stack(z_ref[...], 0))
    for h in range(GDN_HEADS):
        o_ref[:, h * HEAD_DIM:(h + 1) * HEAD_DIM] = o[h].astype(o_ref.dtype)

    @pl.when(c == pl.num_programs(1) - 1)
    def _():
        sfin_ref[...] = S_ref[...]


def gdn_scan(prep, P, norm_g, s0, *, n_seq, C):
    u, w, qg, kd, qk, gl = prep
    M = u.shape[0]
    T = M // n_seq
    n = T // C
    row_spec = pl.BlockSpec((C, GDN_WIDTH), lambda b, c: (b * n + c, 0))
    st_spec = pl.BlockSpec((None, GDN_HEADS, HEAD_DIM, HEAD_DIM), lambda b, c: (b, 0, 0, 0))
    return pl.pallas_call(
        functools.partial(_gdn_scan_kernel, C=C),
        grid=(n_seq, n),
        in_specs=[row_spec, row_spec, row_spec, row_spec,
                  pl.BlockSpec((C, GDN_HEADS * C), lambda b, c: (b * n + c, 0)),
                  pl.BlockSpec((None, GDN_HEADS, HEAD_DIM), lambda b, c: (b * n + c, 0, 0)),
                  pl.BlockSpec((C, GDN_WIDTH), lambda b, c: (b * n + c, Z0 // GDN_WIDTH)),
                  pl.BlockSpec((1, HEAD_DIM), lambda b, c: (0, 0)),
                  st_spec],
        out_specs=[row_spec, st_spec],
        out_shape=[jax.ShapeDtypeStruct((M, GDN_WIDTH), BF16),
                   jax.ShapeDtypeStruct((n_seq, GDN_HEADS, HEAD_DIM, HEAD_DIM), F32)],
        scratch_shapes=[pltpu.VMEM((GDN_HEADS, HEAD_DIM, HEAD_DIM), F32)],
        compiler_params=_cparams("parallel", "arbitrary"),
        name="gdn_scan",
    )(u, w, qg, kd, qk, gl, P, norm_g, s0)


PAGES_PER_STEP = 32
BLOCK_ROWS = NSA_BLOCK * 4
PAGE_ROWS = 2 * BLOCK_ROWS


def _compress_sample_kernel(pt_ref, cache_ref, pe_ref, w1_ref, w2_ref, out_ref, xbuf, sem, *, layer, n_pages, pps):
    b = pl.program_id(0)
    gi = pl.program_id(1)
    ng = pl.num_programs(1)
    step = b * ng + gi
    nsteps = pl.num_programs(0) * ng
    slot = step % 2

    def copies(st, sl):
        bb = st // ng
        g0 = (st % ng) * pps
        out = []
        for jl in range(pps):
            page = pt_ref[bb * n_pages + g0 + jl]
            for half in range(2):
                out.append(pltpu.make_async_copy(
                    cache_ref.at[layer, page, pl.ds(half * BLOCK_ROWS, BLOCK_ROWS), :],
                    xbuf.at[sl, :, 2 * jl + half, :],
                    sem.at[sl]))
        return out

    @pl.when(step == 0)
    def _():
        for cp in copies(step, slot):
            cp.start()

    @pl.when(step + 1 < nsteps)
    def _():
        for cp in copies(step + 1, 1 - slot):
            cp.start()

    for cp in copies(step, slot):
        cp.wait()

    nblk = 2 * pps

    _compress_rows(lambda gc, p: xbuf[slot, p * 4 + gc], pe_ref, w1_ref, w2_ref, out_ref, nblk)


def compress_sample(page_table, cache_rows, pe_t, w1, w2, *, layer):
    Bs, n_pages = page_table.shape
    pps = min(PAGES_PER_STEP, n_pages)
    assert n_pages % pps == 0
    ng = n_pages // pps
    nblk = 2 * pps
    grid_spec = pltpu.PrefetchScalarGridSpec(
        num_scalar_prefetch=1,
        grid=(Bs, ng),
        in_specs=[pl.BlockSpec(memory_space=pl.ANY),
                  pl.BlockSpec(pe_t.shape, lambda b, g, pt: (0, 0, 0)),
                  pl.BlockSpec(w1.shape, lambda b, g, pt: (0, 0, 0)),
                  pl.BlockSpec(w2.shape, lambda b, g, pt: (0, 0, 0))],
        out_specs=pl.BlockSpec((None, None, 2, NSA_KV_HEADS * nblk, HEAD_DIM), lambda b, g, pt: (b, g, 0, 0, 0)),
        scratch_shapes=[pltpu.VMEM((2, BLOCK_ROWS, nblk, HEAD_DIM), F32),
                        pltpu.SemaphoreType.DMA((2,))],
    )
    return pl.pallas_call(
        functools.partial(_compress_sample_kernel, layer=layer, n_pages=n_pages, pps=pps),
        grid_spec=grid_spec,
        out_shape=jax.ShapeDtypeStruct((Bs, ng, 2, NSA_KV_HEADS * nblk, HEAD_DIM), F32),
        compiler_params=_cparams("arbitrary", "arbitrary"),
        name="compress_sample",
    )(page_table.reshape(-1), cache_rows, pe_t, w1, w2)


def _stack_heads(q_ref, g, scale):
    return jnp.concatenate(
        [q_ref[:, (g * NSA_GROUP + r) * HEAD_DIM:(g * NSA_GROUP + r + 1) * HEAD_DIM] * scale for r in range(NSA_GROUP)],
        axis=0).astype(BF16)


def _stacked_row_info(past):
    rows = NSA_GROUP * ZROWS
    ridx = lax.broadcasted_iota(jnp.int32, (rows, 1), 0)
    zrow = ridx % ZROWS
    qpos = past + zrow - ZTOK0
    slope = jnp.zeros((rows, 1), F32)
    return ridx, zrow, qpos, slope


def _slope_col(g):
    ridx = lax.broadcasted_iota(jnp.int32, (NSA_GROUP * ZROWS, 1), 0)
    sl = jnp.zeros((NSA_GROUP * ZROWS, 1), F32)
    for r in range(NSA_GROUP):
        sl = jnp.where(ridx // ZROWS == r, SLOPES[g][r], sl)
    return sl


def _nsa_sample_a_kernel(q_ref, kvw_ref, ck_ref, cv_ref, win_ref, ocmp_ref, owin_ref, sel_ref, wout_ref, *, past, nc, n_blocks, seq):
    scale = HEAD_DIM ** -0.5
    _, zrow, qpos, _ = _stacked_row_info(past)
    wb = win_ref.shape[0] // 4
    lanes_pad = sel_ref.shape[-1]
    sel_lanes = ((n_blocks + HEAD_DIM - 1) // HEAD_DIM) * HEAD_DIM
    top_n = min(NSA_TOP_N, n_blocks)
    for g in range(NSA_KV_HEADS):
        q4 = _stack_heads(q_ref, g, scale)
        slope = _slope_col(g)
        cpos = lax.broadcasted_iota(jnp.int32, (1, nc), 1) * NSA_BLOCK + (NSA_BLOCK - 1)
        vm = cpos <= qpos
        s = jnp.where(vm, _dot_nt(q4, ck_ref[g]) + slope * (cpos - past).astype(F32), NEG)
        e = jnp.exp(s - jnp.max(s, axis=-1, keepdims=True))
        p = jnp.where(vm, e / jnp.sum(e, axis=-1, keepdims=True), 0.0)
        ocmp_ref[g] = _dot(p, cv_ref[g])
        imp = p[0:ZROWS]
        for r in range(1, NSA_GROUP):
            imp = imp + p[r * ZROWS:(r + 1) * ZROWS]
        imp = jnp.concatenate([imp, jnp.zeros((ZROWS, sel_lanes - nc), F32)], axis=1)
        rank = _rank_desc(_block_scores(imp, qpos[0:ZROWS], n_blocks), n_blocks)
        lane = lax.broadcasted_iota(jnp.int32, rank.shape, 1)
        olane = lax.broadcasted_iota(jnp.int32, (ZROWS, lanes_pad), 1)
        out = jnp.zeros((ZROWS, lanes_pad), jnp.int32)
        for t in range(top_n):
            idx = jnp.sum(jnp.where(rank == float(t), lane.astype(F32), 0.0), axis=-1, keepdims=True)
            out = jnp.where(olane == t, idx.astype(jnp.int32), out)
        sel_ref[g] = out
        kold = win_ref[pl.ds(g * 2, wb, stride=4), :]
        vold = win_ref[pl.ds(g * 2 + 1, wb, stride=4), :]
        knew = kvw_ref[:, g * 2 * HEAD_DIM:(g * 2 + 1) * HEAD_DIM]
        vnew = kvw_ref[:, (g * 2 + 1) * HEAD_DIM:(g * 2 + 2) * HEAD_DIM]
        kpos_o = past - wb + lax.broadcasted_iota(jnp.int32, (1, wb), 1)
        kz = lax.broadcasted_iota(jnp.int32, (1, ZROWS), 1)
        kpos_n = past + kz - ZTOK0
        d_o = qpos - kpos_o
        d_n = qpos - kpos_n
        ok_o = (kpos_o >= 0) & (d_o >= 0) & (d_o < NSA_WINDOW)
        ok_n = (kz >= ZTOK0) & (kz < ZTOK0 + seq) & (d_n >= 0) & (d_n < NSA_WINDOW)
        s_o = jnp.where(ok_o, _dot_nt(q4, kold) + slope * (kpos_o - past).astype(F32), NEG)
        s_n = jnp.where(ok_n, _dot_nt(q4, knew) + slope * (kpos_n - past).astype(F32), NEG)
        m = jnp.maximum(jnp.max(s_o, axis=-1, keepdims=True), jnp.max(s_n, axis=-1, keepdims=True))
        p_o = jnp.exp(s_o - m)
        p_n = jnp.exp(s_n - m)
        den = jnp.sum(p_o, axis=-1, keepdims=True) + jnp.sum(p_n, axis=-1, keepdims=True)
        owin_ref[g] = (_dot(p_o, vold) + _dot(p_n, vnew)) / den
    keep = (wb - seq) * 4
    wout_ref[0:keep, :] = win_ref[seq * 4:wb * 4, :]
    ridx = lax.broadcasted_iota(jnp.int32, (seq * 4, 1), 0)
    new_rows = jnp.zeros((seq * 4, HEAD_DIM), F32)
    for t in range(seq):
        for gc in range(4):
            new_rows = jnp.where(ridx == t * 4 + gc,
                                 kvw_ref[ZTOK0 + t:ZTOK0 + t + 1, gc * HEAD_DIM:(gc + 1) * HEAD_DIM], new_rows)
    wout_ref[keep:wb * 4, :] = new_rows


def nsa_sample_a(Ps, ckv, cache_win_rows, *, layer, past, seq):
    Bs = Ps.shape[0] // ZROWS
    nc = ckv.shape[3]
    wrows = cache_win_rows.shape[2]
    n_blocks = -(-(past + seq) // NSA_BLOCK)
    assert 1 <= seq <= 4 and (seq * 4) % 8 == 0 and wrows % 8 == 0
    rows = NSA_GROUP * ZROWS
    big = pl.BlockSpec((None, NSA_KV_HEADS, rows, HEAD_DIM), lambda b: (b, 0, 0, 0))
    return pl.pallas_call(
        functools.partial(_nsa_sample_a_kernel, past=past, nc=nc, n_blocks=n_blocks, seq=seq),
        grid=(Bs,),
        in_specs=[pl.BlockSpec((ZROWS, NSA_WIDTH), lambda b: (b, Q0 // NSA_WIDTH)),
                  pl.BlockSpec((ZROWS, KV_COLS), lambda b: (b, KVW0 // KV_COLS)),
                  pl.BlockSpec((None, None, NSA_KV_HEADS, nc, HEAD_DIM), lambda b: (b, 0, 0, 0, 0)),
                  pl.BlockSpec((None, None, NSA_KV_HEADS, nc, HEAD_DIM), lambda b: (b, 1, 0, 0, 0)),
                  pl.BlockSpec((None, None, wrows, HEAD_DIM), lambda b: (layer, b, 0, 0))],
        out_specs=[big, big,
                   pl.BlockSpec((None, NSA_KV_HEADS, ZROWS, HEAD_DIM), lambda b: (b, 0, 0, 0)),
                   pl.BlockSpec((None, wrows, HEAD_DIM), lambda b: (b, 0, 0))],
        out_shape=[jax.ShapeDtypeStruct((Bs, NSA_KV_HEADS, rows, HEAD_DIM), F32),
                   jax.ShapeDtypeStruct((Bs, NSA_KV_HEADS, rows, HEAD_DIM), F32),
                   jax.ShapeDtypeStruct((Bs, NSA_KV_HEADS, ZROWS, HEAD_DIM), jnp.int32),
                   jax.ShapeDtypeStruct((Bs, wrows, HEAD_DIM), F32)],
        compiler_params=_cparams("parallel"),
        name="nsa_sample_a",
    )(Ps, Ps, ckv, ckv, cache_win_rows)


def _nsa_sample_b_kernel(pt_ref, sel_ref, q_ref, kvs_ref, misc_ref, ocmp_ref, owin_ref, cache_ref, o_ref, kvbuf, sem,
                         *, layer, past, seq, n_past_blocks, n_pages):
    b = pl.program_id(0)
    g = pl.program_id(1)
    ngrp = pl.num_programs(1)
    step = b * ngrp + g
    nsteps = pl.num_programs(0) * ngrp
    slot = step % 2

    def copies(st, sl):
        bb = st // ngrp
        gg = st % ngrp
        out = []
        for t in range(seq):
            for j in range(NSA_TOP_N):
                blk = jnp.minimum(sel_ref[((bb * seq + t) * NSA_KV_HEADS + gg) * NSA_TOP_N + j], n_past_blocks - 1)
                page = pt_ref[bb * n_pages + blk // 2]
                row0 = pl.multiple_of((blk % 2) * BLOCK_ROWS, BLOCK_ROWS)
                out.append(pltpu.make_async_copy(
                    cache_ref.at[layer, page, pl.ds(row0, BLOCK_ROWS), :],
                    kvbuf.at[sl, t * NSA_TOP_N + j],
                    sem.at[sl]))
        return out

    @pl.when(step == 0)
    def _():
        for cp in copies(step, slot):
            cp.start()

    @pl.when(step + 1 < nsteps)
    def _():
        for cp in copies(step + 1, 1 - slot):
            cp.start()

    for cp in copies(step, slot):
        cp.wait()

    scale = HEAD_DIM ** -0.5
    nkeys = NSA_TOP_N * NSA_BLOCK
    _, zrow, qpos, _ = _stacked_row_info(past)
    q4 = jnp.where(g == 0, _stack_heads(q_ref, 0, scale), _stack_heads(q_ref, 1, scale))
    slope = jnp.where(g == 0, _slope_col(0), _slope_col(1))

    kn = jnp.where(g == 0, kvs_ref[:, 0:HEAD_DIM], kvs_ref[:, 2 * HEAD_DIM:3 * HEAD_DIM])
    vn = jnp.where(g == 0, kvs_ref[:, HEAD_DIM:2 * HEAD_DIM], kvs_ref[:, 3 * HEAD_DIM:4 * HEAD_DIM])
    kz = lax.broadcasted_iota(jnp.int32, (1, ZROWS), 1)
    kpos_n = past + kz - ZTOK0
    ok_n = (kz >= ZTOK0) & (kz < ZTOK0 + seq) & (kpos_n <= qpos)
    s_n = jnp.where(ok_n, _dot_nt(q4, kn) + slope * (kpos_n - past).astype(F32), NEG)

    lane = lax.broadcasted_iota(jnp.int32, (1, nkeys), 1)
    slot_of_lane = lane // NSA_BLOCK
    s_g = jnp.full((NSA_GROUP * ZROWS, nkeys), NEG, F32)
    v_tok = []
    for t in range(seq):
        blk_of_lane = jnp.zeros((1, nkeys), jnp.int32)
        for j in range(NSA_TOP_N):
            blk = sel_ref[((b * seq + t) * NSA_KV_HEADS + g) * NSA_TOP_N + j]
            blk_of_lane = jnp.where(slot_of_lane == j, blk, blk_of_lane)
        k_t = jnp.concatenate([kvbuf[slot, t * NSA_TOP_N + j, pl.ds(g * 2, NSA_BLOCK, stride=4), :]
                               for j in range(NSA_TOP_N)], axis=0)
        v_tok.append(jnp.concatenate([kvbuf[slot, t * NSA_TOP_N + j, pl.ds(g * 2 + 1, NSA_BLOCK, stride=4), :]
                                      for j in range(NSA_TOP_N)], axis=0))
        kpos = blk_of_lane * NSA_BLOCK + lane % NSA_BLOCK
        st = _dot_nt(q4, k_t) + slope * (kpos - past).astype(F32)
        mine = (zrow == ZTOK0 + t) & (blk_of_lane < n_past_blocks) & (kpos <= qpos)
        s_g = jnp.where(mine, st, s_g)

    m = jnp.maximum(jnp.max(s_g, axis=-1, keepdims=True), jnp.max(s_n, axis=-1, keepdims=True))
    p_g = jnp.exp(s_g - m)
    p_n = jnp.exp(s_n - m)
    den = jnp.sum(p_g, axis=-1, keepdims=True) + jnp.sum(p_n, axis=-1, keepdims=True)
    pv = _dot(p_n, vn)
    for t in range(seq):
        pv = pv + jnp.where(zrow == ZTOK0 + t, _dot(p_g, v_tok[t]), 0.0)
    o_slc = pv / den

    gates = jax.nn.sigmoid(misc_ref[...])
    o_cmp = ocmp_ref[...]
    o_win = owin_ref[...]
    for r in range(NSA_GROUP):
        rs = slice(r * ZROWS, (r + 1) * ZROWS)
        outs = []
        for gg in range(NSA_KV_HEADS):
            c = gg * NSA_GROUP * 3 + r * 3 + GATE_LANE
            outs.append(gates[:, c:c + 1] * o_cmp[rs] + gates[:, c + 1:c + 2] * o_slc[rs]
                        + gates[:, c + 2:c + 3] * o_win[rs])
        o_ref[:, r * HEAD_DIM:(r + 1) * HEAD_DIM] = jnp.where(g == 0, outs[0], outs[1]).astype(o_ref.dtype)


def nsa_sample_b(page_table, sel, Ps, ocmp, owin, cache_rows, *, layer, past, seq):
    Bs, n_pages = page_table.shape
    rows = NSA_GROUP * ZROWS
    n_past_blocks = past // NSA_BLOCK
    gw = NSA_GROUP * HEAD_DIM

    big = pl.BlockSpec((None, None, rows, HEAD_DIM), lambda b, g, pt, sl: (b, g, 0, 0))
    in_specs = [pl.BlockSpec((ZROWS, NSA_WIDTH), lambda b, g, pt, sl: (b, Q0 // NSA_WIDTH)),
                pl.BlockSpec((ZROWS, KV_COLS), lambda b, g, pt, sl: (b, KVS0 // KV_COLS)),
                pl.BlockSpec((ZROWS, HEAD_DIM), lambda b, g, pt, sl: (b, MISC0 // HEAD_DIM)),
                big, big,
                pl.BlockSpec(memory_space=pl.ANY)]
    grid_spec = pltpu.PrefetchScalarGridSpec(
        num_scalar_prefetch=2,
        grid=(Bs, NSA_KV_HEADS),
        in_specs=in_specs,
        out_specs=pl.BlockSpec((ZROWS, gw), lambda b, g, pt, sl: (b, g)),
        scratch_shapes=[pltpu.VMEM((2, seq * NSA_TOP_N, BLOCK_ROWS, HEAD_DIM), F32),
                        pltpu.SemaphoreType.DMA((2,))],
    )
    return pl.pallas_call(
        functools.partial(_nsa_sample_b_kernel, layer=layer, past=past, seq=seq, n_past_blocks=n_past_blocks,
                          n_pages=n_pages),
        grid_spec=grid_spec,
        out_shape=jax.ShapeDtypeStruct((Bs * ZROWS, NSA_WIDTH), BF16),
        compiler_params=_cparams("arbitrary", "arbitrary"),
        name="nsa_sample_b",
    )(page_table.reshape(-1), sel, Ps, Ps, Ps, ocmp, owin, cache_rows)


def _reorder_w_in(w_in):
    q, kvc, kvs, kvw = w_in[..., 0:1024], w_in[..., 1024:1536], w_in[..., 1536:2048], w_in[..., 2048:2560]
    gates, qkv, z, ab = w_in[..., 2560:2584], w_in[..., 2584:5656], w_in[..., 5656:6680], w_in[..., 6680:6696]
    pad = jnp.zeros(w_in.shape[:-1] + (IN_COLS_PAD - MISC0 - 40,), w_in.dtype)
    return jnp.concatenate([qkv, q, z, kvc, kvs, kvw, gates, ab, pad], axis=-1).astype(BF16)


def _lane_row(vec, lane0):
    return jnp.zeros((1, HEAD_DIM), F32).at[0, lane0:lane0 + vec.shape[0]].set(vec.astype(F32))


def _to_zrows(x, first_row):
    Bs, n, C = x.shape
    z = jnp.zeros((Bs, ZROWS, C), x.dtype).at[:, first_row:first_row + n].set(x)
    return z.reshape(Bs * ZROWS, C)


def kernel(x_prompt, x_sample, cache_cmp_kv, cache_slc_kv, cache_win_kv, state_gdn, state_gdn_conv, state_ffn_conv, page_table, w_in, w_o, cmp_pe, cmp_w1, cmp_w2, gdn_conv_w, gdn_a_log, gdn_dt_bias, gdn_norm_g, ln1_g, ln1_b, ffn_w_up, ffn_conv_w, ffn_w_down, ln2_g, ln2_b):
    B, S, D = x_prompt.shape
    Bs, seq, _ = x_sample.shape
    depth = w_in.shape[0]
    n_pool, page = cache_cmp_kv.shape[1], cache_cmp_kv.shape[2]
    n_pages = page_table.shape[1]
    past = n_pages * page
    wb = cache_win_kv.shape[2]
    d_ff = ffn_w_down.shape[1]
    alpha = (2 * depth) ** 0.25
    assert page == 2 * NSA_BLOCK

    w_in_r = _reorder_w_in(w_in)
    w_o_b = w_o.astype(BF16)
    w_up_b = ffn_w_up.astype(BF16)
    w_down_b = ffn_w_down.astype(BF16)
    w1_b = cmp_w1.astype(BF16)
    w2_b = cmp_w2.astype(BF16)
    pe_t = jnp.swapaxes(cmp_pe, 1, 2)
    cache_cmp_rows = cache_cmp_kv.reshape(depth, n_pool, PAGE_ROWS, HEAD_DIM)
    cache_slc_rows = cache_slc_kv.reshape(depth, n_pool, PAGE_ROWS, HEAD_DIM)
    cache_win_rows = cache_win_kv.reshape(depth, Bs, wb * 4, HEAD_DIM)

    xp = x_prompt.reshape(B * S, D)
    xs = _to_zrows(x_sample, ZTOK0)
    xp_b, xs_b = xp, xs
    tm = 512 if (B * S) % 512 == 0 else B * S
    tm_in = 1024 if (B * S) % 1024 == 0 else tm
    tk_down = 1408 if d_ff % 1408 == 0 else 512
    tq = 128
    p_out = [[] for _ in range(6)]
    s_out = [[] for _ in range(6)]
    zero_state = jnp.zeros((B, GDN_HEADS, HEAD_DIM, HEAD_DIM), F32)

    for l in range(depth):
        alog_row = _lane_row(gdn_a_log[l], A_LANE)
        dtb_row = _lane_row(gdn_dt_bias[l], A_LANE)
        ng = gdn_norm_g[l].reshape(1, HEAD_DIM)
        g1, b1 = ln1_g[l].reshape(1, D), ln1_b[l].reshape(1, D)
        g2, b2 = ln2_g[l].reshape(1, D), ln2_b[l].reshape(1, D)

        P = matmul(xp_b, w_in_r[l], tm=tm_in, tn=1024)
        kmat, vt = kv_prep(P, n_seq=B, tm=tm)
        ckv = compress_prompt(P, pe_t[l], w1_b[l], w2_b[l], n_seq=B)
        ckv = ckv.reshape(B, 2, NSA_KV_HEADS, S // NSA_BLOCK, HEAD_DIM).astype(BF16)
        o_nsa = nsa_prompt(P, kmat, vt, ckv[:, 0], jnp.swapaxes(ckv[:, 1], -1, -2), n_seq=B, tq=tq,
                           tk=256 if S % 256 == 0 else tq)
        prep = gdn_prep(P, gdn_conv_w[l], alog_row, dtb_row, n_seq=B, C=GDN_CHUNK)
        o_gdn, s_fin = gdn_scan(prep, P, ng, zero_state, n_seq=B, C=GDN_CHUNK)
        h, h_b = proj_ln(o_nsa, o_gdn, w_o_b[l], xp, g1, b1, alpha=alpha, tm=256)
        act, utail = ffn_up_prompt(h_b, w_up_b[l], ffn_conv_w[l], n_seq=B, tm=tm, tf=tk_down)
        xp, xp_b = ffn_down_ln_resident(act, w_down_b[l], h, g2, b2, alpha=alpha, tm=256)

        P3 = P.reshape(B, S, IN_COLS_PAD)
        p_out[0].append(P3[:, :, KVC0:KVC0 + KV_COLS].reshape(B, S, NSA_KV_HEADS, 2, HEAD_DIM))
        p_out[1].append(P3[:, :, KVS0:KVS0 + KV_COLS].reshape(B, S, NSA_KV_HEADS, 2, HEAD_DIM))
        wn = min(NSA_WINDOW, S)
        p_out[2].append(P3[:, S - wn:, KVW0:KVW0 + KV_COLS].reshape(B, wn, NSA_KV_HEADS, 2, HEAD_DIM))
        p_out[3].append(s_fin)
        p_out[4].append(P3[:, S - (GDN_CONV - 1):, QKV0:QKV0 + 3 * GDN_WIDTH])
        p_out[5].append(utail[:, 8 - (FFN_CONV - 1):, :])

        Ps = matmul(xs_b, w_in_r[l], tm=Bs * ZROWS, tn=1024)
        ckv_s = compress_sample(page_table, cache_cmp_rows, pe_t[l], w1_b[l], w2_b[l], layer=l)
        ng_grp = ckv_s.shape[1]
        nblk_step = ckv_s.shape[3] // NSA_KV_HEADS
        ckv_s = ckv_s.reshape(Bs, ng_grp, 2, NSA_KV_HEADS, nblk_step, HEAD_DIM)
        ckv_s = jnp.transpose(ckv_s, (0, 2, 3, 1, 4, 5)).reshape(Bs, 2, NSA_KV_HEADS, ng_grp * nblk_step, HEAD_DIM)
        ocmp, owin, sel, win_new = nsa_sample_a(Ps, ckv_s, cache_win_rows, layer=l, past=past, seq=seq)
        sel_flat = jnp.transpose(sel[:, :, ZTOK0:ZTOK0 + seq, :NSA_TOP_N], (0, 2, 1, 3)).reshape(-1)
        o_nsa_s = nsa_sample_b(page_table, sel_flat, Ps, ocmp, owin, cache_slc_rows, layer=l, past=past, seq=seq)
        gbufz = _to_zrows(state_gdn_conv[l], ZTOK0 - (GDN_CONV - 1))
        prep_s = gdn_prep(Ps, gdn_conv_w[l], alog_row, dtb_row, n_seq=Bs, C=ZROWS, bufz=gbufz)
        o_gdn_s, s_fin_s = gdn_scan(prep_s, Ps, ng, state_gdn[l], n_seq=Bs, C=ZROWS)
        hs, hs_b = proj_ln(o_nsa_s, o_gdn_s, w_o_b[l], xs, g1, b1, alpha=alpha, tm=Bs * ZROWS)
        fbufz = _to_zrows(state_ffn_conv[l], ZTOK0 - (FFN_CONV - 1))
        act_s, u_s = ffn_up_sample(hs_b, w_up_b[l], ffn_conv_w[l], fbufz, tf=tk_down)
        xs, xs_b = ffn_down_ln(act_s, w_down_b[l], hs, g2, b2, alpha=alpha, tm=Bs * ZROWS, tk=tk_down)

        Ps3 = Ps.reshape(Bs, ZROWS, IN_COLS_PAD)[:, ZTOK0:ZTOK0 + seq]
        s_out[0].append(Ps3[:, :, KVC0:KVC0 + KV_COLS].reshape(Bs, seq, NSA_KV_HEADS, 2, HEAD_DIM))
        s_out[1].append(Ps3[:, :, KVS0:KVS0 + KV_COLS].reshape(Bs, seq, NSA_KV_HEADS, 2, HEAD_DIM))
        s_out[2].append(win_new.reshape(Bs, wb, NSA_KV_HEADS, 2, HEAD_DIM))
        s_out[3].append(s_fin_s)
        ext_g = jnp.concatenate([state_gdn_conv[l], Ps3[:, :, QKV0:QKV0 + 3 * GDN_WIDTH]], axis=1)
        s_out[4].append(ext_g[:, -(GDN_CONV - 1):])
        u3 = u_s.reshape(Bs, ZROWS, d_ff)[:, ZTOK0:ZTOK0 + seq]
        ext_f = jnp.concatenate([state_ffn_conv[l], u3], axis=1)
        s_out[5].append(ext_f[:, -(FFN_CONV - 1):])

    y_p = xp.reshape(B, S, D)
    y_s = xs.reshape(Bs, ZROWS, D)[:, ZTOK0:ZTOK0 + seq]
    return (y_p, y_s, *[jnp.stack(v, 0) for v in p_out], *[jnp.stack(v, 0) for v in s_out])
```

```python
import functools

import jax
import jax.numpy as jnp
import numpy as np
from jax import lax
from jax.experimental import pallas as pl
from jax.experimental.pallas import tpu as pltpu

F32 = jnp.float32
BF16 = jnp.bfloat16

HEAD_DIM = 128
NSA_HEADS = 8
NSA_KV_HEADS = 2
NSA_GROUP = 4
NSA_WIDTH = 1024
NSA_BLOCK = 64
NSA_TOP_N = 16
NSA_WINDOW = 512
GDN_HEADS = 8
GDN_WIDTH = 1024
GDN_CONV = 4
GDN_CHUNK = 64
FFN_CONV = 3
KV_COLS = 512
LN_EPS = 1e-5
NORM_EPS = 1e-6
NEG = -1e30
FORCE = 1e4
LOG2E = 1.4426950408889634
SLOPES = [[2.0 ** -(g * NSA_GROUP + r + 1) for r in range(NSA_GROUP)] for g in range(NSA_KV_HEADS)]

QKV0, Q0, Z0, KVC0, KVS0, KVW0, MISC0, IN_COLS_PAD = 0, 3072, 4096, 5120, 5632, 6144, 6656, 7168
GATE_LANE, A_LANE, B_LANE = 0, 24, 32

ZROWS = 16
ZTOK0 = 8

VMEM_LIMIT_BYTES = 56 * 1024 * 1024


def _cparams(*sem):
    return pltpu.CompilerParams(dimension_semantics=sem, vmem_limit_bytes=VMEM_LIMIT_BYTES)


def _dot(a, b):
    return jnp.dot(a.astype(BF16), b.astype(BF16), preferred_element_type=F32)


def _dot_nt(a, b):
    return lax.dot_general(a.astype(BF16), b.astype(BF16), (((1,), (1,)), ((), ())), preferred_element_type=F32)


def _dot_tn(a, b):
    return lax.dot_general(a.astype(BF16), b.astype(BF16), (((0,), (0,)), ((), ())), preferred_element_type=F32)


def _layer_norm(t, g, b):
    mu = jnp.mean(t, -1, keepdims=True)
    d = t - mu
    var = jnp.mean(d * d, -1, keepdims=True)
    return d * lax.rsqrt(var + LN_EPS) * g + b


def _mm_kernel(x_ref, w_ref, o_ref):
    o_ref[...] = _dot(x_ref[...], w_ref[...])


def matmul(x, w, layer, *, tm, tn):
    M, K = x.shape
    N = w.shape[2]
    assert M % tm == 0 and N % tn == 0
    return pl.pallas_call(
        _mm_kernel,
        grid=(M // tm, N // tn),
        in_specs=[pl.BlockSpec((tm, K), lambda i, j: (i, 0)),
                  pl.BlockSpec((None, K, tn), lambda i, j: (layer, 0, j))],
        out_specs=pl.BlockSpec((tm, tn), lambda i, j: (i, j)),
        out_shape=jax.ShapeDtypeStruct((M, N), F32),
        compiler_params=_cparams("parallel", "arbitrary"),
        name="in_proj",
    )(x, w)


def _proj_ln_kernel(a1_ref, a2_ref, w_ref, x_ref, g_ref, b_ref, o_ref, ob_ref, *, alpha):
    k1 = a1_ref.shape[1]
    acc = _dot(a1_ref[...], w_ref[:k1, :]) + _dot(a2_ref[...], w_ref[k1:, :])
    o = _layer_norm(alpha * x_ref[...] + acc, g_ref[...], b_ref[...])
    o_ref[...] = o
    ob_ref[...] = o.astype(ob_ref.dtype)


def proj_ln(a1, a2, w, layer, x, g, b, *, alpha, tm):
    M, D = x.shape
    k1, k2 = a1.shape[1], a2.shape[1]
    assert M % tm == 0 and w.shape[1:] == (k1 + k2, D)
    return pl.pallas_call(
        functools.partial(_proj_ln_kernel, alpha=alpha),
        grid=(M // tm,),
        in_specs=[pl.BlockSpec((tm, k1), lambda i: (i, 0)),
                  pl.BlockSpec((tm, k2), lambda i: (i, 0)),
                  pl.BlockSpec((None, k1 + k2, D), lambda i: (layer, 0, 0)),
                  pl.BlockSpec((tm, D), lambda i: (i, 0)),
                  pl.BlockSpec((1, D), lambda i: (0, 0)),
                  pl.BlockSpec((1, D), lambda i: (0, 0))],
        out_specs=[pl.BlockSpec((tm, D), lambda i: (i, 0)), pl.BlockSpec((tm, D), lambda i: (i, 0))],
        out_shape=[jax.ShapeDtypeStruct((M, D), F32), jax.ShapeDtypeStruct((M, D), BF16)],
        compiler_params=_cparams("parallel"),
        name="out_proj_ln",
    )(a1, a2, w, x, g, b)


def _conv3(u, prev8, cw):
    ue = jnp.concatenate([prev8, u], axis=0)
    u1 = pltpu.roll(ue, 1, 0)[8:]
    u2 = pltpu.roll(ue, 2, 0)[8:]
    return cw[2:3, :] * u + cw[1:2, :] * u1 + cw[0:1, :] * u2


MXU_COLS = 256


def _col_chunks(width):
    return [slice(c0, min(c0 + MXU_COLS, width)) for c0 in range(0, width, MXU_COLS)]


def _ffn_up_prompt_kernel(h_ref, wu_ref, wv_ref, cw_ref, act_ref, tail_ref, carry_ref, *, tiles_per_seq):
    i = pl.program_id(1)
    h = h_ref[...]
    tm = h.shape[0]

    @pl.when(i % tiles_per_seq == 0)
    def _():
        carry_ref[...] = jnp.zeros_like(carry_ref)

    for cs in _col_chunks(act_ref.shape[1]):
        u = _dot(h, wu_ref[:, cs])
        v = _dot(h, wv_ref[:, cs])
        uc = _conv3(u, carry_ref[:, cs], cw_ref[:, cs])
        act_ref[:, cs] = (jax.nn.gelu(uc) * v).astype(act_ref.dtype)
        tail = u[tm - 8:, :]
        carry_ref[:, cs] = tail
        tail_ref[:, cs] = tail


def ffn_up_prompt(h, w_up, layer, cw, *, n_seq, tm, tf):
    M, D = h.shape
    F = w_up.shape[2] // 2
    T = M // n_seq
    assert M % tm == 0 and F % tf == 0 and T % tm == 0 and tf % HEAD_DIM == 0
    nf = F // tf
    tps = T // tm
    return pl.pallas_call(
        functools.partial(_ffn_up_prompt_kernel, tiles_per_seq=tps),
        grid=(nf, M // tm),
        in_specs=[pl.BlockSpec((tm, D), lambda f, i: (i, 0)),
                  pl.BlockSpec((None, D, tf), lambda f, i: (layer, 0, f), pipeline_mode=pl.Buffered(1)),
                  pl.BlockSpec((None, D, tf), lambda f, i: (layer, 0, nf + f), pipeline_mode=pl.Buffered(1)),
                  pl.BlockSpec((FFN_CONV, tf), lambda f, i: (0, f))],
        out_specs=[pl.BlockSpec((tm, tf), lambda f, i: (i, f)),
                   pl.BlockSpec((None, 8, tf), lambda f, i: (i // tps, 0, f))],
        out_shape=[jax.ShapeDtypeStruct((M, F), BF16),
                   jax.ShapeDtypeStruct((n_seq, 8, F), F32)],
        scratch_shapes=[pltpu.VMEM((8, tf), F32)],
        compiler_params=_cparams("arbitrary", "arbitrary"),
        name="ffn_up_prompt",
    )(h, w_up, w_up, cw)


def _ffn_up_sample_kernel(h_ref, wu_ref, wv_ref, cw_ref, buf_ref, act_ref, u_ref):
    h = h_ref[...]
    zrow = lax.broadcasted_iota(jnp.int32, (h.shape[0], 1), 0) % ZROWS
    isbuf = (zrow >= ZTOK0 - (FFN_CONV - 1)) & (zrow < ZTOK0)
    for cs in _col_chunks(act_ref.shape[1]):
        u = _dot(h, wu_ref[:, cs])
        v = _dot(h, wv_ref[:, cs])
        uz = jnp.where(isbuf, buf_ref[:, cs], u)
        cw = cw_ref[:, cs]
        uc = cw[2:3, :] * uz + cw[1:2, :] * pltpu.roll(uz, 1, 0) + cw[0:1, :] * pltpu.roll(uz, 2, 0)
        act_ref[:, cs] = (jax.nn.gelu(uc) * v).astype(act_ref.dtype)
        u_ref[:, cs] = u


def ffn_up_sample(h, w_up, layer, cw, bufz, *, tf):
    M, D = h.shape
    F = w_up.shape[2] // 2
    nf = F // tf
    return pl.pallas_call(
        _ffn_up_sample_kernel,
        grid=(nf,),
        in_specs=[pl.BlockSpec((M, D), lambda f: (0, 0)),
                  pl.BlockSpec((None, D, tf), lambda f: (layer, 0, f)),
                  pl.BlockSpec((None, D, tf), lambda f: (layer, 0, nf + f)),
                  pl.BlockSpec((FFN_CONV, tf), lambda f: (0, f)),
                  pl.BlockSpec((M, tf), lambda f: (0, f))],
        out_specs=[pl.BlockSpec((M, tf), lambda f: (0, f)),
                   pl.BlockSpec((M, tf), lambda f: (0, f))],
        out_shape=[jax.ShapeDtypeStruct((M, F), BF16),
                   jax.ShapeDtypeStruct((M, F), F32)],
        compiler_params=_cparams("parallel"),
        name="ffn_up_sample",
    )(h, w_up, w_up, cw, bufz)


def _ffn_down_ln_kernel(a_ref, w_ref, h_ref, g_ref, b_ref, o_ref, ob_ref, acc_ref, *, alpha):
    k = pl.program_id(1)

    @pl.when(k == 0)
    def _():
        acc_ref[...] = jnp.zeros_like(acc_ref)

    acc_ref[...] += _dot(a_ref[...], w_ref[...])

    @pl.when(k == pl.num_programs(1) - 1)
    def _():
        o = _layer_norm(alpha * h_ref[...] + acc_ref[...], g_ref[...], b_ref[...])
        o_ref[...] = o
        ob_ref[...] = o.astype(ob_ref.dtype)


def _ffn_down_ln_resident_kernel(a_ref, w_ref, h_ref, g_ref, b_ref, o_ref, ob_ref, *, alpha):
    acc = jnp.dot(a_ref[...], w_ref[...], preferred_element_type=F32)
    o = _layer_norm(alpha * h_ref[...] + acc, g_ref[...], b_ref[...])
    o_ref[...] = o
    ob_ref[...] = o.astype(ob_ref.dtype)


def ffn_down_ln_resident(a, w, layer, h, g, b, *, alpha, tm):
    M, Fd = a.shape
    D = w.shape[2]
    assert M % tm == 0
    return pl.pallas_call(
        functools.partial(_ffn_down_ln_resident_kernel, alpha=alpha),
        grid=(M // tm,),
        in_specs=[pl.BlockSpec((tm, Fd), lambda i: (i, 0)),
                  pl.BlockSpec((None, Fd, D), lambda i: (layer, 0, 0), pipeline_mode=pl.Buffered(1)),
                  pl.BlockSpec((tm, D), lambda i: (i, 0)),
                  pl.BlockSpec((1, D), lambda i: (0, 0)),
                  pl.BlockSpec((1, D), lambda i: (0, 0))],
        out_specs=[pl.BlockSpec((tm, D), lambda i: (i, 0)), pl.BlockSpec((tm, D), lambda i: (i, 0))],
        out_shape=[jax.ShapeDtypeStruct((M, D), F32), jax.ShapeDtypeStruct((M, D), BF16)],
        compiler_params=_cparams("parallel"),
        name="ffn_down_ln",
    )(a, w, h, g, b)


def ffn_down_ln(a, w, layer, h, g, b, *, alpha, tm, tk):
    M, Fd = a.shape
    D = w.shape[2]
    assert M % tm == 0 and Fd % tk == 0
    return pl.pallas_call(
        functools.partial(_ffn_down_ln_kernel, alpha=alpha),
        grid=(M // tm, Fd // tk),
        in_specs=[pl.BlockSpec((tm, tk), lambda i, k: (i, k)),
                  pl.BlockSpec((None, tk, D), lambda i, k: (layer, k, 0)),
                  pl.BlockSpec((tm, D), lambda i, k: (i, 0)),
                  pl.BlockSpec((1, D), lambda i, k: (0, 0)),
                  pl.BlockSpec((1, D), lambda i, k: (0, 0))],
        out_specs=[pl.BlockSpec((tm, D), lambda i, k: (i, 0)), pl.BlockSpec((tm, D), lambda i, k: (i, 0))],
        out_shape=[jax.ShapeDtypeStruct((M, D), F32), jax.ShapeDtypeStruct((M, D), BF16)],
        scratch_shapes=[pltpu.VMEM((tm, D), F32)],
        compiler_params=_cparams("parallel", "arbitrary"),
        name="ffn_down_ln",
    )(a, w, h, g, b)


def _compress_rows(read_rows, pe_ref, w1_ref, w2_ref, out_ref, nblk):
    del nblk
    for c in range(2):
        parts = []
        for p in range(NSA_BLOCK):
            pe_row = pe_ref[c, p:p + 1, :]
            rows = [read_rows(g * 2 + c, p) + pe_row for g in range(NSA_KV_HEADS)]
            parts.append(jnp.concatenate(rows, axis=0).astype(BF16))
        hid = jnp.dot(jnp.concatenate(parts, axis=1), w1_ref[c], preferred_element_type=F32)
        out_ref[c] = _dot(jax.nn.silu(hid), w2_ref[c])


def _compress_prompt_kernel(x0, x1, x2, x3, pe_ref, w1_ref, w2_ref, out_ref, *, nblk):
    xs = [x0, x1, x2, x3]
    _compress_rows(lambda gc, p: xs[gc][:, p, :], pe_ref, w1_ref, w2_ref, out_ref, nblk)


def compress_prompt(P, pe_t, w1, w2, *, n_seq):
    M = P.shape[0]
    T = M // n_seq
    nblk = T // NSA_BLOCK
    P3 = P.reshape(n_seq * nblk, NSA_BLOCK, IN_COLS_PAD)
    cb0 = KVC0 // HEAD_DIM
    in_specs = [pl.BlockSpec((nblk, NSA_BLOCK, HEAD_DIM), (lambda b, gc=gc: (b, 0, cb0 + gc))) for gc in range(4)]
    in_specs += [pl.BlockSpec(pe_t.shape, lambda b: (0, 0, 0)),
                 pl.BlockSpec(w1.shape, lambda b: (0, 0, 0)),
                 pl.BlockSpec(w2.shape, lambda b: (0, 0, 0))]
    return pl.pallas_call(
        functools.partial(_compress_prompt_kernel, nblk=nblk),
        grid=(n_seq,),
        in_specs=in_specs,
        out_specs=pl.BlockSpec((None, 2, NSA_KV_HEADS * nblk, HEAD_DIM), lambda b: (b, 0, 0, 0)),
        out_shape=jax.ShapeDtypeStruct((n_seq, 2, NSA_KV_HEADS * nblk, HEAD_DIM), F32),
        compiler_params=_cparams("parallel"),
        name="compress_prompt",
    )(P3, P3, P3, P3, pe_t, w1, w2)


def _rank_desc(score, n_real):
    lane = lax.broadcasted_iota(jnp.int32, score.shape, 1)
    cnt = jnp.zeros(score.shape, F32)
    for i in range(n_real):
        col = score[:, i:i + 1]
        ge = jnp.where(col >= score, 1.0, 0.0)
        gt = jnp.where(col > score, 1.0, 0.0)
        cnt = cnt + jnp.where(lane > i, ge, gt)
    return cnt


def _block_scores(imp, qpos, n_blocks):
    bidx = lax.broadcasted_iota(jnp.int32, imp.shape, 1)
    cur = qpos // NSA_BLOCK
    sc = jnp.where(bidx == 0, FORCE, jnp.where(bidx == cur, FORCE, jnp.where(bidx == cur - 1, FORCE, imp)))
    sc = jnp.where(bidx * NSA_BLOCK <= qpos, sc, NEG)
    return jnp.where(bidx < n_blocks, sc, -jnp.inf)


def _rank_desc_t(score, n_real):
    L = score.shape[0]
    assert L % 8 == 0
    groups = [score[8 * a:8 * a + 8, :] for a in range(L // 8)]
    sub = lax.broadcasted_iota(jnp.int32, groups[0].shape, 0)
    cnts = [jnp.zeros(groups[0].shape, F32) for _ in groups]
    for i in range(n_real):
        row = score[i:i + 1, :]
        for a, x in enumerate(groups):
            if 8 * a > i:
                c = jnp.where(row >= x, 1.0, 0.0)
            elif 8 * a + 7 < i:
                c = jnp.where(row > x, 1.0, 0.0)
            else:
                c = jnp.where(sub + 8 * a > i, jnp.where(row >= x, 1.0, 0.0), jnp.where(row > x, 1.0, 0.0))
            cnts[a] = cnts[a] + c
    return jnp.concatenate(cnts, axis=0)


def _block_scores_t(imp, qpos, n_blocks):
    bidx = lax.broadcasted_iota(jnp.int32, imp.shape, 0)
    cur = qpos // NSA_BLOCK
    sc = jnp.where(bidx == 0, FORCE, jnp.where(bidx == cur, FORCE, jnp.where(bidx == cur - 1, FORCE, imp)))
    sc = jnp.where(bidx * NSA_BLOCK <= qpos, sc, NEG)
    return jnp.where(bidx < n_blocks, sc, -jnp.inf)


def _nsa_prompt_kernel(q_ref, misc_ref, k_ref, vts_ref, vtw_ref, ck_ref, cvt_ref, o_ref, m_ref, l_ref, acc_ref, flag_ref, *, tq, tk, nblk):
    i = pl.program_id(1)
    t0 = i * tq
    bpt = tk // NSA_BLOCK
    wspan = min(NSA_WINDOW + tq + HEAD_DIM, k_ref.shape[0])
    scale = HEAD_DIM ** -0.5 * LOG2E
    top_n = min(NSA_TOP_N, nblk)
    qpos_row = t0 + lax.broadcasted_iota(jnp.int32, (1, tq), 1)
    sub_k = lax.broadcasted_iota(jnp.int32, (tk, tq), 0)
    d_kq = sub_k - lax.broadcasted_iota(jnp.int32, (tk, tq), 1)
    sub_kf = sub_k.astype(F32)
    wsub_k = lax.broadcasted_iota(jnp.int32, (wspan, tq), 0)
    wsub_kf = wsub_k.astype(F32)
    wwin_code = -2 * (wsub_k - lax.broadcasted_iota(jnp.int32, (wspan, tq), 1)) - (NSA_WINDOW - 1)
    e_rel = (lax.broadcasted_iota(jnp.int32, (tk, nblk), 1)
             - lax.broadcasted_iota(jnp.int32, (tk, nblk), 0) // NSA_BLOCK)
    gates_t = jax.nn.sigmoid(misc_ref[...]).T

    def reset():
        m_ref[...] = jnp.full(m_ref.shape, NEG, F32)
        l_ref[...] = jnp.zeros(l_ref.shape, F32)
        acc_ref[...] = jnp.zeros(acc_ref.shape, F32)

    def online_update(s4, vt):
        m_old = m_ref[...]
        m_new = jnp.maximum(m_old, jnp.max(s4, axis=0, keepdims=True))
        p = jnp.exp2(s4 - m_new)
        alpha = jnp.exp2(m_old - m_new)
        l_ref[...] = alpha * l_ref[...] + jnp.sum(p, axis=0, keepdims=True)
        acc_ref[...] = alpha * acc_ref[...] + jnp.dot(vt, p.astype(BF16), preferred_element_type=F32)
        m_ref[...] = m_new

    def masked_scores(s, valid, base, g):
        return jnp.concatenate(
            [jnp.where(valid, s[:, r * tq:(r + 1) * tq] + (SLOPES[g][r] * LOG2E) * base, NEG)
             for r in range(NSA_GROUP)], axis=1)

    for g in range(NSA_KV_HEADS):
        q4t = jnp.concatenate(
            [(q_ref[:, (g * NSA_GROUP + r) * HEAD_DIM:(g * NSA_GROUP + r + 1) * HEAD_DIM] * scale).T
             for r in range(NSA_GROUP)], axis=1).astype(BF16)
        sc = jnp.dot(ck_ref[g], q4t, preferred_element_type=F32)
        cpos = lax.broadcasted_iota(jnp.int32, (nblk, tq), 0) * NSA_BLOCK + (NSA_BLOCK - 1)
        vm = cpos <= qpos_row
        cposrel = (cpos - t0).astype(F32)
        imp = jnp.zeros((nblk, tq), F32)
        parts = []
        for r in range(NSA_GROUP):
            s = jnp.where(vm, sc[:, r * tq:(r + 1) * tq] + (SLOPES[g][r] * LOG2E) * cposrel, NEG)
            e = jnp.exp2(s - jnp.max(s, axis=0, keepdims=True))
            p = jnp.where(vm, e / jnp.sum(e, axis=0, keepdims=True), 0.0)
            imp = imp + p
            parts.append(p)
        ocmp_t = jnp.dot(cvt_ref[g], jnp.concatenate(parts, axis=1).astype(BF16), preferred_element_type=F32)
        rank = _rank_desc_t(_block_scores_t(imp, qpos_row, nblk), nblk)
        sel_f = jnp.where(rank < top_n, 1.0, 0.0)
        sel_t = sel_f.astype(BF16)
        blk_any = jnp.max(sel_f, axis=1, keepdims=True)
        for jt in range(nblk // bpt):
            flag_ref[jt] = (jnp.max(blk_any[jt * bpt:(jt + 1) * bpt, :]) > 0.5).astype(jnp.int32)

        reset()

        def slc_body(j, carry):
            k0 = pl.multiple_of(j * tk, tk)

            @pl.when(flag_ref[j] > 0)
            def _():
                k = k_ref[pl.ds(k0, tk), g * HEAD_DIM:(g + 1) * HEAD_DIM]
                vt = vts_ref[g, :, pl.ds(k0, tk)]
                off = k0 - t0
                expand = jnp.where(e_rel == k0 // NSA_BLOCK, 1.0, 0.0).astype(BF16)
                chosen = jnp.dot(expand, sel_t, preferred_element_type=F32)
                valid = jnp.where(d_kq <= -off, chosen, 0.0) > 0.5
                s = jnp.dot(k, q4t, preferred_element_type=F32)
                online_update(masked_scores(s, valid, sub_kf + off.astype(F32), g), vt)

            return carry

        lax.fori_loop(0, (t0 + tq + tk - 1) // tk, slc_body, 0)
        oslc_t = acc_ref[...] / l_ref[...]

        w0 = pl.multiple_of(jnp.maximum(t0 + tq - wspan, 0), HEAD_DIM)
        kw = k_ref[pl.ds(w0, wspan), (NSA_KV_HEADS + g) * HEAD_DIM:(NSA_KV_HEADS + g + 1) * HEAD_DIM]
        woff = w0 - t0
        wvalid = jnp.abs(wwin_code - 2 * woff) <= (NSA_WINDOW - 1)
        wbase = wsub_kf + woff.astype(F32)
        sw = jnp.dot(kw, q4t, preferred_element_type=F32)
        sw = jnp.concatenate(
            [jnp.where(wvalid, sw[:, r * tq:(r + 1) * tq] + (SLOPES[g][r] * LOG2E) * wbase, NEG)
             for r in range(NSA_GROUP)], axis=1)
        pw = jnp.exp2(sw - jnp.max(sw, axis=0, keepdims=True))
        owin_t = (jnp.dot(vtw_ref[g, :, pl.ds(w0, wspan)], pw.astype(BF16), preferred_element_type=F32)
                  / jnp.sum(pw, axis=0, keepdims=True))

        for r in range(NSA_GROUP):
            c = g * NSA_GROUP * 3 + r * 3 + GATE_LANE
            rs = slice(r * tq, (r + 1) * tq)
            o_t = (gates_t[c:c + 1, :] * ocmp_t[:, rs] + gates_t[c + 1:c + 2, :] * oslc_t[:, rs]
                   + gates_t[c + 2:c + 3, :] * owin_t[:, rs])
            h = g * NSA_GROUP + r
            o_ref[:, h * HEAD_DIM:(h + 1) * HEAD_DIM] = o_t.T.astype(o_ref.dtype)


def _kv_prep_kernel(kvs_ref, kvw_ref, kmat_ref, vt_ref):
    for idx, (src, g) in enumerate([(kvs_ref, 0), (kvs_ref, 1), (kvw_ref, 0), (kvw_ref, 1)]):
        kmat_ref[:, idx * HEAD_DIM:(idx + 1) * HEAD_DIM] = src[:, 2 * g * HEAD_DIM:(2 * g + 1) * HEAD_DIM].astype(BF16)
        vt_ref[idx] = src[:, (2 * g + 1) * HEAD_DIM:(2 * g + 2) * HEAD_DIM].T.astype(BF16)


def kv_prep(P, *, n_seq, tm):
    M = P.shape[0]
    T = M // n_seq
    assert T % tm == 0
    tps = T // tm
    return pl.pallas_call(
        _kv_prep_kernel,
        grid=(M // tm,),
        in_specs=[pl.BlockSpec((tm, KV_COLS), lambda i: (i, KVS0 // KV_COLS)),
                  pl.BlockSpec((tm, KV_COLS), lambda i: (i, KVW0 // KV_COLS))],
        out_specs=[pl.BlockSpec((tm, 4 * HEAD_DIM), lambda i: (i, 0)),
                   pl.BlockSpec((None, 4, HEAD_DIM, tm), lambda i: (i // tps, 0, 0, i % tps))],
        out_shape=[jax.ShapeDtypeStruct((M, 4 * HEAD_DIM), BF16),
                   jax.ShapeDtypeStruct((n_seq, 4, HEAD_DIM, T), BF16)],
        compiler_params=_cparams("parallel"),
        name="kv_prep",
    )(P, P)


def nsa_prompt(P, kmat, vt, ck, cvt, *, n_seq, tq, tk):
    M = P.shape[0]
    T = M // n_seq
    nblk = T // NSA_BLOCK
    nq = T // tq
    W = NSA_GROUP * tq
    assert T % tq == 0 and T % tk == 0 and tk % NSA_BLOCK == 0 and tk % HEAD_DIM == 0
    return pl.pallas_call(
        functools.partial(_nsa_prompt_kernel, tq=tq, tk=tk, nblk=nblk),
        grid=(n_seq, nq),
        in_specs=[pl.BlockSpec((tq, NSA_WIDTH), lambda b, i: (b * nq + i, Q0 // NSA_WIDTH)),
                  pl.BlockSpec((tq, HEAD_DIM), lambda b, i: (b * nq + i, MISC0 // HEAD_DIM)),
                  pl.BlockSpec((T, 2 * NSA_KV_HEADS * HEAD_DIM), lambda b, i: (b, 0)),
                  pl.BlockSpec((None, NSA_KV_HEADS, HEAD_DIM, T), lambda b, i: (b, 0, 0, 0)),
                  pl.BlockSpec((None, NSA_KV_HEADS, HEAD_DIM, T), lambda b, i: (b, 1, 0, 0)),
                  pl.BlockSpec((None, NSA_KV_HEADS, nblk, HEAD_DIM), lambda b, i: (b, 0, 0, 0)),
                  pl.BlockSpec((None, NSA_KV_HEADS, HEAD_DIM, nblk), lambda b, i: (b, 0, 0, 0))],
        out_specs=pl.BlockSpec((tq, NSA_WIDTH), lambda b, i: (b * nq + i, 0)),
        out_shape=jax.ShapeDtypeStruct((M, NSA_WIDTH), BF16),
        scratch_shapes=[pltpu.VMEM((1, W), F32),
                        pltpu.VMEM((1, W), F32),
                        pltpu.VMEM((HEAD_DIM, W), F32),
                        pltpu.SMEM((T // tk,), jnp.int32)],
        compiler_params=_cparams("parallel", "arbitrary"),
        name="nsa_prompt",
    )(P, P, kmat, vt, vt, ck, cvt)


def _bdot(a, b):
    return lax.dot_general(a.astype(BF16), b.astype(BF16), (((2,), (1,)), ((0,), (0,))), preferred_element_type=F32)


def _bdot_nt(a, b):
    return lax.dot_general(a.astype(BF16), b.astype(BF16), (((2,), (2,)), ((0,), (0,))), preferred_element_type=F32)


def _bdot_tn(a, b):
    return lax.dot_general(a.astype(BF16), b.astype(BF16), (((1,), (1,)), ((0,), (0,))), preferred_element_type=F32)


def _head_stack(a, off, width=HEAD_DIM):
    return jnp.stack([a[:, off + h * width:off + (h + 1) * width] for h in range(GDN_HEADS)], axis=0)


def _unit_lower_inverse_minus_eye(A, C):
    row = lax.broadcasted_iota(jnp.int32, (C, C), 0)
    col = lax.broadcasted_iota(jnp.int32, (C, C), 1)
    A8 = jnp.where((row // 8) == (col // 8), A, 0.0)
    B2 = _bdot(A8, A8)
    B4 = _bdot(B2, B2)
    P1 = B2 - A8 - _bdot(A8, B2)
    N = P1 + B4 + _bdot(P1, B4)
    size = 16
    while size <= C:
        AL = jnp.where((row // size) == (col // size), jnp.where((row // (size // 2)) == (col // (size // 2)), 0.0, A), 0.0)
        X = AL + _bdot(N, AL)
        N = N - X - _bdot(X, N)
        size *= 2
    return N


def _gdn_prep_kernel(*refs, C, sample):
    if sample:
        x_ref, buf_ref, misc_ref, cw_ref, alog_ref, dtb_ref = refs[:6]
    else:
        x_ref, prev_ref, misc_ref, cw_ref, alog_ref, dtb_ref = refs[:6]
    u_ref, w_ref, qg_ref, kd_ref, qk_ref, gl_ref = refs[6:12]
    cw = cw_ref[...]
    x = x_ref[...]
    rowv = lax.broadcasted_iota(jnp.int32, (C, 1), 0)
    if sample:
        isbuf = (rowv >= ZTOK0 - (GDN_CONV - 1)) & (rowv < ZTOK0)
        xz = jnp.where(isbuf, buf_ref[...], x)
        y = (cw[3:4, :] * xz + cw[2:3, :] * pltpu.roll(xz, 1, 0) + cw[1:2, :] * pltpu.roll(xz, 2, 0)
             + cw[0:1, :] * pltpu.roll(xz, 3, 0))
        valid = jnp.where((rowv >= ZTOK0) & (rowv < ZTOK0 + 4), 1.0, 0.0)
    else:
        prev = jnp.where(pl.program_id(1) == 0, 0.0, prev_ref[...])
        xe = jnp.concatenate([prev, x], axis=0)
        y = (cw[3:4, :] * x + cw[2:3, :] * pltpu.roll(xe, 1, 0)[8:] + cw[1:2, :] * pltpu.roll(xe, 2, 0)[8:]
             + cw[0:1, :] * pltpu.roll(xe, 3, 0)[8:])
        valid = None
    y = jax.nn.silu(y)

    misc = misc_ref[...]
    gfull = -jnp.exp(alog_ref[...]) * jax.nn.softplus(misc + dtb_ref[...])
    bfull = jax.nn.sigmoid(misc)
    if valid is not None:
        gfull = gfull * valid
        bfull = bfull * valid
    row128 = lax.broadcasted_iota(jnp.int32, (C, HEAD_DIM), 0)
    G = gfull
    s = 1
    while s < C:
        G = G + jnp.where(row128 >= s, pltpu.roll(G, s, 0), 0.0)
        s *= 2
    expG = jnp.exp(G)
    glast = G[C - 1:C, :]
    kdfac = jnp.exp(glast - G)
    if C < HEAD_DIM:
        Gpad = jnp.concatenate([G, jnp.zeros((HEAD_DIM - C, HEAD_DIM), F32)], axis=0)
    else:
        Gpad = G
    GT = Gpad.T
    gl_ref[...] = jnp.exp(jnp.broadcast_to(GT[A_LANE:A_LANE + GDN_HEADS, C - 1:C], (GDN_HEADS, HEAD_DIM)))

    row = lax.broadcasted_iota(jnp.int32, (C, C), 0)
    col = lax.broadcasted_iota(jnp.int32, (C, C), 1)
    q = _head_stack(y, 0)
    k = _head_stack(y, GDN_WIDTH)
    v = _head_stack(y, 2 * GDN_WIDTH)
    q = q * lax.rsqrt(jnp.sum(q * q, -1, keepdims=True) + NORM_EPS) * (HEAD_DIM ** -0.5)
    k = k * lax.rsqrt(jnp.sum(k * k, -1, keepdims=True) + NORM_EPS)
    if valid is not None:
        q, k, v = q * valid, k * valid, v * valid
    beta = _head_stack(bfull, B_LANE, 1)
    gcol = _head_stack(G, A_LANE, 1)
    egc = _head_stack(expG, A_LANE, 1)
    kdf = _head_stack(kdfac, A_LANE, 1)
    grow = jnp.stack([GT[A_LANE + h:A_LANE + h + 1, :C] for h in range(GDN_HEADS)], axis=0)
    decay = jnp.exp(jnp.where(row >= col, gcol - grow, NEG))
    kb = k * beta
    A = jnp.where(row > col, _bdot_nt(kb, k) * decay, 0.0)
    N = _unit_lower_inverse_minus_eye(A, C)
    vb = v * beta
    kbg = kb * egc
    U = vb + _bdot(N, vb)
    Wm = kbg + _bdot(N, kbg)
    QK = _bdot_nt(q, k) * decay
    QG = q * egc
    KD = k * kdf
    for h in range(GDN_HEADS):
        sl = slice(h * HEAD_DIM, (h + 1) * HEAD_DIM)
        u_ref[:, sl] = U[h]
        w_ref[:, sl] = Wm[h].astype(w_ref.dtype)
        qk_ref[:, h * C:(h + 1) * C] = QK[h].astype(qk_ref.dtype)
        qg_ref[:, sl] = QG[h].astype(qg_ref.dtype)
        kd_ref[:, sl] = KD[h].astype(kd_ref.dtype)


def gdn_prep(P, cw, alog_row, dtb_row, *, n_seq, C, bufz=None):
    M = P.shape[0]
    T = M // n_seq
    n = T // C
    sample = bufz is not None
    assert T % C == 0 and C % 8 == 0
    qkv_spec = pl.BlockSpec((C, 3 * GDN_WIDTH), lambda b, c: (b * n + c, 0))
    if sample:
        second = pl.BlockSpec((C, 3 * GDN_WIDTH), lambda b, c: (b * n + c, 0))
        second_arr = bufz
    else:
        second = pl.BlockSpec((8, 3 * GDN_WIDTH), lambda b, c: (jnp.maximum((b * n + c) * (C // 8) - 1, 0), 0))
        second_arr = P
    in_specs = [qkv_spec, second,
                pl.BlockSpec((C, HEAD_DIM), lambda b, c: (b * n + c, MISC0 // HEAD_DIM)),
                pl.BlockSpec((GDN_CONV, 3 * GDN_WIDTH), lambda b, c: (0, 0)),
                pl.BlockSpec((1, HEAD_DIM), lambda b, c: (0, 0)),
                pl.BlockSpec((1, HEAD_DIM), lambda b, c: (0, 0))]
    row_spec = pl.BlockSpec((C, GDN_WIDTH), lambda b, c: (b * n + c, 0))
    out_specs = [row_spec, row_spec, row_spec, row_spec,
                 pl.BlockSpec((C, GDN_HEADS * C), lambda b, c: (b * n + c, 0)),
                 pl.BlockSpec((None, GDN_HEADS, HEAD_DIM), lambda b, c: (b * n + c, 0, 0))]
    out_shape = [jax.ShapeDtypeStruct((M, GDN_WIDTH), F32),
                 jax.ShapeDtypeStruct((M, GDN_WIDTH), BF16),
                 jax.ShapeDtypeStruct((M, GDN_WIDTH), BF16),
                 jax.ShapeDtypeStruct((M, GDN_WIDTH), BF16),
                 jax.ShapeDtypeStruct((M, GDN_HEADS * C), BF16),
                 jax.ShapeDtypeStruct((n_seq * n, GDN_HEADS, HEAD_DIM), F32)]
    return pl.pallas_call(
        functools.partial(_gdn_prep_kernel, C=C, sample=sample),
        grid=(n_seq, n),
        in_specs=in_specs,
        out_specs=out_specs,
        out_shape=out_shape,
        compiler_params=_cparams("parallel", "arbitrary"),
        name="gdn_prep_sample" if sample else "gdn_prep_prompt",
    )(P, second_arr, P, cw, alog_row, dtb_row)


def _gdn_scan_kernel(u_ref, w_ref, qg_ref, kd_ref, qk_ref, gl_ref, z_ref, ng_ref, s0_ref, o_ref, sfin_ref, S_ref, *, C):
    c = pl.program_id(1)

    @pl.when(c == 0)
    def _():
        S_ref[...] = s0_ref[...]

    S = S_ref[...]
    Sb = S.astype(BF16)
    v_new = _head_stack(u_ref[...], 0) - _bdot(_head_stack(w_ref[...], 0), Sb)
    vb = v_new.astype(BF16)
    o = _bdot(_head_stack(qg_ref[...], 0), Sb) + _bdot(_head_stack(qk_ref[...], 0, C), vb)
    gl = jnp.stack([gl_ref[h:h + 1, :] for h in range(GDN_HEADS)], axis=0)
    S_ref[...] = S * gl + _bdot_tn(_head_stack(kd_ref[...], 0), vb)
    o = o * lax.rsqrt(jnp.mean(o * o, -1, keepdims=True) + NORM_EPS) * ng_ref[...]
    o = o * jax.nn.silu(_head_stack(z_ref[...], 0))
    for h in range(GDN_HEADS):
        o_ref[:, h * HEAD_DIM:(h + 1) * HEAD_DIM] = o[h].astype(o_ref.dtype)

    @pl.when(c == pl.num_programs(1) - 1)
    def _():
        sfin_ref[...] = S_ref[...]


def gdn_scan(prep, P, norm_g, s0, *, n_seq, C):
    u, w, qg, kd, qk, gl = prep
    M = u.shape[0]
    T = M // n_seq
    n = T // C
    row_spec = pl.BlockSpec((C, GDN_WIDTH), lambda b, c: (b * n + c, 0))
    st_spec = pl.BlockSpec((None, GDN_HEADS, HEAD_DIM, HEAD_DIM), lambda b, c: (b, 0, 0, 0))
    return pl.pallas_call(
        functools.partial(_gdn_scan_kernel, C=C),
        grid=(n_seq, n),
        in_specs=[row_spec, row_spec, row_spec, row_spec,
                  pl.BlockSpec((C, GDN_HEADS * C), lambda b, c: (b * n + c, 0)),
                  pl.BlockSpec((None, GDN_HEADS, HEAD_DIM), lambda b, c: (b * n + c, 0, 0)),
                  pl.BlockSpec((C, GDN_WIDTH), lambda b, c: (b * n + c, Z0 // GDN_WIDTH)),
                  pl.BlockSpec((1, HEAD_DIM), lambda b, c: (0, 0)),
                  st_spec],
        out_specs=[row_spec, st_spec],
        out_shape=[jax.ShapeDtypeStruct((M, GDN_WIDTH), BF16),
                   jax.ShapeDtypeStruct((n_seq, GDN_HEADS, HEAD_DIM, HEAD_DIM), F32)],
        scratch_shapes=[pltpu.VMEM((GDN_HEADS, HEAD_DIM, HEAD_DIM), F32)],
        compiler_params=_cparams("parallel", "arbitrary"),
        name="gdn_scan",
    )(u, w, qg, kd, qk, gl, P, norm_g, s0)


PAGES_PER_STEP = 32
BLOCK_ROWS = NSA_BLOCK * 4
PAGE_ROWS = 2 * BLOCK_ROWS


def _compress_sample_kernel(pt_ref, cache_ref, pe_ref, w1_ref, w2_ref, out_ref, xbuf, sem, *, layer, n_pages, pps):
    b = pl.program_id(0)
    gi = pl.program_id(1)
    ng = pl.num_programs(1)
    step = b * ng + gi
    nsteps = pl.num_programs(0) * ng
    slot = step % 2

    def copies(st, sl):
        bb = st // ng
        g0 = (st % ng) * pps
        out = []
        for jl in range(pps):
            page = pt_ref[bb * n_pages + g0 + jl]
            for half in range(2):
                out.append(pltpu.make_async_copy(
                    cache_ref.at[layer, page, pl.ds(half * BLOCK_ROWS, BLOCK_ROWS), :],
                    xbuf.at[sl, :, 2 * jl + half, :],
                    sem.at[sl]))
        return out

    @pl.when(step == 0)
    def _():
        for cp in copies(step, slot):
            cp.start()

    @pl.when(step + 1 < nsteps)
    def _():
        for cp in copies(step + 1, 1 - slot):
            cp.start()

    for cp in copies(step, slot):
        cp.wait()

    nblk = 2 * pps

    _compress_rows(lambda gc, p: xbuf[slot, p * 4 + gc], pe_ref, w1_ref, w2_ref, out_ref, nblk)


def compress_sample(page_table, cache_rows, pe_t, w1, w2, *, layer):
    Bs, n_pages = page_table.shape
    pps = min(PAGES_PER_STEP, n_pages)
    assert n_pages % pps == 0
    ng = n_pages // pps
    nblk = 2 * pps
    grid_spec = pltpu.PrefetchScalarGridSpec(
        num_scalar_prefetch=1,
        grid=(Bs, ng),
        in_specs=[pl.BlockSpec(memory_space=pl.ANY),
                  pl.BlockSpec(pe_t.shape, lambda b, g, pt: (0, 0, 0)),
                  pl.BlockSpec(w1.shape, lambda b, g, pt: (0, 0, 0)),
                  pl.BlockSpec(w2.shape, lambda b, g, pt: (0, 0, 0))],
        out_specs=pl.BlockSpec((None, None, 2, NSA_KV_HEADS * nblk, HEAD_DIM), lambda b, g, pt: (b, g, 0, 0, 0)),
        scratch_shapes=[pltpu.VMEM((2, BLOCK_ROWS, nblk, HEAD_DIM), F32),
                        pltpu.SemaphoreType.DMA((2,))],
    )
    return pl.pallas_call(
        functools.partial(_compress_sample_kernel, layer=layer, n_pages=n_pages, pps=pps),
        grid_spec=grid_spec,
        out_shape=jax.ShapeDtypeStruct((Bs, ng, 2, NSA_KV_HEADS * nblk, HEAD_DIM), F32),
        compiler_params=_cparams("arbitrary", "arbitrary"),
        name="compress_sample",
    )(page_table.reshape(-1), cache_rows, pe_t, w1, w2)


def _stack_heads(q_ref, g, scale):
    return jnp.concatenate(
        [q_ref[:, (g * NSA_GROUP + r) * HEAD_DIM:(g * NSA_GROUP + r + 1) * HEAD_DIM] * scale for r in range(NSA_GROUP)],
        axis=0).astype(BF16)


def _stacked_row_info(past):
    rows = NSA_GROUP * ZROWS
    ridx = lax.broadcasted_iota(jnp.int32, (rows, 1), 0)
    zrow = ridx % ZROWS
    qpos = past + zrow - ZTOK0
    slope = jnp.zeros((rows, 1), F32)
    return ridx, zrow, qpos, slope


def _slope_col(g):
    ridx = lax.broadcasted_iota(jnp.int32, (NSA_GROUP * ZROWS, 1), 0)
    sl = jnp.zeros((NSA_GROUP * ZROWS, 1), F32)
    for r in range(NSA_GROUP):
        sl = jnp.where(ridx // ZROWS == r, SLOPES[g][r], sl)
    return sl


def _nsa_sample_a_kernel(q_ref, kvw_ref, ck_ref, cv_ref, win_ref, ocmp_ref, owin_ref, sel_ref, wout_ref, *, past, nc, n_blocks, seq):
    scale = HEAD_DIM ** -0.5
    _, zrow, qpos, _ = _stacked_row_info(past)
    wb = win_ref.shape[0] // 4
    lanes_pad = sel_ref.shape[-1]
    sel_lanes = ((n_blocks + HEAD_DIM - 1) // HEAD_DIM) * HEAD_DIM
    top_n = min(NSA_TOP_N, n_blocks)
    for g in range(NSA_KV_HEADS):
        q4 = _stack_heads(q_ref, g, scale)
        slope = _slope_col(g)
        cpos = lax.broadcasted_iota(jnp.int32, (1, nc), 1) * NSA_BLOCK + (NSA_BLOCK - 1)
        vm = cpos <= qpos
        s = jnp.where(vm, _dot_nt(q4, ck_ref[g]) + slope * (cpos - past).astype(F32), NEG)
        e = jnp.exp(s - jnp.max(s, axis=-1, keepdims=True))
        p = jnp.where(vm, e / jnp.sum(e, axis=-1, keepdims=True), 0.0)
        ocmp_ref[g] = _dot(p, cv_ref[g])
        imp = p[0:ZROWS]
        for r in range(1, NSA_GROUP):
            imp = imp + p[r * ZROWS:(r + 1) * ZROWS]
        imp = jnp.concatenate([imp, jnp.zeros((ZROWS, sel_lanes - nc), F32)], axis=1)
        rank = _rank_desc(_block_scores(imp, qpos[0:ZROWS], n_blocks), n_blocks)
        lane = lax.broadcasted_iota(jnp.int32, rank.shape, 1)
        olane = lax.broadcasted_iota(jnp.int32, (ZROWS, lanes_pad), 1)
        out = jnp.zeros((ZROWS, lanes_pad), jnp.int32)
        for t in range(top_n):
            idx = jnp.sum(jnp.where(rank == float(t), lane.astype(F32), 0.0), axis=-1, keepdims=True)
            out = jnp.where(olane == t, idx.astype(jnp.int32), out)
        sel_ref[g] = out
        kold = win_ref[pl.ds(g * 2, wb, stride=4), :]
        vold = win_ref[pl.ds(g * 2 + 1, wb, stride=4), :]
        knew = kvw_ref[:, g * 2 * HEAD_DIM:(g * 2 + 1) * HEAD_DIM]
        vnew = kvw_ref[:, (g * 2 + 1) * HEAD_DIM:(g * 2 + 2) * HEAD_DIM]
        kpos_o = past - wb + lax.broadcasted_iota(jnp.int32, (1, wb), 1)
        kz = lax.broadcasted_iota(jnp.int32, (1, ZROWS), 1)
        kpos_n = past + kz - ZTOK0
        d_o = qpos - kpos_o
        d_n = qpos - kpos_n
        ok_o = (kpos_o >= 0) & (d_o >= 0) & (d_o < NSA_WINDOW)
        ok_n = (kz >= ZTOK0) & (kz < ZTOK0 + seq) & (d_n >= 0) & (d_n < NSA_WINDOW)
        s_o = jnp.where(ok_o, _dot_nt(q4, kold) + slope * (kpos_o - past).astype(F32), NEG)
        s_n = jnp.where(ok_n, _dot_nt(q4, knew) + slope * (kpos_n - past).astype(F32), NEG)
        m = jnp.maximum(jnp.max(s_o, axis=-1, keepdims=True), jnp.max(s_n, axis=-1, keepdims=True))
        p_o = jnp.exp(s_o - m)
        p_n = jnp.exp(s_n - m)
        den = jnp.sum(p_o, axis=-1, keepdims=True) + jnp.sum(p_n, axis=-1, keepdims=True)
        owin_ref[g] = (_dot(p_o, vold) + _dot(p_n, vnew)) / den
    keep = (wb - seq) * 4
    wout_ref[0:keep, :] = win_ref[seq * 4:wb * 4, :]
    ridx = lax.broadcasted_iota(jnp.int32, (seq * 4, 1), 0)
    new_rows = jnp.zeros((seq * 4, HEAD_DIM), F32)
    for t in range(seq):
        for gc in range(4):
            new_rows = jnp.where(ridx == t * 4 + gc,
                                 kvw_ref[ZTOK0 + t:ZTOK0 + t + 1, gc * HEAD_DIM:(gc + 1) * HEAD_DIM], new_rows)
    wout_ref[keep:wb * 4, :] = new_rows


def nsa_sample_a(Ps, ckv, cache_win_rows, *, layer, past, seq):
    Bs = Ps.shape[0] // ZROWS
    nc = ckv.shape[3]
    wrows = cache_win_rows.shape[2]
    n_blocks = -(-(past + seq) // NSA_BLOCK)
    assert 1 <= seq <= 4 and (seq * 4) % 8 == 0 and wrows % 8 == 0
    rows = NSA_GROUP * ZROWS
    big = pl.BlockSpec((None, NSA_KV_HEADS, rows, HEAD_DIM), lambda b: (b, 0, 0, 0))
    return pl.pallas_call(
        functools.partial(_nsa_sample_a_kernel, past=past, nc=nc, n_blocks=n_blocks, seq=seq),
        grid=(Bs,),
        in_specs=[pl.BlockSpec((ZROWS, NSA_WIDTH), lambda b: (b, Q0 // NSA_WIDTH)),
                  pl.BlockSpec((ZROWS, KV_COLS), lambda b: (b, KVW0 // KV_COLS)),
                  pl.BlockSpec((None, None, NSA_KV_HEADS, nc, HEAD_DIM), lambda b: (b, 0, 0, 0, 0)),
                  pl.BlockSpec((None, None, NSA_KV_HEADS, nc, HEAD_DIM), lambda b: (b, 1, 0, 0, 0)),
                  pl.BlockSpec((None, None, wrows, HEAD_DIM), lambda b: (layer, b, 0, 0))],
        out_specs=[big, big,
                   pl.BlockSpec((None, NSA_KV_HEADS, ZROWS, HEAD_DIM), lambda b: (b, 0, 0, 0)),
                   pl.BlockSpec((None, wrows, HEAD_DIM), lambda b: (b, 0, 0))],
        out_shape=[jax.ShapeDtypeStruct((Bs, NSA_KV_HEADS, rows, HEAD_DIM), F32),
                   jax.ShapeDtypeStruct((Bs, NSA_KV_HEADS, rows, HEAD_DIM), F32),
                   jax.ShapeDtypeStruct((Bs, NSA_KV_HEADS, ZROWS, HEAD_DIM), jnp.int32),
                   jax.ShapeDtypeStruct((Bs, wrows, HEAD_DIM), F32)],
        compiler_params=_cparams("parallel"),
        name="nsa_sample_a",
    )(Ps, Ps, ckv, ckv, cache_win_rows)


def _nsa_sample_b_kernel(pt_ref, sel_ref, q_ref, kvs_ref, misc_ref, ocmp_ref, owin_ref, cache_ref, o_ref, kvbuf, sem,
                         *, layer, past, seq, n_past_blocks, n_pages):
    b = pl.program_id(0)
    g = pl.program_id(1)
    ngrp = pl.num_programs(1)
    step = b * ngrp + g
    nsteps = pl.num_programs(0) * ngrp
    slot = step % 2

    def copies(st, sl):
        bb = st // ngrp
        gg = st % ngrp
        out = []
        for t in range(seq):
            for j in range(NSA_TOP_N):
                blk = jnp.minimum(sel_ref[((bb * seq + t) * NSA_KV_HEADS + gg) * NSA_TOP_N + j], n_past_blocks - 1)
                page = pt_ref[bb * n_pages + blk // 2]
                row0 = pl.multiple_of((blk % 2) * BLOCK_ROWS, BLOCK_ROWS)
                out.append(pltpu.make_async_copy(
                    cache_ref.at[layer, page, pl.ds(row0, BLOCK_ROWS), :],
                    kvbuf.at[sl, t * NSA_TOP_N + j],
                    sem.at[sl]))
        return out

    @pl.when(step == 0)
    def _():
        for cp in copies(step, slot):
            cp.start()

    @pl.when(step + 1 < nsteps)
    def _():
        for cp in copies(step + 1, 1 - slot):
            cp.start()

    for cp in copies(step, slot):
        cp.wait()

    scale = HEAD_DIM ** -0.5
    nkeys = NSA_TOP_N * NSA_BLOCK
    _, zrow, qpos, _ = _stacked_row_info(past)
    q4 = jnp.where(g == 0, _stack_heads(q_ref, 0, scale), _stack_heads(q_ref, 1, scale))
    slope = jnp.where(g == 0, _slope_col(0), _slope_col(1))

    kn = jnp.where(g == 0, kvs_ref[:, 0:HEAD_DIM], kvs_ref[:, 2 * HEAD_DIM:3 * HEAD_DIM])
    vn = jnp.where(g == 0, kvs_ref[:, HEAD_DIM:2 * HEAD_DIM], kvs_ref[:, 3 * HEAD_DIM:4 * HEAD_DIM])
    kz = lax.broadcasted_iota(jnp.int32, (1, ZROWS), 1)
    kpos_n = past + kz - ZTOK0
    ok_n = (kz >= ZTOK0) & (kz < ZTOK0 + seq) & (kpos_n <= qpos)
    s_n = jnp.where(ok_n, _dot_nt(q4, kn) + slope * (kpos_n - past).astype(F32), NEG)

    lane = lax.broadcasted_iota(jnp.int32, (1, nkeys), 1)
    slot_of_lane = lane // NSA_BLOCK
    s_g = jnp.full((NSA_GROUP * ZROWS, nkeys), NEG, F32)
    v_tok = []
    for t in range(seq):
        blk_of_lane = jnp.zeros((1, nkeys), jnp.int32)
        for j in range(NSA_TOP_N):
            blk = sel_ref[((b * seq + t) * NSA_KV_HEADS + g) * NSA_TOP_N + j]
            blk_of_lane = jnp.where(slot_of_lane == j, blk, blk_of_lane)
        k_t = jnp.concatenate([kvbuf[slot, t * NSA_TOP_N + j, pl.ds(g * 2, NSA_BLOCK, stride=4), :]
                               for j in range(NSA_TOP_N)], axis=0)
        v_tok.append(jnp.concatenate([kvbuf[slot, t * NSA_TOP_N + j, pl.ds(g * 2 + 1, NSA_BLOCK, stride=4), :]
                                      for j in range(NSA_TOP_N)], axis=0))
        kpos = blk_of_lane * NSA_BLOCK + lane % NSA_BLOCK
        st = _dot_nt(q4, k_t) + slope * (kpos - past).astype(F32)
        mine = (zrow == ZTOK0 + t) & (blk_of_lane < n_past_blocks) & (kpos <= qpos)
        s_g = jnp.where(mine, st, s_g)

    m = jnp.maximum(jnp.max(s_g, axis=-1, keepdims=True), jnp.max(s_n, axis=-1, keepdims=True))
    p_g = jnp.exp(s_g - m)
    p_n = jnp.exp(s_n - m)
    den = jnp.sum(p_g, axis=-1, keepdims=True) + jnp.sum(p_n, axis=-1, keepdims=True)
    pv = _dot(p_n, vn)
    for t in range(seq):
        pv = pv + jnp.where(zrow == ZTOK0 + t, _dot(p_g, v_tok[t]), 0.0)
    o_slc = pv / den

    gates = jax.nn.sigmoid(misc_ref[...])
    o_cmp = ocmp_ref[...]
    o_win = owin_ref[...]
    for r in range(NSA_GROUP):
        rs = slice(r * ZROWS, (r + 1) * ZROWS)
        outs = []
        for gg in range(NSA_KV_HEADS):
            c = gg * NSA_GROUP * 3 + r * 3 + GATE_LANE
            outs.append(gates[:, c:c + 1] * o_cmp[rs] + gates[:, c + 1:c + 2] * o_slc[rs]
                        + gates[:, c + 2:c + 3] * o_win[rs])
        o_ref[:, r * HEAD_DIM:(r + 1) * HEAD_DIM] = jnp.where(g == 0, outs[0], outs[1]).astype(o_ref.dtype)


def nsa_sample_b(page_table, sel, Ps, ocmp, owin, cache_rows, *, layer, past, seq):
    Bs, n_pages = page_table.shape
    rows = NSA_GROUP * ZROWS
    n_past_blocks = past // NSA_BLOCK
    gw = NSA_GROUP * HEAD_DIM

    big = pl.BlockSpec((None, None, rows, HEAD_DIM), lambda b, g, pt, sl: (b, g, 0, 0))
    in_specs = [pl.BlockSpec((ZROWS, NSA_WIDTH), lambda b, g, pt, sl: (b, Q0 // NSA_WIDTH)),
                pl.BlockSpec((ZROWS, KV_COLS), lambda b, g, pt, sl: (b, KVS0 // KV_COLS)),
                pl.BlockSpec((ZROWS, HEAD_DIM), lambda b, g, pt, sl: (b, MISC0 // HEAD_DIM)),
                big, big,
                pl.BlockSpec(memory_space=pl.ANY)]
    grid_spec = pltpu.PrefetchScalarGridSpec(
        num_scalar_prefetch=2,
        grid=(Bs, NSA_KV_HEADS),
        in_specs=in_specs,
        out_specs=pl.BlockSpec((ZROWS, gw), lambda b, g, pt, sl: (b, g)),
        scratch_shapes=[pltpu.VMEM((2, seq * NSA_TOP_N, BLOCK_ROWS, HEAD_DIM), F32),
                        pltpu.SemaphoreType.DMA((2,))],
    )
    return pl.pallas_call(
        functools.partial(_nsa_sample_b_kernel, layer=layer, past=past, seq=seq, n_past_blocks=n_past_blocks,
                          n_pages=n_pages),
        grid_spec=grid_spec,
        out_shape=jax.ShapeDtypeStruct((Bs * ZROWS, NSA_WIDTH), BF16),
        compiler_params=_cparams("arbitrary", "arbitrary"),
        name="nsa_sample_b",
    )(page_table.reshape(-1), sel, Ps, Ps, Ps, ocmp, owin, cache_rows)


def _reorder_w_in(w_in):
    q, kvc, kvs, kvw = w_in[..., 0:1024], w_in[..., 1024:1536], w_in[..., 1536:2048], w_in[..., 2048:2560]
    gates, qkv, z, ab = w_in[..., 2560:2584], w_in[..., 2584:5656], w_in[..., 5656:6680], w_in[..., 6680:6696]
    pad = jnp.zeros(w_in.shape[:-1] + (IN_COLS_PAD - MISC0 - 40,), w_in.dtype)
    return jnp.concatenate([qkv, q, z, kvc, kvs, kvw, gates, ab, pad], axis=-1).astype(BF16)


def _lane_row(vec, lane0):
    return jnp.zeros((1, HEAD_DIM), F32).at[0, lane0:lane0 + vec.shape[0]].set(vec.astype(F32))


def _to_zrows(x, first_row):
    Bs, n, C = x.shape
    z = jnp.zeros((Bs, ZROWS, C), x.dtype).at[:, first_row:first_row + n].set(x)
    return z.reshape(Bs * ZROWS, C)


def kernel(x_prompt, x_sample, cache_cmp_kv, cache_slc_kv, cache_win_kv, state_gdn, state_gdn_conv, state_ffn_conv, page_table, w_in, w_o, cmp_pe, cmp_w1, cmp_w2, gdn_conv_w, gdn_a_log, gdn_dt_bias, gdn_norm_g, ln1_g, ln1_b, ffn_w_up, ffn_conv_w, ffn_w_down, ln2_g, ln2_b):
    B, S, D = x_prompt.shape
    Bs, seq, _ = x_sample.shape
    depth = w_in.shape[0]
    n_pool, page = cache_cmp_kv.shape[1], cache_cmp_kv.shape[2]
    n_pages = page_table.shape[1]
    past = n_pages * page
    wb = cache_win_kv.shape[2]
    d_ff = ffn_w_down.shape[1]
    alpha = (2 * depth) ** 0.25
    assert page == 2 * NSA_BLOCK

    w_in_r = _reorder_w_in(w_in)
    w_o_b = w_o.astype(BF16)
    w_up_b = ffn_w_up.astype(BF16)
    w_down_b = ffn_w_down.astype(BF16)
    w1_b = cmp_w1.astype(BF16)
    w2_b = cmp_w2.astype(BF16)
    pe_t = jnp.swapaxes(cmp_pe, 1, 2)
    cache_cmp_rows = cache_cmp_kv.reshape(depth, n_pool, PAGE_ROWS, HEAD_DIM)
    cache_slc_rows = cache_slc_kv.reshape(depth, n_pool, PAGE_ROWS, HEAD_DIM)
    cache_win_rows = cache_win_kv.reshape(depth, Bs, wb * 4, HEAD_DIM)

    xp = x_prompt.reshape(B * S, D)
    xs = _to_zrows(x_sample, ZTOK0)
    xp_b, xs_b = xp, xs
    tm = 512 if (B * S) % 512 == 0 else B * S
    tm_in = 1024 if (B * S) % 1024 == 0 else tm
    tk_down = 1408 if d_ff % 1408 == 0 else 512
    tf_up = d_ff // 2 if d_ff % (2 * MXU_COLS) == 0 else tk_down
    tq = 256 if S % 256 == 0 else 128
    p_out = [[] for _ in range(6)]
    s_out = [[] for _ in range(6)]
    zero_state = jnp.zeros((B, GDN_HEADS, HEAD_DIM, HEAD_DIM), F32)

    for l in range(depth):
        alog_row = _lane_row(gdn_a_log[l], A_LANE)
        dtb_row = _lane_row(gdn_dt_bias[l], A_LANE)
        ng = gdn_norm_g[l].reshape(1, HEAD_DIM)
        g1, b1 = ln1_g[l].reshape(1, D), ln1_b[l].reshape(1, D)
        g2, b2 = ln2_g[l].reshape(1, D), ln2_b[l].reshape(1, D)

        P = matmul(xp_b, w_in_r, l, tm=tm_in, tn=1024)
        kmat, vt = kv_prep(P, n_seq=B, tm=tm)
        ckv = compress_prompt(P, pe_t[l], w1_b[l], w2_b[l], n_seq=B)
        ckv = ckv.reshape(B, 2, NSA_KV_HEADS, S // NSA_BLOCK, HEAD_DIM).astype(BF16)
        o_nsa = nsa_prompt(P, kmat, vt, ckv[:, 0], jnp.swapaxes(ckv[:, 1], -1, -2), n_seq=B, tq=tq,
                           tk=256 if S % 256 == 0 else tq)
        prep = gdn_prep(P, gdn_conv_w[l], alog_row, dtb_row, n_seq=B, C=GDN_CHUNK)
        o_gdn, s_fin = gdn_scan(prep, P, ng, zero_state, n_seq=B, C=GDN_CHUNK)
        h, h_b = proj_ln(o_nsa, o_gdn, w_o_b, l, xp, g1, b1, alpha=alpha, tm=256)
        act, utail = ffn_up_prompt(h_b, w_up_b, l, ffn_conv_w[l], n_seq=B, tm=tm, tf=tf_up)
        xp, xp_b = ffn_down_ln_resident(act, w_down_b, l, h, g2, b2, alpha=alpha, tm=256)

        P3 = P.reshape(B, S, IN_COLS_PAD)
        p_out[0].append(P3[:, :, KVC0:KVC0 + KV_COLS].reshape(B, S, NSA_KV_HEADS, 2, HEAD_DIM))
        p_out[1].append(P3[:, :, KVS0:KVS0 + KV_COLS].reshape(B, S, NSA_KV_HEADS, 2, HEAD_DIM))
        wn = min(NSA_WINDOW, S)
        p_out[2].append(P3[:, S - wn:, KVW0:KVW0 + KV_COLS].reshape(B, wn, NSA_KV_HEADS, 2, HEAD_DIM))
        p_out[3].append(s_fin)
        p_out[4].append(P3[:, S - (GDN_CONV - 1):, QKV0:QKV0 + 3 * GDN_WIDTH])
        p_out[5].append(utail[:, 8 - (FFN_CONV - 1):, :])

        Ps = matmul(xs_b, w_in_r, l, tm=Bs * ZROWS, tn=1024)
        ckv_s = compress_sample(page_table, cache_cmp_rows, pe_t[l], w1_b[l], w2_b[l], layer=l)
        ng_grp = ckv_s.shape[1]
        nblk_step = ckv_s.shape[3] // NSA_KV_HEADS
        ckv_s = ckv_s.reshape(Bs, ng_grp, 2, NSA_KV_HEADS, nblk_step, HEAD_DIM)
        ckv_s = jnp.transpose(ckv_s, (0, 2, 3, 1, 4, 5)).reshape(Bs, 2, NSA_KV_HEADS, ng_grp * nblk_step, HEAD_DIM)
        ocmp, owin, sel, win_new = nsa_sample_a(Ps, ckv_s, cache_win_rows, layer=l, past=past, seq=seq)
        sel_flat = jnp.transpose(sel[:, :, ZTOK0:ZTOK0 + seq, :NSA_TOP_N], (0, 2, 1, 3)).reshape(-1)
        o_nsa_s = nsa_sample_b(page_table, sel_flat, Ps, ocmp, owin, cache_slc_rows, layer=l, past=past, seq=seq)
        gbufz = _to_zrows(state_gdn_conv[l], ZTOK0 - (GDN_CONV - 1))
        prep_s = gdn_prep(Ps, gdn_conv_w[l], alog_row, dtb_row, n_seq=Bs, C=ZROWS, bufz=gbufz)
        o_gdn_s, s_fin_s = gdn_scan(prep_s, Ps, ng, state_gdn[l], n_seq=Bs, C=ZROWS)
        hs, hs_b = proj_ln(o_nsa_s, o_gdn_s, w_o_b, l, xs, g1, b1, alpha=alpha, tm=Bs * ZROWS)
        fbufz = _to_zrows(state_ffn_conv[l], ZTOK0 - (FFN_CONV - 1))
        act_s, u_s = ffn_up_sample(hs_b, w_up_b, l, ffn_conv_w[l], fbufz, tf=tk_down)
        xs, xs_b = ffn_down_ln(act_s, w_down_b, l, hs, g2, b2, alpha=alpha, tm=Bs * ZROWS, tk=tk_down)

        Ps3 = Ps.reshape(Bs, ZROWS, IN_COLS_PAD)[:, ZTOK0:ZTOK0 + seq]
        s_out[0].append(Ps3[:, :, KVC0:KVC0 + KV_COLS].reshape(Bs, seq, NSA_KV_HEADS, 2, HEAD_DIM))
        s_out[1].append(Ps3[:, :, KVS0:KVS0 + KV_COLS].reshape(Bs, seq, NSA_KV_HEADS, 2, HEAD_DIM))
        s_out[2].append(win_new.reshape(Bs, wb, NSA_KV_HEADS, 2, HEAD_DIM))
        s_out[3].append(s_fin_s)
        ext_g = jnp.concatenate([state_gdn_conv[l], Ps3[:, :, QKV0:QKV0 + 3 * GDN_WIDTH]], axis=1)
        s_out[4].append(ext_g[:, -(GDN_CONV - 1):])
        u3 = u_s.reshape(Bs, ZROWS, d_ff)[:, ZTOK0:ZTOK0 + seq]
        ext_f = jnp.concatenate([state_ffn_conv[l], u3], axis=1)
        s_out[5].append(ext_f[:, -(FFN_CONV - 1):])

    y_p = xp.reshape(B, S, D)
    y_s = xs.reshape(Bs, ZROWS, D)[:, ZTOK0:ZTOK0 + seq]
    return (y_p, y_s, *[jnp.stack(v, 0) for v in p_out], *[jnp.stack(v, 0) for v in s_out])
```

```python
import functools

import jax
import jax.numpy as jnp
import numpy as np
from jax import lax
from jax.experimental import pallas as pl
from jax.experimental.pallas import tpu as pltpu

F32 = jnp.float32
BF16 = jnp.bfloat16

HEAD_DIM = 128
NSA_HEADS = 8
NSA_KV_HEADS = 2
NSA_GROUP = 4
NSA_WIDTH = 1024
NSA_BLOCK = 64
NSA_TOP_N = 16
NSA_WINDOW = 512
GDN_HEADS = 8
GDN_WIDTH = 1024
GDN_CONV = 4
GDN_CHUNK = 64
FFN_CONV = 3
KV_COLS = 512
LN_EPS = 1e-5
NORM_EPS = 1e-6
NEG = -1e30
FORCE = 1e4
LOG2E = 1.4426950408889634
SLOPES = [[2.0 ** -(g * NSA_GROUP + r + 1) for r in range(NSA_GROUP)] for g in range(NSA_KV_HEADS)]

QKV0, Q0, Z0, KVC0, KVS0, KVW0, MISC0, IN_COLS_PAD = 0, 3072, 4096, 5120, 5632, 6144, 6656, 7168
GATE_LANE, A_LANE, B_LANE = 0, 24, 32

ZROWS = 16
ZTOK0 = 8

VMEM_LIMIT_BYTES = 56 * 1024 * 1024


def _cparams(*sem):
    return pltpu.CompilerParams(dimension_semantics=sem, vmem_limit_bytes=VMEM_LIMIT_BYTES)


def _dot(a, b):
    return jnp.dot(a.astype(BF16), b.astype(BF16), preferred_element_type=F32)


def _dot_nt(a, b):
    return lax.dot_general(a.astype(BF16), b.astype(BF16), (((1,), (1,)), ((), ())), preferred_element_type=F32)


def _dot_tn(a, b):
    return lax.dot_general(a.astype(BF16), b.astype(BF16), (((0,), (0,)), ((), ())), preferred_element_type=F32)


def _layer_norm(t, g, b):
    mu = jnp.mean(t, -1, keepdims=True)
    d = t - mu
    var = jnp.mean(d * d, -1, keepdims=True)
    return d * lax.rsqrt(var + LN_EPS) * g + b


def _mm_kernel(x_ref, w_ref, o_ref):
    o_ref[...] = _dot(x_ref[...], w_ref[...])


def matmul(x, w, layer, *, tm, tn):
    M, K = x.shape
    N = w.shape[2]
    assert M % tm == 0 and N % tn == 0
    return pl.pallas_call(
        _mm_kernel,
        grid=(M // tm, N // tn),
        in_specs=[pl.BlockSpec((tm, K), lambda i, j: (i, 0)),
                  pl.BlockSpec((None, K, tn), lambda i, j: (layer, 0, j))],
        out_specs=pl.BlockSpec((tm, tn), lambda i, j: (i, j)),
        out_shape=jax.ShapeDtypeStruct((M, N), F32),
        compiler_params=_cparams("parallel", "arbitrary"),
        name="in_proj",
    )(x, w)


def _proj_ln_kernel(a1_ref, a2_ref, w_ref, x_ref, g_ref, b_ref, o_ref, ob_ref, *, alpha):
    k1 = a1_ref.shape[1]
    acc = _dot(a1_ref[...], w_ref[:k1, :]) + _dot(a2_ref[...], w_ref[k1:, :])
    o = _layer_norm(alpha * x_ref[...] + acc, g_ref[...], b_ref[...])
    o_ref[...] = o
    ob_ref[...] = o.astype(ob_ref.dtype)


def proj_ln(a1, a2, w, layer, x, g, b, *, alpha, tm):
    M, D = x.shape
    k1, k2 = a1.shape[1], a2.shape[1]
    assert M % tm == 0 and w.shape[1:] == (k1 + k2, D)
    return pl.pallas_call(
        functools.partial(_proj_ln_kernel, alpha=alpha),
        grid=(M // tm,),
        in_specs=[pl.BlockSpec((tm, k1), lambda i: (i, 0)),
                  pl.BlockSpec((tm, k2), lambda i: (i, 0)),
                  pl.BlockSpec((None, k1 + k2, D), lambda i: (layer, 0, 0)),
                  pl.BlockSpec((tm, D), lambda i: (i, 0)),
                  pl.BlockSpec((1, D), lambda i: (0, 0)),
                  pl.BlockSpec((1, D), lambda i: (0, 0))],
        out_specs=[pl.BlockSpec((tm, D), lambda i: (i, 0)), pl.BlockSpec((tm, D), lambda i: (i, 0))],
        out_shape=[jax.ShapeDtypeStruct((M, D), F32), jax.ShapeDtypeStruct((M, D), BF16)],
        compiler_params=_cparams("parallel"),
        name="out_proj_ln",
    )(a1, a2, w, x, g, b)


def _conv3(u, prev8, cw):
    ue = jnp.concatenate([prev8, u], axis=0)
    u1 = pltpu.roll(ue, 1, 0)[8:]
    u2 = pltpu.roll(ue, 2, 0)[8:]
    return cw[2:3, :] * u + cw[1:2, :] * u1 + cw[0:1, :] * u2


MXU_COLS = 256


def _col_chunks(width):
    return [slice(c0, min(c0 + MXU_COLS, width)) for c0 in range(0, width, MXU_COLS)]


def _ffn_up_prompt_kernel(h_ref, wu_ref, wv_ref, cw_ref, act_ref, tail_ref, carry_ref, *, tiles_per_seq):
    i = pl.program_id(1)
    h = h_ref[...]
    tm = h.shape[0]

    @pl.when(i % tiles_per_seq == 0)
    def _():
        carry_ref[...] = jnp.zeros_like(carry_ref)

    for cs in _col_chunks(act_ref.shape[1]):
        u = _dot(h, wu_ref[:, cs])
        v = _dot(h, wv_ref[:, cs])
        uc = _conv3(u, carry_ref[:, cs], cw_ref[:, cs])
        act_ref[:, cs] = (jax.nn.gelu(uc) * v).astype(act_ref.dtype)
        tail = u[tm - 8:, :]
        carry_ref[:, cs] = tail
        tail_ref[:, cs] = tail


def ffn_up_prompt(h, w_up, layer, cw, *, n_seq, tm, tf):
    M, D = h.shape
    F = w_up.shape[2] // 2
    T = M // n_seq
    assert M % tm == 0 and F % tf == 0 and T % tm == 0 and tf % HEAD_DIM == 0
    nf = F // tf
    tps = T // tm
    return pl.pallas_call(
        functools.partial(_ffn_up_prompt_kernel, tiles_per_seq=tps),
        grid=(nf, M // tm),
        in_specs=[pl.BlockSpec((tm, D), lambda f, i: (i, 0)),
                  pl.BlockSpec((None, D, tf), lambda f, i: (layer, 0, f), pipeline_mode=pl.Buffered(1)),
                  pl.BlockSpec((None, D, tf), lambda f, i: (layer, 0, nf + f), pipeline_mode=pl.Buffered(1)),
                  pl.BlockSpec((FFN_CONV, tf), lambda f, i: (0, f))],
        out_specs=[pl.BlockSpec((tm, tf), lambda f, i: (i, f)),
                   pl.BlockSpec((None, 8, tf), lambda f, i: (i // tps, 0, f))],
        out_shape=[jax.ShapeDtypeStruct((M, F), BF16),
                   jax.ShapeDtypeStruct((n_seq, 8, F), F32)],
        scratch_shapes=[pltpu.VMEM((8, tf), F32)],
        compiler_params=_cparams("arbitrary", "arbitrary"),
        name="ffn_up_prompt",
    )(h, w_up, w_up, cw)


def _ffn_up_sample_kernel(h_ref, wu_ref, wv_ref, cw_ref, buf_ref, act_ref, u_ref):
    h = h_ref[...]
    zrow = lax.broadcasted_iota(jnp.int32, (h.shape[0], 1), 0) % ZROWS
    isbuf = (zrow >= ZTOK0 - (FFN_CONV - 1)) & (zrow < ZTOK0)
    for cs in _col_chunks(act_ref.shape[1]):
        u = _dot(h, wu_ref[:, cs])
        v = _dot(h, wv_ref[:, cs])
        uz = jnp.where(isbuf, buf_ref[:, cs], u)
        cw = cw_ref[:, cs]
        uc = cw[2:3, :] * uz + cw[1:2, :] * pltpu.roll(uz, 1, 0) + cw[0:1, :] * pltpu.roll(uz, 2, 0)
        act_ref[:, cs] = (jax.nn.gelu(uc) * v).astype(act_ref.dtype)
        u_ref[:, cs] = u


def ffn_up_sample(h, w_up, layer, cw, bufz, *, tf):
    M, D = h.shape
    F = w_up.shape[2] // 2
    nf = F // tf
    return pl.pallas_call(
        _ffn_up_sample_kernel,
        grid=(nf,),
        in_specs=[pl.BlockSpec((M, D), lambda f: (0, 0)),
                  pl.BlockSpec((None, D, tf), lambda f: (layer, 0, f)),
                  pl.BlockSpec((None, D, tf), lambda f: (layer, 0, nf + f)),
                  pl.BlockSpec((FFN_CONV, tf), lambda f: (0, f)),
                  pl.BlockSpec((M, tf), lambda f: (0, f))],
        out_specs=[pl.BlockSpec((M, tf), lambda f: (0, f)),
                   pl.BlockSpec((M, tf), lambda f: (0, f))],
        out_shape=[jax.ShapeDtypeStruct((M, F), BF16),
                   jax.ShapeDtypeStruct((M, F), F32)],
        compiler_params=_cparams("parallel"),
        name="ffn_up_sample",
    )(h, w_up, w_up, cw, bufz)


def _ffn_down_ln_kernel(a_ref, w_ref, h_ref, g_ref, b_ref, o_ref, ob_ref, acc_ref, *, alpha):
    k = pl.program_id(1)

    @pl.when(k == 0)
    def _():
        acc_ref[...] = jnp.zeros_like(acc_ref)

    acc_ref[...] += _dot(a_ref[...], w_ref[...])

    @pl.when(k == pl.num_programs(1) - 1)
    def _():
        o = _layer_norm(alpha * h_ref[...] + acc_ref[...], g_ref[...], b_ref[...])
        o_ref[...] = o
        ob_ref[...] = o.astype(ob_ref.dtype)


def _ffn_down_ln_resident_kernel(a_ref, w_ref, h_ref, g_ref, b_ref, o_ref, ob_ref, *, alpha):
    acc = jnp.dot(a_ref[...], w_ref[...], preferred_element_type=F32)
    o = _layer_norm(alpha * h_ref[...] + acc, g_ref[...], b_ref[...])
    o_ref[...] = o
    ob_ref[...] = o.astype(ob_ref.dtype)


def ffn_down_ln_resident(a, w, layer, h, g, b, *, alpha, tm):
    M, Fd = a.shape
    D = w.shape[2]
    assert M % tm == 0
    return pl.pallas_call(
        functools.partial(_ffn_down_ln_resident_kernel, alpha=alpha),
        grid=(M // tm,),
        in_specs=[pl.BlockSpec((tm, Fd), lambda i: (i, 0)),
                  pl.BlockSpec((None, Fd, D), lambda i: (layer, 0, 0), pipeline_mode=pl.Buffered(1)),
                  pl.BlockSpec((tm, D), lambda i: (i, 0)),
                  pl.BlockSpec((1, D), lambda i: (0, 0)),
                  pl.BlockSpec((1, D), lambda i: (0, 0))],
        out_specs=[pl.BlockSpec((tm, D), lambda i: (i, 0)), pl.BlockSpec((tm, D), lambda i: (i, 0))],
        out_shape=[jax.ShapeDtypeStruct((M, D), F32), jax.ShapeDtypeStruct((M, D), BF16)],
        compiler_params=_cparams("parallel"),
        name="ffn_down_ln",
    )(a, w, h, g, b)


def ffn_down_ln(a, w, layer, h, g, b, *, alpha, tm, tk):
    M, Fd = a.shape
    D = w.shape[2]
    assert M % tm == 0 and Fd % tk == 0
    return pl.pallas_call(
        functools.partial(_ffn_down_ln_kernel, alpha=alpha),
        grid=(M // tm, Fd // tk),
        in_specs=[pl.BlockSpec((tm, tk), lambda i, k: (i, k)),
                  pl.BlockSpec((None, tk, D), lambda i, k: (layer, k, 0)),
                  pl.BlockSpec((tm, D), lambda i, k: (i, 0)),
                  pl.BlockSpec((1, D), lambda i, k: (0, 0)),
                  pl.BlockSpec((1, D), lambda i, k: (0, 0))],
        out_specs=[pl.BlockSpec((tm, D), lambda i, k: (i, 0)), pl.BlockSpec((tm, D), lambda i, k: (i, 0))],
        out_shape=[jax.ShapeDtypeStruct((M, D), F32), jax.ShapeDtypeStruct((M, D), BF16)],
        scratch_shapes=[pltpu.VMEM((tm, D), F32)],
        compiler_params=_cparams("parallel", "arbitrary"),
        name="ffn_down_ln",
    )(a, w, h, g, b)


def _compress_rows(read_rows, pe_ref, w1_ref, w2_ref, out_ref, nblk):
    del nblk
    for c in range(2):
        parts = []
        for p in range(NSA_BLOCK):
            pe_row = pe_ref[c, p:p + 1, :]
            rows = [read_rows(g * 2 + c, p) + pe_row for g in range(NSA_KV_HEADS)]
            parts.append(jnp.concatenate(rows, axis=0).astype(BF16))
        hid = jnp.dot(jnp.concatenate(parts, axis=1), w1_ref[c], preferred_element_type=F32)
        out_ref[c] = _dot(jax.nn.silu(hid), w2_ref[c])


def _compress_prompt_kernel(x0, x1, x2, x3, pe_ref, w1_ref, w2_ref, out_ref, *, nblk):
    xs = [x0, x1, x2, x3]
    _compress_rows(lambda gc, p: xs[gc][:, p, :], pe_ref, w1_ref, w2_ref, out_ref, nblk)


def compress_prompt(P, pe_t, w1, w2, *, n_seq):
    M = P.shape[0]
    T = M // n_seq
    nblk = T // NSA_BLOCK
    P3 = P.reshape(n_seq * nblk, NSA_BLOCK, IN_COLS_PAD)
    cb0 = KVC0 // HEAD_DIM
    in_specs = [pl.BlockSpec((nblk, NSA_BLOCK, HEAD_DIM), (lambda b, gc=gc: (b, 0, cb0 + gc))) for gc in range(4)]
    in_specs += [pl.BlockSpec(pe_t.shape, lambda b: (0, 0, 0)),
                 pl.BlockSpec(w1.shape, lambda b: (0, 0, 0)),
                 pl.BlockSpec(w2.shape, lambda b: (0, 0, 0))]
    return pl.pallas_call(
        functools.partial(_compress_prompt_kernel, nblk=nblk),
        grid=(n_seq,),
        in_specs=in_specs,
        out_specs=pl.BlockSpec((None, 2, NSA_KV_HEADS * nblk, HEAD_DIM), lambda b: (b, 0, 0, 0)),
        out_shape=jax.ShapeDtypeStruct((n_seq, 2, NSA_KV_HEADS * nblk, HEAD_DIM), F32),
        compiler_params=_cparams("parallel"),
        name="compress_prompt",
    )(P3, P3, P3, P3, pe_t, w1, w2)


def _rank_desc(score, n_real):
    lane = lax.broadcasted_iota(jnp.int32, score.shape, 1)
    cnt = jnp.zeros(score.shape, F32)
    for i in range(n_real):
        col = score[:, i:i + 1]
        ge = jnp.where(col >= score, 1.0, 0.0)
        gt = jnp.where(col > score, 1.0, 0.0)
        cnt = cnt + jnp.where(lane > i, ge, gt)
    return cnt


def _block_scores(imp, qpos, n_blocks):
    bidx = lax.broadcasted_iota(jnp.int32, imp.shape, 1)
    cur = qpos // NSA_BLOCK
    sc = jnp.where(bidx == 0, FORCE, jnp.where(bidx == cur, FORCE, jnp.where(bidx == cur - 1, FORCE, imp)))
    sc = jnp.where(bidx * NSA_BLOCK <= qpos, sc, NEG)
    return jnp.where(bidx < n_blocks, sc, -jnp.inf)


def _rank_desc_t(score, n_real):
    L = score.shape[0]
    assert L % 8 == 0
    groups = [score[8 * a:8 * a + 8, :] for a in range(L // 8)]
    sub = lax.broadcasted_iota(jnp.int32, groups[0].shape, 0)
    cnts = [jnp.zeros(groups[0].shape, F32) for _ in groups]
    for i in range(n_real):
        row = score[i:i + 1, :]
        for a, x in enumerate(groups):
            if 8 * a > i:
                c = jnp.where(row >= x, 1.0, 0.0)
            elif 8 * a + 7 < i:
                c = jnp.where(row > x, 1.0, 0.0)
            else:
                c = jnp.where(sub + 8 * a > i, jnp.where(row >= x, 1.0, 0.0), jnp.where(row > x, 1.0, 0.0))
            cnts[a] = cnts[a] + c
    return jnp.concatenate(cnts, axis=0)


def _block_scores_t(imp, qpos, n_blocks):
    bidx = lax.broadcasted_iota(jnp.int32, imp.shape, 0)
    cur = qpos // NSA_BLOCK
    sc = jnp.where(bidx == 0, FORCE, jnp.where(bidx == cur, FORCE, jnp.where(bidx == cur - 1, FORCE, imp)))
    sc = jnp.where(bidx * NSA_BLOCK <= qpos, sc, NEG)
    return jnp.where(bidx < n_blocks, sc, -jnp.inf)


def _nsa_prompt_kernel(q_ref, misc_ref, k_ref, vts_ref, vtw_ref, ck_ref, cvt_ref, o_ref, m_ref, l_ref, acc_ref, flag_ref, *, tq, tk, nblk):
    i = pl.program_id(1)
    t0 = i * tq
    bpt = tk // NSA_BLOCK
    wspan = min(NSA_WINDOW + tq + HEAD_DIM, k_ref.shape[0])
    scale = HEAD_DIM ** -0.5 * LOG2E
    top_n = min(NSA_TOP_N, nblk)
    qpos_row = t0 + lax.broadcasted_iota(jnp.int32, (1, tq), 1)
    sub_k = lax.broadcasted_iota(jnp.int32, (tk, tq), 0)
    d_kq = sub_k - lax.broadcasted_iota(jnp.int32, (tk, tq), 1)
    sub_kf = sub_k.astype(F32)
    wsub_k = lax.broadcasted_iota(jnp.int32, (wspan, tq), 0)
    wsub_kf = wsub_k.astype(F32)
    wwin_code = -2 * (wsub_k - lax.broadcasted_iota(jnp.int32, (wspan, tq), 1)) - (NSA_WINDOW - 1)
    e_rel = (lax.broadcasted_iota(jnp.int32, (tk, nblk), 1)
             - lax.broadcasted_iota(jnp.int32, (tk, nblk), 0) // NSA_BLOCK)
    gates_t = jax.nn.sigmoid(misc_ref[...]).T

    def reset():
        m_ref[...] = jnp.full(m_ref.shape, NEG, F32)
        l_ref[...] = jnp.zeros(l_ref.shape, F32)
        acc_ref[...] = jnp.zeros(acc_ref.shape, F32)

    def online_update(s4, vt):
        m_old = m_ref[...]
        m_new = jnp.maximum(m_old, jnp.max(s4, axis=0, keepdims=True))
        p = jnp.exp2(s4 - m_new)
        alpha = jnp.exp2(m_old - m_new)
        l_ref[...] = alpha * l_ref[...] + jnp.sum(p, axis=0, keepdims=True)
        acc_ref[...] = alpha * acc_ref[...] + jnp.dot(vt, p.astype(BF16), preferred_element_type=F32)
        m_ref[...] = m_new

    def masked_scores(s, valid, base, g):
        return jnp.concatenate(
            [jnp.where(valid, s[:, r * tq:(r + 1) * tq] + (SLOPES[g][r] * LOG2E) * base, NEG)
             for r in range(NSA_GROUP)], axis=1)

    for g in range(NSA_KV_HEADS):
        q4t = jnp.concatenate(
            [(q_ref[:, (g * NSA_GROUP + r) * HEAD_DIM:(g * NSA_GROUP + r + 1) * HEAD_DIM] * scale).T
             for r in range(NSA_GROUP)], axis=1).astype(BF16)
        sc = jnp.dot(ck_ref[g], q4t, preferred_element_type=F32)
        cpos = lax.broadcasted_iota(jnp.int32, (nblk, tq), 0) * NSA_BLOCK + (NSA_BLOCK - 1)
        vm = cpos <= qpos_row
        cposrel = (cpos - t0).astype(F32)
        imp = jnp.zeros((nblk, tq), F32)
        parts = []
        for r in range(NSA_GROUP):
            s = jnp.where(vm, sc[:, r * tq:(r + 1) * tq] + (SLOPES[g][r] * LOG2E) * cposrel, NEG)
            e = jnp.exp2(s - jnp.max(s, axis=0, keepdims=True))
            p = jnp.where(vm, e / jnp.sum(e, axis=0, keepdims=True), 0.0)
            imp = imp + p
            parts.append(p)
        ocmp_t = jnp.dot(cvt_ref[g], jnp.concatenate(parts, axis=1).astype(BF16), preferred_element_type=F32)
        rank = _rank_desc_t(_block_scores_t(imp, qpos_row, nblk), nblk)
        sel_f = jnp.where(rank < top_n, 1.0, 0.0)
        sel_t = sel_f.astype(BF16)
        blk_any = jnp.max(sel_f, axis=1, keepdims=True)
        for jt in range(nblk // bpt):
            flag_ref[jt] = (jnp.max(blk_any[jt * bpt:(jt + 1) * bpt, :]) > 0.5).astype(jnp.int32)

        reset()

        def slc_body(j, carry):
            k0 = pl.multiple_of(j * tk, tk)

            @pl.when(flag_ref[j] > 0)
            def _():
                k = k_ref[pl.ds(k0, tk), g * HEAD_DIM:(g + 1) * HEAD_DIM]
                vt = vts_ref[g, :, pl.ds(k0, tk)]
                off = k0 - t0
                expand = jnp.where(e_rel == k0 // NSA_BLOCK, 1.0, 0.0).astype(BF16)
                chosen = jnp.dot(expand, sel_t, preferred_element_type=F32)
                valid = jnp.where(d_kq <= -off, chosen, 0.0) > 0.5
                s = jnp.dot(k, q4t, preferred_element_type=F32)
                online_update(masked_scores(s, valid, sub_kf + off.astype(F32), g), vt)

            return carry

        lax.fori_loop(0, (t0 + tq + tk - 1) // tk, slc_body, 0)
        oslc_t = acc_ref[...] / l_ref[...]

        w0 = pl.multiple_of(jnp.maximum(t0 + tq - wspan, 0), HEAD_DIM)
        kw = k_ref[pl.ds(w0, wspan), (NSA_KV_HEADS + g) * HEAD_DIM:(NSA_KV_HEADS + g + 1) * HEAD_DIM]
        woff = w0 - t0
        wvalid = jnp.abs(wwin_code - 2 * woff) <= (NSA_WINDOW - 1)
        wbase = wsub_kf + woff.astype(F32)
        sw = jnp.dot(kw, q4t, preferred_element_type=F32)
        sw = jnp.concatenate(
            [jnp.where(wvalid, sw[:, r * tq:(r + 1) * tq] + (SLOPES[g][r] * LOG2E) * wbase, NEG)
             for r in range(NSA_GROUP)], axis=1)
        pw = jnp.exp2(sw - jnp.max(sw, axis=0, keepdims=True))
        owin_t = (jnp.dot(vtw_ref[g, :, pl.ds(w0, wspan)], pw.astype(BF16), preferred_element_type=F32)
                  / jnp.sum(pw, axis=0, keepdims=True))

        for r in range(NSA_GROUP):
            c = g * NSA_GROUP * 3 + r * 3 + GATE_LANE
            rs = slice(r * tq, (r + 1) * tq)
            o_t = (gates_t[c:c + 1, :] * ocmp_t[:, rs] + gates_t[c + 1:c + 2, :] * oslc_t[:, rs]
                   + gates_t[c + 2:c + 3, :] * owin_t[:, rs])
            h = g * NSA_GROUP + r
            o_ref[:, h * HEAD_DIM:(h + 1) * HEAD_DIM] = o_t.T.astype(o_ref.dtype)


def _kv_prep_kernel(kvs_ref, kvw_ref, kmat_ref, vt_ref):
    for idx, (src, g) in enumerate([(kvs_ref, 0), (kvs_ref, 1), (kvw_ref, 0), (kvw_ref, 1)]):
        kmat_ref[:, idx * HEAD_DIM:(idx + 1) * HEAD_DIM] = src[:, 2 * g * HEAD_DIM:(2 * g + 1) * HEAD_DIM].astype(BF16)
        vt_ref[idx] = src[:, (2 * g + 1) * HEAD_DIM:(2 * g + 2) * HEAD_DIM].T.astype(BF16)


def kv_prep(P, *, n_seq, tm):
    M = P.shape[0]
    T = M // n_seq
    assert T % tm == 0
    tps = T // tm
    return pl.pallas_call(
        _kv_prep_kernel,
        grid=(M // tm,),
        in_specs=[pl.BlockSpec((tm, KV_COLS), lambda i: (i, KVS0 // KV_COLS)),
                  pl.BlockSpec((tm, KV_COLS), lambda i: (i, KVW0 // KV_COLS))],
        out_specs=[pl.BlockSpec((tm, 4 * HEAD_DIM), lambda i: (i, 0)),
                   pl.BlockSpec((None, 4, HEAD_DIM, tm), lambda i: (i // tps, 0, 0, i % tps))],
        out_shape=[jax.ShapeDtypeStruct((M, 4 * HEAD_DIM), BF16),
                   jax.ShapeDtypeStruct((n_seq, 4, HEAD_DIM, T), BF16)],
        compiler_params=_cparams("parallel"),
        name="kv_prep",
    )(P, P)


def nsa_prompt(P, kmat, vt, ck, cvt, *, n_seq, tq, tk):
    M = P.shape[0]
    T = M // n_seq
    nblk = T // NSA_BLOCK
    nq = T // tq
    W = NSA_GROUP * tq
    assert T % tq == 0 and T % tk == 0 and tk % NSA_BLOCK == 0 and tk % HEAD_DIM == 0
    return pl.pallas_call(
        functools.partial(_nsa_prompt_kernel, tq=tq, tk=tk, nblk=nblk),
        grid=(n_seq, nq),
        in_specs=[pl.BlockSpec((tq, NSA_WIDTH), lambda b, i: (b * nq + i, Q0 // NSA_WIDTH)),
                  pl.BlockSpec((tq, HEAD_DIM), lambda b, i: (b * nq + i, MISC0 // HEAD_DIM)),
                  pl.BlockSpec((T, 2 * NSA_KV_HEADS * HEAD_DIM), lambda b, i: (b, 0)),
                  pl.BlockSpec((None, NSA_KV_HEADS, HEAD_DIM, T), lambda b, i: (b, 0, 0, 0)),
                  pl.BlockSpec((None, NSA_KV_HEADS, HEAD_DIM, T), lambda b, i: (b, 1, 0, 0)),
                  pl.BlockSpec((None, NSA_KV_HEADS, nblk, HEAD_DIM), lambda b, i: (b, 0, 0, 0)),
                  pl.BlockSpec((None, NSA_KV_HEADS, HEAD_DIM, nblk), lambda b, i: (b, 0, 0, 0))],
        out_specs=pl.BlockSpec((tq, NSA_WIDTH), lambda b, i: (b * nq + i, 0)),
        out_shape=jax.ShapeDtypeStruct((M, NSA_WIDTH), BF16),
        scratch_shapes=[pltpu.VMEM((1, W), F32),
                        pltpu.VMEM((1, W), F32),
                        pltpu.VMEM((HEAD_DIM, W), F32),
                        pltpu.SMEM((T // tk,), jnp.int32)],
        compiler_params=_cparams("parallel", "arbitrary"),
        name="nsa_prompt",
    )(P, P, kmat, vt, vt, ck, cvt)


def _bdot(a, b):
    return lax.dot_general(a.astype(BF16), b.astype(BF16), (((2,), (1,)), ((0,), (0,))), preferred_element_type=F32)


def _bdot_nt(a, b):
    return lax.dot_general(a.astype(BF16), b.astype(BF16), (((2,), (2,)), ((0,), (0,))), preferred_element_type=F32)


def _bdot_tn(a, b):
    return lax.dot_general(a.astype(BF16), b.astype(BF16), (((1,), (1,)), ((0,), (0,))), preferred_element_type=F32)


def _head_stack(a, off, width=HEAD_DIM):
    return jnp.stack([a[:, off + h * width:off + (h + 1) * width] for h in range(GDN_HEADS)], axis=0)


def _unit_lower_inverse_minus_eye(A, C):
    row = lax.broadcasted_iota(jnp.int32, (C, C), 0)
    col = lax.broadcasted_iota(jnp.int32, (C, C), 1)
    A8 = jnp.where((row // 8) == (col // 8), A, 0.0)
    B2 = _bdot(A8, A8)
    B4 = _bdot(B2, B2)
    P1 = B2 - A8 - _bdot(A8, B2)
    N = P1 + B4 + _bdot(P1, B4)
    size = 16
    while size <= C:
        AL = jnp.where((row // size) == (col // size), jnp.where((row // (size // 2)) == (col // (size // 2)), 0.0, A), 0.0)
        X = AL + _bdot(N, AL)
        N = N - X - _bdot(X, N)
        size *= 2
    return N


def _gdn_prep_kernel(*refs, C, sample):
    if sample:
        x_ref, buf_ref, misc_ref, cw_ref, alog_ref, dtb_ref = refs[:6]
    else:
        x_ref, prev_ref, misc_ref, cw_ref, alog_ref, dtb_ref = refs[:6]
    out_refs = refs[6:12]
    cw = cw_ref[...]
    x = x_ref[...]
    nsub = x.shape[0] // C
    rowv = lax.broadcasted_iota(jnp.int32, (C, 1), 0)
    if sample:
        isbuf = (rowv >= ZTOK0 - (GDN_CONV - 1)) & (rowv < ZTOK0)
        xz = jnp.where(isbuf, buf_ref[...], x)
        y = (cw[3:4, :] * xz + cw[2:3, :] * pltpu.roll(xz, 1, 0) + cw[1:2, :] * pltpu.roll(xz, 2, 0)
             + cw[0:1, :] * pltpu.roll(xz, 3, 0))
        valid = jnp.where((rowv >= ZTOK0) & (rowv < ZTOK0 + 4), 1.0, 0.0)
    else:
        prev = jnp.where(pl.program_id(1) == 0, 0.0, prev_ref[...])
        xe = jnp.concatenate([prev, x], axis=0)
        y = (cw[3:4, :] * x + cw[2:3, :] * pltpu.roll(xe, 1, 0)[8:] + cw[1:2, :] * pltpu.roll(xe, 2, 0)[8:]
             + cw[0:1, :] * pltpu.roll(xe, 3, 0)[8:])
        valid = None
    y = jax.nn.silu(y)

    misc = misc_ref[...]
    gfull = -jnp.exp(alog_ref[...]) * jax.nn.softplus(misc + dtb_ref[...])
    bfull = jax.nn.sigmoid(misc)
    if valid is not None:
        gfull = gfull * valid
        bfull = bfull * valid
    for sub in range(nsub):
        rs = slice(sub * C, (sub + 1) * C)
        _gdn_prep_chunk(y[rs], gfull[rs], bfull[rs], valid, out_refs, sub, C)


def _gdn_prep_chunk(y, gfull, bfull, valid, out_refs, sub, C):
    u_ref, w_ref, qg_ref, kd_ref, qk_ref, gl_ref = out_refs
    rs = slice(sub * C, (sub + 1) * C)
    row128 = lax.broadcasted_iota(jnp.int32, (C, HEAD_DIM), 0)
    G = gfull
    s = 1
    while s < C:
        G = G + jnp.where(row128 >= s, pltpu.roll(G, s, 0), 0.0)
        s *= 2
    expG = jnp.exp(G)
    glast = G[C - 1:C, :]
    kdfac = jnp.exp(glast - G)
    if C < HEAD_DIM:
        Gpad = jnp.concatenate([G, jnp.zeros((HEAD_DIM - C, HEAD_DIM), F32)], axis=0)
    else:
        Gpad = G
    GT = Gpad.T
    gl_ref[sub] = jnp.exp(jnp.broadcast_to(GT[A_LANE:A_LANE + GDN_HEADS, C - 1:C], (GDN_HEADS, HEAD_DIM)))

    row = lax.broadcasted_iota(jnp.int32, (C, C), 0)
    col = lax.broadcasted_iota(jnp.int32, (C, C), 1)
    q = _head_stack(y, 0)
    k = _head_stack(y, GDN_WIDTH)
    v = _head_stack(y, 2 * GDN_WIDTH)
    q = q * lax.rsqrt(jnp.sum(q * q, -1, keepdims=True) + NORM_EPS) * (HEAD_DIM ** -0.5)
    k = k * lax.rsqrt(jnp.sum(k * k, -1, keepdims=True) + NORM_EPS)
    if valid is not None:
        q, k, v = q * valid, k * valid, v * valid
    beta = _head_stack(bfull, B_LANE, 1)
    gcol = _head_stack(G, A_LANE, 1)
    egc = _head_stack(expG, A_LANE, 1)
    kdf = _head_stack(kdfac, A_LANE, 1)
    grow = jnp.stack([GT[A_LANE + h:A_LANE + h + 1, :C] for h in range(GDN_HEADS)], axis=0)
    decay = jnp.exp(jnp.where(row >= col, gcol - grow, NEG))
    kb = k * beta
    A = jnp.where(row > col, _bdot_nt(kb, k) * decay, 0.0)
    N = _unit_lower_inverse_minus_eye(A, C)
    vb = v * beta
    kbg = kb * egc
    U = vb + _bdot(N, vb)
    Wm = kbg + _bdot(N, kbg)
    QK = _bdot_nt(q, k) * decay
    QG = q * egc
    KD = k * kdf
    for h in range(GDN_HEADS):
        sl = slice(h * HEAD_DIM, (h + 1) * HEAD_DIM)
        u_ref[rs, sl] = U[h]
        w_ref[rs, sl] = Wm[h].astype(w_ref.dtype)
        qk_ref[rs, h * C:(h + 1) * C] = QK[h].astype(qk_ref.dtype)
        qg_ref[rs, sl] = QG[h].astype(qg_ref.dtype)
        kd_ref[rs, sl] = KD[h].astype(kd_ref.dtype)


def gdn_prep(P, cw, alog_row, dtb_row, *, n_seq, C, bufz=None, nsub=1):
    M = P.shape[0]
    T = M // n_seq
    R = nsub * C
    n = T // R
    sample = bufz is not None
    assert T % R == 0 and C % 8 == 0 and not (sample and nsub != 1)
    qkv_spec = pl.BlockSpec((R, 3 * GDN_WIDTH), lambda b, c: (b * n + c, 0))
    if sample:
        second = pl.BlockSpec((R, 3 * GDN_WIDTH), lambda b, c: (b * n + c, 0))
        second_arr = bufz
    else:
        second = pl.BlockSpec((8, 3 * GDN_WIDTH), lambda b, c: (jnp.maximum((b * n + c) * (R // 8) - 1, 0), 0))
        second_arr = P
    in_specs = [qkv_spec, second,
                pl.BlockSpec((R, HEAD_DIM), lambda b, c: (b * n + c, MISC0 // HEAD_DIM)),
                pl.BlockSpec((GDN_CONV, 3 * GDN_WIDTH), lambda b, c: (0, 0)),
                pl.BlockSpec((1, HEAD_DIM), lambda b, c: (0, 0)),
                pl.BlockSpec((1, HEAD_DIM), lambda b, c: (0, 0))]
    row_spec = pl.BlockSpec((R, GDN_WIDTH), lambda b, c: (b * n + c, 0))
    out_specs = [row_spec, row_spec, row_spec, row_spec,
                 pl.BlockSpec((R, GDN_HEADS * C), lambda b, c: (b * n + c, 0)),
                 pl.BlockSpec((nsub, GDN_HEADS, HEAD_DIM), lambda b, c: (b * n + c, 0, 0))]
    out_shape = [jax.ShapeDtypeStruct((M, GDN_WIDTH), F32),
                 jax.ShapeDtypeStruct((M, GDN_WIDTH), BF16),
                 jax.ShapeDtypeStruct((M, GDN_WIDTH), BF16),
                 jax.ShapeDtypeStruct((M, GDN_WIDTH), BF16),
                 jax.ShapeDtypeStruct((M, GDN_HEADS * C), BF16),
                 jax.ShapeDtypeStruct((n_seq * n * nsub, GDN_HEADS, HEAD_DIM), F32)]
    return pl.pallas_call(
        functools.partial(_gdn_prep_kernel, C=C, sample=sample),
        grid=(n_seq, n),
        in_specs=in_specs,
        out_specs=out_specs,
        out_shape=out_shape,
        compiler_params=_cparams("parallel", "arbitrary"),
        name="gdn_prep_sample" if sample else "gdn_prep_prompt",
    )(P, second_arr, P, cw, alog_row, dtb_row)


def _gdn_scan_kernel(u_ref, w_ref, qg_ref, kd_ref, qk_ref, gl_ref, z_ref, ng_ref, s0_ref, o_ref, sfin_ref, S_ref, *, C):
    c = pl.program_id(1)

    @pl.when(c == 0)
    def _():
        S_ref[...] = s0_ref[...]

    S = S_ref[...]
    Sb = S.astype(BF16)
    v_new = _head_stack(u_ref[...], 0) - _bdot(_head_stack(w_ref[...], 0), Sb)
    vb = v_new.astype(BF16)
    o = _bdot(_head_stack(qg_ref[...], 0), Sb) + _bdot(_head_stack(qk_ref[...], 0, C), vb)
    gl = jnp.stack([gl_ref[h:h + 1, :] for h in range(GDN_HEADS)], axis=0)
    S_ref[...] = S * gl + _bdot_tn(_head_stack(kd_ref[...], 0), vb)
    o = o * lax.rsqrt(jnp.mean(o * o, -1, keepdims=True) + NORM_EPS) * ng_ref[...]
    o = o * jax.nn.silu(_head_stack(z_ref[...], 0))
    for h in range(GDN_HEADS):
        o_ref[:, h * HEAD_DIM:(h + 1) * HEAD_DIM] = o[h].astype(o_ref.dtype)

    @pl.when(c == pl.num_programs(1) - 1)
    def _():
        sfin_ref[...] = S_ref[...]


def gdn_scan(prep, P, norm_g, s0, *, n_seq, C):
    u, w, qg, kd, qk, gl = prep
    M = u.shape[0]
    T = M // n_seq
    n = T // C
    row_spec = pl.BlockSpec((C, GDN_WIDTH), lambda b, c: (b * n + c, 0))
    st_spec = pl.BlockSpec((None, GDN_HEADS, HEAD_DIM, HEAD_DIM), lambda b, c: (b, 0, 0, 0))
    return pl.pallas_call(
        functools.partial(_gdn_scan_kernel, C=C),
        grid=(n_seq, n),
        in_specs=[row_spec, row_spec, row_spec, row_spec,
                  pl.BlockSpec((C, GDN_HEADS * C), lambda b, c: (b * n + c, 0)),
                  pl.BlockSpec((None, GDN_HEADS, HEAD_DIM), lambda b, c: (b * n + c, 0, 0)),
                  pl.BlockSpec((C, GDN_WIDTH), lambda b, c: (b * n + c, Z0 // GDN_WIDTH)),
                  pl.BlockSpec((1, HEAD_DIM), lambda b, c: (0, 0)),
                  st_spec],
        out_specs=[row_spec, st_spec],
        out_shape=[jax.ShapeDtypeStruct((M, GDN_WIDTH), BF16),
                   jax.ShapeDtypeStruct((n_seq, GDN_HEADS, HEAD_DIM, HEAD_DIM), F32)],
        scratch_shapes=[pltpu.VMEM((GDN_HEADS, HEAD_DIM, HEAD_DIM), F32)],
        compiler_params=_cparams("parallel", "arbitrary"),
        name="gdn_scan",
    )(u, w, qg, kd, qk, gl, P, norm_g, s0)


PAGES_PER_STEP = 32
BLOCK_ROWS = NSA_BLOCK * 4
PAGE_ROWS = 2 * BLOCK_ROWS


def _compress_sample_kernel(pt_ref, cache_ref, pe_ref, w1_ref, w2_ref, out_ref, xbuf, sem, *, layer, n_pages, pps):
    b = pl.program_id(0)
    gi = pl.program_id(1)
    ng = pl.num_programs(1)
    step = b * ng + gi
    nsteps = pl.num_programs(0) * ng
    slot = step % 2

    def copies(st, sl):
        bb = st // ng
        g0 = (st % ng) * pps
        out = []
        for jl in range(pps):
            page = pt_ref[bb * n_pages + g0 + jl]
            for half in range(2):
                out.append(pltpu.make_async_copy(
                    cache_ref.at[layer, page, pl.ds(half * BLOCK_ROWS, BLOCK_ROWS), :],
                    xbuf.at[sl, :, 2 * jl + half, :],
                    sem.at[sl]))
        return out

    @pl.when(step == 0)
    def _():
        for cp in copies(step, slot):
            cp.start()

    @pl.when(step + 1 < nsteps)
    def _():
        for cp in copies(step + 1, 1 - slot):
            cp.start()

    for cp in copies(step, slot):
        cp.wait()

    nblk = 2 * pps

    _compress_rows(lambda gc, p: xbuf[slot, p * 4 + gc], pe_ref, w1_ref, w2_ref, out_ref, nblk)


def compress_sample(page_table, cache_rows, pe_t, w1, w2, *, layer):
    Bs, n_pages = page_table.shape
    pps = min(PAGES_PER_STEP, n_pages)
    assert n_pages % pps == 0
    ng = n_pages // pps
    nblk = 2 * pps
    grid_spec = pltpu.PrefetchScalarGridSpec(
        num_scalar_prefetch=1,
        grid=(Bs, ng),
        in_specs=[pl.BlockSpec(memory_space=pl.ANY),
                  pl.BlockSpec(pe_t.shape, lambda b, g, pt: (0, 0, 0)),
                  pl.BlockSpec(w1.shape, lambda b, g, pt: (0, 0, 0)),
                  pl.BlockSpec(w2.shape, lambda b, g, pt: (0, 0, 0))],
        out_specs=pl.BlockSpec((None, None, 2, NSA_KV_HEADS * nblk, HEAD_DIM), lambda b, g, pt: (b, g, 0, 0, 0)),
        scratch_shapes=[pltpu.VMEM((2, BLOCK_ROWS, nblk, HEAD_DIM), F32),
                        pltpu.SemaphoreType.DMA((2,))],
    )
    return pl.pallas_call(
        functools.partial(_compress_sample_kernel, layer=layer, n_pages=n_pages, pps=pps),
        grid_spec=grid_spec,
        out_shape=jax.ShapeDtypeStruct((Bs, ng, 2, NSA_KV_HEADS * nblk, HEAD_DIM), F32),
        compiler_params=_cparams("arbitrary", "arbitrary"),
        name="compress_sample",
    )(page_table.reshape(-1), cache_rows, pe_t, w1, w2)


def _stack_heads(q_ref, g, scale):
    return jnp.concatenate(
        [q_ref[:, (g * NSA_GROUP + r) * HEAD_DIM:(g * NSA_GROUP + r + 1) * HEAD_DIM] * scale for r in range(NSA_GROUP)],
        axis=0).astype(BF16)


def _stacked_row_info(past):
    rows = NSA_GROUP * ZROWS
    ridx = lax.broadcasted_iota(jnp.int32, (rows, 1), 0)
    zrow = ridx % ZROWS
    qpos = past + zrow - ZTOK0
    slope = jnp.zeros((rows, 1), F32)
    return ridx, zrow, qpos, slope


def _slope_col(g):
    ridx = lax.broadcasted_iota(jnp.int32, (NSA_GROUP * ZROWS, 1), 0)
    sl = jnp.zeros((NSA_GROUP * ZROWS, 1), F32)
    for r in range(NSA_GROUP):
        sl = jnp.where(ridx // ZROWS == r, SLOPES[g][r], sl)
    return sl


def _nsa_sample_a_kernel(q_ref, kvw_ref, ck_ref, cv_ref, win_ref, ocmp_ref, owin_ref, sel_ref, wout_ref, *, past, nc, n_blocks, seq):
    scale = HEAD_DIM ** -0.5
    _, zrow, qpos, _ = _stacked_row_info(past)
    wb = win_ref.shape[0] // 4
    lanes_pad = sel_ref.shape[-1]
    sel_lanes = ((n_blocks + HEAD_DIM - 1) // HEAD_DIM) * HEAD_DIM
    top_n = min(NSA_TOP_N, n_blocks)
    for g in range(NSA_KV_HEADS):
        q4 = _stack_heads(q_ref, g, scale)
        slope = _slope_col(g)
        cpos = lax.broadcasted_iota(jnp.int32, (1, nc), 1) * NSA_BLOCK + (NSA_BLOCK - 1)
        vm = cpos <= qpos
        s = jnp.where(vm, _dot_nt(q4, ck_ref[g]) + slope * (cpos - past).astype(F32), NEG)
        e = jnp.exp(s - jnp.max(s, axis=-1, keepdims=True))
        p = jnp.where(vm, e / jnp.sum(e, axis=-1, keepdims=True), 0.0)
        ocmp_ref[g] = _dot(p, cv_ref[g])
        imp = p[0:ZROWS]
        for r in range(1, NSA_GROUP):
            imp = imp + p[r * ZROWS:(r + 1) * ZROWS]
        imp = jnp.concatenate([imp, jnp.zeros((ZROWS, sel_lanes - nc), F32)], axis=1)
        rank = _rank_desc(_block_scores(imp, qpos[0:ZROWS], n_blocks), n_blocks)
        lane = lax.broadcasted_iota(jnp.int32, rank.shape, 1)
        olane = lax.broadcasted_iota(jnp.int32, (ZROWS, lanes_pad), 1)
        out = jnp.zeros((ZROWS, lanes_pad), jnp.int32)
        for t in range(top_n):
            idx = jnp.sum(jnp.where(rank == float(t), lane.astype(F32), 0.0), axis=-1, keepdims=True)
            out = jnp.where(olane == t, idx.astype(jnp.int32), out)
        sel_ref[g] = out
        kold = win_ref[pl.ds(g * 2, wb, stride=4), :]
        vold = win_ref[pl.ds(g * 2 + 1, wb, stride=4), :]
        knew = kvw_ref[:, g * 2 * HEAD_DIM:(g * 2 + 1) * HEAD_DIM]
        vnew = kvw_ref[:, (g * 2 + 1) * HEAD_DIM:(g * 2 + 2) * HEAD_DIM]
        kpos_o = past - wb + lax.broadcasted_iota(jnp.int32, (1, wb), 1)
        kz = lax.broadcasted_iota(jnp.int32, (1, ZROWS), 1)
        kpos_n = past + kz - ZTOK0
        d_o = qpos - kpos_o
        d_n = qpos - kpos_n
        ok_o = (kpos_o >= 0) & (d_o >= 0) & (d_o < NSA_WINDOW)
        ok_n = (kz >= ZTOK0) & (kz < ZTOK0 + seq) & (d_n >= 0) & (d_n < NSA_WINDOW)
        s_o = jnp.where(ok_o, _dot_nt(q4, kold) + slope * (kpos_o - past).astype(F32), NEG)
        s_n = jnp.where(ok_n, _dot_nt(q4, knew) + slope * (kpos_n - past).astype(F32), NEG)
        m = jnp.maximum(jnp.max(s_o, axis=-1, keepdims=True), jnp.max(s_n, axis=-1, keepdims=True))
        p_o = jnp.exp(s_o - m)
        p_n = jnp.exp(s_n - m)
        den = jnp.sum(p_o, axis=-1, keepdims=True) + jnp.sum(p_n, axis=-1, keepdims=True)
        owin_ref[g] = (_dot(p_o, vold) + _dot(p_n, vnew)) / den
    keep = (wb - seq) * 4
    wout_ref[0:keep, :] = win_ref[seq * 4:wb * 4, :]
    ridx = lax.broadcasted_iota(jnp.int32, (seq * 4, 1), 0)
    new_rows = jnp.zeros((seq * 4, HEAD_DIM), F32)
    for t in range(seq):
        for gc in range(4):
            new_rows = jnp.where(ridx == t * 4 + gc,
                                 kvw_ref[ZTOK0 + t:ZTOK0 + t + 1, gc * HEAD_DIM:(gc + 1) * HEAD_DIM], new_rows)
    wout_ref[keep:wb * 4, :] = new_rows


def nsa_sample_a(Ps, ckv, cache_win_rows, *, layer, past, seq):
    Bs = Ps.shape[0] // ZROWS
    nc = ckv.shape[3]
    wrows = cache_win_rows.shape[2]
    n_blocks = -(-(past + seq) // NSA_BLOCK)
    assert 1 <= seq <= 4 and (seq * 4) % 8 == 0 and wrows % 8 == 0
    rows = NSA_GROUP * ZROWS
    big = pl.BlockSpec((None, NSA_KV_HEADS, rows, HEAD_DIM), lambda b: (b, 0, 0, 0))
    return pl.pallas_call(
        functools.partial(_nsa_sample_a_kernel, past=past, nc=nc, n_blocks=n_blocks, seq=seq),
        grid=(Bs,),
        in_specs=[pl.BlockSpec((ZROWS, NSA_WIDTH), lambda b: (b, Q0 // NSA_WIDTH)),
                  pl.BlockSpec((ZROWS, KV_COLS), lambda b: (b, KVW0 // KV_COLS)),
                  pl.BlockSpec((None, None, NSA_KV_HEADS, nc, HEAD_DIM), lambda b: (b, 0, 0, 0, 0)),
                  pl.BlockSpec((None, None, NSA_KV_HEADS, nc, HEAD_DIM), lambda b: (b, 1, 0, 0, 0)),
                  pl.BlockSpec((None, None, wrows, HEAD_DIM), lambda b: (layer, b, 0, 0))],
        out_specs=[big, big,
                   pl.BlockSpec((None, NSA_KV_HEADS, ZROWS, HEAD_DIM), lambda b: (b, 0, 0, 0)),
                   pl.BlockSpec((None, wrows, HEAD_DIM), lambda b: (b, 0, 0))],
        out_shape=[jax.ShapeDtypeStruct((Bs, NSA_KV_HEADS, rows, HEAD_DIM), F32),
                   jax.ShapeDtypeStruct((Bs, NSA_KV_HEADS, rows, HEAD_DIM), F32),
                   jax.ShapeDtypeStruct((Bs, NSA_KV_HEADS, ZROWS, HEAD_DIM), jnp.int32),
                   jax.ShapeDtypeStruct((Bs, wrows, HEAD_DIM), F32)],
        compiler_params=_cparams("parallel"),
        name="nsa_sample_a",
    )(Ps, Ps, ckv, ckv, cache_win_rows)


def _nsa_sample_b_kernel(pt_ref, sel_ref, q_ref, kvs_ref, misc_ref, ocmp_ref, owin_ref, cache_ref, o_ref, kvbuf, sem,
                         *, layer, past, seq, n_past_blocks, n_pages):
    b = pl.program_id(0)
    g = pl.program_id(1)
    ngrp = pl.num_programs(1)
    step = b * ngrp + g
    nsteps = pl.num_programs(0) * ngrp
    slot = step % 2

    def copies(st, sl):
        bb = st // ngrp
        gg = st % ngrp
        out = []
        for t in range(seq):
            for j in range(NSA_TOP_N):
                blk = jnp.minimum(sel_ref[((bb * seq + t) * NSA_KV_HEADS + gg) * NSA_TOP_N + j], n_past_blocks - 1)
                page = pt_ref[bb * n_pages + blk // 2]
                row0 = pl.multiple_of((blk % 2) * BLOCK_ROWS, BLOCK_ROWS)
                out.append(pltpu.make_async_copy(
                    cache_ref.at[layer, page, pl.ds(row0, BLOCK_ROWS), :],
                    kvbuf.at[sl, t * NSA_TOP_N + j],
                    sem.at[sl]))
        return out

    @pl.when(step == 0)
    def _():
        for cp in copies(step, slot):
            cp.start()

    @pl.when(step + 1 < nsteps)
    def _():
        for cp in copies(step + 1, 1 - slot):
            cp.start()

    for cp in copies(step, slot):
        cp.wait()

    scale = HEAD_DIM ** -0.5
    nkeys = NSA_TOP_N * NSA_BLOCK
    _, zrow, qpos, _ = _stacked_row_info(past)
    q4 = jnp.where(g == 0, _stack_heads(q_ref, 0, scale), _stack_heads(q_ref, 1, scale))
    slope = jnp.where(g == 0, _slope_col(0), _slope_col(1))

    kn = jnp.where(g == 0, kvs_ref[:, 0:HEAD_DIM], kvs_ref[:, 2 * HEAD_DIM:3 * HEAD_DIM])
    vn = jnp.where(g == 0, kvs_ref[:, HEAD_DIM:2 * HEAD_DIM], kvs_ref[:, 3 * HEAD_DIM:4 * HEAD_DIM])
    kz = lax.broadcasted_iota(jnp.int32, (1, ZROWS), 1)
    kpos_n = past + kz - ZTOK0
    ok_n = (kz >= ZTOK0) & (kz < ZTOK0 + seq) & (kpos_n <= qpos)
    s_n = jnp.where(ok_n, _dot_nt(q4, kn) + slope * (kpos_n - past).astype(F32), NEG)

    lane = lax.broadcasted_iota(jnp.int32, (1, nkeys), 1)
    slot_of_lane = lane // NSA_BLOCK
    s_g = jnp.full((NSA_GROUP * ZROWS, nkeys), NEG, F32)
    v_tok = []
    for t in range(seq):
        blk_of_lane = jnp.zeros((1, nkeys), jnp.int32)
        for j in range(NSA_TOP_N):
            blk = sel_ref[((b * seq + t) * NSA_KV_HEADS + g) * NSA_TOP_N + j]
            blk_of_lane = jnp.where(slot_of_lane == j, blk, blk_of_lane)
        k_t = jnp.concatenate([kvbuf[slot, t * NSA_TOP_N + j, pl.ds(g * 2, NSA_BLOCK, stride=4), :]
                               for j in range(NSA_TOP_N)], axis=0)
        v_tok.append(jnp.concatenate([kvbuf[slot, t * NSA_TOP_N + j, pl.ds(g * 2 + 1, NSA_BLOCK, stride=4), :]
                                      for j in range(NSA_TOP_N)], axis=0))
        kpos = blk_of_lane * NSA_BLOCK + lane % NSA_BLOCK
        st = _dot_nt(q4, k_t) + slope * (kpos - past).astype(F32)
        mine = (zrow == ZTOK0 + t) & (blk_of_lane < n_past_blocks) & (kpos <= qpos)
        s_g = jnp.where(mine, st, s_g)

    m = jnp.maximum(jnp.max(s_g, axis=-1, keepdims=True), jnp.max(s_n, axis=-1, keepdims=True))
    p_g = jnp.exp(s_g - m)
    p_n = jnp.exp(s_n - m)
    den = jnp.sum(p_g, axis=-1, keepdims=True) + jnp.sum(p_n, axis=-1, keepdims=True)
    pv = _dot(p_n, vn)
    for t in range(seq):
        pv = pv + jnp.where(zrow == ZTOK0 + t, _dot(p_g, v_tok[t]), 0.0)
    o_slc = pv / den

    gates = jax.nn.sigmoid(misc_ref[...])
    o_cmp = ocmp_ref[...]
    o_win = owin_ref[...]
    for r in range(NSA_GROUP):
        rs = slice(r * ZROWS, (r + 1) * ZROWS)
        outs = []
        for gg in range(NSA_KV_HEADS):
            c = gg * NSA_GROUP * 3 + r * 3 + GATE_LANE
            outs.append(gates[:, c:c + 1] * o_cmp[rs] + gates[:, c + 1:c + 2] * o_slc[rs]
                        + gates[:, c + 2:c + 3] * o_win[rs])
        o_ref[:, r * HEAD_DIM:(r + 1) * HEAD_DIM] = jnp.where(g == 0, outs[0], outs[1]).astype(o_ref.dtype)


def nsa_sample_b(page_table, sel, Ps, ocmp, owin, cache_rows, *, layer, past, seq):
    Bs, n_pages = page_table.shape
    rows = NSA_GROUP * ZROWS
    n_past_blocks = past // NSA_BLOCK
    gw = NSA_GROUP * HEAD_DIM

    big = pl.BlockSpec((None, None, rows, HEAD_DIM), lambda b, g, pt, sl: (b, g, 0, 0))
    in_specs = [pl.BlockSpec((ZROWS, NSA_WIDTH), lambda b, g, pt, sl: (b, Q0 // NSA_WIDTH)),
                pl.BlockSpec((ZROWS, KV_COLS), lambda b, g, pt, sl: (b, KVS0 // KV_COLS)),
                pl.BlockSpec((ZROWS, HEAD_DIM), lambda b, g, pt, sl: (b, MISC0 // HEAD_DIM)),
                big, big,
                pl.BlockSpec(memory_space=pl.ANY)]
    grid_spec = pltpu.PrefetchScalarGridSpec(
        num_scalar_prefetch=2,
        grid=(Bs, NSA_KV_HEADS),
        in_specs=in_specs,
        out_specs=pl.BlockSpec((ZROWS, gw), lambda b, g, pt, sl: (b, g)),
        scratch_shapes=[pltpu.VMEM((2, seq * NSA_TOP_N, BLOCK_ROWS, HEAD_DIM), F32),
                        pltpu.SemaphoreType.DMA((2,))],
    )
    return pl.pallas_call(
        functools.partial(_nsa_sample_b_kernel, layer=layer, past=past, seq=seq, n_past_blocks=n_past_blocks,
                          n_pages=n_pages),
        grid_spec=grid_spec,
        out_shape=jax.ShapeDtypeStruct((Bs * ZROWS, NSA_WIDTH), BF16),
        compiler_params=_cparams("arbitrary", "arbitrary"),
        name="nsa_sample_b",
    )(page_table.reshape(-1), sel, Ps, Ps, Ps, ocmp, owin, cache_rows)


def _reorder_w_in(w_in):
    q, kvc, kvs, kvw = w_in[..., 0:1024], w_in[..., 1024:1536], w_in[..., 1536:2048], w_in[..., 2048:2560]
    gates, qkv, z, ab = w_in[..., 2560:2584], w_in[..., 2584:5656], w_in[..., 5656:6680], w_in[..., 6680:6696]
    pad = jnp.zeros(w_in.shape[:-1] + (IN_COLS_PAD - MISC0 - 40,), w_in.dtype)
    return jnp.concatenate([qkv, q, z, kvc, kvs, kvw, gates, ab, pad], axis=-1).astype(BF16)


def _lane_row(vec, lane0):
    return jnp.zeros((1, HEAD_DIM), F32).at[0, lane0:lane0 + vec.shape[0]].set(vec.astype(F32))


def _to_zrows(x, first_row):
    Bs, n, C = x.shape
    z = jnp.zeros((Bs, ZROWS, C), x.dtype).at[:, first_row:first_row + n].set(x)
    return z.reshape(Bs * ZROWS, C)


def kernel(x_prompt, x_sample, cache_cmp_kv, cache_slc_kv, cache_win_kv, state_gdn, state_gdn_conv, state_ffn_conv, page_table, w_in, w_o, cmp_pe, cmp_w1, cmp_w2, gdn_conv_w, gdn_a_log, gdn_dt_bias, gdn_norm_g, ln1_g, ln1_b, ffn_w_up, ffn_conv_w, ffn_w_down, ln2_g, ln2_b):
    B, S, D = x_prompt.shape
    Bs, seq, _ = x_sample.shape
    depth = w_in.shape[0]
    n_pool, page = cache_cmp_kv.shape[1], cache_cmp_kv.shape[2]
    n_pages = page_table.shape[1]
    past = n_pages * page
    wb = cache_win_kv.shape[2]
    d_ff = ffn_w_down.shape[1]
    alpha = (2 * depth) ** 0.25
    assert page == 2 * NSA_BLOCK

    w_in_r = _reorder_w_in(w_in)
    w_o_b = w_o.astype(BF16)
    w_up_b = ffn_w_up.astype(BF16)
    w_down_b = ffn_w_down.astype(BF16)
    w1_b = cmp_w1.astype(BF16)
    w2_b = cmp_w2.astype(BF16)
    pe_t = jnp.swapaxes(cmp_pe, 1, 2)
    cache_cmp_rows = cache_cmp_kv.reshape(depth, n_pool, PAGE_ROWS, HEAD_DIM)
    cache_slc_rows = cache_slc_kv.reshape(depth, n_pool, PAGE_ROWS, HEAD_DIM)
    cache_win_rows = cache_win_kv.reshape(depth, Bs, wb * 4, HEAD_DIM)

    xp = x_prompt.reshape(B * S, D)
    xs = _to_zrows(x_sample, ZTOK0)
    xp_b, xs_b = xp, xs
    tm = 512 if (B * S) % 512 == 0 else B * S
    tm_in = 1024 if (B * S) % 1024 == 0 else tm
    tk_down = 1408 if d_ff % 1408 == 0 else 512
    tf_up = d_ff // 2 if d_ff % (2 * MXU_COLS) == 0 else tk_down
    tq = 256 if S % 256 == 0 else 128
    p_out = [[] for _ in range(6)]
    s_out = [[] for _ in range(6)]
    zero_state = jnp.zeros((B, GDN_HEADS, HEAD_DIM, HEAD_DIM), F32)

    for l in range(depth):
        alog_row = _lane_row(gdn_a_log[l], A_LANE)
        dtb_row = _lane_row(gdn_dt_bias[l], A_LANE)
        ng = gdn_norm_g[l].reshape(1, HEAD_DIM)
        g1, b1 = ln1_g[l].reshape(1, D), ln1_b[l].reshape(1, D)
        g2, b2 = ln2_g[l].reshape(1, D), ln2_b[l].reshape(1, D)

        P = matmul(xp_b, w_in_r, l, tm=tm_in, tn=1024)
        kmat, vt = kv_prep(P, n_seq=B, tm=tm)
        ckv = compress_prompt(P, pe_t[l], w1_b[l], w2_b[l], n_seq=B)
        ckv = ckv.reshape(B, 2, NSA_KV_HEADS, S // NSA_BLOCK, HEAD_DIM).astype(BF16)
        o_nsa = nsa_prompt(P, kmat, vt, ckv[:, 0], jnp.swapaxes(ckv[:, 1], -1, -2), n_seq=B, tq=tq,
                           tk=256 if S % 256 == 0 else tq)
        prep = gdn_prep(P, gdn_conv_w[l], alog_row, dtb_row, n_seq=B, C=GDN_CHUNK,
                        nsub=2 if S % (2 * GDN_CHUNK) == 0 else 1)
        o_gdn, s_fin = gdn_scan(prep, P, ng, zero_state, n_seq=B, C=GDN_CHUNK)
        h, h_b = proj_ln(o_nsa, o_gdn, w_o_b, l, xp, g1, b1, alpha=alpha, tm=256)
        act, utail = ffn_up_prompt(h_b, w_up_b, l, ffn_conv_w[l], n_seq=B, tm=tm, tf=tf_up)
        xp, xp_b = ffn_down_ln_resident(act, w_down_b, l, h, g2, b2, alpha=alpha, tm=256)

        P3 = P.reshape(B, S, IN_COLS_PAD)
        p_out[0].append(P3[:, :, KVC0:KVC0 + KV_COLS].reshape(B, S, NSA_KV_HEADS, 2, HEAD_DIM))
        p_out[1].append(P3[:, :, KVS0:KVS0 + KV_COLS].reshape(B, S, NSA_KV_HEADS, 2, HEAD_DIM))
        wn = min(NSA_WINDOW, S)
        p_out[2].append(P3[:, S - wn:, KVW0:KVW0 + KV_COLS].reshape(B, wn, NSA_KV_HEADS, 2, HEAD_DIM))
        p_out[3].append(s_fin)
        p_out[4].append(P3[:, S - (GDN_CONV - 1):, QKV0:QKV0 + 3 * GDN_WIDTH])
        p_out[5].append(utail[:, 8 - (FFN_CONV - 1):, :])

        Ps = matmul(xs_b, w_in_r, l, tm=Bs * ZROWS, tn=1024)
        ckv_s = compress_sample(page_table, cache_cmp_rows, pe_t[l], w1_b[l], w2_b[l], layer=l)
        ng_grp = ckv_s.shape[1]
        nblk_step = ckv_s.shape[3] // NSA_KV_HEADS
        ckv_s = ckv_s.reshape(Bs, ng_grp, 2, NSA_KV_HEADS, nblk_step, HEAD_DIM)
        ckv_s = jnp.transpose(ckv_s, (0, 2, 3, 1, 4, 5)).reshape(Bs, 2, NSA_KV_HEADS, ng_grp * nblk_step, HEAD_DIM)
        ocmp, owin, sel, win_new = nsa_sample_a(Ps, ckv_s, cache_win_rows, layer=l, past=past, seq=seq)
        sel_flat = jnp.transpose(sel[:, :, ZTOK0:ZTOK0 + seq, :NSA_TOP_N], (0, 2, 1, 3)).reshape(-1)
        o_nsa_s = nsa_sample_b(page_table, sel_flat, Ps, ocmp, owin, cache_slc_rows, layer=l, past=past, seq=seq)
        gbufz = _to_zrows(state_gdn_conv[l], ZTOK0 - (GDN_CONV - 1))
        prep_s = gdn_prep(Ps, gdn_conv_w[l], alog_row, dtb_row, n_seq=Bs, C=ZROWS, bufz=gbufz)
        o_gdn_s, s_fin_s = gdn_scan(prep_s, Ps, ng, state_gdn[l], n_seq=Bs, C=ZROWS)
        hs, hs_b = proj_ln(o_nsa_s, o_gdn_s, w_o_b, l, xs, g1, b1, alpha=alpha, tm=Bs * ZROWS)
        fbufz = _to_zrows(state_ffn_conv[l], ZTOK0 - (FFN_CONV - 1))
        act_s, u_s = ffn_up_sample(hs_b, w_up_b, l, ffn_conv_w[l], fbufz, tf=tk_down)
        xs, xs_b = ffn_down_ln(act_s, w_down_b, l, hs, g2, b2, alpha=alpha, tm=Bs * ZROWS, tk=tk_down)

        Ps3 = Ps.reshape(Bs, ZROWS, IN_COLS_PAD)[:, ZTOK0:ZTOK0 + seq]
        s_out[0].append(Ps3[:, :, KVC0:KVC0 + KV_COLS].reshape(Bs, seq, NSA_KV_HEADS, 2, HEAD_DIM))
        s_out[1].append(Ps3[:, :, KVS0:KVS0 + KV_COLS].reshape(Bs, seq, NSA_KV_HEADS, 2, HEAD_DIM))
        s_out[2].append(win_new.reshape(Bs, wb, NSA_KV_HEADS, 2, HEAD_DIM))
        s_out[3].append(s_fin_s)
        ext_g = jnp.concatenate([state_gdn_conv[l], Ps3[:, :, QKV0:QKV0 + 3 * GDN_WIDTH]], axis=1)
        s_out[4].append(ext_g[:, -(GDN_CONV - 1):])
        u3 = u_s.reshape(Bs, ZROWS, d_ff)[:, ZTOK0:ZTOK0 + seq]
        ext_f = jnp.concatenate([state_ffn_conv[l], u3], axis=1)
        s_out[5].append(ext_f[:, -(FFN_CONV - 1):])

    y_p = xp.reshape(B, S, D)
    y_s = xs.reshape(Bs, ZROWS, D)[:, ZTOK0:ZTOK0 + seq]
    return (y_p, y_s, *[jnp.stack(v, 0) for v in p_out], *[jnp.stack(v, 0) for v in s_out])
```

```python
import functools

import jax
import jax.numpy as jnp
import numpy as np
from jax import lax
from jax.experimental import pallas as pl
from jax.experimental.pallas import tpu as pltpu

F32 = jnp.float32
BF16 = jnp.bfloat16

HEAD_DIM = 128
NSA_HEADS = 8
NSA_KV_HEADS = 2
NSA_GROUP = 4
NSA_WIDTH = 1024
NSA_BLOCK = 64
NSA_TOP_N = 16
NSA_WINDOW = 512
GDN_HEADS = 8
GDN_WIDTH = 1024
GDN_CONV = 4
GDN_CHUNK = 64
FFN_CONV = 3
KV_COLS = 512
LN_EPS = 1e-5
NORM_EPS = 1e-6
NEG = -1e30
FORCE = 1e4
LOG2E = 1.4426950408889634
SLOPES = [[2.0 ** -(g * NSA_GROUP + r + 1) for r in range(NSA_GROUP)] for g in range(NSA_KV_HEADS)]

QKV0, Q0, Z0, KVC0, KVS0, KVW0, MISC0, IN_COLS_PAD = 0, 3072, 4096, 5120, 5632, 6144, 6656, 7168
GATE_LANE, A_LANE, B_LANE = 0, 24, 32

ZROWS = 16
ZTOK0 = 8

VMEM_LIMIT_BYTES = 56 * 1024 * 1024


def _cparams(*sem):
    return pltpu.CompilerParams(dimension_semantics=sem, vmem_limit_bytes=VMEM_LIMIT_BYTES)


def _dot(a, b):
    return jnp.dot(a.astype(BF16), b.astype(BF16), preferred_element_type=F32)


def _dot_nt(a, b):
    return lax.dot_general(a.astype(BF16), b.astype(BF16), (((1,), (1,)), ((), ())), preferred_element_type=F32)


def _dot_tn(a, b):
    return lax.dot_general(a.astype(BF16), b.astype(BF16), (((0,), (0,)), ((), ())), preferred_element_type=F32)


def _layer_norm(t, g, b):
    mu = jnp.mean(t, -1, keepdims=True)
    d = t - mu
    var = jnp.mean(d * d, -1, keepdims=True)
    return d * lax.rsqrt(var + LN_EPS) * g + b


def _mm_kernel(x_ref, w_ref, o_ref):
    o_ref[...] = _dot(x_ref[...], w_ref[...])


def matmul(x, w, layer, *, tm, tn):
    M, K = x.shape
    N = w.shape[2]
    assert M % tm == 0 and N % tn == 0
    return pl.pallas_call(
        _mm_kernel,
        grid=(M // tm, N // tn),
        in_specs=[pl.BlockSpec((tm, K), lambda i, j: (i, 0)),
                  pl.BlockSpec((None, K, tn), lambda i, j: (layer, 0, j))],
        out_specs=pl.BlockSpec((tm, tn), lambda i, j: (i, j)),
        out_shape=jax.ShapeDtypeStruct((M, N), F32),
        compiler_params=_cparams("parallel", "arbitrary"),
        name="in_proj",
    )(x, w)


def _proj_ln_kernel(a1_ref, a2_ref, w_ref, x_ref, g_ref, b_ref, o_ref, ob_ref, *, alpha):
    k1 = a1_ref.shape[1]
    acc = _dot(a1_ref[...], w_ref[:k1, :]) + _dot(a2_ref[...], w_ref[k1:, :])
    o = _layer_norm(alpha * x_ref[...] + acc, g_ref[...], b_ref[...])
    o_ref[...] = o
    ob_ref[...] = o.astype(ob_ref.dtype)


def proj_ln(a1, a2, w, layer, x, g, b, *, alpha, tm):
    M, D = x.shape
    k1, k2 = a1.shape[1], a2.shape[1]
    assert M % tm == 0 and w.shape[1:] == (k1 + k2, D)
    return pl.pallas_call(
        functools.partial(_proj_ln_kernel, alpha=alpha),
        grid=(M // tm,),
        in_specs=[pl.BlockSpec((tm, k1), lambda i: (i, 0)),
                  pl.BlockSpec((tm, k2), lambda i: (i, 0)),
                  pl.BlockSpec((None, k1 + k2, D), lambda i: (layer, 0, 0)),
                  pl.BlockSpec((tm, D), lambda i: (i, 0)),
                  pl.BlockSpec((1, D), lambda i: (0, 0)),
                  pl.BlockSpec((1, D), lambda i: (0, 0))],
        out_specs=[pl.BlockSpec((tm, D), lambda i: (i, 0)), pl.BlockSpec((tm, D), lambda i: (i, 0))],
        out_shape=[jax.ShapeDtypeStruct((M, D), F32), jax.ShapeDtypeStruct((M, D), BF16)],
        compiler_params=_cparams("parallel"),
        name="out_proj_ln",
    )(a1, a2, w, x, g, b)


def _conv3(u, prev8, cw):
    ue = jnp.concatenate([prev8, u], axis=0)
    u1 = pltpu.roll(ue, 1, 0)[8:]
    u2 = pltpu.roll(ue, 2, 0)[8:]
    return cw[2:3, :] * u + cw[1:2, :] * u1 + cw[0:1, :] * u2


MXU_COLS = 256


def _col_chunks(width):
    return [slice(c0, min(c0 + MXU_COLS, width)) for c0 in range(0, width, MXU_COLS)]


def _ffn_up_prompt_kernel(h_ref, wu_ref, wv_ref, cw_ref, act_ref, tail_ref, carry_ref, *, tiles_per_seq):
    i = pl.program_id(1)
    h = h_ref[...]
    tm = h.shape[0]

    @pl.when(i % tiles_per_seq == 0)
    def _():
        carry_ref[...] = jnp.zeros_like(carry_ref)

    for cs in _col_chunks(act_ref.shape[1]):
        u = _dot(h, wu_ref[:, cs])
        v = _dot(h, wv_ref[:, cs])
        uc = _conv3(u, carry_ref[:, cs], cw_ref[:, cs])
        act_ref[:, cs] = (jax.nn.gelu(uc) * v).astype(act_ref.dtype)
        tail = u[tm - 8:, :]
        carry_ref[:, cs] = tail
        tail_ref[:, cs] = tail


def ffn_up_prompt(h, w_up, layer, cw, *, n_seq, tm, tf):
    M, D = h.shape
    F = w_up.shape[2] // 2
    T = M // n_seq
    assert M % tm == 0 and F % tf == 0 and T % tm == 0 and tf % HEAD_DIM == 0
    nf = F // tf
    tps = T // tm
    return pl.pallas_call(
        functools.partial(_ffn_up_prompt_kernel, tiles_per_seq=tps),
        grid=(nf, M // tm),
        in_specs=[pl.BlockSpec((tm, D), lambda f, i: (i, 0)),
                  pl.BlockSpec((None, D, tf), lambda f, i: (layer, 0, f), pipeline_mode=pl.Buffered(1)),
                  pl.BlockSpec((None, D, tf), lambda f, i: (layer, 0, nf + f), pipeline_mode=pl.Buffered(1)),
                  pl.BlockSpec((FFN_CONV, tf), lambda f, i: (0, f))],
        out_specs=[pl.BlockSpec((tm, tf), lambda f, i: (i, f)),
                   pl.BlockSpec((None, 8, tf), lambda f, i: (i // tps, 0, f))],
        out_shape=[jax.ShapeDtypeStruct((M, F), BF16),
                   jax.ShapeDtypeStruct((n_seq, 8, F), F32)],
        scratch_shapes=[pltpu.VMEM((8, tf), F32)],
        compiler_params=_cparams("arbitrary", "arbitrary"),
        name="ffn_up_prompt",
    )(h, w_up, w_up, cw)


def _ffn_up_sample_kernel(h_ref, wu_ref, wv_ref, cw_ref, buf_ref, act_ref, u_ref):
    h = h_ref[...]
    zrow = lax.broadcasted_iota(jnp.int32, (h.shape[0], 1), 0) % ZROWS
    isbuf = (zrow >= ZTOK0 - (FFN_CONV - 1)) & (zrow < ZTOK0)
    for cs in _col_chunks(act_ref.shape[1]):
        u = _dot(h, wu_ref[:, cs])
        v = _dot(h, wv_ref[:, cs])
        uz = jnp.where(isbuf, buf_ref[:, cs], u)
        cw = cw_ref[:, cs]
        uc = cw[2:3, :] * uz + cw[1:2, :] * pltpu.roll(uz, 1, 0) + cw[0:1, :] * pltpu.roll(uz, 2, 0)
        act_ref[:, cs] = (jax.nn.gelu(uc) * v).astype(act_ref.dtype)
        u_ref[:, cs] = u


def ffn_up_sample(h, w_up, layer, cw, bufz, *, tf):
    M, D = h.shape
    F = w_up.shape[2] // 2
    nf = F // tf
    return pl.pallas_call(
        _ffn_up_sample_kernel,
        grid=(nf,),
        in_specs=[pl.BlockSpec((M, D), lambda f: (0, 0)),
                  pl.BlockSpec((None, D, tf), lambda f: (layer, 0, f)),
                  pl.BlockSpec((None, D, tf), lambda f: (layer, 0, nf + f)),
                  pl.BlockSpec((FFN_CONV, tf), lambda f: (0, f)),
                  pl.BlockSpec((M, tf), lambda f: (0, f))],
        out_specs=[pl.BlockSpec((M, tf), lambda f: (0, f)),
                   pl.BlockSpec((M, tf), lambda f: (0, f))],
        out_shape=[jax.ShapeDtypeStruct((M, F), BF16),
                   jax.ShapeDtypeStruct((M, F), F32)],
        compiler_params=_cparams("parallel"),
        name="ffn_up_sample",
    )(h, w_up, w_up, cw, bufz)


def _ffn_down_ln_kernel(a_ref, w_ref, h_ref, g_ref, b_ref, o_ref, ob_ref, acc_ref, *, alpha):
    k = pl.program_id(1)

    @pl.when(k == 0)
    def _():
        acc_ref[...] = jnp.zeros_like(acc_ref)

    acc_ref[...] += _dot(a_ref[...], w_ref[...])

    @pl.when(k == pl.num_programs(1) - 1)
    def _():
        o = _layer_norm(alpha * h_ref[...] + acc_ref[...], g_ref[...], b_ref[...])
        o_ref[...] = o
        ob_ref[...] = o.astype(ob_ref.dtype)


def _ffn_down_ln_resident_kernel(a_ref, w_ref, h_ref, g_ref, b_ref, o_ref, ob_ref, *, alpha):
    acc = jnp.dot(a_ref[...], w_ref[...], preferred_element_type=F32)
    o = _layer_norm(alpha * h_ref[...] + acc, g_ref[...], b_ref[...])
    o_ref[...] = o
    ob_ref[...] = o.astype(ob_ref.dtype)


def ffn_down_ln_resident(a, w, layer, h, g, b, *, alpha, tm):
    M, Fd = a.shape
    D = w.shape[2]
    assert M % tm == 0
    return pl.pallas_call(
        functools.partial(_ffn_down_ln_resident_kernel, alpha=alpha),
        grid=(M // tm,),
        in_specs=[pl.BlockSpec((tm, Fd), lambda i: (i, 0)),
                  pl.BlockSpec((None, Fd, D), lambda i: (layer, 0, 0), pipeline_mode=pl.Buffered(1)),
                  pl.BlockSpec((tm, D), lambda i: (i, 0)),
                  pl.BlockSpec((1, D), lambda i: (0, 0)),
                  pl.BlockSpec((1, D), lambda i: (0, 0))],
        out_specs=[pl.BlockSpec((tm, D), lambda i: (i, 0)), pl.BlockSpec((tm, D), lambda i: (i, 0))],
        out_shape=[jax.ShapeDtypeStruct((M, D), F32), jax.ShapeDtypeStruct((M, D), BF16)],
        compiler_params=_cparams("parallel"),
        name="ffn_down_ln",
    )(a, w, h, g, b)


def ffn_down_ln(a, w, layer, h, g, b, *, alpha, tm, tk):
    M, Fd = a.shape
    D = w.shape[2]
    assert M % tm == 0 and Fd % tk == 0
    return pl.pallas_call(
        functools.partial(_ffn_down_ln_kernel, alpha=alpha),
        grid=(M // tm, Fd // tk),
        in_specs=[pl.BlockSpec((tm, tk), lambda i, k: (i, k)),
                  pl.BlockSpec((None, tk, D), lambda i, k: (layer, k, 0)),
                  pl.BlockSpec((tm, D), lambda i, k: (i, 0)),
                  pl.BlockSpec((1, D), lambda i, k: (0, 0)),
                  pl.BlockSpec((1, D), lambda i, k: (0, 0))],
        out_specs=[pl.BlockSpec((tm, D), lambda i, k: (i, 0)), pl.BlockSpec((tm, D), lambda i, k: (i, 0))],
        out_shape=[jax.ShapeDtypeStruct((M, D), F32), jax.ShapeDtypeStruct((M, D), BF16)],
        scratch_shapes=[pltpu.VMEM((tm, D), F32)],
        compiler_params=_cparams("parallel", "arbitrary"),
        name="ffn_down_ln",
    )(a, w, h, g, b)


def _compress_rows(read_rows, pe_ref, w1_ref, w2_ref, out_ref, nblk):
    del nblk
    for c in range(2):
        parts = []
        for p in range(NSA_BLOCK):
            pe_row = pe_ref[c, p:p + 1, :]
            rows = [read_rows(g * 2 + c, p) + pe_row for g in range(NSA_KV_HEADS)]
            parts.append(jnp.concatenate(rows, axis=0).astype(BF16))
        hid = jnp.dot(jnp.concatenate(parts, axis=1), w1_ref[c], preferred_element_type=F32)
        out_ref[c] = _dot(jax.nn.silu(hid), w2_ref[c])


def _compress_prompt_kernel(x0, x1, x2, x3, pe_ref, w1_ref, w2_ref, out_ref, *, nblk):
    xs = [x0, x1, x2, x3]
    _compress_rows(lambda gc, p: xs[gc][:, p, :], pe_ref, w1_ref, w2_ref, out_ref, nblk)


def compress_prompt(P, pe_t, w1, w2, *, n_seq):
    M = P.shape[0]
    T = M // n_seq
    nblk = T // NSA_BLOCK
    P3 = P.reshape(n_seq * nblk, NSA_BLOCK, IN_COLS_PAD)
    cb0 = KVC0 // HEAD_DIM
    in_specs = [pl.BlockSpec((nblk, NSA_BLOCK, HEAD_DIM), (lambda b, gc=gc: (b, 0, cb0 + gc))) for gc in range(4)]
    in_specs += [pl.BlockSpec(pe_t.shape, lambda b: (0, 0, 0)),
                 pl.BlockSpec(w1.shape, lambda b: (0, 0, 0)),
                 pl.BlockSpec(w2.shape, lambda b: (0, 0, 0))]
    return pl.pallas_call(
        functools.partial(_compress_prompt_kernel, nblk=nblk),
        grid=(n_seq,),
        in_specs=in_specs,
        out_specs=pl.BlockSpec((None, 2, NSA_KV_HEADS * nblk, HEAD_DIM), lambda b: (b, 0, 0, 0)),
        out_shape=jax.ShapeDtypeStruct((n_seq, 2, NSA_KV_HEADS * nblk, HEAD_DIM), F32),
        compiler_params=_cparams("parallel"),
        name="compress_prompt",
    )(P3, P3, P3, P3, pe_t, w1, w2)


def _rank_desc(score, n_real):
    lane = lax.broadcasted_iota(jnp.int32, score.shape, 1)
    cnt = jnp.zeros(score.shape, F32)
    for i in range(n_real):
        col = score[:, i:i + 1]
        ge = jnp.where(col >= score, 1.0, 0.0)
        gt = jnp.where(col > score, 1.0, 0.0)
        cnt = cnt + jnp.where(lane > i, ge, gt)
    return cnt


def _block_scores(imp, qpos, n_blocks):
    bidx = lax.broadcasted_iota(jnp.int32, imp.shape, 1)
    cur = qpos // NSA_BLOCK
    sc = jnp.where(bidx == 0, FORCE, jnp.where(bidx == cur, FORCE, jnp.where(bidx == cur - 1, FORCE, imp)))
    sc = jnp.where(bidx * NSA_BLOCK <= qpos, sc, NEG)
    return jnp.where(bidx < n_blocks, sc, -jnp.inf)


def _rank_desc_t(score, n_real):
    L = score.shape[0]
    assert L % 8 == 0
    groups = [score[8 * a:8 * a + 8, :] for a in range(L // 8)]
    sub = lax.broadcasted_iota(jnp.int32, groups[0].shape, 0)
    cnts = [jnp.zeros(groups[0].shape, F32) for _ in groups]
    for i in range(n_real):
        row = score[i:i + 1, :]
        for a, x in enumerate(groups):
            if 8 * a > i:
                c = jnp.where(row >= x, 1.0, 0.0)
            elif 8 * a + 7 < i:
                c = jnp.where(row > x, 1.0, 0.0)
            else:
                c = jnp.where(sub + 8 * a > i, jnp.where(row >= x, 1.0, 0.0), jnp.where(row > x, 1.0, 0.0))
            cnts[a] = cnts[a] + c
    return jnp.concatenate(cnts, axis=0)


def _block_scores_t(imp, qpos, n_blocks):
    bidx = lax.broadcasted_iota(jnp.int32, imp.shape, 0)
    cur = qpos // NSA_BLOCK
    sc = jnp.where(bidx == 0, FORCE, jnp.where(bidx == cur, FORCE, jnp.where(bidx == cur - 1, FORCE, imp)))
    sc = jnp.where(bidx * NSA_BLOCK <= qpos, sc, NEG)
    return jnp.where(bidx < n_blocks, sc, -jnp.inf)


def _nsa_prompt_kernel(q_ref, misc_ref, k_ref, vts_ref, vtw_ref, ck_ref, cvt_ref, o_ref, m_ref, l_ref, acc_ref, flag_ref, *, tq, tk, nblk):
    i = pl.program_id(1)
    t0 = i * tq
    bpt = tk // NSA_BLOCK
    wspan = min(NSA_WINDOW + tq + HEAD_DIM, k_ref.shape[0])
    scale = HEAD_DIM ** -0.5 * LOG2E
    top_n = min(NSA_TOP_N, nblk)
    qpos_row = t0 + lax.broadcasted_iota(jnp.int32, (1, tq), 1)
    sub_k = lax.broadcasted_iota(jnp.int32, (tk, tq), 0)
    d_kq = sub_k - lax.broadcasted_iota(jnp.int32, (tk, tq), 1)
    sub_kf = sub_k.astype(F32)
    wsub_k = lax.broadcasted_iota(jnp.int32, (wspan, tq), 0)
    wsub_kf = wsub_k.astype(F32)
    wwin_code = -2 * (wsub_k - lax.broadcasted_iota(jnp.int32, (wspan, tq), 1)) - (NSA_WINDOW - 1)
    e_rel = (lax.broadcasted_iota(jnp.int32, (tk, nblk), 1)
             - lax.broadcasted_iota(jnp.int32, (tk, nblk), 0) // NSA_BLOCK)
    gates_t = jax.nn.sigmoid(misc_ref[...]).T

    def reset():
        m_ref[...] = jnp.full(m_ref.shape, NEG, F32)
        l_ref[...] = jnp.zeros(l_ref.shape, F32)
        acc_ref[...] = jnp.zeros(acc_ref.shape, F32)

    def online_update(s4, vt):
        m_old = m_ref[...]
        m_new = jnp.maximum(m_old, jnp.max(s4, axis=0, keepdims=True))
        p = jnp.exp2(s4 - m_new)
        alpha = jnp.exp2(m_old - m_new)
        l_ref[...] = alpha * l_ref[...] + jnp.sum(p, axis=0, keepdims=True)
        acc_ref[...] = alpha * acc_ref[...] + jnp.dot(vt, p.astype(BF16), preferred_element_type=F32)
        m_ref[...] = m_new

    def masked_scores(s, valid, base, g):
        return jnp.concatenate(
            [jnp.where(valid, s[:, r * tq:(r + 1) * tq] + (SLOPES[g][r] * LOG2E) * base, NEG)
             for r in range(NSA_GROUP)], axis=1)

    for g in range(NSA_KV_HEADS):
        q4t = jnp.concatenate(
            [(q_ref[:, (g * NSA_GROUP + r) * HEAD_DIM:(g * NSA_GROUP + r + 1) * HEAD_DIM] * scale).T
             for r in range(NSA_GROUP)], axis=1).astype(BF16)
        sc = jnp.dot(ck_ref[g], q4t, preferred_element_type=F32)
        cpos = lax.broadcasted_iota(jnp.int32, (nblk, tq), 0) * NSA_BLOCK + (NSA_BLOCK - 1)
        vm = cpos <= qpos_row
        cposrel = (cpos - t0).astype(F32)
        imp = jnp.zeros((nblk, tq), F32)
        parts = []
        for r in range(NSA_GROUP):
            s = jnp.where(vm, sc[:, r * tq:(r + 1) * tq] + (SLOPES[g][r] * LOG2E) * cposrel, NEG)
            e = jnp.exp2(s - jnp.max(s, axis=0, keepdims=True))
            p = jnp.where(vm, e / jnp.sum(e, axis=0, keepdims=True), 0.0)
            imp = imp + p
            parts.append(p)
        ocmp_t = jnp.dot(cvt_ref[g], jnp.concatenate(parts, axis=1).astype(BF16), preferred_element_type=F32)
        rank = _rank_desc_t(_block_scores_t(imp, qpos_row, nblk), nblk)
        sel_f = jnp.where(rank < top_n, 1.0, 0.0)
        sel_t = sel_f.astype(BF16)
        blk_any = jnp.max(sel_f, axis=1, keepdims=True)
        for jt in range(nblk // bpt):
            flag_ref[jt] = (jnp.max(blk_any[jt * bpt:(jt + 1) * bpt, :]) > 0.5).astype(jnp.int32)

        reset()

        def slc_body(j, carry):
            k0 = pl.multiple_of(j * tk, tk)

            @pl.when(flag_ref[j] > 0)
            def _():
                k = k_ref[pl.ds(k0, tk), g * HEAD_DIM:(g + 1) * HEAD_DIM]
                vt = vts_ref[g, :, pl.ds(k0, tk)]
                off = k0 - t0
                expand = jnp.where(e_rel == k0 // NSA_BLOCK, 1.0, 0.0).astype(BF16)
                chosen = jnp.dot(expand, sel_t, preferred_element_type=F32)
                valid = jnp.where(d_kq <= -off, chosen, 0.0) > 0.5
                s = jnp.dot(k, q4t, preferred_element_type=F32)
                online_update(masked_scores(s, valid, sub_kf + off.astype(F32), g), vt)

            return carry

        lax.fori_loop(0, (t0 + tq + tk - 1) // tk, slc_body, 0)
        oslc_t = acc_ref[...] / l_ref[...]

        w0 = pl.multiple_of(jnp.maximum(t0 + tq - wspan, 0), HEAD_DIM)
        kw = k_ref[pl.ds(w0, wspan), (NSA_KV_HEADS + g) * HEAD_DIM:(NSA_KV_HEADS + g + 1) * HEAD_DIM]
        woff = w0 - t0
        wvalid = jnp.abs(wwin_code - 2 * woff) <= (NSA_WINDOW - 1)
        wbase = wsub_kf + woff.astype(F32)
        sw = jnp.dot(kw, q4t, preferred_element_type=F32)
        sw = jnp.concatenate(
            [jnp.where(wvalid, sw[:, r * tq:(r + 1) * tq] + (SLOPES[g][r] * LOG2E) * wbase, NEG)
             for r in range(NSA_GROUP)], axis=1)
        pw = jnp.exp2(sw - jnp.max(sw, axis=0, keepdims=True))
        owin_t = (jnp.dot(vtw_ref[g, :, pl.ds(w0, wspan)], pw.astype(BF16), preferred_element_type=F32)
                  / jnp.sum(pw, axis=0, keepdims=True))

        for r in range(NSA_GROUP):
            c = g * NSA_GROUP * 3 + r * 3 + GATE_LANE
            rs = slice(r * tq, (r + 1) * tq)
            o_t = (gates_t[c:c + 1, :] * ocmp_t[:, rs] + gates_t[c + 1:c + 2, :] * oslc_t[:, rs]
                   + gates_t[c + 2:c + 3, :] * owin_t[:, rs])
            h = g * NSA_GROUP + r
            o_ref[:, h * HEAD_DIM:(h + 1) * HEAD_DIM] = o_t.T.astype(o_ref.dtype)


def _kv_prep_kernel(kvs_ref, kvw_ref, kmat_ref, vt_ref):
    for idx, (src, g) in enumerate([(kvs_ref, 0), (kvs_ref, 1), (kvw_ref, 0), (kvw_ref, 1)]):
        kmat_ref[:, idx * HEAD_DIM:(idx + 1) * HEAD_DIM] = src[:, 2 * g * HEAD_DIM:(2 * g + 1) * HEAD_DIM].astype(BF16)
        vt_ref[idx] = src[:, (2 * g + 1) * HEAD_DIM:(2 * g + 2) * HEAD_DIM].T.astype(BF16)


def kv_prep(P, *, n_seq, tm):
    M = P.shape[0]
    T = M // n_seq
    assert T % tm == 0
    tps = T // tm
    return pl.pallas_call(
        _kv_prep_kernel,
        grid=(M // tm,),
        in_specs=[pl.BlockSpec((tm, KV_COLS), lambda i: (i, KVS0 // KV_COLS)),
                  pl.BlockSpec((tm, KV_COLS), lambda i: (i, KVW0 // KV_COLS))],
        out_specs=[pl.BlockSpec((tm, 4 * HEAD_DIM), lambda i: (i, 0)),
                   pl.BlockSpec((None, 4, HEAD_DIM, tm), lambda i: (i // tps, 0, 0, i % tps))],
        out_shape=[jax.ShapeDtypeStruct((M, 4 * HEAD_DIM), BF16),
                   jax.ShapeDtypeStruct((n_seq, 4, HEAD_DIM, T), BF16)],
        compiler_params=_cparams("parallel"),
        name="kv_prep",
    )(P, P)


def nsa_prompt(P, kmat, vt, ck, cvt, *, n_seq, tq, tk):
    M = P.shape[0]
    T = M // n_seq
    nblk = T // NSA_BLOCK
    nq = T // tq
    W = NSA_GROUP * tq
    assert T % tq == 0 and T % tk == 0 and tk % NSA_BLOCK == 0 and tk % HEAD_DIM == 0
    return pl.pallas_call(
        functools.partial(_nsa_prompt_kernel, tq=tq, tk=tk, nblk=nblk),
        grid=(n_seq, nq),
        in_specs=[pl.BlockSpec((tq, NSA_WIDTH), lambda b, i: (b * nq + i, Q0 // NSA_WIDTH)),
                  pl.BlockSpec((tq, HEAD_DIM), lambda b, i: (b * nq + i, MISC0 // HEAD_DIM)),
                  pl.BlockSpec((T, 2 * NSA_KV_HEADS * HEAD_DIM), lambda b, i: (b, 0)),
                  pl.BlockSpec((None, NSA_KV_HEADS, HEAD_DIM, T), lambda b, i: (b, 0, 0, 0)),
                  pl.BlockSpec((None, NSA_KV_HEADS, HEAD_DIM, T), lambda b, i: (b, 1, 0, 0)),
                  pl.BlockSpec((None, NSA_KV_HEADS, nblk, HEAD_DIM), lambda b, i: (b, 0, 0, 0)),
                  pl.BlockSpec((None, NSA_KV_HEADS, HEAD_DIM, nblk), lambda b, i: (b, 0, 0, 0))],
        out_specs=pl.BlockSpec((tq, NSA_WIDTH), lambda b, i: (b * nq + i, 0)),
        out_shape=jax.ShapeDtypeStruct((M, NSA_WIDTH), BF16),
        scratch_shapes=[pltpu.VMEM((1, W), F32),
                        pltpu.VMEM((1, W), F32),
                        pltpu.VMEM((HEAD_DIM, W), F32),
                        pltpu.SMEM((T // tk,), jnp.int32)],
        compiler_params=_cparams("parallel", "arbitrary"),
        name="nsa_prompt",
    )(P, P, kmat, vt, vt, ck, cvt)


def _bdot(a, b):
    return lax.dot_general(a.astype(BF16), b.astype(BF16), (((2,), (1,)), ((0,), (0,))), preferred_element_type=F32)


def _bdot_nt(a, b):
    return lax.dot_general(a.astype(BF16), b.astype(BF16), (((2,), (2,)), ((0,), (0,))), preferred_element_type=F32)


def _bdot_tn(a, b):
    return lax.dot_general(a.astype(BF16), b.astype(BF16), (((1,), (1,)), ((0,), (0,))), preferred_element_type=F32)


def _head_stack(a, off, width=HEAD_DIM):
    return jnp.stack([a[:, off + h * width:off + (h + 1) * width] for h in range(GDN_HEADS)], axis=0)


def _unit_lower_inverse_minus_eye(A, C):
    row = lax.broadcasted_iota(jnp.int32, (C, C), 0)
    col = lax.broadcasted_iota(jnp.int32, (C, C), 1)
    A8 = jnp.where((row // 8) == (col // 8), A, 0.0)
    B2 = _bdot(A8, A8)
    B4 = _bdot(B2, B2)
    P1 = B2 - A8 - _bdot(A8, B2)
    N = P1 + B4 + _bdot(P1, B4)
    size = 16
    while size <= C:
        AL = jnp.where((row // size) == (col // size), jnp.where((row // (size // 2)) == (col // (size // 2)), 0.0, A), 0.0)
        X = AL + _bdot(N, AL)
        N = N - X - _bdot(X, N)
        size *= 2
    return N


def _gdn_prep_kernel(*refs, C, sample):
    if sample:
        x_ref, buf_ref, misc_ref, cw_ref, alog_ref, dtb_ref = refs[:6]
    else:
        x_ref, prev_ref, misc_ref, cw_ref, alog_ref, dtb_ref = refs[:6]
    out_refs = refs[6:12]
    cw = cw_ref[...]
    x = x_ref[...]
    nsub = x.shape[0] // C
    rowv = lax.broadcasted_iota(jnp.int32, (C, 1), 0)
    if sample:
        isbuf = (rowv >= ZTOK0 - (GDN_CONV - 1)) & (rowv < ZTOK0)
        xz = jnp.where(isbuf, buf_ref[...], x)
        y = (cw[3:4, :] * xz + cw[2:3, :] * pltpu.roll(xz, 1, 0) + cw[1:2, :] * pltpu.roll(xz, 2, 0)
             + cw[0:1, :] * pltpu.roll(xz, 3, 0))
        valid = jnp.where((rowv >= ZTOK0) & (rowv < ZTOK0 + 4), 1.0, 0.0)
    else:
        prev = jnp.where(pl.program_id(1) == 0, 0.0, prev_ref[...])
        xe = jnp.concatenate([prev, x], axis=0)
        y = (cw[3:4, :] * x + cw[2:3, :] * pltpu.roll(xe, 1, 0)[8:] + cw[1:2, :] * pltpu.roll(xe, 2, 0)[8:]
             + cw[0:1, :] * pltpu.roll(xe, 3, 0)[8:])
        valid = None
    y = jax.nn.silu(y)

    misc = misc_ref[...]
    gfull = -jnp.exp(alog_ref[...]) * jax.nn.softplus(misc + dtb_ref[...])
    bfull = jax.nn.sigmoid(misc)
    if valid is not None:
        gfull = gfull * valid
        bfull = bfull * valid
    for sub in range(nsub):
        rs = slice(sub * C, (sub + 1) * C)
        _gdn_prep_chunk(y[rs], gfull[rs], bfull[rs], valid, out_refs, sub, C)


def _gdn_prep_chunk(y, gfull, bfull, valid, out_refs, sub, C):
    u_ref, w_ref, qg_ref, kd_ref, qk_ref, gl_ref = out_refs
    rs = slice(sub * C, (sub + 1) * C)
    row128 = lax.broadcasted_iota(jnp.int32, (C, HEAD_DIM), 0)
    G = gfull
    s = 1
    while s < C:
        G = G + jnp.where(row128 >= s, pltpu.roll(G, s, 0), 0.0)
        s *= 2
    expG = jnp.exp(G)
    glast = G[C - 1:C, :]
    kdfac = jnp.exp(glast - G)
    if C < HEAD_DIM:
        Gpad = jnp.concatenate([G, jnp.zeros((HEAD_DIM - C, HEAD_DIM), F32)], axis=0)
    else:
        Gpad = G
    GT = Gpad.T
    gl_ref[sub] = jnp.exp(jnp.broadcast_to(GT[A_LANE:A_LANE + GDN_HEADS, C - 1:C], (GDN_HEADS, HEAD_DIM)))

    row = lax.broadcasted_iota(jnp.int32, (C, C), 0)
    col = lax.broadcasted_iota(jnp.int32, (C, C), 1)
    q = _head_stack(y, 0)
    k = _head_stack(y, GDN_WIDTH)
    v = _head_stack(y, 2 * GDN_WIDTH)
    q = q * lax.rsqrt(jnp.sum(q * q, -1, keepdims=True) + NORM_EPS) * (HEAD_DIM ** -0.5)
    k = k * lax.rsqrt(jnp.sum(k * k, -1, keepdims=True) + NORM_EPS)
    if valid is not None:
        q, k, v = q * valid, k * valid, v * valid
    beta = _head_stack(bfull, B_LANE, 1)
    gcol = _head_stack(G, A_LANE, 1)
    egc = _head_stack(expG, A_LANE, 1)
    kdf = _head_stack(kdfac, A_LANE, 1)
    grow = jnp.stack([GT[A_LANE + h:A_LANE + h + 1, :C] for h in range(GDN_HEADS)], axis=0)
    decay = jnp.exp(jnp.where(row >= col, gcol - grow, NEG))
    kb = k * beta
    A = jnp.where(row > col, _bdot_nt(kb, k) * decay, 0.0)
    N = _unit_lower_inverse_minus_eye(A, C)
    vb = v * beta
    kbg = kb * egc
    U = vb + _bdot(N, vb)
    Wm = kbg + _bdot(N, kbg)
    QK = _bdot_nt(q, k) * decay
    QG = q * egc
    KD = k * kdf
    for h in range(GDN_HEADS):
        sl = slice(h * HEAD_DIM, (h + 1) * HEAD_DIM)
        u_ref[rs, sl] = U[h]
        w_ref[rs, sl] = Wm[h].astype(w_ref.dtype)
        qk_ref[rs, h * C:(h + 1) * C] = QK[h].astype(qk_ref.dtype)
        qg_ref[rs, sl] = QG[h].astype(qg_ref.dtype)
        kd_ref[rs, sl] = KD[h].astype(kd_ref.dtype)


def gdn_prep(P, cw, alog_row, dtb_row, *, n_seq, C, bufz=None, nsub=1):
    M = P.shape[0]
    T = M // n_seq
    R = nsub * C
    n = T // R
    sample = bufz is not None
    assert T % R == 0 and C % 8 == 0 and not (sample and nsub != 1)
    qkv_spec = pl.BlockSpec((R, 3 * GDN_WIDTH), lambda b, c: (b * n + c, 0))
    if sample:
        second = pl.BlockSpec((R, 3 * GDN_WIDTH), lambda b, c: (b * n + c, 0))
        second_arr = bufz
    else:
        second = pl.BlockSpec((8, 3 * GDN_WIDTH), lambda b, c: (jnp.maximum((b * n + c) * (R // 8) - 1, 0), 0))
        second_arr = P
    in_specs = [qkv_spec, second,
                pl.BlockSpec((R, HEAD_DIM), lambda b, c: (b * n + c, MISC0 // HEAD_DIM)),
                pl.BlockSpec((GDN_CONV, 3 * GDN_WIDTH), lambda b, c: (0, 0)),
                pl.BlockSpec((1, HEAD_DIM), lambda b, c: (0, 0)),
                pl.BlockSpec((1, HEAD_DIM), lambda b, c: (0, 0))]
    row_spec = pl.BlockSpec((R, GDN_WIDTH), lambda b, c: (b * n + c, 0))
    out_specs = [row_spec, row_spec, row_spec, row_spec,
                 pl.BlockSpec((R, GDN_HEADS * C), lambda b, c: (b * n + c, 0)),
                 pl.BlockSpec((nsub, GDN_HEADS, HEAD_DIM), lambda b, c: (b * n + c, 0, 0))]
    out_shape = [jax.ShapeDtypeStruct((M, GDN_WIDTH), F32),
                 jax.ShapeDtypeStruct((M, GDN_WIDTH), BF16),
                 jax.ShapeDtypeStruct((M, GDN_WIDTH), BF16),
                 jax.ShapeDtypeStruct((M, GDN_WIDTH), BF16),
                 jax.ShapeDtypeStruct((M, GDN_HEADS * C), BF16),
                 jax.ShapeDtypeStruct((n_seq * n * nsub, GDN_HEADS, HEAD_DIM), F32)]
    return pl.pallas_call(
        functools.partial(_gdn_prep_kernel, C=C, sample=sample),
        grid=(n_seq, n),
        in_specs=in_specs,
        out_specs=out_specs,
        out_shape=out_shape,
        compiler_params=_cparams("parallel", "arbitrary"),
        name="gdn_prep_sample" if sample else "gdn_prep_prompt",
    )(P, second_arr, P, cw, alog_row, dtb_row)


def _gdn_scan_kernel(u_ref, w_ref, qg_ref, kd_ref, qk_ref, gl_ref, z_ref, ng_ref, s0_ref, o_ref, sfin_ref, S_ref, *, C):
    c = pl.program_id(1)

    @pl.when(c == 0)
    def _():
        S_ref[...] = s0_ref[...]

    S = S_ref[...]
    Sb = S.astype(BF16)
    v_new = _head_stack(u_ref[...], 0) - _bdot(_head_stack(w_ref[...], 0), Sb)
    vb = v_new.astype(BF16)
    o = _bdot(_head_stack(qg_ref[...], 0), Sb) + _bdot(_head_stack(qk_ref[...], 0, C), vb)
    gl = jnp.stack([gl_ref[h:h + 1, :] for h in range(GDN_HEADS)], axis=0)
    S_ref[...] = S * gl + _bdot_tn(_head_stack(kd_ref[...], 0), vb)
    o = o * lax.rsqrt(jnp.mean(o * o, -1, keepdims=True) + NORM_EPS) * ng_ref[...]
    o = o * jax.nn.silu(_head_stack(z_ref[...], 0))
    for h in range(GDN_HEADS):
        o_ref[:, h * HEAD_DIM:(h + 1) * HEAD_DIM] = o[h].astype(o_ref.dtype)

    @pl.when(c == pl.num_programs(1) - 1)
    def _():
        sfin_ref[...] = S_ref[...]


def gdn_scan(prep, P, norm_g, s0, *, n_seq, C):
    u, w, qg, kd, qk, gl = prep
    M = u.shape[0]
    T = M // n_seq
    n = T // C
    row_spec = pl.BlockSpec((C, GDN_WIDTH), lambda b, c: (b * n + c, 0))
    st_spec = pl.BlockSpec((None, GDN_HEADS, HEAD_DIM, HEAD_DIM), lambda b, c: (b, 0, 0, 0))
    return pl.pallas_call(
        functools.partial(_gdn_scan_kernel, C=C),
        grid=(n_seq, n),
        in_specs=[row_spec, row_spec, row_spec, row_spec,
                  pl.BlockSpec((C, GDN_HEADS * C), lambda b, c: (b * n + c, 0)),
                  pl.BlockSpec((None, GDN_HEADS, HEAD_DIM), lambda b, c: (b * n + c, 0, 0)),
                  pl.BlockSpec((C, GDN_WIDTH), lambda b, c: (b * n + c, Z0 // GDN_WIDTH)),
                  pl.BlockSpec((1, HEAD_DIM), lambda b, c: (0, 0)),
                  st_spec],
        out_specs=[row_spec, st_spec],
        out_shape=[jax.ShapeDtypeStruct((M, GDN_WIDTH), BF16),
                   jax.ShapeDtypeStruct((n_seq, GDN_HEADS, HEAD_DIM, HEAD_DIM), F32)],
        scratch_shapes=[pltpu.VMEM((GDN_HEADS, HEAD_DIM, HEAD_DIM), F32)],
        compiler_params=_cparams("parallel", "arbitrary"),
        name="gdn_scan",
    )(u, w, qg, kd, qk, gl, P, norm_g, s0)


PAGES_PER_STEP = 32
BLOCK_ROWS = NSA_BLOCK * 4
PAGE_ROWS = 2 * BLOCK_ROWS


def _compress_sample_kernel(pt_ref, cache_ref, pe_ref, w1_ref, w2_ref, out_ref, xbuf, sem, *, layer, n_pages, pps):
    b = pl.program_id(0)
    gi = pl.program_id(1)
    ng = pl.num_programs(1)
    step = b * ng + gi
    nsteps = pl.num_programs(0) * ng
    slot = step % 2

    def copies(st, sl):
        bb = st // ng
        g0 = (st % ng) * pps
        out = []
        for jl in range(pps):
            page = pt_ref[bb * n_pages + g0 + jl]
            for half in range(2):
                out.append(pltpu.make_async_copy(
                    cache_ref.at[layer, page, pl.ds(half * BLOCK_ROWS, BLOCK_ROWS), :],
                    xbuf.at[sl, :, 2 * jl + half, :],
                    sem.at[sl]))
        return out

    @pl.when(step == 0)
    def _():
        for cp in copies(step, slot):
            cp.start()

    @pl.when(step + 1 < nsteps)
    def _():
        for cp in copies(step + 1, 1 - slot):
            cp.start()

    for cp in copies(step, slot):
        cp.wait()

    nblk = 2 * pps

    _compress_rows(lambda gc, p: xbuf[slot, p * 4 + gc], pe_ref, w1_ref, w2_ref, out_ref, nblk)


def compress_sample(page_table, cache_rows, pe_t, w1, w2, *, layer):
    Bs, n_pages = page_table.shape
    pps = min(PAGES_PER_STEP, n_pages)
    assert n_pages % pps == 0
    ng = n_pages // pps
    nblk = 2 * pps
    grid_spec = pltpu.PrefetchScalarGridSpec(
        num_scalar_prefetch=1,
        grid=(Bs, ng),
        in_specs=[pl.BlockSpec(memory_space=pl.ANY),
                  pl.BlockSpec(pe_t.shape, lambda b, g, pt: (0, 0, 0)),
                  pl.BlockSpec(w1.shape, lambda b, g, pt: (0, 0, 0)),
                  pl.BlockSpec(w2.shape, lambda b, g, pt: (0, 0, 0))],
        out_specs=pl.BlockSpec((None, None, 2, NSA_KV_HEADS * nblk, HEAD_DIM), lambda b, g, pt: (b, g, 0, 0, 0)),
        scratch_shapes=[pltpu.VMEM((2, BLOCK_ROWS, nblk, HEAD_DIM), F32),
                        pltpu.SemaphoreType.DMA((2,))],
    )
    return pl.pallas_call(
        functools.partial(_compress_sample_kernel, layer=layer, n_pages=n_pages, pps=pps),
        grid_spec=grid_spec,
        out_shape=jax.ShapeDtypeStruct((Bs, ng, 2, NSA_KV_HEADS * nblk, HEAD_DIM), F32),
        compiler_params=_cparams("arbitrary", "arbitrary"),
        name="compress_sample",
    )(page_table.reshape(-1), cache_rows, pe_t, w1, w2)


def _stack_heads(q_ref, g, scale):
    return jnp.concatenate(
        [q_ref[:, (g * NSA_GROUP + r) * HEAD_DIM:(g * NSA_GROUP + r + 1) * HEAD_DIM] * scale for r in range(NSA_GROUP)],
        axis=0).astype(BF16)


def _stacked_row_info(past):
    rows = NSA_GROUP * ZROWS
    ridx = lax.broadcasted_iota(jnp.int32, (rows, 1), 0)
    zrow = ridx % ZROWS
    qpos = past + zrow - ZTOK0
    slope = jnp.zeros((rows, 1), F32)
    return ridx, zrow, qpos, slope


def _slope_col(g):
    ridx = lax.broadcasted_iota(jnp.int32, (NSA_GROUP * ZROWS, 1), 0)
    sl = jnp.zeros((NSA_GROUP * ZROWS, 1), F32)
    for r in range(NSA_GROUP):
        sl = jnp.where(ridx // ZROWS == r, SLOPES[g][r], sl)
    return sl


def _nsa_sample_a_kernel(q_ref, kvw_ref, ck_ref, cv_ref, win_ref, ocmp_ref, owin_ref, sel_ref, wout_ref, *, past, nc, n_blocks, seq):
    scale = HEAD_DIM ** -0.5
    _, zrow, qpos, _ = _stacked_row_info(past)
    wb = win_ref.shape[0] // 4
    lanes_pad = sel_ref.shape[-1]
    sel_lanes = ((n_blocks + HEAD_DIM - 1) // HEAD_DIM) * HEAD_DIM
    top_n = min(NSA_TOP_N, n_blocks)
    for g in range(NSA_KV_HEADS):
        q4 = _stack_heads(q_ref, g, scale)
        slope = _slope_col(g)
        cpos = lax.broadcasted_iota(jnp.int32, (1, nc), 1) * NSA_BLOCK + (NSA_BLOCK - 1)
        vm = cpos <= qpos
        s = jnp.where(vm, _dot_nt(q4, ck_ref[g]) + slope * (cpos - past).astype(F32), NEG)
        e = jnp.exp(s - jnp.max(s, axis=-1, keepdims=True))
        p = jnp.where(vm, e / jnp.sum(e, axis=-1, keepdims=True), 0.0)
        ocmp_ref[g] = _dot(p, cv_ref[g])
        imp = p[0:ZROWS]
        for r in range(1, NSA_GROUP):
            imp = imp + p[r * ZROWS:(r + 1) * ZROWS]
        imp = jnp.concatenate([imp, jnp.zeros((ZROWS, sel_lanes - nc), F32)], axis=1)
        rank = _rank_desc(_block_scores(imp, qpos[0:ZROWS], n_blocks), n_blocks)
        lane = lax.broadcasted_iota(jnp.int32, rank.shape, 1)
        olane = lax.broadcasted_iota(jnp.int32, (ZROWS, lanes_pad), 1)
        out = jnp.zeros((ZROWS, lanes_pad), jnp.int32)
        for t in range(top_n):
            idx = jnp.sum(jnp.where(rank == float(t), lane.astype(F32), 0.0), axis=-1, keepdims=True)
            out = jnp.where(olane == t, idx.astype(jnp.int32), out)
        sel_ref[g] = out
        kold = win_ref[pl.ds(g * 2, wb, stride=4), :]
        vold = win_ref[pl.ds(g * 2 + 1, wb, stride=4), :]
        knew = kvw_ref[:, g * 2 * HEAD_DIM:(g * 2 + 1) * HEAD_DIM]
        vnew = kvw_ref[:, (g * 2 + 1) * HEAD_DIM:(g * 2 + 2) * HEAD_DIM]
        kpos_o = past - wb + lax.broadcasted_iota(jnp.int32, (1, wb), 1)
        kz = lax.broadcasted_iota(jnp.int32, (1, ZROWS), 1)
        kpos_n = past + kz - ZTOK0
        d_o = qpos - kpos_o
        d_n = qpos - kpos_n
        ok_o = (kpos_o >= 0) & (d_o >= 0) & (d_o < NSA_WINDOW)
        ok_n = (kz >= ZTOK0) & (kz < ZTOK0 + seq) & (d_n >= 0) & (d_n < NSA_WINDOW)
        s_o = jnp.where(ok_o, _dot_nt(q4, kold) + slope * (kpos_o - past).astype(F32), NEG)
        s_n = jnp.where(ok_n, _dot_nt(q4, knew) + slope * (kpos_n - past).astype(F32), NEG)
        m = jnp.maximum(jnp.max(s_o, axis=-1, keepdims=True), jnp.max(s_n, axis=-1, keepdims=True))
        p_o = jnp.exp(s_o - m)
        p_n = jnp.exp(s_n - m)
        den = jnp.sum(p_o, axis=-1, keepdims=True) + jnp.sum(p_n, axis=-1, keepdims=True)
        owin_ref[g] = (_dot(p_o, vold) + _dot(p_n, vnew)) / den
    keep = (wb - seq) * 4
    wout_ref[0:keep, :] = win_ref[seq * 4:wb * 4, :]
    ridx = lax.broadcasted_iota(jnp.int32, (seq * 4, 1), 0)
    new_rows = jnp.zeros((seq * 4, HEAD_DIM), F32)
    for t in range(seq):
        for gc in range(4):
            new_rows = jnp.where(ridx == t * 4 + gc,
                                 kvw_ref[ZTOK0 + t:ZTOK0 + t + 1, gc * HEAD_DIM:(gc + 1) * HEAD_DIM], new_rows)
    wout_ref[keep:wb * 4, :] = new_rows


def nsa_sample_a(Ps, ckv, cache_win_rows, *, layer, past, seq):
    Bs = Ps.shape[0] // ZROWS
    nc = ckv.shape[3]
    wrows = cache_win_rows.shape[2]
    n_blocks = -(-(past + seq) // NSA_BLOCK)
    assert 1 <= seq <= 4 and (seq * 4) % 8 == 0 and wrows % 8 == 0
    rows = NSA_GROUP * ZROWS
    big = pl.BlockSpec((None, NSA_KV_HEADS, rows, HEAD_DIM), lambda b: (b, 0, 0, 0))
    return pl.pallas_call(
        functools.partial(_nsa_sample_a_kernel, past=past, nc=nc, n_blocks=n_blocks, seq=seq),
        grid=(Bs,),
        in_specs=[pl.BlockSpec((ZROWS, NSA_WIDTH), lambda b: (b, Q0 // NSA_WIDTH)),
                  pl.BlockSpec((ZROWS, KV_COLS), lambda b: (b, KVW0 // KV_COLS)),
                  pl.BlockSpec((None, None, NSA_KV_HEADS, nc, HEAD_DIM), lambda b: (b, 0, 0, 0, 0)),
                  pl.BlockSpec((None, None, NSA_KV_HEADS, nc, HEAD_DIM), lambda b: (b, 1, 0, 0, 0)),
                  pl.BlockSpec((None, None, wrows, HEAD_DIM), lambda b: (layer, b, 0, 0))],
        out_specs=[big, big,
                   pl.BlockSpec((None, NSA_KV_HEADS, ZROWS, HEAD_DIM), lambda b: (b, 0, 0, 0)),
                   pl.BlockSpec((None, wrows, HEAD_DIM), lambda b: (b, 0, 0))],
        out_shape=[jax.ShapeDtypeStruct((Bs, NSA_KV_HEADS, rows, HEAD_DIM), F32),
                   jax.ShapeDtypeStruct((Bs, NSA_KV_HEADS, rows, HEAD_DIM), F32),
                   jax.ShapeDtypeStruct((Bs, NSA_KV_HEADS, ZROWS, HEAD_DIM), jnp.int32),
                   jax.ShapeDtypeStruct((Bs, wrows, HEAD_DIM), F32)],
        compiler_params=_cparams("parallel"),
        name="nsa_sample_a",
    )(Ps, Ps, ckv, ckv, cache_win_rows)


def _nsa_sample_b_kernel(pt_ref, sel_ref, q_ref, kvs_ref, misc_ref, ocmp_ref, owin_ref, cache_ref, o_ref, kvbuf, sem,
                         *, layer, past, seq, n_past_blocks, n_pages):
    b = pl.program_id(0)
    g = pl.program_id(1)
    ngrp = pl.num_programs(1)
    step = b * ngrp + g
    nsteps = pl.num_programs(0) * ngrp
    slot = step % 2

    def copies(st, sl):
        bb = st // ngrp
        gg = st % ngrp
        out = []
        for t in range(seq):
            for j in range(NSA_TOP_N):
                blk = jnp.minimum(sel_ref[((bb * seq + t) * NSA_KV_HEADS + gg) * NSA_TOP_N + j], n_past_blocks - 1)
                page = pt_ref[bb * n_pages + blk // 2]
                row0 = pl.multiple_of((blk % 2) * BLOCK_ROWS, BLOCK_ROWS)
                out.append(pltpu.make_async_copy(
                    cache_ref.at[layer, page, pl.ds(row0, BLOCK_ROWS), :],
                    kvbuf.at[sl, t * NSA_TOP_N + j],
                    sem.at[sl]))
        return out

    @pl.when(step == 0)
    def _():
        for cp in copies(step, slot):
            cp.start()

    @pl.when(step + 1 < nsteps)
    def _():
        for cp in copies(step + 1, 1 - slot):
            cp.start()

    for cp in copies(step, slot):
        cp.wait()

    scale = HEAD_DIM ** -0.5
    nkeys = NSA_TOP_N * NSA_BLOCK
    _, zrow, qpos, _ = _stacked_row_info(past)
    q4 = jnp.where(g == 0, _stack_heads(q_ref, 0, scale), _stack_heads(q_ref, 1, scale))
    slope = jnp.where(g == 0, _slope_col(0), _slope_col(1))

    kn = jnp.where(g == 0, kvs_ref[:, 0:HEAD_DIM], kvs_ref[:, 2 * HEAD_DIM:3 * HEAD_DIM])
    vn = jnp.where(g == 0, kvs_ref[:, HEAD_DIM:2 * HEAD_DIM], kvs_ref[:, 3 * HEAD_DIM:4 * HEAD_DIM])
    kz = lax.broadcasted_iota(jnp.int32, (1, ZROWS), 1)
    kpos_n = past + kz - ZTOK0
    ok_n = (kz >= ZTOK0) & (kz < ZTOK0 + seq) & (kpos_n <= qpos)
    s_n = jnp.where(ok_n, _dot_nt(q4, kn) + slope * (kpos_n - past).astype(F32), NEG)

    lane = lax.broadcasted_iota(jnp.int32, (1, nkeys), 1)
    slot_of_lane = lane // NSA_BLOCK
    s_g = jnp.full((NSA_GROUP * ZROWS, nkeys), NEG, F32)
    v_tok = []
    for t in range(seq):
        blk_of_lane = jnp.zeros((1, nkeys), jnp.int32)
        for j in range(NSA_TOP_N):
            blk = sel_ref[((b * seq + t) * NSA_KV_HEADS + g) * NSA_TOP_N + j]
            blk_of_lane = jnp.where(slot_of_lane == j, blk, blk_of_lane)
        k_t = jnp.concatenate([kvbuf[slot, t * NSA_TOP_N + j, pl.ds(g * 2, NSA_BLOCK, stride=4), :]
                               for j in range(NSA_TOP_N)], axis=0)
        v_tok.append(jnp.concatenate([kvbuf[slot, t * NSA_TOP_N + j, pl.ds(g * 2 + 1, NSA_BLOCK, stride=4), :]
                                      for j in range(NSA_TOP_N)], axis=0))
        kpos = blk_of_lane * NSA_BLOCK + lane % NSA_BLOCK
        st = _dot_nt(q4, k_t) + slope * (kpos - past).astype(F32)
        mine = (zrow == ZTOK0 + t) & (blk_of_lane < n_past_blocks) & (kpos <= qpos)
        s_g = jnp.where(mine, st, s_g)

    m = jnp.maximum(jnp.max(s_g, axis=-1, keepdims=True), jnp.max(s_n, axis=-1, keepdims=True))
    p_g = jnp.exp(s_g - m)
    p_n = jnp.exp(s_n - m)
    den = jnp.sum(p_g, axis=-1, keepdims=True) + jnp.sum(p_n, axis=-1, keepdims=True)
    pv = _dot(p_n, vn)
    for t in range(seq):
        pv = pv + jnp.where(zrow == ZTOK0 + t, _dot(p_g, v_tok[t]), 0.0)
    o_slc = pv / den

    gates = jax.nn.sigmoid(misc_ref[...])
    o_cmp = ocmp_ref[...]
    o_win = owin_ref[...]
    for r in range(NSA_GROUP):
        rs = slice(r * ZROWS, (r + 1) * ZROWS)
        outs = []
        for gg in range(NSA_KV_HEADS):
            c = gg * NSA_GROUP * 3 + r * 3 + GATE_LANE
            outs.append(gates[:, c:c + 1] * o_cmp[rs] + gates[:, c + 1:c + 2] * o_slc[rs]
                        + gates[:, c + 2:c + 3] * o_win[rs])
        o_ref[:, r * HEAD_DIM:(r + 1) * HEAD_DIM] = jnp.where(g == 0, outs[0], outs[1]).astype(o_ref.dtype)


def nsa_sample_b(page_table, sel, Ps, ocmp, owin, cache_rows, *, layer, past, seq):
    Bs, n_pages = page_table.shape
    rows = NSA_GROUP * ZROWS
    n_past_blocks = past // NSA_BLOCK
    gw = NSA_GROUP * HEAD_DIM

    big = pl.BlockSpec((None, None, rows, HEAD_DIM), lambda b, g, pt, sl: (b, g, 0, 0))
    in_specs = [pl.BlockSpec((ZROWS, NSA_WIDTH), lambda b, g, pt, sl: (b, Q0 // NSA_WIDTH)),
                pl.BlockSpec((ZROWS, KV_COLS), lambda b, g, pt, sl: (b, KVS0 // KV_COLS)),
                pl.BlockSpec((ZROWS, HEAD_DIM), lambda b, g, pt, sl: (b, MISC0 // HEAD_DIM)),
                big, big,
                pl.BlockSpec(memory_space=pl.ANY)]
    grid_spec = pltpu.PrefetchScalarGridSpec(
        num_scalar_prefetch=2,
        grid=(Bs, NSA_KV_HEADS),
        in_specs=in_specs,
        out_specs=pl.BlockSpec((ZROWS, gw), lambda b, g, pt, sl: (b, g)),
        scratch_shapes=[pltpu.VMEM((2, seq * NSA_TOP_N, BLOCK_ROWS, HEAD_DIM), F32),
                        pltpu.SemaphoreType.DMA((2,))],
    )
    return pl.pallas_call(
        functools.partial(_nsa_sample_b_kernel, layer=layer, past=past, seq=seq, n_past_blocks=n_past_blocks,
                          n_pages=n_pages),
        grid_spec=grid_spec,
        out_shape=jax.ShapeDtypeStruct((Bs * ZROWS, NSA_WIDTH), BF16),
        compiler_params=_cparams("arbitrary", "arbitrary"),
        name="nsa_sample_b",
    )(page_table.reshape(-1), sel, Ps, Ps, Ps, ocmp, owin, cache_rows)


def _reorder_w_in(w_in):
    q, kvc, kvs, kvw = w_in[..., 0:1024], w_in[..., 1024:1536], w_in[..., 1536:2048], w_in[..., 2048:2560]
    gates, qkv, z, ab = w_in[..., 2560:2584], w_in[..., 2584:5656], w_in[..., 5656:6680], w_in[..., 6680:6696]
    pad = jnp.zeros(w_in.shape[:-1] + (IN_COLS_PAD - MISC0 - 40,), w_in.dtype)
    return jnp.concatenate([qkv, q, z, kvc, kvs, kvw, gates, ab, pad], axis=-1).astype(BF16)


def _lane_row(vec, lane0):
    return jnp.zeros((1, HEAD_DIM), F32).at[0, lane0:lane0 + vec.shape[0]].set(vec.astype(F32))


def _to_zrows(x, first_row):
    Bs, n, C = x.shape
    z = jnp.zeros((Bs, ZROWS, C), x.dtype).at[:, first_row:first_row + n].set(x)
    return z.reshape(Bs * ZROWS, C)


def kernel(x_prompt, x_sample, cache_cmp_kv, cache_slc_kv, cache_win_kv, state_gdn, state_gdn_conv, state_ffn_conv, page_table, w_in, w_o, cmp_pe, cmp_w1, cmp_w2, gdn_conv_w, gdn_a_log, gdn_dt_bias, gdn_norm_g, ln1_g, ln1_b, ffn_w_up, ffn_conv_w, ffn_w_down, ln2_g, ln2_b):
    B, S, D = x_prompt.shape
    Bs, seq, _ = x_sample.shape
    depth = w_in.shape[0]
    n_pool, page = cache_cmp_kv.shape[1], cache_cmp_kv.shape[2]
    n_pages = page_table.shape[1]
    past = n_pages * page
    wb = cache_win_kv.shape[2]
    d_ff = ffn_w_down.shape[1]
    alpha = (2 * depth) ** 0.25
    assert page == 2 * NSA_BLOCK

    w_in_r = _reorder_w_in(w_in)
    w_o_b = w_o.astype(BF16)
    w_up_b = ffn_w_up.astype(BF16)
    w_down_b = ffn_w_down.astype(BF16)
    w1_b = cmp_w1.astype(BF16)
    w2_b = cmp_w2.astype(BF16)
    pe_t = jnp.swapaxes(cmp_pe, 1, 2)
    cache_cmp_rows = cache_cmp_kv.reshape(depth, n_pool, PAGE_ROWS, HEAD_DIM)
    cache_slc_rows = cache_slc_kv.reshape(depth, n_pool, PAGE_ROWS, HEAD_DIM)
    cache_win_rows = cache_win_kv.reshape(depth, Bs, wb * 4, HEAD_DIM)

    xp = x_prompt.reshape(B * S, D)
    xs = _to_zrows(x_sample, ZTOK0)
    xp_b, xs_b = xp, xs
    tm = 512 if (B * S) % 512 == 0 else B * S
    tm_in = 1024 if (B * S) % 1024 == 0 else tm
    tk_down = 1408 if d_ff % 1408 == 0 else 512
    tf_up = d_ff // 2 if d_ff % (2 * MXU_COLS) == 0 else tk_down
    tq = 256 if S % 256 == 0 else 128
    p_out = [[] for _ in range(6)]
    s_out = [[] for _ in range(6)]
    zero_state = jnp.zeros((B, GDN_HEADS, HEAD_DIM, HEAD_DIM), F32)

    for l in range(depth):
        alog_row = _lane_row(gdn_a_log[l], A_LANE)
        dtb_row = _lane_row(gdn_dt_bias[l], A_LANE)
        ng = gdn_norm_g[l].reshape(1, HEAD_DIM)
        g1, b1 = ln1_g[l].reshape(1, D), ln1_b[l].reshape(1, D)
        g2, b2 = ln2_g[l].reshape(1, D), ln2_b[l].reshape(1, D)

        P = matmul(xp_b, w_in_r, l, tm=tm_in, tn=1024)
        kmat, vt = kv_prep(P, n_seq=B, tm=tm)
        ckv = compress_prompt(P, pe_t[l], w1_b[l], w2_b[l], n_seq=B)
        ckv = ckv.reshape(B, 2, NSA_KV_HEADS, S // NSA_BLOCK, HEAD_DIM).astype(BF16)
        o_nsa = nsa_prompt(P, kmat, vt, ckv[:, 0], jnp.swapaxes(ckv[:, 1], -1, -2), n_seq=B, tq=tq,
                           tk=256 if S % 256 == 0 else tq)
        prep = gdn_prep(P, gdn_conv_w[l], alog_row, dtb_row, n_seq=B, C=GDN_CHUNK,
                        nsub=4 if S % (4 * GDN_CHUNK) == 0 else 1)
        o_gdn, s_fin = gdn_scan(prep, P, ng, zero_state, n_seq=B, C=GDN_CHUNK)
        h, h_b = proj_ln(o_nsa, o_gdn, w_o_b, l, xp, g1, b1, alpha=alpha, tm=256)
        act, utail = ffn_up_prompt(h_b, w_up_b, l, ffn_conv_w[l], n_seq=B, tm=tm, tf=tf_up)
        xp, xp_b = ffn_down_ln_resident(act, w_down_b, l, h, g2, b2, alpha=alpha, tm=256)

        P3 = P.reshape(B, S, IN_COLS_PAD)
        p_out[0].append(P3[:, :, KVC0:KVC0 + KV_COLS].reshape(B, S, NSA_KV_HEADS, 2, HEAD_DIM))
        p_out[1].append(P3[:, :, KVS0:KVS0 + KV_COLS].reshape(B, S, NSA_KV_HEADS, 2, HEAD_DIM))
        wn = min(NSA_WINDOW, S)
        p_out[2].append(P3[:, S - wn:, KVW0:KVW0 + KV_COLS].reshape(B, wn, NSA_KV_HEADS, 2, HEAD_DIM))
        p_out[3].append(s_fin)
        p_out[4].append(P3[:, S - (GDN_CONV - 1):, QKV0:QKV0 + 3 * GDN_WIDTH])
        p_out[5].append(utail[:, 8 - (FFN_CONV - 1):, :])

        Ps = matmul(xs_b, w_in_r, l, tm=Bs * ZROWS, tn=1024)
        ckv_s = compress_sample(page_table, cache_cmp_rows, pe_t[l], w1_b[l], w2_b[l], layer=l)
        ng_grp = ckv_s.shape[1]
        nblk_step = ckv_s.shape[3] // NSA_KV_HEADS
        ckv_s = ckv_s.reshape(Bs, ng_grp, 2, NSA_KV_HEADS, nblk_step, HEAD_DIM)
        ckv_s = jnp.transpose(ckv_s, (0, 2, 3, 1, 4, 5)).reshape(Bs, 2, NSA_KV_HEADS, ng_grp * nblk_step, HEAD_DIM)
        ocmp, owin, sel, win_new = nsa_sample_a(Ps, ckv_s, cache_win_rows, layer=l, past=past, seq=seq)
        sel_flat = jnp.transpose(sel[:, :, ZTOK0:ZTOK0 + seq, :NSA_TOP_N], (0, 2, 1, 3)).reshape(-1)
        o_nsa_s = nsa_sample_b(page_table, sel_flat, Ps, ocmp, owin, cache_slc_rows, layer=l, past=past, seq=seq)
        gbufz = _to_zrows(state_gdn_conv[l], ZTOK0 - (GDN_CONV - 1))
        prep_s = gdn_prep(Ps, gdn_conv_w[l], alog_row, dtb_row, n_seq=Bs, C=ZROWS, bufz=gbufz)
        o_gdn_s, s_fin_s = gdn_scan(prep_s, Ps, ng, state_gdn[l], n_seq=Bs, C=ZROWS)
        hs, hs_b = proj_ln(o_nsa_s, o_gdn_s, w_o_b, l, xs, g1, b1, alpha=alpha, tm=Bs * ZROWS)
        fbufz = _to_zrows(state_ffn_conv[l], ZTOK0 - (FFN_CONV - 1))
        act_s, u_s = ffn_up_sample(hs_b, w_up_b, l, ffn_conv_w[l], fbufz, tf=tk_down)
        xs, xs_b = ffn_down_ln(act_s, w_down_b, l, hs, g2, b2, alpha=alpha, tm=Bs * ZROWS, tk=tk_down)

        Ps3 = Ps.reshape(Bs, ZROWS, IN_COLS_PAD)[:, ZTOK0:ZTOK0 + seq]
        s_out[0].append(Ps3[:, :, KVC0:KVC0 + KV_COLS].reshape(Bs, seq, NSA_KV_HEADS, 2, HEAD_DIM))
        s_out[1].append(Ps3[:, :, KVS0:KVS0 + KV_COLS].reshape(Bs, seq, NSA_KV_HEADS, 2, HEAD_DIM))
        s_out[2].append(win_new.reshape(Bs, wb, NSA_KV_HEADS, 2, HEAD_DIM))
        s_out[3].append(s_fin_s)
        ext_g = jnp.concatenate([state_gdn_conv[l], Ps3[:, :, QKV0:QKV0 + 3 * GDN_WIDTH]], axis=1)
        s_out[4].append(ext_g[:, -(GDN_CONV - 1):])
        u3 = u_s.reshape(Bs, ZROWS, d_ff)[:, ZTOK0:ZTOK0 + seq]
        ext_f = jnp.concatenate([state_ffn_conv[l], u3], axis=1)
        s_out[5].append(ext_f[:, -(FFN_CONV - 1):])

    y_p = xp.reshape(B, S, D)
    y_s = xs.reshape(Bs, ZROWS, D)[:, ZTOK0:ZTOK0 + seq]
    return (y_p, y_s, *[jnp.stack(v, 0) for v in p_out], *[jnp.stack(v, 0) for v in s_out])
```
